```python
import math
import jax
import jax.numpy as jnp
from jax import lax
import numpy as np

D_MODEL = 1024
BATCH = 2
SEQ = 8192
DEPTH = 2
DEC_BATCH = 32
DEC_SEQ = 4
PAST_LEN = 8192
PAGE_SIZE = 128

N_EVEN = (DEPTH + 1) // 2
N_ODD = DEPTH // 2
MIX_WIDTH = D_MODEL
A_WIDTH = MIX_WIDTH // 2
A_GROUPS = 4
A_GROUP_DIM = A_WIDTH // A_GROUPS
A_CHUNK = 128
B_HEADS = 8
B_HEAD_DIM = (MIX_WIDTH // 2) // B_HEADS
IDX_HEADS = 8
IDX_DIM = 64
TOPK_MAX = 256
C_HEADS = 4
C_DK = 128
C_DV = (MIX_WIDTH // 2) // C_HEADS
C_CONV = 4
C_CHUNK = 64
C_CONV_CH = C_HEADS * (2 * C_DK + C_DV)
D_HEADS = 8
D_QK_DIM = 32
D_V_DIM = (MIX_WIDTH // 2) // D_HEADS
N_BUCKETS = 32
MAX_DISTANCE = 128
REL_HEADS = 8
N_EXPERTS = 32
TOP_K = 4
D_FF = D_MODEL
SWIGLU_LIMIT = 7.0
SWIGLU_ALPHA = 1.702
MOE_BLOCK = 128
Q_BLOCK = 128
EPS = 1e-6
F32 = jnp.float32

EVEN_SPLITS = (A_WIDTH, A_WIDTH, B_HEADS * B_HEAD_DIM, B_HEADS * B_HEAD_DIM, B_HEADS * B_HEAD_DIM, IDX_HEADS * IDX_DIM, IDX_DIM, IDX_HEADS)
ODD_SPLITS = (C_HEADS * C_DK, C_HEADS * C_DK, C_HEADS * C_DV, C_HEADS * C_DV, C_HEADS, C_HEADS, D_HEADS * 2 * D_QK_DIM, D_HEADS * 2 * D_QK_DIM, D_HEADS * D_V_DIM)

kernel_name = 'hybrid_gmlp_dsa_gdn_diffattn_moe_step'


def split_cols(z, sizes):
    return jnp.split(z, np.cumsum(sizes)[:-1].tolist(), axis=-1)


def rms(x):
    xf = x.astype(F32)
    return (xf * lax.rsqrt(jnp.mean(xf * xf, axis=-1, keepdims=True) + EPS)).astype(x.dtype)


def layer_norm(x):
    xf = x.astype(F32)
    xc = xf - jnp.mean(xf, axis=-1, keepdims=True)
    return (xc * lax.rsqrt(jnp.mean(xc * xc, axis=-1, keepdims=True) + EPS)).astype(x.dtype)


def l2norm(x):
    xf = x.astype(F32)
    return (xf * lax.rsqrt(jnp.sum(xf * xf, axis=-1, keepdims=True) + EPS)).astype(x.dtype)


def lambda_init(layer):
    return 0.8 - 0.6 * math.exp(-0.3 * layer)


def rel_bucket(dist):
    n = jnp.maximum(dist, 0)
    exact = N_BUCKETS // 2
    scaled = jnp.log(jnp.maximum(n, 1).astype(F32) / exact) / math.log(MAX_DISTANCE / exact)
    large = jnp.minimum(exact + (scaled * (N_BUCKETS - exact)).astype(jnp.int32), N_BUCKETS - 1)
    return jnp.where(n < exact, n, large)


def chunk_gmlp(u, v, ln_g, ln_b, w_sp, b_sp):
    Bn, T, _ = u.shape
    c = min(T, A_CHUNK)
    vn = layer_norm(v) * ln_g + ln_b
    mask = jnp.tril(jnp.ones((c, c), bool))
    ws = jnp.where(mask, w_sp[:, :c, :c], 0.0)
    vr = vn.reshape(Bn, T // c, c, A_GROUPS, A_GROUP_DIM)
    mixed = jnp.einsum('gts,bnsgd->bntgd', ws, vr) + b_sp[:, :c].T[None, None, :, :, None]
    return u * mixed.reshape(Bn, T, A_WIDTH), vn


def index_scores(iq, ik, iw):
    dots = jnp.einsum('bthd,bsd->bths', iq, ik).astype(F32) * IDX_DIM ** -0.5
    return jnp.einsum('bth,bths->bts', iw.astype(F32), jax.nn.relu(dots))


def sparse_attend(q, ks, vs, sel, tpos, rel_bias):
    logits = jnp.einsum('bqhd,bqkhd->bhqk', q, ks).astype(F32) * B_HEAD_DIM ** -0.5
    dist = tpos[None, :, None] - sel
    bias = rel_bias[rel_bucket(dist)].astype(F32)
    logits = logits + jnp.transpose(bias, (0, 3, 1, 2))
    p = jax.nn.softmax(jnp.where((dist >= 0)[:, None], logits, -jnp.inf), axis=-1)
    return jnp.einsum('bhqk,bqkhd->bqhd', p.astype(vs.dtype), vs)


def dsa_prompt(q, k, v, iq, ik, iw, rel_bias):
    Bn, S = q.shape[:2]
    ksel = min(TOPK_MAX, S // 4)
    spos = jnp.arange(S)
    take = jax.vmap(lambda a, i: a[i])

    def block(i):
        t0 = i * Q_BLOCK
        qb = lax.dynamic_slice_in_dim(q, t0, Q_BLOCK, axis=1)
        iqb = lax.dynamic_slice_in_dim(iq, t0, Q_BLOCK, axis=1)
        iwb = lax.dynamic_slice_in_dim(iw, t0, Q_BLOCK, axis=1)
        tpos = t0 + jnp.arange(Q_BLOCK)
        sc = index_scores(iqb, ik, iwb)
        sc = jnp.where(spos[None, None, :] <= tpos[None, :, None], sc, -jnp.inf)
        _, sel = lax.top_k(sc, ksel)
        return sparse_attend(qb, take(k, sel), take(v, sel), sel, tpos, rel_bias)

    out = lax.map(block, jnp.arange(S // Q_BLOCK))
    return jnp.moveaxis(out, 0, 1).reshape(Bn, S, B_HEADS * B_HEAD_DIM)


def dsa_sample(q, k, v, iq, ik, iw, li, cache_k, cache_v, cache_idx, page_table, rel_bias):
    Bn, T = q.shape[:2]
    L = PAST_LEN + T
    ksel = min(TOPK_MAX, L // 4)
    ik_past = cache_idx[li, page_table].reshape(Bn, PAST_LEN, IDX_DIM)
    sc = index_scores(iq, jnp.concatenate([ik_past.astype(ik.dtype), ik], axis=1), iw)
    tpos = PAST_LEN + jnp.arange(T)
    sc = jnp.where(jnp.arange(L)[None, None, :] <= tpos[None, :, None], sc, -jnp.inf)
    _, sel = lax.top_k(sc, ksel)
    bidx = jnp.arange(Bn)[:, None, None]
    sp = jnp.minimum(sel, PAST_LEN - 1)
    phys = page_table[bidx, sp // PAGE_SIZE]
    off = sp % PAGE_SIZE
    sn = jnp.clip(sel - PAST_LEN, 0, T - 1)
    in_past = (sel < PAST_LEN)[..., None, None]
    ks = jnp.where(in_past, cache_k[li, phys, off].astype(k.dtype), k[bidx, sn])
    vs = jnp.where(in_past, cache_v[li, phys, off].astype(v.dtype), v[bidx, sn])
    return sparse_attend(q, ks, vs, sel, tpos, rel_bias).reshape(Bn, T, B_HEADS * B_HEAD_DIM)


def even_mixer(h, w_in, w_out, ln_g, ln_b, w_sp, b_sp, q_g, k_g, ik_g, rel_bias, past):
    Bn, T, _ = h.shape
    au, av, bq, bk, bv, iq, ik, iw = split_cols(h @ w_in, EVEN_SPLITS)
    a_out, a_v = chunk_gmlp(jax.nn.gelu(au), jax.nn.gelu(av), ln_g, ln_b, w_sp, b_sp)
    q = rms(bq.reshape(Bn, T, B_HEADS, B_HEAD_DIM)) * q_g
    k = rms(bk.reshape(Bn, T, B_HEADS, B_HEAD_DIM)) * k_g
    v = bv.reshape(Bn, T, B_HEADS, B_HEAD_DIM)
    iq = iq.reshape(Bn, T, IDX_HEADS, IDX_DIM)
    ik = rms(ik) * ik_g
    iw = iw * IDX_HEADS ** -0.5
    if past is None:
        b_out = dsa_prompt(q, k, v, iq, ik, iw, rel_bias)
    else:
        li, cache_k, cache_v, cache_idx, page_table = past
        b_out = dsa_sample(q, k, v, iq, ik, iw, li, cache_k, cache_v, cache_idx, page_table, rel_bias)
    y = jnp.concatenate([a_out, b_out.astype(a_out.dtype)], axis=-1) @ w_out
    return y, a_v, k, v, ik


def causal_conv(xc, buf, w):
    xp = jnp.concatenate([buf.astype(xc.dtype), xc], axis=1)
    y = lax.conv_general_dilated(xp, w[:, None, :].astype(xc.dtype), (1,), 'VALID',
                                 dimension_numbers=('NWC', 'WIO', 'NWC'), feature_group_count=xc.shape[-1])
    return jax.nn.silu(y), xp[:, xp.shape[1] - (C_CONV - 1):]


def gated_delta(q, k, v, g, beta, s0, chunk):
    Bn, T, H, _ = q.shape
    n = T // chunk

    def chunks(x):
        return x.astype(F32).reshape(Bn, n, chunk, H, x.shape[-1]).transpose(1, 0, 3, 2, 4)

    q, k, v = chunks(q), chunks(k), chunks(v)
    g = g.reshape(Bn, n, chunk, H).transpose(1, 0, 3, 2)
    beta = beta.reshape(Bn, n, chunk, H).transpose(1, 0, 3, 2)
    gc = jnp.cumsum(g, axis=-1)
    incl = jnp.tril(jnp.ones((chunk, chunk), bool))
    eye = jnp.eye(chunk, dtype=F32)
    decay = jnp.exp(jnp.where(incl, gc[..., :, None] - gc[..., None, :], -jnp.inf))
    kb = k * beta[..., None]
    tmat = jnp.einsum('nbhid,nbhjd->nbhij', kb, k) * decay * (1.0 - eye) + eye
    u = lax.linalg.triangular_solve(tmat, v * beta[..., None], left_side=True, lower=True, unit_diagonal=True)
    w = lax.linalg.triangular_solve(tmat, kb * jnp.exp(gc)[..., None], left_side=True, lower=True, unit_diagonal=True)

    def step(S, xs):
        qi, ki, ui, wi, gci, di = xs
        v_new = ui - jnp.einsum('bhid,bhde->bhie', wi, S)
        intra = jnp.einsum('bhid,bhjd->bhij', qi, ki) * di
        o = jnp.einsum('bhid,bhde->bhie', qi * jnp.exp(gci)[..., None], S) + jnp.einsum('bhij,bhje->bhie', intra, v_new)
        gl = gci[..., -1:]
        S = S * jnp.exp(gl)[..., None] + jnp.einsum('bhid,bhie->bhde', ki * jnp.exp(gl - gci)[..., None], v_new)
        return S, o

    S, o = lax.scan(step, s0.astype(F32), (q, k, u, w, gc, decay))
    return o.transpose(1, 0, 3, 2, 4).reshape(Bn, T, H, v.shape[-1]), S


def diff_core(q, k, v, bias, allowed, lam):
    logits = jnp.einsum('bqhmd,bkhmd->bhmqk', q, k).astype(F32) * D_QK_DIM ** -0.5 + bias[None, :, None]
    p = jax.nn.softmax(jnp.where(allowed, logits, -jnp.inf), axis=-1)
    pd = p[:, :, 0] - lam * p[:, :, 1]
    return jnp.einsum('bhqk,bkhe->bqhe', pd.astype(v.dtype), v)


def diff_prompt(q, k, v, lam, rel_bias):
    Bn, S = q.shape[:2]
    spos = jnp.arange(S)

    def block(i):
        t0 = i * Q_BLOCK
        qb = lax.dynamic_slice_in_dim(q, t0, Q_BLOCK, axis=1)
        dist = (t0 + jnp.arange(Q_BLOCK))[:, None] - spos[None, :]
        bias = jnp.transpose(rel_bias[rel_bucket(dist)], (2, 0, 1)).astype(F32)
        return diff_core(qb, k, v, bias, dist >= 0, lam)

    out = lax.map(block, jnp.arange(S // Q_BLOCK))
    return jnp.moveaxis(out, 0, 1).reshape(Bn, S, D_HEADS, D_V_DIM)


def diff_sample(q, k, v, lam, li, cache_k, cache_v, page_table, rel_bias):
    T = q.shape[1]
    dist = (PAST_LEN + jnp.arange(T))[:, None] - jnp.arange(PAST_LEN + T)[None, :]
    bias = jnp.transpose(rel_bias[rel_bucket(dist)], (2, 0, 1)).astype(F32)

    def one(args):
        qs, ks, vs, pt = args
        kp = cache_k[li, pt].reshape(PAST_LEN, D_HEADS, 2, D_QK_DIM).astype(ks.dtype)
        vp = cache_v[li, pt].reshape(PAST_LEN, D_HEADS, D_V_DIM).astype(vs.dtype)
        kk = jnp.concatenate([kp, ks], axis=0)[None]
        vv = jnp.concatenate([vp, vs], axis=0)[None]
        return diff_core(qs[None], kk, vv, bias, dist >= 0, lam)[0]

    return lax.map(one, (q, k, v, page_table))


def odd_mixer(h, w_in, w_out, conv_w, a_log, dt_bias, norm_w, q_g, k_g, lam_p, subln_g, rel_bias, lam_init, past):
    Bn, T, _ = h.shape
    cq, ck, cv, cz, cb, ca, d_q, d_k, d_v = split_cols(h @ w_in, ODD_SPLITS)
    if past is None:
        s0 = jnp.zeros((Bn, C_HEADS, C_DK, C_DV), F32)
        buf = jnp.zeros((Bn, C_CONV - 1, C_CONV_CH), h.dtype)
    else:
        li, s0, buf, cache_k, cache_v, page_table = past
    qkv, new_buf = causal_conv(jnp.concatenate([cq, ck, cv], axis=-1), buf, conv_w)
    q, k, v = split_cols(qkv, (C_HEADS * C_DK, C_HEADS * C_DK, C_HEADS * C_DV))
    q = l2norm(q.reshape(Bn, T, C_HEADS, C_DK)) * C_DK ** -0.5
    k = l2norm(k.reshape(Bn, T, C_HEADS, C_DK))
    beta = jax.nn.sigmoid(cb.astype(F32))
    g = -jnp.exp(a_log.astype(F32)) * jax.nn.softplus(ca.astype(F32) + dt_bias.astype(F32))
    chunk = C_CHUNK if T % C_CHUNK == 0 else T
    o, s_new = gated_delta(q, k, v.reshape(Bn, T, C_HEADS, C_DV), g, beta, s0, chunk)
    c_out = rms(o) * norm_w.astype(F32) * jax.nn.silu(cz.reshape(Bn, T, C_HEADS, C_DV).astype(F32))
    q2 = rms(d_q.reshape(Bn, T, D_HEADS, 2, D_QK_DIM)) * q_g
    k2 = rms(d_k.reshape(Bn, T, D_HEADS, 2, D_QK_DIM)) * k_g
    v2 = d_v.reshape(Bn, T, D_HEADS, D_V_DIM)
    lp = lam_p.astype(F32)
    lam = jnp.exp(jnp.sum(lp[0] * lp[1])) - jnp.exp(jnp.sum(lp[2] * lp[3])) + lam_init
    if past is None:
        att = diff_prompt(q2, k2, v2, lam, rel_bias)
    else:
        att = diff_sample(q2, k2, v2, lam, li, cache_k, cache_v, page_table, rel_bias)
    d_out = rms(att) * subln_g * (1.0 - lam_init)
    y = jnp.concatenate([c_out.reshape(Bn, T, -1).astype(h.dtype), d_out.reshape(Bn, T, -1).astype(h.dtype)], axis=-1) @ w_out
    return y, s_new, new_buf, k2.reshape(Bn, T, D_HEADS, 2 * D_QK_DIM), v2


def expert_ffn(x, w1, b1, w2, b2):
    gate, up = jnp.split(x @ w1 + b1, 2, axis=-1)
    gate = jnp.minimum(gate, SWIGLU_LIMIT)
    up = jnp.clip(up, -SWIGLU_LIMIT, SWIGLU_LIMIT)
    return ((up + 1.0) * gate * jax.nn.sigmoid(SWIGLU_ALPHA * gate)) @ w2 + b2


def moe(h, layer, router_w, router_b, w1, b1, w2, b2):
    shape = h.shape
    xf = h.reshape(-1, D_MODEL)
    n_tok = xf.shape[0]
    logits = xf.astype(F32) @ router_w[layer].astype(F32) + router_b[layer].astype(F32)
    top_val, top_idx = lax.top_k(logits, TOP_K)
    gates = jax.nn.softmax(top_val, axis=-1)
    n_assign = n_tok * TOP_K
    e_flat = top_idx.reshape(-1)
    order = jnp.argsort(e_flat)
    e_sorted = e_flat[order]
    counts = jnp.bincount(e_flat, length=N_EXPERTS)
    padded = (counts + MOE_BLOCK - 1) // MOE_BLOCK * MOE_BLOCK
    pad_end = jnp.cumsum(padded)
    pad_start = pad_end - padded
    start = jnp.cumsum(counts) - counts
    dest = pad_start[e_sorted] + jnp.arange(n_assign) - start[e_sorted]
    n_blocks = -(-(n_assign + N_EXPERTS * (MOE_BLOCK - 1)) // MOE_BLOCK)
    n_slots = n_blocks * MOE_BLOCK
    slot_tok = jnp.full((n_slots,), n_tok, jnp.int32).at[dest].set((order // TOP_K).astype(jnp.int32))
    slot_gate = jnp.zeros((n_slots,), F32).at[dest].set(gates.reshape(-1)[order])
    block_e = jnp.minimum(jnp.searchsorted(pad_end, jnp.arange(n_blocks) * MOE_BLOCK, side='right'), N_EXPERTS - 1)
    x_pad = jnp.concatenate([xf, jnp.zeros((1, D_MODEL), xf.dtype)], axis=0)

    def run_block(args):
        tok, gate, e = args
        y = expert_ffn(x_pad[tok], w1[layer, e], b1[layer, e], w2[layer, e], b2[layer, e])
        return y * gate[:, None].astype(y.dtype)

    ys = lax.map(run_block, (slot_tok.reshape(n_blocks, MOE_BLOCK), slot_gate.reshape(n_blocks, MOE_BLOCK), block_e))
    out = jnp.zeros((n_tok + 1, D_MODEL), ys.dtype).at[slot_tok].add(ys.reshape(n_slots, D_MODEL))
    return out[:n_tok].reshape(shape)


def run_trunk(x, c, st, p):
    mod_c = jax.nn.silu(c)
    new = {'a_v': [], 'b_k': [], 'b_v': [], 'b_idx': [], 'c_rec': [], 'c_conv': [], 'd_k': [], 'd_v': []}
    for layer in range(DEPTH):
        li = layer // 2
        mod = mod_c @ p['ada_w'][layer] + p['ada_b'][layer]
        sh1, sc1, g1, sh2, sc2, g2 = [m[:, None, :] for m in jnp.split(mod, 6, axis=-1)]
        h = rms(x) * (1.0 + sc1) + sh1
        if layer % 2 == 0:
            past = None if st is None else (li, st['cache_b_k'], st['cache_b_v'], st['cache_b_idx'], st['page_table'])
            y, a_v, bk, bv, bidx = even_mixer(h, p['even_w_in'][li], p['even_w_out'][li], p['a_ln_g'][li], p['a_ln_b'][li],
                                             p['a_w_sp'][li], p['a_b_sp'][li], p['b_q_norm'][li], p['b_k_norm'][li],
                                             p['b_idx_norm'][li], p['rel_bias'], past)
            new['a_v'].append(a_v)
            new['b_k'].append(bk)
            new['b_v'].append(bv)
            new['b_idx'].append(bidx)
        else:
            past = None if st is None else (li, st['state_c_rec'][li], st['state_c_conv'][li], st['cache_d_k'], st['cache_d_v'], st['page_table'])
            y, c_rec, c_conv, dk, dv = odd_mixer(h, p['odd_w_in'][li], p['odd_w_out'][li], p['c_conv_w'][li], p['c_a_log'][li],
                                                p['c_dt_bias'][li], p['c_norm_w'][li], p['d_q_norm'][li], p['d_k_norm'][li],
                                                p['d_lambda'][li], p['d_subln'][li], p['rel_bias'], lambda_init(layer), past)
            new['c_rec'].append(c_rec)
            new['c_conv'].append(c_conv)
            new['d_k'].append(dk)
            new['d_v'].append(dv)
        x = x + g1 * y
        h = rms(x) * (1.0 + sc2) + sh2
        x = x + g2 * moe(h, layer, p['router_w'], p['router_b'], p['moe_w1'], p['moe_b1'], p['moe_w2'], p['moe_b2'])
    return x, new


def setup_inputs(seed: int = 0) -> dict:
    key = jax.random.key(seed)
    keys = iter(jax.random.split(key, 48))

    def nrm(shape, scale):
        return jax.random.normal(next(keys), shape, F32) * scale

    def gain(shape):
        return 1.0 + nrm(shape, 0.1)

    n_pages = PAST_LEN // PAGE_SIZE
    n_used = DEC_BATCH * n_pages
    n_pool = n_used + (n_used + 3) // 4
    page_table = jax.random.permutation(next(keys), n_pool)[:n_used].reshape(DEC_BATCH, n_pages).astype(jnp.int32)
    even_cols = sum(EVEN_SPLITS)
    odd_cols = sum(ODD_SPLITS)
    even_mix = A_WIDTH + B_HEADS * B_HEAD_DIM
    odd_mix = C_HEADS * C_DV + D_HEADS * D_V_DIM
    return {
        'x_prompt': nrm((BATCH, SEQ, D_MODEL), 1.0),
        'x_sample': nrm((DEC_BATCH, DEC_SEQ, D_MODEL), 1.0),
        'cache_b_k': nrm((N_EVEN, n_pool, PAGE_SIZE, B_HEADS, B_HEAD_DIM), 1.0),
        'cache_b_v': nrm((N_EVEN, n_pool, PAGE_SIZE, B_HEADS, B_HEAD_DIM), 1.0),
        'cache_b_idx': nrm((N_EVEN, n_pool, PAGE_SIZE, IDX_DIM), 1.0),
        'state_c_rec': nrm((N_ODD, DEC_BATCH, C_HEADS, C_DK, C_DV), 0.1),
        'state_c_conv': nrm((N_ODD, DEC_BATCH, C_CONV - 1, C_CONV_CH), 1.0),
        'cache_d_k': nrm((N_ODD, n_pool, PAGE_SIZE, D_HEADS, 2 * D_QK_DIM), 1.0),
        'cache_d_v': nrm((N_ODD, n_pool, PAGE_SIZE, D_HEADS, D_V_DIM), 1.0),
        'page_table': page_table,
        'c_prompt': nrm((BATCH, D_MODEL), 1.0),
        'c_sample': nrm((DEC_BATCH, D_MODEL), 1.0),
        'rel_bias': nrm((N_BUCKETS, REL_HEADS), 0.5),
        'ada_w': nrm((DEPTH, D_MODEL, 6 * D_MODEL), 0.5 * D_MODEL ** -0.5),
        'ada_b': nrm((DEPTH, 6 * D_MODEL), 0.02),
        'even_w_in': nrm((N_EVEN, D_MODEL, even_cols), D_MODEL ** -0.5),
        'even_w_out': nrm((N_EVEN, even_mix, D_MODEL), even_mix ** -0.5),
        'a_ln_g': gain((N_EVEN, A_WIDTH)),
        'a_ln_b': nrm((N_EVEN, A_WIDTH), 0.1),
        'a_w_sp': nrm((N_EVEN, A_GROUPS, A_CHUNK, A_CHUNK), A_CHUNK ** -0.5),
        'a_b_sp': gain((N_EVEN, A_GROUPS, A_CHUNK)),
        'b_q_norm': gain((N_EVEN, B_HEAD_DIM)),
        'b_k_norm': gain((N_EVEN, B_HEAD_DIM)),
        'b_idx_norm': gain((N_EVEN, IDX_DIM)),
        'odd_w_in': nrm((N_ODD, D_MODEL, odd_cols), D_MODEL ** -0.5),
        'odd_w_out': nrm((N_ODD, odd_mix, D_MODEL), odd_mix ** -0.5),
        'c_conv_w': nrm((N_ODD, C_CONV, C_CONV_CH), C_CONV ** -0.5),
        'c_a_log': jnp.log(jax.random.uniform(next(keys), (N_ODD, C_HEADS), F32, 0.5, 8.0)),
        'c_dt_bias': -2.0 + nrm((N_ODD, C_HEADS), 0.1),
        'c_norm_w': gain((N_ODD, C_DV)),
        'd_q_norm': gain((N_ODD, D_QK_DIM)),
        'd_k_norm': gain((N_ODD, D_QK_DIM)),
        'd_lambda': nrm((N_ODD, 4, D_QK_DIM), 0.1),
        'd_subln': gain((N_ODD, D_V_DIM)),
        'router_w': nrm((DEPTH, D_MODEL, N_EXPERTS), D_MODEL ** -0.5),
        'router_b': nrm((DEPTH, N_EXPERTS), 0.01),
        'moe_w1': nrm((DEPTH, N_EXPERTS, D_MODEL, 2 * D_FF), D_MODEL ** -0.5),
        'moe_b1': nrm((DEPTH, N_EXPERTS, 2 * D_FF), 0.02),
        'moe_w2': nrm((DEPTH, N_EXPERTS, D_FF, D_MODEL), D_FF ** -0.5),
        'moe_b2': nrm((DEPTH, N_EXPERTS, D_MODEL), 0.02),
    }


def reference(x_prompt, x_sample, cache_b_k, cache_b_v, cache_b_idx, state_c_rec, state_c_conv, cache_d_k, cache_d_v,
              page_table, c_prompt, c_sample, rel_bias, ada_w, ada_b, even_w_in, even_w_out, a_ln_g, a_ln_b, a_w_sp,
              a_b_sp, b_q_norm, b_k_norm, b_idx_norm, odd_w_in, odd_w_out, c_conv_w, c_a_log, c_dt_bias, c_norm_w,
              d_q_norm, d_k_norm, d_lambda, d_subln, router_w, router_b, moe_w1, moe_b1, moe_w2, moe_b2):
    p = {'rel_bias': rel_bias, 'ada_w': ada_w, 'ada_b': ada_b, 'even_w_in': even_w_in, 'even_w_out': even_w_out,
         'a_ln_g': a_ln_g, 'a_ln_b': a_ln_b, 'a_w_sp': a_w_sp, 'a_b_sp': a_b_sp, 'b_q_norm': b_q_norm,
         'b_k_norm': b_k_norm, 'b_idx_norm': b_idx_norm, 'odd_w_in': odd_w_in, 'odd_w_out': odd_w_out,
         'c_conv_w': c_conv_w, 'c_a_log': c_a_log, 'c_dt_bias': c_dt_bias, 'c_norm_w': c_norm_w,
         'd_q_norm': d_q_norm, 'd_k_norm': d_k_norm, 'd_lambda': d_lambda, 'd_subln': d_subln,
         'router_w': router_w, 'router_b': router_b, 'moe_w1': moe_w1, 'moe_b1': moe_b1, 'moe_w2': moe_w2, 'moe_b2': moe_b2}
    st = {'cache_b_k': cache_b_k, 'cache_b_v': cache_b_v, 'cache_b_idx': cache_b_idx, 'state_c_rec': state_c_rec,
          'state_c_conv': state_c_conv, 'cache_d_k': cache_d_k, 'cache_d_v': cache_d_v, 'page_table': page_table}
    y_prompt, npr = run_trunk(x_prompt, c_prompt, None, p)
    y_sample, nsm = run_trunk(x_sample, c_sample, st, p)
    return (y_prompt, y_sample,
            jnp.stack(npr['b_k']), jnp.stack(npr['b_v']), jnp.stack(npr['b_idx']),
            jnp.stack(npr['c_rec']), jnp.stack(npr['c_conv']), jnp.stack(npr['d_k']), jnp.stack(npr['d_v']),
            jnp.stack(nsm['a_v']), jnp.stack(nsm['b_k']), jnp.stack(nsm['b_v']), jnp.stack(nsm['b_idx']),
            jnp.stack(nsm['c_rec']), jnp.stack(nsm['c_conv']), jnp.stack(nsm['d_k']), jnp.stack(nsm['d_v']))
```

```python
import functools
import math

import numpy as np
import jax
import jax.numpy as jnp
from jax import lax
from jax.experimental import pallas as pl
from jax.experimental.pallas import tpu as pltpu

F32 = jnp.float32
BF16 = jnp.bfloat16
I32 = jnp.int32
HIGHEST = lax.Precision.HIGHEST

A_GROUPS = 4
A_CHUNK = 128
B_HEADS = 8
IDX_HEADS = 8
IDX_DIM = 64
TOPK_MAX = 256
C_HEADS = 4
C_DK = 128
C_CONV = 4
C_CHUNK = 64
D_HEADS = 8
D_QK_DIM = 32
N_BUCKETS = 32
MAX_DISTANCE = 128
TOP_K = 4
SWIGLU_LIMIT = 7.0
SWIGLU_ALPHA = 1.702
EPS = 1e-6

LANES = 128
SUBLANES = 8
VMEM_LIMIT_BYTES = 56 * 1024 * 1024

NEG = -1e30
INT_MIN = -2 ** 31


def _cparams(sem):
    return pltpu.CompilerParams(dimension_semantics=sem, vmem_limit_bytes=VMEM_LIMIT_BYTES)


def _dot(a, b):
    return jnp.dot(a.astype(BF16), b.astype(BF16), preferred_element_type=F32)


def _dot_nt(a, b):
    return lax.dot_general(a.astype(BF16), b.astype(BF16), (((1,), (1,)), ((), ())), preferred_element_type=F32)


def _dot_hi(a, b):
    return jnp.dot(a, b, preferred_element_type=F32, precision=HIGHEST)


def _dot_nt_hi(a, b):
    return lax.dot_general(a, b, (((1,), (1,)), ((), ())), preferred_element_type=F32, precision=HIGHEST)


def _dot_tn_hi(a, b):
    return lax.dot_general(a, b, (((0,), (0,)), ((), ())), preferred_element_type=F32, precision=HIGHEST)


def _silu(x):
    return x * jax.nn.sigmoid(x)


def _gelu_tanh(x):
    return 0.5 * x * (1.0 + jnp.tanh(math.sqrt(2.0 / math.pi) * (x + 0.044715 * (x * x * x))))


def _group_rms(x, ones_ref, gsize):
    xx = x * x
    hi = xx.astype(BF16)
    lo = (xx - hi.astype(F32)).astype(BF16)
    e = ones_ref[...]
    ss = jnp.dot(hi, e, preferred_element_type=F32) + jnp.dot(lo, e, preferred_element_type=F32)
    return x * lax.rsqrt(ss * (1.0 / gsize) + EPS)


def _block_ones(width, gsize):
    g = np.arange(width) // gsize
    return jnp.asarray(g[:, None] == g[None, :], BF16)


def _bucket_table():
    n = np.arange(MAX_DISTANCE)
    exact = N_BUCKETS // 2
    scaled = np.log(np.maximum(n, 1).astype(np.float32) / np.float32(exact)) / np.float32(math.log(MAX_DISTANCE / exact))
    large = np.minimum(exact + (scaled.astype(np.float32) * (N_BUCKETS - exact)).astype(np.int32), N_BUCKETS - 1)
    return np.where(n < exact, n, large).astype(np.int32)


def _ada_kernel(c_ref, w_ref, b_ref, o_ref):
    o_ref[...] = _dot(_silu(c_ref[...]), w_ref[...]) + b_ref[...]


def _ada_mod(c_all, ada_w, ada_b):
    depth, d, n6 = ada_w.shape
    rows = c_all.shape[0]
    tn = 1536 if n6 % 1536 == 0 else n6
    return pl.pallas_call(
        _ada_kernel,
        grid=(depth, n6 // tn),
        in_specs=[pl.BlockSpec((rows, d), lambda l, j: (0, 0)),
                  pl.BlockSpec((None, d, tn), lambda l, j: (l, 0, j)),
                  pl.BlockSpec((None, 1, tn), lambda l, j: (l, 0, j))],
        out_specs=pl.BlockSpec((None, rows, tn), lambda l, j: (l, 0, j)),
        out_shape=jax.ShapeDtypeStruct((depth, rows, n6), F32),
        compiler_params=_cparams(("arbitrary", "arbitrary")),
        name="ada_mod",
    )(c_all, ada_w, ada_b.reshape(depth, 1, n6))


def _mod_specs(per_token, tm, d, tiles_per_batch):
    if per_token:
        return pl.BlockSpec((tm, d), lambda i: (i, 0))
    return pl.BlockSpec((None, 1, d), lambda i: (i // tiles_per_batch, 0, 0))


def _modulated_rms(x, sh, sc):
    xn = x * lax.rsqrt(jnp.mean(x * x, axis=-1, keepdims=True) + EPS)
    return xn * (1.0 + sc) + sh


def _even_in_kernel(x_ref, sh_ref, sc_ref, wm_ref, wt_ref, lng_ref, lnb_ref, msp_ref, bsp_ref,
                    qg_ref, kg_ref, ikg_ref, e_ref,
                    aout_ref, av_ref, q_ref, k32_ref, kbf_ref, v32_ref, vbf_ref, iq_ref, ik32_ref, ikbf_ref, iw_ref):
    tm = x_ref.shape[0]
    aw = aout_ref.shape[1]
    gd = aw // A_GROUPS
    h = _modulated_rms(x_ref[...], sh_ref[...], sc_ref[...]).astype(BF16)

    def seg(i):
        return jnp.dot(h, wm_ref[:, i * aw:(i + 1) * aw], preferred_element_type=F32)

    au = _gelu_tanh(seg(0))
    av = _gelu_tanh(seg(1))
    avc = av - jnp.mean(av, axis=-1, keepdims=True)
    vn = avc * lax.rsqrt(jnp.mean(avc * avc, axis=-1, keepdims=True) + EPS) * lng_ref[...] + lnb_ref[...]
    av_ref[...] = vn
    vnb = vn.astype(BF16)
    cr = msp_ref.shape[1]
    for c in range(tm // cr):
        r0 = c * cr
        for g in range(A_GROUPS):
            mixed = jnp.dot(msp_ref[g], vnb[r0:r0 + cr, g * gd:(g + 1) * gd], preferred_element_type=F32)
            mixed = mixed + bsp_ref[:, g:g + 1]
            aout_ref[r0:r0 + cr, g * gd:(g + 1) * gd] = (au[r0:r0 + cr, g * gd:(g + 1) * gd] * mixed).astype(BF16)

    hd = aw // B_HEADS
    q = _group_rms(seg(2), e_ref, hd) * qg_ref[...]
    q_ref[...] = (q * (hd ** -0.5)).astype(BF16)
    k = _group_rms(seg(3), e_ref, hd) * kg_ref[...]
    k32_ref[...] = k
    kbf_ref[...] = k.astype(BF16)
    v = seg(4)
    v32_ref[...] = v
    vbf_ref[...] = v.astype(BF16)
    iq_ref[...] = (seg(5) * (IDX_DIM ** -0.5)).astype(BF16)
    tail = jnp.dot(h, wt_ref[...], preferred_element_type=F32)
    ik = tail[:, :IDX_DIM]
    ik = ik * lax.rsqrt(jnp.mean(ik * ik, axis=-1, keepdims=True) + EPS) * ikg_ref[...]
    ik32_ref[...] = ik
    ikbf_ref[...] = ik.astype(BF16)
    iw_ref[...] = tail[:, IDX_DIM:IDX_DIM + IDX_HEADS] * (IDX_HEADS ** -0.5)


def _even_in(x, sh, sc, per_token, tiles_per_batch, tm, w_in, ln_g, ln_b, msp, bsp, q_g, k_g, ik_g):
    n, d = x.shape
    aw = d // 2
    wm = w_in[:, :6 * aw].astype(BF16)
    wt = jnp.pad(w_in[:, 6 * aw:], ((0, 0), (0, LANES - (IDX_DIM + IDX_HEADS)))).astype(BF16)
    full = lambda shape: pl.BlockSpec(shape, lambda i: (0,) * len(shape))
    row = lambda w: pl.BlockSpec((tm, w), lambda i: (i, 0))
    mod = _mod_specs(per_token, tm, d, tiles_per_batch)
    outs = [(aw, BF16), (aw, F32), (aw, BF16), (aw, F32), (aw, BF16), (aw, F32), (aw, BF16), (aw, BF16),
            (IDX_DIM, F32), (IDX_DIM, BF16), (IDX_HEADS, F32)]
    return pl.pallas_call(
        _even_in_kernel,
        grid=(n // tm,),
        in_specs=[row(d), mod, mod, full(wm.shape), full(wt.shape), full((1, aw)), full((1, aw)),
                  full(msp.shape), full(bsp.shape), full((1, aw)), full((1, aw)), full((1, IDX_DIM)), full((aw, aw))],
        out_specs=[row(w) for w, _ in outs],
        out_shape=[jax.ShapeDtypeStruct((n, w), dt) for w, dt in outs],
        compiler_params=_cparams(("arbitrary",)),
        name="even_in",
    )(x, sh, sc, wm, wt, ln_g.reshape(1, aw), ln_b.reshape(1, aw), msp, bsp,
      jnp.tile(q_g, B_HEADS).reshape(1, aw), jnp.tile(k_g, B_HEADS).reshape(1, aw), ik_g.reshape(1, IDX_DIM),
      _block_ones(aw, aw // B_HEADS))


def _odd_in_kernel(x_ref, sh_ref, sc_ref, wm_ref, wt_ref, qg_ref, kg_ref, e_ref,
                   xc_ref, cz_ref, tail_ref, q_ref, k32_ref, kbf_ref, v32_ref, vbf_ref):
    hw = cz_ref.shape[1]
    h = _modulated_rms(x_ref[...], sh_ref[...], sc_ref[...]).astype(BF16)

    def seg(i):
        return jnp.dot(h, wm_ref[:, i * hw:(i + 1) * hw], preferred_element_type=F32)

    for i in range(3):
        xc_ref[:, i * hw:(i + 1) * hw] = seg(i)
    cz_ref[...] = seg(3)
    tail_ref[...] = jnp.dot(h, wt_ref[...], preferred_element_type=F32)
    q = _group_rms(seg(4), e_ref, D_QK_DIM) * qg_ref[...]
    q_ref[...] = (q * (D_QK_DIM ** -0.5)).astype(BF16)
    k = _group_rms(seg(5), e_ref, D_QK_DIM) * kg_ref[...]
    k32_ref[...] = k
    kbf_ref[...] = k.astype(BF16)
    v = seg(6)
    v32_ref[...] = v
    vbf_ref[...] = v.astype(BF16)


def _odd_in(x, sh, sc, per_token, tiles_per_batch, tm, w_in, q_g, k_g):
    n, d = x.shape
    hw = d // 2
    ng = 2 * C_HEADS
    wm = jnp.concatenate([w_in[:, :4 * hw], w_in[:, 4 * hw + ng:]], axis=1).astype(BF16)
    wt = jnp.pad(w_in[:, 4 * hw:4 * hw + ng], ((0, 0), (0, LANES - ng))).astype(BF16)
    full = lambda shape: pl.BlockSpec(shape, lambda i: (0,) * len(shape))
    row = lambda w: pl.BlockSpec((tm, w), lambda i: (i, 0))
    mod = _mod_specs(per_token, tm, d, tiles_per_batch)
    outs = [(3 * hw, F32), (hw, F32), (LANES, F32), (hw, BF16), (hw, F32), (hw, BF16), (hw, F32), (hw, BF16)]
    reps = hw // D_QK_DIM
    return pl.pallas_call(
        _odd_in_kernel,
        grid=(n // tm,),
        in_specs=[row(d), mod, mod, full(wm.shape), full(wt.shape), full((1, hw)), full((1, hw)), full((hw, hw))],
        out_specs=[row(w) for w, _ in outs],
        out_shape=[jax.ShapeDtypeStruct((n, w), dt) for w, dt in outs],
        compiler_params=_cparams(("arbitrary",)),
        name="odd_in",
    )(x, sh, sc, wm, wt, jnp.tile(q_g, reps).reshape(1, hw), jnp.tile(k_g, reps).reshape(1, hw),
      _block_ones(hw, D_QK_DIM))


def _out_proj_kernel(a_ref, b_ref, x_ref, g1_ref, sh_ref, sc_ref, w_ref, rw_ref, rb_ref, x1_ref, h2_ref, lg_ref):
    hw = a_ref.shape[1]
    y = jnp.dot(a_ref[...], w_ref[:hw, :], preferred_element_type=F32)
    y = y + jnp.dot(b_ref[...], w_ref[hw:, :], preferred_element_type=F32)
    x1 = x_ref[...] + g1_ref[...] * y
    x1_ref[...] = x1
    h2 = _modulated_rms(x1, sh_ref[...], sc_ref[...])
    h2_ref[...] = h2.astype(BF16)
    lg_ref[...] = _dot_hi(h2, rw_ref[...]) + rb_ref[...]


def _out_proj(a, b, x, g1, sh, sc, per_token, tiles_per_batch, tm, w_out, router_w, router_b):
    n, d = x.shape
    hw = a.shape[1]
    ne = router_w.shape[1]
    rw = jnp.pad(router_w, ((0, 0), (0, LANES - ne)))
    rb = jnp.pad(router_b, (0, LANES - ne)).reshape(1, LANES)
    full = lambda shape: pl.BlockSpec(shape, lambda i: (0,) * len(shape))
    row = lambda w: pl.BlockSpec((tm, w), lambda i: (i, 0))
    mod = _mod_specs(per_token, tm, d, tiles_per_batch)
    x1, h2, lg = pl.pallas_call(
        _out_proj_kernel,
        grid=(n // tm,),
        in_specs=[row(hw), row(hw), row(d), mod, mod, mod, full((d, d)), full((d, LANES)), full((1, LANES))],
        out_specs=[row(d), row(d), row(LANES)],
        out_shape=[jax.ShapeDtypeStruct((n, d), F32), jax.ShapeDtypeStruct((n, d), BF16),
                   jax.ShapeDtypeStruct((n, LANES), F32)],
        compiler_params=_cparams(("arbitrary",)),
        name="out_proj",
    )(a, b, x, g1, sh, sc, w_out.astype(BF16), rw, rb)
    return x1, h2, lg[:, :ne]


def _idx_scores_kernel(iq_ref, iw_ref, ik_ref, o_ref, *, tq, tk):
    i = pl.program_id(1)
    j = pl.program_id(2)

    @pl.when(j * tk <= i * tq + tq - 1)
    def _():
        ik = ik_ref[...]
        acc = jnp.zeros((tq, tk), F32)
        for h in range(IDX_HEADS):
            d = _dot_nt(iq_ref[:, h * IDX_DIM:(h + 1) * IDX_DIM], ik)
            acc = acc + iw_ref[:, h:h + 1] * jnp.maximum(d, 0.0)
        o_ref[...] = acc

    @pl.when(j * tk > i * tq + tq - 1)
    def _():
        o_ref[...] = jnp.zeros((tq, tk), F32)


def _idx_scores_prompt(iq, iw, ik, tq, tk):
    bn, s, _ = iq.shape
    last = lambda i: (i * tq + tq - 1) // tk
    return pl.pallas_call(
        functools.partial(_idx_scores_kernel, tq=tq, tk=tk),
        grid=(bn, s // tq, s // tk),
        in_specs=[pl.BlockSpec((None, tq, IDX_HEADS * IDX_DIM), lambda b, i, j: (b, i, 0)),
                  pl.BlockSpec((None, tq, IDX_HEADS), lambda b, i, j: (b, i, 0)),
                  pl.BlockSpec((None, tk, IDX_DIM), lambda b, i, j: (b, jnp.minimum(j, last(i)), 0))],
        out_specs=pl.BlockSpec((None, tq, tk), lambda b, i, j: (b, i, j)),
        out_shape=jax.ShapeDtypeStruct((bn, s, s), F32),
        compiler_params=_cparams(("arbitrary", "arbitrary", "arbitrary")),
        name="idx_scores_prompt",
    )(iq, iw, ik)


def _select_kernel(nch_ref, s_ref, lim_ref, m_ref, key_ref, *, ksel, kc):
    rows, width = s_ref.shape
    nch = nch_ref[pl.program_id(0)]
    lim = lim_ref[...]
    fold = kc // LANES
    col_bits = max(1, int(width - 1).bit_length())

    def cols(c):
        return c * kc + lax.broadcasted_iota(I32, (rows, kc), 1)

    def fill(c, carry):
        off = pl.multiple_of(c * kc, kc)
        bits = lax.bitcast_convert_type(s_ref[:, pl.ds(off, kc)] + 0.0, I32)
        key = jnp.where(bits < 0, bits ^ jnp.int32(0x7FFFFFFF), bits)
        key_ref[:, pl.ds(off, kc)] = jnp.where(cols(c) <= lim, key, jnp.int32(INT_MIN))
        return carry

    lax.fori_loop(0, nch, fill, 0)

    def count(pred):
        def body(c, acc):
            off = pl.multiple_of(c * kc, kc)
            hit = jnp.where(pred(c, key_ref[:, pl.ds(off, kc)]), 1.0, 0.0)
            part = hit[:, :LANES]
            for f in range(1, fold):
                part = part + hit[:, f * LANES:(f + 1) * LANES]
            return acc + part
        acc = lax.fori_loop(0, nch, body, jnp.zeros((rows, LANES), F32))
        return jnp.sum(acc, axis=1, keepdims=True)

    def thr_bit(b, prefix):
        cand = prefix + (jnp.int32(1) << (31 - b))
        cnt = count(lambda c, key: key >= cand)
        return jnp.where(cnt >= ksel, cand, prefix)

    thr = lax.fori_loop(0, 32, thr_bit, jnp.full((rows, 1), INT_MIN, I32))
    need = ksel - count(lambda c, key: key > thr)

    def tie_bit(b, pos):
        cand = pos + (jnp.int32(1) << (col_bits - 1 - b))
        cnt = count(lambda c, key: (key == thr) & (cols(c) < cand))
        return jnp.where(cnt < need, cand, pos)

    last = lax.fori_loop(0, col_bits, tie_bit, jnp.zeros((rows, 1), I32))

    def emit(c, carry):
        off = pl.multiple_of(c * kc, kc)
        key = key_ref[:, pl.ds(off, kc)]
        col = cols(c)
        sel = ((key > thr) | ((key == thr) & (col <= last))) & (col <= lim)
        m_ref[:, pl.ds(off, kc)] = jnp.where(sel, 0.0, NEG).astype(m_ref.dtype)
        return carry

    lax.fori_loop(0, nch, emit, 0)

    def blank(c, carry):
        off = pl.multiple_of(c * kc, kc)
        m_ref[:, pl.ds(off, kc)] = jnp.full((rows, kc), NEG, m_ref.dtype)
        return carry

    lax.fori_loop(nch, width // kc, blank, 0)


def _select_mask(scores, lim, nch, ksel, kc, out_dtype):
    r, width = scores.shape
    tr = 128
    return pl.pallas_call(
        functools.partial(_select_kernel, ksel=ksel, kc=kc),
        grid_spec=pltpu.PrefetchScalarGridSpec(
            num_scalar_prefetch=1,
            grid=(r // tr,),
            in_specs=[pl.BlockSpec((tr, width), lambda i, n: (i, 0)),
                      pl.BlockSpec((tr, 1), lambda i, n: (i, 0))],
            out_specs=pl.BlockSpec((tr, width), lambda i, n: (i, 0)),
            scratch_shapes=[pltpu.VMEM((tr, width), I32)]),
        out_shape=jax.ShapeDtypeStruct((r, width), out_dtype),
        compiler_params=_cparams(("arbitrary",)),
        name="select_mask",
    )(nch, scores, lim)


def _online_update(s, vh, idx, m_scr, l_scr, acc_scr):
    m_prev = m_scr[idx]
    m_new = jnp.maximum(m_prev, jnp.max(s, axis=-1, keepdims=True))
    alpha = jnp.exp(m_prev - m_new)
    p = jnp.exp(s - m_new)
    l_scr[idx] = alpha * l_scr[idx] + jnp.sum(p, axis=-1, keepdims=True)
    acc_scr[idx] = alpha * acc_scr[idx] + jnp.dot(p.astype(BF16), vh, preferred_element_type=F32)
    m_scr[idx] = m_new


def _prompt_attn_kernel(*refs, nh, nm, dqk, dv, tq, has_mask, lam_init):
    refs = list(refs)
    q_ref, k_ref, v_ref, tbl_ref = refs[:4]
    pos = 4
    madd_ref = None
    if has_mask:
        madd_ref = refs[pos]
        pos += 1
    lam_ref = subg_ref = None
    if nm == 2:
        lam_ref, subg_ref = refs[pos], refs[pos + 1]
        pos += 2
    o_ref, m_scr, l_scr, acc_scr = refs[pos:pos + 4]
    qi = pl.program_id(1)
    m_scr[...] = jnp.full(m_scr.shape, -jnp.inf, F32)
    l_scr[...] = jnp.zeros(l_scr.shape, F32)
    acc_scr[...] = jnp.zeros(acc_scr.shape, F32)

    def chunk(j, back):
        koff = pl.multiple_of(j * tq, tq)
        ma = madd_ref[:, pl.ds(koff, tq)].astype(F32) if has_mask else None
        for h in range(nh):
            vh = v_ref[pl.ds(koff, tq), h * dv:(h + 1) * dv]
            for m in range(nm):
                idx = h * nm + m
                s = _dot_nt(q_ref[:, idx * dqk:(idx + 1) * dqk], k_ref[pl.ds(koff, tq), idx * dqk:(idx + 1) * dqk])
                if back is not None:
                    s = s + tbl_ref[h, back]
                if has_mask:
                    s = s + ma
                _online_update(s, vh, idx, m_scr, l_scr, acc_scr)

    def far(j, carry):
        chunk(j, None)
        return carry

    lax.fori_loop(0, jnp.maximum(qi - 1, 0), far, 0)

    @pl.when(qi >= 1)
    def _():
        chunk(qi - 1, 1)

    chunk(qi, 0)

    for h in range(nh):
        if nm == 1:
            o_ref[:, h * dv:(h + 1) * dv] = (acc_scr[h] / l_scr[h]).astype(o_ref.dtype)
        else:
            att = acc_scr[2 * h] / l_scr[2 * h] - lam_ref[0] * (acc_scr[2 * h + 1] / l_scr[2 * h + 1])
            att = att * lax.rsqrt(jnp.mean(att * att, axis=-1, keepdims=True) + EPS)
            o_ref[:, h * dv:(h + 1) * dv] = (att * subg_ref[...] * (1.0 - lam_init)).astype(o_ref.dtype)


def _bias_tables(rel_bias, tq):
    bucket = jnp.asarray(np.concatenate([_bucket_table(), [N_BUCKETS - 1]]))
    per_dist = rel_bias[bucket].astype(F32)
    far = per_dist[MAX_DISTANCE]
    r = np.arange(tq)[:, None]
    c = np.arange(tq)[None, :]
    tabs = []
    for d in range(2):
        dist = d * tq + r - c
        vals = per_dist[np.clip(dist, 0, MAX_DISTANCE)] - far
        vals = jnp.where(jnp.asarray(dist >= 0)[..., None], vals, NEG)
        tabs.append(jnp.transpose(vals, (2, 0, 1)))
    return jnp.stack(tabs, axis=1)


def _prompt_attn(q, k, v, tbl, madd, lam, subg, *, nh, nm, dqk, dv, tq, lam_init):
    bn, s, w = q.shape
    assert MAX_DISTANCE <= tq and s % tq == 0
    has_mask = madd is not None
    kv_spec = pl.BlockSpec((None, s, w), lambda b, i: (b, 0, 0), pipeline_mode=pl.Buffered(1))
    in_specs = [pl.BlockSpec((None, tq, w), lambda b, i: (b, i, 0)), kv_spec, kv_spec,
                pl.BlockSpec(tbl.shape, lambda b, i: (0, 0, 0, 0), pipeline_mode=pl.Buffered(1))]
    args = [q, k, v, tbl]
    if has_mask:
        in_specs.append(pl.BlockSpec((None, tq, s), lambda b, i: (b, i, 0)))
        args.append(madd)
    if nm == 2:
        in_specs += [pl.BlockSpec(memory_space=pltpu.SMEM), pl.BlockSpec((1, dv), lambda b, i: (0, 0))]
        args += [lam, subg]
    return pl.pallas_call(
        functools.partial(_prompt_attn_kernel, nh=nh, nm=nm, dqk=dqk, dv=dv, tq=tq, has_mask=has_mask, lam_init=lam_init),
        grid=(bn, s // tq),
        in_specs=in_specs,
        out_specs=pl.BlockSpec((None, tq, nh * dv), lambda b, i: (b, i, 0)),
        out_shape=jax.ShapeDtypeStruct((bn, s, nh * dv), BF16),
        scratch_shapes=[pltpu.VMEM((nh * nm, tq, 1), F32), pltpu.VMEM((nh * nm, tq, 1), F32),
                        pltpu.VMEM((nh * nm, tq, dv), F32)],
        compiler_params=_cparams(("arbitrary", "arbitrary")),
        name="prompt_attn_dsa" if nm == 1 else "prompt_attn_diff",
    )(*args)


def _paged_scores_kernel(pt_ref, iq_ref, iw_ref, *refs, gp, page):
    pages = refs[:gp]
    new_ref, o_ref = refs[gp], refs[gp + 1]
    s_id = pl.program_id(1)
    last = pl.num_programs(1) - 1
    rows = iq_ref.shape[0]
    tp = rows // IDX_HEADS

    def score(ik):
        d = jnp.maximum(_dot_nt(iq_ref[...], ik), 0.0) * iw_ref[...]
        acc = d[:tp]
        for h in range(1, IDX_HEADS):
            acc = acc + d[h * tp:(h + 1) * tp]
        return acc

    @pl.when(s_id < last)
    def _():
        for g in range(gp):
            o_ref[:, g * page:(g + 1) * page] = score(pages[g][...])

    @pl.when(s_id == last)
    def _():
        o_ref[...] = jnp.zeros(o_ref.shape, F32)
        o_ref[:, :page] = score(new_ref[...])


def _paged_scores(page_table, iq_rows, iw_rows, cache_idx, ik_new, gp):
    bn, rows, _ = iq_rows.shape
    tp = rows // IDX_HEADS
    page = cache_idx.shape[1]
    npages = page_table.shape[1]
    nsteps = npages // gp + 1
    width = nsteps * gp * page

    def page_map(g):
        return lambda b, s, pt: (pt[b, jnp.minimum(s * gp + g, npages - 1)], 0, 0)

    return pl.pallas_call(
        functools.partial(_paged_scores_kernel, gp=gp, page=page),
        grid_spec=pltpu.PrefetchScalarGridSpec(
            num_scalar_prefetch=1,
            grid=(bn, nsteps),
            in_specs=[pl.BlockSpec((None, rows, IDX_DIM), lambda b, s, pt: (b, 0, 0)),
                      pl.BlockSpec((None, rows, 1), lambda b, s, pt: (b, 0, 0))]
                     + [pl.BlockSpec((None, page, IDX_DIM), page_map(g)) for g in range(gp)]
                     + [pl.BlockSpec((None, page, IDX_DIM), lambda b, s, pt: (b, 0, 0))],
            out_specs=pl.BlockSpec((None, tp, gp * page), lambda b, s, pt: (b, 0, s))),
        out_shape=jax.ShapeDtypeStruct((bn, tp, width), F32),
        compiler_params=_cparams(("arbitrary", "arbitrary")),
        name="paged_idx_scores",
    )(page_table, iq_rows, iw_rows, *([cache_idx] * gp), ik_new)


def _paged_attn_kernel(pt_ref, q_ref, am_ref, *refs, gp, page, nh, nm, dv, tp, lam_init):
    kpages = refs[:gp]
    vpages = refs[gp:2 * gp]
    knew_ref, vnew_ref = refs[2 * gp], refs[2 * gp + 1]
    pos = 2 * gp + 2
    lam_ref = subg_ref = None
    if nm == 2:
        lam_ref, subg_ref = refs[pos], refs[pos + 1]
        pos += 2
    o_ref, m_scr, l_scr, acc_scr = refs[pos:pos + 4]
    s_id = pl.program_id(1)
    last = pl.num_programs(1) - 1

    @pl.when(s_id == 0)
    def _():
        m_scr[...] = jnp.full(m_scr.shape, -jnp.inf, F32)
        l_scr[...] = jnp.zeros(l_scr.shape, F32)
        acc_scr[...] = jnp.zeros(acc_scr.shape, F32)

    def block(kp, vp, g):
        s = _dot_nt(q_ref[...], kp) + am_ref[:, g * page:(g + 1) * page]
        m_prev = m_scr[...]
        m_new = jnp.maximum(m_prev, jnp.max(s, axis=-1, keepdims=True))
        alpha = jnp.exp(m_prev - m_new)
        p = jnp.exp(s - m_new)
        l_scr[...] = alpha * l_scr[...] + jnp.sum(p, axis=-1, keepdims=True)
        acc_scr[...] = alpha * acc_scr[...] + _dot(p, vp)
        m_scr[...] = m_new

    @pl.when(s_id < last)
    def _():
        for g in range(gp):
            block(kpages[g][...], vpages[g][...], g)

    @pl.when(s_id == last)
    def _():
        block(knew_ref[...], vnew_ref[...], 0)
        accn = acc_scr[...] / l_scr[...]
        for h in range(nh):
            cols = slice(h * dv, (h + 1) * dv)
            if nm == 1:
                o_ref[:, cols] = accn[h * tp:(h + 1) * tp, cols]
            else:
                att = accn[2 * h * tp:(2 * h + 1) * tp, cols] - lam_ref[0] * accn[(2 * h + 1) * tp:(2 * h + 2) * tp, cols]
                att = att * lax.rsqrt(jnp.mean(att * att, axis=-1, keepdims=True) + EPS)
                o_ref[:, cols] = att * subg_ref[...] * (1.0 - lam_init)


def _paged_attn(page_table, q_rows, addmask, cache_k, cache_v, k_new, v_new, lam, subg, *, gp, nh, nm, dv, tp, lam_init):
    bn, rows, w = q_rows.shape
    page = cache_k.shape[1]
    npages = page_table.shape[1]
    nsteps = npages // gp + 1

    def page_map(g):
        return lambda b, s, pt: (pt[b, jnp.minimum(s * gp + g, npages - 1)], 0, 0)

    new_spec = pl.BlockSpec((None, page, w), lambda b, s, pt: (b, 0, 0))
    in_specs = ([pl.BlockSpec((None, rows, w), lambda b, s, pt: (b, 0, 0)),
                 pl.BlockSpec((None, rows, gp * page), lambda b, s, pt: (b, 0, s))]
                + [pl.BlockSpec((None, page, w), page_map(g)) for g in range(gp)] * 2
                + [new_spec, new_spec])
    args = [q_rows, addmask] + [cache_k] * gp + [cache_v] * gp + [k_new, v_new]
    if nm == 2:
        in_specs += [pl.BlockSpec(memory_space=pltpu.SMEM), pl.BlockSpec((1, dv), lambda b, s, pt: (0, 0))]
        args += [lam, subg]
    return pl.pallas_call(
        functools.partial(_paged_attn_kernel, gp=gp, page=page, nh=nh, nm=nm, dv=dv, tp=tp, lam_init=lam_init),
        grid_spec=pltpu.PrefetchScalarGridSpec(
            num_scalar_prefetch=1,
            grid=(bn, nsteps),
            in_specs=in_specs,
            out_specs=pl.BlockSpec((None, tp, nh * dv), lambda b, s, pt: (b, 0, 0)),
            scratch_shapes=[pltpu.VMEM((rows, 1), F32), pltpu.VMEM((rows, 1), F32), pltpu.VMEM((rows, w), F32)]),
        out_shape=jax.ShapeDtypeStruct((bn, tp, nh * dv), F32),
        compiler_params=_cparams(("arbitrary", "arbitrary")),
        name="paged_attn_dsa" if nm == 1 else "paged_attn_diff",
    )(page_table, *args)


def _gdn_kernel(xc_ref, prev_ref, buf_ref, cz_ref, tail_ref, s0_ref, cw_ref, alog_ref, dtb_ref, nw_ref,
                o_ref, sfin_ref, s_scr, *, n_valid):
    ci = pl.program_id(1)
    c, ch = xc_ref.shape
    dk = C_DK
    dv = (ch - 2 * C_HEADS * dk) // C_HEADS

    @pl.when(ci == 0)
    def _():
        s_scr[...] = s0_ref[...]

    hist = jnp.where(ci == 0, buf_ref[...], prev_ref[...])
    nh_rows = hist.shape[0]
    xfull = jnp.concatenate([hist, xc_ref[...]], axis=0)
    y = cw_ref[C_CONV - 1:C_CONV, :] * xfull[nh_rows:, :]
    for j in range(1, C_CONV):
        y = y + cw_ref[C_CONV - 1 - j:C_CONV - j, :] * pltpu.roll(xfull, j, 0)[nh_rows:, :]
    y = _silu(y)

    row = lax.broadcasted_iota(I32, (c, c), 0)
    colm = lax.broadcasted_iota(I32, (c, c), 1)
    incl = row >= colm
    tril = incl.astype(F32)
    eye = (row == colm).astype(F32)
    tail = tail_ref[...]
    beta_all = jax.nn.sigmoid(tail)
    sp_in = tail + dtb_ref[...]
    g_all = -jnp.exp(alog_ref[...]) * (jnp.maximum(sp_in, 0.0) + jnp.log1p(jnp.exp(-jnp.abs(sp_in))))
    if n_valid < c:
        valid = lax.broadcasted_iota(I32, tail.shape, 0) < n_valid
        beta_all = jnp.where(valid, beta_all, 0.0)
        g_all = jnp.where(valid, g_all, 0.0)
    gc_all = _dot_hi(tril, g_all)
    gc_rows = jnp.transpose(gc_all)

    for h in range(C_HEADS):
        q = y[:, h * dk:(h + 1) * dk]
        k = y[:, (C_HEADS + h) * dk:(C_HEADS + h + 1) * dk]
        v = y[:, 2 * C_HEADS * dk + h * dv:2 * C_HEADS * dk + (h + 1) * dv]
        q = q * lax.rsqrt(jnp.sum(q * q, axis=-1, keepdims=True) + EPS) * (dk ** -0.5)
        k = k * lax.rsqrt(jnp.sum(k * k, axis=-1, keepdims=True) + EPS)
        beta = beta_all[:, h:h + 1]
        gc = gc_all[:, C_HEADS + h:C_HEADS + h + 1]
        decay = jnp.exp(jnp.where(incl, gc - gc_rows[C_HEADS + h:C_HEADS + h + 1, :], -jnp.inf))
        kb = k * beta
        a = _dot_nt_hi(kb, k) * decay * (1.0 - eye)
        inv = eye - a
        pw = a
        for _ in range(max(0, int(c - 1).bit_length() - 1)):
            pw = _dot_hi(pw, pw)
            inv = inv + _dot_hi(inv, pw)
        u = _dot_hi(inv, v * beta)
        w = _dot_hi(inv, kb * jnp.exp(gc))
        s = s_scr[h]
        v_new = u - _dot_hi(w, s)
        intra = _dot_nt_hi(q, k) * decay
        o = _dot_hi(q * jnp.exp(gc), s) + _dot_hi(intra, v_new)
        gl = gc[c - 1:c, :]
        s_scr[h] = s * jnp.exp(gl) + _dot_tn_hi(k * jnp.exp(gl - gc), v_new)
        o = o * lax.rsqrt(jnp.mean(o * o, axis=-1, keepdims=True) + EPS) * nw_ref[...]
        o_ref[:, h * dv:(h + 1) * dv] = (o * _silu(cz_ref[:, h * dv:(h + 1) * dv])).astype(o_ref.dtype)

    @pl.when(ci == pl.num_programs(1) - 1)
    def _():
        sfin_ref[...] = s_scr[...]


def _gdn(xc, buf, cz, tail, s0, conv_w, a_log, dt_bias, norm_w, chunk, n_valid):
    bn, t, ch = xc.shape
    hw = cz.shape[2]
    nck = t // chunk
    dvh = hw // C_HEADS
    hist = SUBLANES
    per_chunk = chunk // hist
    gate_lanes = lambda p: jnp.pad(p.astype(F32), (C_HEADS, LANES - 2 * C_HEADS)).reshape(1, LANES)
    return pl.pallas_call(
        functools.partial(_gdn_kernel, n_valid=n_valid),
        grid=(bn, nck),
        in_specs=[pl.BlockSpec((None, chunk, ch), lambda b, i: (b, i, 0)),
                  pl.BlockSpec((None, hist, ch), lambda b, i: (b, jnp.maximum(i * per_chunk - 1, 0), 0)),
                  pl.BlockSpec((None, hist, ch), lambda b, i: (b, 0, 0)),
                  pl.BlockSpec((None, chunk, hw), lambda b, i: (b, i, 0)),
                  pl.BlockSpec((None, chunk, LANES), lambda b, i: (b, i, 0)),
                  pl.BlockSpec((None, C_HEADS, C_DK, dvh), lambda b, i: (b, 0, 0, 0)),
                  pl.BlockSpec((C_CONV, ch), lambda b, i: (0, 0)),
                  pl.BlockSpec((1, LANES), lambda b, i: (0, 0)),
                  pl.BlockSpec((1, LANES), lambda b, i: (0, 0)),
                  pl.BlockSpec((1, dvh), lambda b, i: (0, 0))],
        out_specs=[pl.BlockSpec((None, chunk, hw), lambda b, i: (b, i, 0)),
                   pl.BlockSpec((None, C_HEADS, C_DK, dvh), lambda b, i: (b, 0, 0, 0))],
        out_shape=[jax.ShapeDtypeStruct((bn, t, hw), BF16), jax.ShapeDtypeStruct((bn, C_HEADS, C_DK, dvh), F32)],
        scratch_shapes=[pltpu.VMEM((C_HEADS, C_DK, dvh), F32)],
        compiler_params=_cparams(("arbitrary", "arbitrary")),
        name="gated_delta",
    )(xc, xc, buf, cz, tail, s0, conv_w, gate_lanes(a_log), gate_lanes(dt_bias), norm_w.reshape(1, dvh))


def _moe_kernel(be_ref, x_ref, gate_ref, w1_ref, b1_ref, w2_ref, b2_ref, o_ref, w1_scr, w2_scr):
    i = pl.program_id(0)
    dff = w2_ref.shape[0]
    changed = jnp.logical_or(i == 0, be_ref[i] != be_ref[jnp.maximum(i - 1, 0)])

    @pl.when(changed)
    def _():
        w1_scr[...] = w1_ref[...].astype(BF16)
        w2_scr[...] = w2_ref[...].astype(BF16)

    hmid = jnp.dot(x_ref[...], w1_scr[...], preferred_element_type=F32) + b1_ref[...]
    gate = jnp.minimum(hmid[:, :dff], SWIGLU_LIMIT)
    up = jnp.clip(hmid[:, dff:], -SWIGLU_LIMIT, SWIGLU_LIMIT)
    act = (up + 1.0) * gate * jax.nn.sigmoid(SWIGLU_ALPHA * gate)
    y = jnp.dot(act.astype(BF16), w2_scr[...], preferred_element_type=F32) + b2_ref[...]
    o_ref[...] = y * gate_ref[...]


def _moe_ffn(xs, slot_gate, block_e, layer, w1, b1, w2, b2, bm):
    n_slots, d = xs.shape
    _, ne, _, d2 = w1.shape
    dff = w2.shape[2]
    return pl.pallas_call(
        _moe_kernel,
        grid_spec=pltpu.PrefetchScalarGridSpec(
            num_scalar_prefetch=1,
            grid=(n_slots // bm,),
            in_specs=[pl.BlockSpec((bm, d), lambda i, be: (i, 0)),
                      pl.BlockSpec((bm, 1), lambda i, be: (i, 0)),
                      pl.BlockSpec((None, None, d, d2), lambda i, be: (layer, be[i], 0, 0)),
                      pl.BlockSpec((None, None, 1, d2), lambda i, be: (layer, be[i], 0, 0)),
                      pl.BlockSpec((None, None, dff, d), lambda i, be: (layer, be[i], 0, 0)),
                      pl.BlockSpec((None, None, 1, d), lambda i, be: (layer, be[i], 0, 0))],
            out_specs=pl.BlockSpec((bm, d), lambda i, be: (i, 0)),
            scratch_shapes=[pltpu.VMEM((d, d2), BF16), pltpu.VMEM((dff, d), BF16)]),
        out_shape=jax.ShapeDtypeStruct((n_slots, d), F32),
        compiler_params=_cparams(("arbitrary",)),
        name="moe_ffn",
    )(block_e, xs, slot_gate.reshape(n_slots, 1), w1, b1.reshape(b1.shape[0], ne, 1, d2), w2, b2.reshape(b2.shape[0], ne, 1, d))


def _moe(h2, logits, layer, w1, b1, w2, b2, bm):
    n_tok = h2.shape[0]
    ne = logits.shape[1]
    top_val, top_idx = lax.top_k(logits, TOP_K)
    gates = jax.nn.softmax(top_val, axis=-1)
    n_assign = n_tok * TOP_K
    e_flat = top_idx.reshape(-1)
    order = jnp.argsort(e_flat)
    e_sorted = e_flat[order]
    counts = jnp.bincount(e_flat, length=ne)
    padded = (counts + bm - 1) // bm * bm
    pad_end = jnp.cumsum(padded)
    pad_start = pad_end - padded
    start = jnp.cumsum(counts) - counts
    dest = (pad_start[e_sorted] + jnp.arange(n_assign) - start[e_sorted]).astype(I32)
    n_blocks = -(-(n_assign + ne * (bm - 1)) // bm)
    n_slots = n_blocks * bm
    slot_tok = jnp.zeros((n_slots,), I32).at[dest].set((order // TOP_K).astype(I32))
    slot_gate = jnp.zeros((n_slots,), F32).at[dest].set(gates.reshape(-1)[order])
    block_e = jnp.minimum(jnp.searchsorted(pad_end, jnp.arange(n_blocks) * bm, side='right'), ne - 1).astype(I32)
    slot_of = jnp.zeros((n_assign,), I32).at[order].set(dest)
    ys = _moe_ffn(h2[slot_tok], slot_gate, block_e, layer, w1, b1, w2, b2, bm)
    return jnp.sum(ys[slot_of].reshape(n_tok, TOP_K, -1), axis=1)


def _rel_bias_sample(rel_bias, past_len, t):
    dist = (past_len + np.arange(t))[:, None] - np.arange(past_len + t)[None, :]
    n = np.maximum(dist, 0)
    bucket = np.where(n < MAX_DISTANCE, _bucket_table()[np.minimum(n, MAX_DISTANCE - 1)], N_BUCKETS - 1)
    return jnp.transpose(rel_bias[jnp.asarray(bucket)].astype(F32), (2, 0, 1)), dist >= 0


def _head_rows(x, tp):
    bn, t, g, w = x.shape
    xp = jnp.pad(x, ((0, 0), (0, tp - t), (0, 0), (0, 0)))
    eye = jnp.eye(g, dtype=x.dtype)
    return jnp.einsum('btgw,gk->bgtkw', xp, eye).reshape(bn, g * tp, g * w)


def kernel(x_prompt, x_sample, cache_b_k, cache_b_v, cache_b_idx, state_c_rec, state_c_conv, cache_d_k, cache_d_v,
           page_table, c_prompt, c_sample, rel_bias, ada_w, ada_b, even_w_in, even_w_out, a_ln_g, a_ln_b, a_w_sp,
           a_b_sp, b_q_norm, b_k_norm, b_idx_norm, odd_w_in, odd_w_out, c_conv_w, c_a_log, c_dt_bias, c_norm_w,
           d_q_norm, d_k_norm, d_lambda, d_subln, router_w, router_b, moe_w1, moe_b1, moe_w2, moe_b2):
    bp, sp, d = x_prompt.shape
    bs, ts, _ = x_sample.shape
    depth = ada_w.shape[0]
    hw = d // 2
    page = cache_b_k.shape[2]
    past_len = page_table.shape[1] * page
    n_p, n_s = bp * sp, bs * ts
    tm_p = 256
    tq = 256
    tp = SUBLANES
    gp = 8 if page_table.shape[1] % 8 == 0 else 1
    assert sp % tm_p == 0 and n_s % SUBLANES == 0 and ts <= tp and ts >= C_CONV - 1

    n_c = bp + bs
    c_all = jnp.pad(jnp.concatenate([c_prompt, c_sample], axis=0), ((0, -n_c % SUBLANES), (0, 0)))
    mod_all = _ada_mod(c_all, ada_w, ada_b)

    def mods(layer):
        m = mod_all[layer].reshape(-1, 6, d)
        mp = [m[:bp, i][:, None, :] for i in range(6)]
        ms = [jnp.repeat(m[bp:n_c, i], ts, axis=0) for i in range(6)]
        return mp, ms

    xp = x_prompt.reshape(n_p, d)
    xs = x_sample.reshape(n_s, d)
    tpb = sp // tm_p
    tbl = _bias_tables(rel_bias, tq)
    outs_p, outs_s = {}, {}

    for layer in range(depth):
        li = layer // 2
        mp, ms = mods(layer)
        if layer % 2 == 0:
            ws_p = jnp.where(np.tril(np.ones((A_CHUNK, A_CHUNK), bool)), a_w_sp[li], 0.0).astype(BF16)
            bsp_p = a_b_sp[li].T
            (aout, _, q, k32, kbf, v32, vbf, iq, ik32, ikbf, iw) = _even_in(
                xp, mp[0], mp[1], False, tpb, tm_p, even_w_in[li], a_ln_g[li], a_ln_b[li], ws_p, bsp_p,
                b_q_norm[li], b_k_norm[li], b_idx_norm[li])
            ksel = min(TOPK_MAX, sp // 4)
            kc = 512 if sp % 512 == 0 else sp
            scores = _idx_scores_prompt(iq.reshape(bp, sp, hw), iw.reshape(bp, sp, IDX_HEADS),
                                        ikbf.reshape(bp, sp, IDX_DIM), tq, kc)
            lim = jnp.tile(jnp.arange(sp, dtype=I32), bp).reshape(n_p, 1)
            nch = jnp.tile((jnp.arange(sp // 128, dtype=I32) * 128 + 127) // kc + 1, bp)
            madd = _select_mask(scores.reshape(n_p, sp), lim, nch, ksel, kc, BF16).reshape(bp, sp, sp)
            b_out = _prompt_attn(q.reshape(bp, sp, hw), kbf.reshape(bp, sp, hw), vbf.reshape(bp, sp, hw), tbl, madd,
                                 None, None, nh=B_HEADS, nm=1, dqk=hw // B_HEADS, dv=hw // B_HEADS, tq=tq, lam_init=0.0)
            outs_p.setdefault('b_k', []).append(k32.reshape(bp, sp, B_HEADS, -1))
            outs_p.setdefault('b_v', []).append(v32.reshape(bp, sp, B_HEADS, -1))
            outs_p.setdefault('b_idx', []).append(ik32.reshape(bp, sp, IDX_DIM))
            mix_p = (aout, b_out.reshape(n_p, hw))

            cs = min(ts, A_CHUNK)
            ws_s = jnp.where(np.tril(np.ones((cs, cs), bool)), a_w_sp[li][:, :cs, :cs], 0.0)
            ws_s = jnp.einsum('ab,gts->gatbs', jnp.eye(n_s // cs, dtype=F32), ws_s).reshape(A_GROUPS, n_s, n_s).astype(BF16)
            bsp_s = jnp.tile(a_b_sp[li][:, :cs].T, (n_s // cs, 1))
            (aout, av, q, k32, kbf, v32, vbf, iq, ik32, ikbf, iw) = _even_in(
                xs, ms[0], ms[1], True, 1, n_s, even_w_in[li], a_ln_g[li], a_ln_b[li], ws_s, bsp_s,
                b_q_norm[li], b_k_norm[li], b_idx_norm[li])
            ltot = past_len + ts
            ksel = min(TOPK_MAX, ltot // 4)
            iq_rows = jnp.pad(iq.reshape(bs, ts, IDX_HEADS, IDX_DIM), ((0, 0), (0, tp - ts), (0, 0), (0, 0)))
            iq_rows = jnp.transpose(iq_rows, (0, 2, 1, 3)).reshape(bs, IDX_HEADS * tp, IDX_DIM)
            iw_rows = jnp.pad(iw.reshape(bs, ts, IDX_HEADS), ((0, 0), (0, tp - ts), (0, 0)))
            iw_rows = jnp.transpose(iw_rows, (0, 2, 1)).reshape(bs, IDX_HEADS * tp, 1)
            pad_new = lambda a: jnp.pad(a.reshape(bs, ts, -1), ((0, 0), (0, page - ts), (0, 0)))
            sc_s = _paged_scores(page_table, iq_rows, iw_rows, cache_b_idx[li], pad_new(ikbf), gp)
            width = sc_s.shape[2]
            kc_s = gp * page
            lim_s = jnp.where(np.arange(tp) < ts, past_len + np.arange(tp), -1).astype(I32)
            lim_s = jnp.tile(lim_s, bs).reshape(bs * tp, 1)
            nch_s = jnp.full((bs * tp // 128,), width // kc_s, I32)
            sel_s = _select_mask(sc_s.reshape(bs * tp, width), lim_s, nch_s, ksel, kc_s, F32).reshape(bs, tp, width)
            bias_s, _ = _rel_bias_sample(rel_bias, past_len, ts)
            bias_s = jnp.pad(bias_s, ((0, 0), (0, tp - ts), (0, width - ltot)))
            am = (sel_s[:, None] + bias_s[None]).reshape(bs, B_HEADS * tp, width)
            q_rows = _head_rows(q.reshape(bs, ts, B_HEADS, -1), tp)
            cshape = (cache_b_k.shape[1], page, hw)
            b_out = _paged_attn(page_table, q_rows, am, cache_b_k[li].reshape(cshape), cache_b_v[li].reshape(cshape),
                                pad_new(kbf), pad_new(vbf), None, None, gp=gp, nh=B_HEADS, nm=1, dv=hw // B_HEADS, tp=tp,
                                lam_init=0.0)
            outs_s.setdefault('a_v', []).append(av.reshape(bs, ts, hw))
            outs_s.setdefault('b_k', []).append(k32.reshape(bs, ts, B_HEADS, -1))
            outs_s.setdefault('b_v', []).append(v32.reshape(bs, ts, B_HEADS, -1))
            outs_s.setdefault('b_idx', []).append(ik32.reshape(bs, ts, IDX_DIM))
            mix_s = (aout, b_out[:, :ts].reshape(n_s, hw).astype(BF16))
            w_out = even_w_out[li]
        else:
            lam_init = 0.8 - 0.6 * math.exp(-0.3 * layer)
            lp = d_lambda[li].astype(F32)
            lam = (jnp.exp(jnp.sum(lp[0] * lp[1])) - jnp.exp(jnp.sum(lp[2] * lp[3])) + lam_init).reshape(1)
            subg = d_subln[li].reshape(1, -1)
            dvh = hw // D_HEADS
            xc, cz, tail, q2, k2_32, k2bf, v2_32, v2bf = _odd_in(xp, mp[0], mp[1], False, tpb, tm_p, odd_w_in[li],
                                                                d_q_norm[li], d_k_norm[li])
            ch = xc.shape[1]
            xc3 = xc.reshape(bp, sp, ch)
            zbuf = jnp.zeros((bp, SUBLANES, ch), F32)
            s0 = jnp.zeros((bp, C_HEADS, C_DK, hw // C_HEADS), F32)
            chunk = C_CHUNK if sp % C_CHUNK == 0 else sp
            c_out, c_rec = _gdn(xc3, zbuf, cz.reshape(bp, sp, hw), tail.reshape(bp, sp, LANES), s0, c_conv_w[li],
                                c_a_log[li], c_dt_bias[li], c_norm_w[li], chunk, chunk)
            d_out = _prompt_attn(q2.reshape(bp, sp, hw), k2bf.reshape(bp, sp, hw), v2bf.reshape(bp, sp, hw), tbl, None,
                                 lam, subg, nh=D_HEADS, nm=2, dqk=D_QK_DIM, dv=dvh, tq=tq, lam_init=lam_init)
            outs_p.setdefault('c_rec', []).append(c_rec)
            outs_p.setdefault('c_conv', []).append(xc3[:, sp - (C_CONV - 1):])
            outs_p.setdefault('d_k', []).append(k2_32.reshape(bp, sp, D_HEADS, -1))
            outs_p.setdefault('d_v', []).append(v2_32.reshape(bp, sp, D_HEADS, -1))
            mix_p = (c_out.reshape(n_p, hw), d_out.reshape(n_p, hw))

            xc, cz, tail, q2, k2_32, k2bf, v2_32, v2bf = _odd_in(xs, ms[0], ms[1], True, 1, n_s, odd_w_in[li],
                                                                d_q_norm[li], d_k_norm[li])
            pad_t = lambda a: jnp.pad(a.reshape(bs, ts, -1), ((0, 0), (0, C_CHUNK - ts), (0, 0)))
            buf = state_c_conv[li]
            buf8 = jnp.pad(buf, ((0, 0), (SUBLANES - (C_CONV - 1), 0), (0, 0)))
            c_out, c_rec = _gdn(pad_t(xc), buf8, pad_t(cz), pad_t(tail), state_c_rec[li], c_conv_w[li], c_a_log[li],
                                c_dt_bias[li], c_norm_w[li], C_CHUNK, ts)
            ltot = past_len + ts
            bias_s, allowed = _rel_bias_sample(rel_bias, past_len, ts)
            width = (page_table.shape[1] // gp + 1) * gp * page
            am = jnp.where(jnp.asarray(allowed)[None], bias_s, NEG)
            am = jnp.pad(am, ((0, 0), (0, tp - ts), (0, 0)))
            am = jnp.pad(am, ((0, 0), (0, 0), (0, width - ltot)), constant_values=NEG)
            am = jnp.broadcast_to(am[None, :, None], (bs, D_HEADS, 2, tp, width)).reshape(bs, D_HEADS * 2 * tp, width)
            q_rows = _head_rows(q2.reshape(bs, ts, 2 * D_HEADS, D_QK_DIM), tp)
            pad_new = lambda a: jnp.pad(a.reshape(bs, ts, -1), ((0, 0), (0, page - ts), (0, 0)))
            cshape = (cache_d_k.shape[1], page, hw)
            d_out = _paged_attn(page_table, q_rows, am, cache_d_k[li].reshape(cshape), cache_d_v[li].reshape(cshape),
                                pad_new(k2bf), pad_new(v2bf), lam, subg, gp=gp, nh=D_HEADS, nm=2, dv=dvh, tp=tp,
                                lam_init=lam_init)
            outs_s.setdefault('c_rec', []).append(c_rec)
            outs_s.setdefault('c_conv', []).append(
                jnp.concatenate([buf, xc.reshape(bs, ts, ch)], axis=1)[:, ts:])
            outs_s.setdefault('d_k', []).append(k2_32.reshape(bs, ts, D_HEADS, -1))
            outs_s.setdefault('d_v', []).append(v2_32.reshape(bs, ts, D_HEADS, -1))
            mix_s = (c_out[:, :ts].reshape(n_s, hw), d_out[:, :ts].reshape(n_s, hw).astype(BF16))
            w_out = odd_w_out[li]

        x1p, h2p, lgp = _out_proj(mix_p[0], mix_p[1], xp, mp[2], mp[3], mp[4], False, tpb, tm_p, w_out,
                                  router_w[layer], router_b[layer])
        x1s, h2s, lgs = _out_proj(mix_s[0], mix_s[1], xs, ms[2], ms[3], ms[4], True, 1, n_s, w_out,
                                  router_w[layer], router_b[layer])
        moe_out = _moe(jnp.concatenate([h2p, h2s], axis=0), jnp.concatenate([lgp, lgs], axis=0), layer,
                       moe_w1, moe_b1, moe_w2, moe_b2, 256)
        xp = (x1p.reshape(bp, sp, d) + mp[5] * moe_out[:n_p].reshape(bp, sp, d)).reshape(n_p, d)
        xs = x1s + ms[5] * moe_out[n_p:]

    st = lambda name, src: jnp.stack(src[name])
    return (xp.reshape(bp, sp, d), xs.reshape(bs, ts, d),
            st('b_k', outs_p), st('b_v', outs_p), st('b_idx', outs_p), st('c_rec', outs_p), st('c_conv', outs_p),
            st('d_k', outs_p), st('d_v', outs_p),
            st('a_v', outs_s), st('b_k', outs_s), st('b_v', outs_s), st('b_idx', outs_s), st('c_rec', outs_s),
            st('c_conv', outs_s), st('d_k', outs_s), st('d_v', outs_s))
```

```python
import functools
import math

import numpy as np
import jax
import jax.numpy as jnp
from jax import lax
from jax.experimental import pallas as pl
from jax.experimental.pallas import tpu as pltpu

F32 = jnp.float32
BF16 = jnp.bfloat16
I32 = jnp.int32
HIGHEST = lax.Precision.HIGHEST

A_GROUPS = 4
A_CHUNK = 128
B_HEADS = 8
IDX_HEADS = 8
IDX_DIM = 64
TOPK_MAX = 256
C_HEADS = 4
C_DK = 128
C_CONV = 4
C_CHUNK = 64
D_HEADS = 8
D_QK_DIM = 32
N_BUCKETS = 32
MAX_DISTANCE = 128
TOP_K = 4
SWIGLU_LIMIT = 7.0
SWIGLU_ALPHA = 1.702
EPS = 1e-6

LANES = 128
SUBLANES = 8
VMEM_LIMIT_BYTES = 56 * 1024 * 1024

NEG = -1e30
INT_MIN = -2 ** 31


def _cparams(sem):
    return pltpu.CompilerParams(dimension_semantics=sem, vmem_limit_bytes=VMEM_LIMIT_BYTES)


def _dot(a, b):
    return jnp.dot(a.astype(BF16), b.astype(BF16), preferred_element_type=F32)


def _dot_nt(a, b):
    return lax.dot_general(a.astype(BF16), b.astype(BF16), (((1,), (1,)), ((), ())), preferred_element_type=F32)


def _dot_hi(a, b):
    return jnp.dot(a, b, preferred_element_type=F32, precision=HIGHEST)


def _dot_tn(a, b):
    return lax.dot_general(a.astype(BF16), b.astype(BF16), (((0,), (0,)), ((), ())), preferred_element_type=F32)


def _dot_x3(a, b):
    ah = a.astype(BF16)
    al = (a - ah.astype(F32)).astype(BF16)
    bh = b.astype(BF16)
    bl = (b - bh.astype(F32)).astype(BF16)
    dot = lambda x, y: jnp.dot(x, y, preferred_element_type=F32)
    return dot(ah, bh) + (dot(ah, bl) + dot(al, bh))


def _silu(x):
    return x * jax.nn.sigmoid(x)


def _gelu_tanh(x):
    return 0.5 * x * (1.0 + jnp.tanh(math.sqrt(2.0 / math.pi) * (x + 0.044715 * (x * x * x))))


def _group_rms(x, ones_ref, gsize):
    xx = x * x
    hi = xx.astype(BF16)
    lo = (xx - hi.astype(F32)).astype(BF16)
    e = ones_ref[...]
    ss = jnp.dot(hi, e, preferred_element_type=F32) + jnp.dot(lo, e, preferred_element_type=F32)
    return x * lax.rsqrt(ss * (1.0 / gsize) + EPS)


def _store_v_with_ones(vx_ref, v):
    tm, w = v.shape
    nh = vx_ref.shape[1] // LANES
    dv = w // nh
    tail = jnp.where(lax.broadcasted_iota(I32, (tm, LANES - dv), 1) == 0, 1.0, 0.0)
    for h in range(nh):
        vx_ref[:, h * LANES:(h + 1) * LANES] = jnp.concatenate([v[:, h * dv:(h + 1) * dv], tail], axis=1).astype(BF16)


def _block_ones(width, gsize):
    g = np.arange(width) // gsize
    return jnp.asarray(g[:, None] == g[None, :], BF16)


def _bucket_table():
    n = np.arange(MAX_DISTANCE)
    exact = N_BUCKETS // 2
    scaled = np.log(np.maximum(n, 1).astype(np.float32) / np.float32(exact)) / np.float32(math.log(MAX_DISTANCE / exact))
    large = np.minimum(exact + (scaled.astype(np.float32) * (N_BUCKETS - exact)).astype(np.int32), N_BUCKETS - 1)
    return np.where(n < exact, n, large).astype(np.int32)


def _ada_kernel(c_ref, w_ref, b_ref, o_ref):
    o_ref[...] = _dot(_silu(c_ref[...]), w_ref[...]) + b_ref[...]


def _ada_mod(c_all, ada_w, ada_b):
    depth, d, n6 = ada_w.shape
    rows = c_all.shape[0]
    tn = 1536 if n6 % 1536 == 0 else n6
    return pl.pallas_call(
        _ada_kernel,
        grid=(depth, n6 // tn),
        in_specs=[pl.BlockSpec((rows, d), lambda l, j: (0, 0)),
                  pl.BlockSpec((None, d, tn), lambda l, j: (l, 0, j)),
                  pl.BlockSpec((None, 1, tn), lambda l, j: (l, 0, j))],
        out_specs=pl.BlockSpec((None, rows, tn), lambda l, j: (l, 0, j)),
        out_shape=jax.ShapeDtypeStruct((depth, rows, n6), F32),
        compiler_params=_cparams(("arbitrary", "arbitrary")),
        name="ada_mod",
    )(c_all, ada_w, ada_b.reshape(depth, 1, n6))


def _mod_specs(per_token, tm, d, tiles_per_batch):
    if per_token:
        return pl.BlockSpec((tm, d), lambda i: (i, 0))
    return pl.BlockSpec((None, 1, d), lambda i: (i // tiles_per_batch, 0, 0))


def _modulated_rms(x, sh, sc):
    xn = x * lax.rsqrt(jnp.mean(x * x, axis=-1, keepdims=True) + EPS)
    return xn * (1.0 + sc) + sh


def _even_in_kernel(x_ref, sh_ref, sc_ref, wm_ref, wt_ref, lng_ref, lnb_ref, msp_ref, bsp_ref,
                    qg_ref, kg_ref, ikg_ref, e_ref,
                    aout_ref, av_ref, q_ref, k32_ref, kbf_ref, v32_ref, vx_ref, iq_ref, ik32_ref, ikbf_ref, iw_ref):
    tm = x_ref.shape[0]
    aw = aout_ref.shape[1]
    gd = aw // A_GROUPS
    h = _modulated_rms(x_ref[...], sh_ref[...], sc_ref[...]).astype(BF16)

    def seg(i):
        return jnp.dot(h, wm_ref[:, i * aw:(i + 1) * aw], preferred_element_type=F32)

    au = _gelu_tanh(seg(0))
    av = _gelu_tanh(seg(1))
    avc = av - jnp.mean(av, axis=-1, keepdims=True)
    vn = avc * lax.rsqrt(jnp.mean(avc * avc, axis=-1, keepdims=True) + EPS) * lng_ref[...] + lnb_ref[...]
    av_ref[...] = vn
    vnb = vn.astype(BF16)
    cr = msp_ref.shape[1]
    for c in range(tm // cr):
        r0 = c * cr
        for g in range(A_GROUPS):
            mixed = jnp.dot(msp_ref[g], vnb[r0:r0 + cr, g * gd:(g + 1) * gd], preferred_element_type=F32)
            mixed = mixed + bsp_ref[:, g:g + 1]
            aout_ref[r0:r0 + cr, g * gd:(g + 1) * gd] = (au[r0:r0 + cr, g * gd:(g + 1) * gd] * mixed).astype(BF16)

    hd = aw // B_HEADS
    q = _group_rms(seg(2), e_ref, hd) * qg_ref[...]
    q_ref[...] = (q * (hd ** -0.5)).astype(BF16)
    k = _group_rms(seg(3), e_ref, hd) * kg_ref[...]
    k32_ref[...] = k
    kbf_ref[...] = k.astype(BF16)
    v = seg(4)
    v32_ref[...] = v
    _store_v_with_ones(vx_ref, v)
    iq_ref[...] = (seg(5) * (IDX_DIM ** -0.5)).astype(BF16)
    tail = jnp.dot(h, wt_ref[...], preferred_element_type=F32)
    ik = tail[:, :IDX_DIM]
    ik = ik * lax.rsqrt(jnp.mean(ik * ik, axis=-1, keepdims=True) + EPS) * ikg_ref[...]
    ik32_ref[...] = ik
    ikbf_ref[...] = ik.astype(BF16)
    iw_ref[...] = tail[:, IDX_DIM:IDX_DIM + IDX_HEADS] * (IDX_HEADS ** -0.5)


def _even_in(x, sh, sc, per_token, tiles_per_batch, tm, w_in, ln_g, ln_b, msp, bsp, q_g, k_g, ik_g):
    n, d = x.shape
    aw = d // 2
    wm = w_in[:, :6 * aw].astype(BF16)
    wt = jnp.pad(w_in[:, 6 * aw:], ((0, 0), (0, LANES - (IDX_DIM + IDX_HEADS)))).astype(BF16)
    full = lambda shape: pl.BlockSpec(shape, lambda i: (0,) * len(shape))
    row = lambda w: pl.BlockSpec((tm, w), lambda i: (i, 0))
    mod = _mod_specs(per_token, tm, d, tiles_per_batch)
    outs = [(aw, BF16), (aw, F32), (aw, BF16), (aw, F32), (aw, BF16), (aw, F32), (B_HEADS * LANES, BF16), (aw, BF16),
            (IDX_DIM, F32), (IDX_DIM, BF16), (IDX_HEADS, F32)]
    return pl.pallas_call(
        _even_in_kernel,
        grid=(n // tm,),
        in_specs=[row(d), mod, mod, full(wm.shape), full(wt.shape), full((1, aw)), full((1, aw)),
                  full(msp.shape), full(bsp.shape), full((1, aw)), full((1, aw)), full((1, IDX_DIM)), full((aw, aw))],
        out_specs=[row(w) for w, _ in outs],
        out_shape=[jax.ShapeDtypeStruct((n, w), dt) for w, dt in outs],
        compiler_params=_cparams(("arbitrary",)),
        name="even_in",
    )(x, sh, sc, wm, wt, ln_g.reshape(1, aw), ln_b.reshape(1, aw), msp, bsp,
      jnp.tile(q_g, B_HEADS).reshape(1, aw), jnp.tile(k_g, B_HEADS).reshape(1, aw), ik_g.reshape(1, IDX_DIM),
      _block_ones(aw, aw // B_HEADS))


def _odd_in_kernel(x_ref, sh_ref, sc_ref, wm_ref, wt_ref, qg_ref, kg_ref, e_ref,
                   xc_ref, cz_ref, tail_ref, q_ref, k32_ref, kbf_ref, v32_ref, vx_ref):
    hw = cz_ref.shape[1]
    h = _modulated_rms(x_ref[...], sh_ref[...], sc_ref[...]).astype(BF16)

    def seg(i):
        return jnp.dot(h, wm_ref[:, i * hw:(i + 1) * hw], preferred_element_type=F32)

    for i in range(3):
        xc_ref[:, i * hw:(i + 1) * hw] = seg(i)
    cz_ref[...] = seg(3)
    tail_ref[...] = jnp.dot(h, wt_ref[...], preferred_element_type=F32)
    q = _group_rms(seg(4), e_ref, D_QK_DIM) * qg_ref[...]
    q_ref[...] = (q * (D_QK_DIM ** -0.5)).astype(BF16)
    k = _group_rms(seg(5), e_ref, D_QK_DIM) * kg_ref[...]
    k32_ref[...] = k
    kbf_ref[...] = k.astype(BF16)
    v = seg(6)
    v32_ref[...] = v
    _store_v_with_ones(vx_ref, v)


def _odd_in(x, sh, sc, per_token, tiles_per_batch, tm, w_in, q_g, k_g):
    n, d = x.shape
    hw = d // 2
    ng = 2 * C_HEADS
    wm = jnp.concatenate([w_in[:, :4 * hw], w_in[:, 4 * hw + ng:]], axis=1).astype(BF16)
    wt = jnp.pad(w_in[:, 4 * hw:4 * hw + ng], ((0, 0), (0, LANES - ng))).astype(BF16)
    full = lambda shape: pl.BlockSpec(shape, lambda i: (0,) * len(shape))
    row = lambda w: pl.BlockSpec((tm, w), lambda i: (i, 0))
    mod = _mod_specs(per_token, tm, d, tiles_per_batch)
    outs = [(3 * hw, F32), (hw, F32), (LANES, F32), (hw, BF16), (hw, F32), (hw, BF16), (hw, F32), (D_HEADS * LANES, BF16)]
    reps = hw // D_QK_DIM
    return pl.pallas_call(
        _odd_in_kernel,
        grid=(n // tm,),
        in_specs=[row(d), mod, mod, full(wm.shape), full(wt.shape), full((1, hw)), full((1, hw)), full((hw, hw))],
        out_specs=[row(w) for w, _ in outs],
        out_shape=[jax.ShapeDtypeStruct((n, w), dt) for w, dt in outs],
        compiler_params=_cparams(("arbitrary",)),
        name="odd_in",
    )(x, sh, sc, wm, wt, jnp.tile(q_g, reps).reshape(1, hw), jnp.tile(k_g, reps).reshape(1, hw),
      _block_ones(hw, D_QK_DIM))


def _out_proj_kernel(a_ref, b_ref, x_ref, g1_ref, sh_ref, sc_ref, w_ref, rw_ref, rb_ref, x1_ref, h2_ref, lg_ref):
    hw = a_ref.shape[1]
    y = jnp.dot(a_ref[...], w_ref[:hw, :], preferred_element_type=F32)
    y = y + jnp.dot(b_ref[...], w_ref[hw:, :], preferred_element_type=F32)
    x1 = x_ref[...] + g1_ref[...] * y
    x1_ref[...] = x1
    h2 = _modulated_rms(x1, sh_ref[...], sc_ref[...])
    h2_ref[...] = h2.astype(BF16)
    lg_ref[...] = _dot_hi(h2, rw_ref[...]) + rb_ref[...]


def _out_proj(a, b, x, g1, sh, sc, per_token, tiles_per_batch, tm, w_out, router_w, router_b):
    n, d = x.shape
    hw = a.shape[1]
    ne = router_w.shape[1]
    rw = jnp.pad(router_w, ((0, 0), (0, LANES - ne)))
    rb = jnp.pad(router_b, (0, LANES - ne)).reshape(1, LANES)
    full = lambda shape: pl.BlockSpec(shape, lambda i: (0,) * len(shape))
    row = lambda w: pl.BlockSpec((tm, w), lambda i: (i, 0))
    mod = _mod_specs(per_token, tm, d, tiles_per_batch)
    x1, h2, lg = pl.pallas_call(
        _out_proj_kernel,
        grid=(n // tm,),
        in_specs=[row(hw), row(hw), row(d), mod, mod, mod, full((d, d)), full((d, LANES)), full((1, LANES))],
        out_specs=[row(d), row(d), row(LANES)],
        out_shape=[jax.ShapeDtypeStruct((n, d), F32), jax.ShapeDtypeStruct((n, d), BF16),
                   jax.ShapeDtypeStruct((n, LANES), F32)],
        compiler_params=_cparams(("arbitrary",)),
        name="out_proj",
    )(a, b, x, g1, sh, sc, w_out.astype(BF16), rw, rb)
    return x1, h2, lg


def _idx_scores_kernel(iq_ref, iw_ref, ik_ref, o_ref, *, tq, tk):
    i = pl.program_id(1)
    j = pl.program_id(2)

    @pl.when(j * tk <= i * tq + tq - 1)
    def _():
        ik = ik_ref[...]
        acc = jnp.zeros((tq, tk), F32)
        for h in range(IDX_HEADS):
            d = _dot_nt(iq_ref[:, h * IDX_DIM:(h + 1) * IDX_DIM], ik)
            acc = acc + iw_ref[:, h:h + 1] * jnp.maximum(d, 0.0)
        o_ref[...] = acc

    @pl.when(j * tk > i * tq + tq - 1)
    def _():
        o_ref[...] = jnp.zeros((tq, tk), F32)


def _idx_scores_prompt(iq, iw, ik, tq, tk):
    bn, s, _ = iq.shape
    last = lambda i: (i * tq + tq - 1) // tk
    return pl.pallas_call(
        functools.partial(_idx_scores_kernel, tq=tq, tk=tk),
        grid=(bn, s // tq, s // tk),
        in_specs=[pl.BlockSpec((None, tq, IDX_HEADS * IDX_DIM), lambda b, i, j: (b, i, 0)),
                  pl.BlockSpec((None, tq, IDX_HEADS), lambda b, i, j: (b, i, 0)),
                  pl.BlockSpec((None, tk, IDX_DIM), lambda b, i, j: (b, jnp.minimum(j, last(i)), 0))],
        out_specs=pl.BlockSpec((None, tq, tk), lambda b, i, j: (b, i, j)),
        out_shape=jax.ShapeDtypeStruct((bn, s, s), F32),
        compiler_params=_cparams(("arbitrary", "arbitrary", "arbitrary")),
        name="idx_scores_prompt",
    )(iq, iw, ik)


def _select_kernel(nch_ref, s_ref, lim_ref, m_ref, key_ref, *, ksel, kc):
    rows, width = s_ref.shape
    nch = nch_ref[pl.program_id(0)]
    lim = lim_ref[...]
    fold = kc // LANES
    col_bits = max(1, int(width - 1).bit_length())

    def cols(c):
        return c * kc + lax.broadcasted_iota(I32, (rows, kc), 1)

    def fill(c, carry):
        off = pl.multiple_of(c * kc, kc)
        bits = lax.bitcast_convert_type(s_ref[:, pl.ds(off, kc)] + 0.0, I32)
        key = jnp.where(bits < 0, bits ^ jnp.int32(0x7FFFFFFF), bits)
        key_ref[:, pl.ds(off, kc)] = jnp.where(cols(c) <= lim, key, jnp.int32(INT_MIN))
        return carry

    lax.fori_loop(0, nch, fill, 0)

    def count(pred):
        def body(c, acc):
            off = pl.multiple_of(c * kc, kc)
            hit = jnp.where(pred(c, key_ref[:, pl.ds(off, kc)]), 1.0, 0.0)
            part = hit[:, :LANES]
            for f in range(1, fold):
                part = part + hit[:, f * LANES:(f + 1) * LANES]
            return acc + part
        acc = lax.fori_loop(0, nch, body, jnp.zeros((rows, LANES), F32))
        return jnp.sum(acc, axis=1, keepdims=True)

    def thr_bit(b, carry):
        prefix, n_ge = carry
        cand = prefix + (jnp.int32(1) << (31 - b))
        cnt = count(lambda c, key: key >= cand)
        take = cnt >= ksel
        return jnp.where(take, cand, prefix), jnp.where(take, cnt, n_ge)

    thr, n_ge = lax.fori_loop(0, 32, thr_bit, (jnp.full((rows, 1), INT_MIN, I32),
                                               jnp.full((rows, 1), 1.0, F32) * (nch * kc).astype(F32)))

    def tie_break():
        need = ksel - count(lambda c, key: key > thr)

        def tie_bit(b, pos):
            cand = pos + (jnp.int32(1) << (col_bits - 1 - b))
            cnt = count(lambda c, key: (key == thr) & (cols(c) < cand))
            return jnp.where(cnt < need, cand, pos)

        return lax.fori_loop(0, col_bits, tie_bit, jnp.zeros((rows, 1), I32))

    surplus = jnp.where((n_ge > ksel) & (thr > jnp.int32(INT_MIN)), 1.0, 0.0)
    last = lax.cond(jnp.max(surplus) > 0.0, tie_break, lambda: jnp.full((rows, 1), width, I32))

    def emit(c, carry):
        off = pl.multiple_of(c * kc, kc)
        key = key_ref[:, pl.ds(off, kc)]
        col = cols(c)
        sel = ((key > thr) | ((key == thr) & (col <= last))) & (col <= lim)
        m_ref[:, pl.ds(off, kc)] = jnp.where(sel, 0.0, NEG).astype(m_ref.dtype)
        return carry

    lax.fori_loop(0, nch, emit, 0)

    def blank(c, carry):
        off = pl.multiple_of(c * kc, kc)
        m_ref[:, pl.ds(off, kc)] = jnp.full((rows, kc), NEG, m_ref.dtype)
        return carry

    lax.fori_loop(nch, width // kc, blank, 0)


def _select_mask(scores, lim, nch, ksel, kc, out_dtype):
    r, width = scores.shape
    tr = 128
    return pl.pallas_call(
        functools.partial(_select_kernel, ksel=ksel, kc=kc),
        grid_spec=pltpu.PrefetchScalarGridSpec(
            num_scalar_prefetch=1,
            grid=(r // tr,),
            in_specs=[pl.BlockSpec((tr, width), lambda i, n: (i, 0)),
                      pl.BlockSpec((tr, 1), lambda i, n: (i, 0))],
            out_specs=pl.BlockSpec((tr, width), lambda i, n: (i, 0)),
            scratch_shapes=[pltpu.VMEM((tr, width), I32)]),
        out_shape=jax.ShapeDtypeStruct((r, width), out_dtype),
        compiler_params=_cparams(("arbitrary",)),
        name="select_mask",
    )(nch, scores, lim)


def _prompt_attn_kernel(*refs, nh, nm, dqk, dv, tq, has_mask, lam_init):
    refs = list(refs)
    q_ref, k_ref, vx_ref, tbl_ref = refs[:4]
    pos = 4
    madd_ref = None
    if has_mask:
        madd_ref = refs[pos]
        pos += 1
    lam_ref = subg_ref = None
    if nm == 2:
        lam_ref, subg_ref = refs[pos], refs[pos + 1]
        pos += 2
    o_ref, mx_scr, acc_scr = refs[pos:pos + 3]
    ma_scr = refs[pos + 3] if has_mask else None
    nu = nh * nm
    nf = tq // LANES
    qi = pl.program_id(1)
    mx_scr[...] = jnp.full(mx_scr.shape, -jnp.inf, F32)
    acc_scr[...] = jnp.zeros(acc_scr.shape, F32)

    def logits(koff, back, u):
        s = _dot_nt(q_ref[:, u * dqk:(u + 1) * dqk], k_ref[pl.ds(koff, tq), u * dqk:(u + 1) * dqk])
        if back is not None:
            s = s + tbl_ref[u // nm, back]
        if has_mask:
            s = s + ma_scr[...]
        return s

    def sweep(j, back, second):
        koff = pl.multiple_of(j * tq, tq)
        if has_mask:
            ma_scr[...] = madd_ref[:, pl.ds(koff, tq)].astype(F32)
        for u in range(nu):
            s = logits(koff, back, u)
            if not second:
                r = s[:, :LANES]
                for f in range(1, nf):
                    r = jnp.maximum(r, s[:, f * LANES:(f + 1) * LANES])
                mx_scr[u] = jnp.maximum(mx_scr[u], r)
            else:
                mx = mx_scr[u]
                p = jnp.concatenate([jnp.exp(s[:, f * LANES:(f + 1) * LANES] - mx) for f in range(nf)], axis=1)
                h = u // nm
                acc_scr[u] += jnp.dot(p.astype(BF16), vx_ref[pl.ds(koff, tq), h * LANES:(h + 1) * LANES],
                                      preferred_element_type=F32)

    def all_chunks(second):
        def far(j, carry):
            sweep(j, None, second)
            return carry

        lax.fori_loop(0, jnp.maximum(qi - 1, 0), far, 0)

        @pl.when(qi >= 1)
        def _():
            sweep(qi - 1, 1, second)

        sweep(qi, 0, second)

    all_chunks(False)
    for u in range(nu):
        mx_scr[u] = jnp.broadcast_to(jnp.max(mx_scr[u], axis=-1, keepdims=True), (tq, LANES))
    all_chunks(True)

    def normalised(u):
        a = acc_scr[u]
        return a[:, :dv] / a[:, dv:dv + 1]

    for h in range(nh):
        if nm == 1:
            o_ref[:, h * dv:(h + 1) * dv] = normalised(h).astype(o_ref.dtype)
        else:
            att = normalised(2 * h) - lam_ref[0] * normalised(2 * h + 1)
            att = att * lax.rsqrt(jnp.mean(att * att, axis=-1, keepdims=True) + EPS)
            o_ref[:, h * dv:(h + 1) * dv] = (att * subg_ref[...] * (1.0 - lam_init)).astype(o_ref.dtype)


def _bias_tables(rel_bias, tq):
    bucket = jnp.asarray(np.concatenate([_bucket_table(), [N_BUCKETS - 1]]))
    per_dist = rel_bias[bucket].astype(F32)
    far = per_dist[MAX_DISTANCE]
    r = np.arange(tq)[:, None]
    c = np.arange(tq)[None, :]
    tabs = []
    for d in range(2):
        dist = d * tq + r - c
        vals = per_dist[np.clip(dist, 0, MAX_DISTANCE)] - far
        vals = jnp.where(jnp.asarray(dist >= 0)[..., None], vals, NEG)
        tabs.append(jnp.transpose(vals, (2, 0, 1)))
    return jnp.stack(tabs, axis=1)


def _prompt_attn(q, k, vx, tbl, madd, lam, subg, *, nh, nm, dqk, dv, tq, lam_init):
    bn, s, w = q.shape
    assert MAX_DISTANCE <= tq and s % tq == 0 and tq % LANES == 0 and dv < LANES
    has_mask = madd is not None
    once = lambda width: pl.BlockSpec((None, s, width), lambda b, i: (b, 0, 0), pipeline_mode=pl.Buffered(1))
    in_specs = [pl.BlockSpec((None, tq, w), lambda b, i: (b, i, 0)), once(w), once(nh * LANES),
                pl.BlockSpec(tbl.shape, lambda b, i: (0, 0, 0, 0), pipeline_mode=pl.Buffered(1))]
    args = [q, k, vx, tbl]
    if has_mask:
        in_specs.append(pl.BlockSpec((None, tq, s), lambda b, i: (b, i, 0)))
        args.append(madd)
    if nm == 2:
        in_specs += [pl.BlockSpec(memory_space=pltpu.SMEM), pl.BlockSpec((1, dv), lambda b, i: (0, 0))]
        args += [lam, subg]
    return pl.pallas_call(
        functools.partial(_prompt_attn_kernel, nh=nh, nm=nm, dqk=dqk, dv=dv, tq=tq, has_mask=has_mask, lam_init=lam_init),
        grid=(bn, s // tq),
        in_specs=in_specs,
        out_specs=pl.BlockSpec((None, tq, nh * dv), lambda b, i: (b, i, 0)),
        out_shape=jax.ShapeDtypeStruct((bn, s, nh * dv), BF16),
        scratch_shapes=[pltpu.VMEM((nh * nm, tq, LANES), F32), pltpu.VMEM((nh * nm, tq, LANES), F32)]
                       + ([pltpu.VMEM((tq, tq), F32)] if has_mask else []),
        compiler_params=_cparams(("arbitrary", "arbitrary")),
        name="prompt_attn_dsa" if nm == 1 else "prompt_attn_diff",
    )(*args)


def _paged_scores_kernel(pt_ref, iq_ref, iw_ref, *refs, gp, page):
    pages = refs[:gp]
    new_ref, o_ref = refs[gp], refs[gp + 1]
    s_id = pl.program_id(1)
    last = pl.num_programs(1) - 1
    rows = iq_ref.shape[0]
    tp = rows // IDX_HEADS

    def score(ik):
        d = jnp.maximum(_dot_nt(iq_ref[...], ik), 0.0) * iw_ref[...]
        acc = d[:tp]
        for h in range(1, IDX_HEADS):
            acc = acc + d[h * tp:(h + 1) * tp]
        return acc

    @pl.when(s_id < last)
    def _():
        for g in range(gp):
            o_ref[:, g * page:(g + 1) * page] = score(pages[g][...])

    @pl.when(s_id == last)
    def _():
        o_ref[...] = jnp.zeros(o_ref.shape, F32)
        o_ref[:, :page] = score(new_ref[...])


def _paged_scores(page_table, iq_rows, iw_rows, cache_idx, ik_new, gp):
    bn, rows, _ = iq_rows.shape
    tp = rows // IDX_HEADS
    page = cache_idx.shape[1]
    npages = page_table.shape[1]
    nsteps = npages // gp + 1
    width = nsteps * gp * page

    def page_map(g):
        return lambda b, s, pt: (pt[b, jnp.minimum(s * gp + g, npages - 1)], 0, 0)

    return pl.pallas_call(
        functools.partial(_paged_scores_kernel, gp=gp, page=page),
        grid_spec=pltpu.PrefetchScalarGridSpec(
            num_scalar_prefetch=1,
            grid=(bn, nsteps),
            in_specs=[pl.BlockSpec((None, rows, IDX_DIM), lambda b, s, pt: (b, 0, 0)),
                      pl.BlockSpec((None, rows, 1), lambda b, s, pt: (b, 0, 0))]
                     + [pl.BlockSpec((None, page, IDX_DIM), page_map(g)) for g in range(gp)]
                     + [pl.BlockSpec((None, page, IDX_DIM), lambda b, s, pt: (b, 0, 0))],
            out_specs=pl.BlockSpec((None, tp, gp * page), lambda b, s, pt: (b, 0, s))),
        out_shape=jax.ShapeDtypeStruct((bn, tp, width), F32),
        compiler_params=_cparams(("arbitrary", "arbitrary")),
        name="paged_idx_scores",
    )(page_table, iq_rows, iw_rows, *([cache_idx] * gp), ik_new)


def _paged_attn_kernel(pt_ref, q_ref, am_ref, *refs, gp, page, nh, nm, dv, tp, lam_init):
    kpages = refs[:gp]
    vpages = refs[gp:2 * gp]
    knew_ref, vnew_ref = refs[2 * gp], refs[2 * gp + 1]
    pos = 2 * gp + 2
    lam_ref = subg_ref = None
    if nm == 2:
        lam_ref, subg_ref = refs[pos], refs[pos + 1]
        pos += 2
    o_ref, m_scr, l_scr, acc_scr = refs[pos:pos + 4]
    s_id = pl.program_id(1)
    last = pl.num_programs(1) - 1

    @pl.when(s_id == 0)
    def _():
        m_scr[...] = jnp.full(m_scr.shape, -jnp.inf, F32)
        l_scr[...] = jnp.zeros(l_scr.shape, F32)
        acc_scr[...] = jnp.zeros(acc_scr.shape, F32)

    def blocks(kvs):
        ss = [_dot_nt(q_ref[...], kp[...]) + am_ref[:, g * page:(g + 1) * page] for g, (kp, _) in enumerate(kvs)]
        mx = ss[0]
        for s in ss[1:]:
            mx = jnp.maximum(mx, s)
        m_prev = m_scr[...]
        m_new = jnp.maximum(m_prev, jnp.max(mx, axis=-1, keepdims=True))
        alpha = jnp.exp(m_prev - m_new)
        ps = [jnp.exp(s - m_new) for s in ss]
        psum = ps[0]
        for p in ps[1:]:
            psum = psum + p
        pv = [_dot(p, vp[...]) for p, (_, vp) in zip(ps, kvs)]
        tot = pv[0]
        for x in pv[1:]:
            tot = tot + x
        l_scr[...] = alpha * l_scr[...] + jnp.sum(psum, axis=-1, keepdims=True)
        acc_scr[...] = alpha * acc_scr[...] + tot
        m_scr[...] = m_new

    @pl.when(s_id < last)
    def _():
        blocks(list(zip(kpages, vpages)))

    @pl.when(s_id == last)
    def _():
        blocks([(knew_ref, vnew_ref)])
        accn = acc_scr[...] / l_scr[...]
        for h in range(nh):
            cols = slice(h * dv, (h + 1) * dv)
            if nm == 1:
                o_ref[:, cols] = accn[h * tp:(h + 1) * tp, cols]
            else:
                att = accn[2 * h * tp:(2 * h + 1) * tp, cols] - lam_ref[0] * accn[(2 * h + 1) * tp:(2 * h + 2) * tp, cols]
                att = att * lax.rsqrt(jnp.mean(att * att, axis=-1, keepdims=True) + EPS)
                o_ref[:, cols] = att * subg_ref[...] * (1.0 - lam_init)


def _paged_attn(page_table, q_rows, addmask, cache_k, cache_v, k_new, v_new, lam, subg, *, gp, nh, nm, dv, tp, lam_init):
    bn, rows, w = q_rows.shape
    page = cache_k.shape[1]
    npages = page_table.shape[1]
    nsteps = npages // gp + 1

    def page_map(g):
        return lambda b, s, pt: (pt[b, jnp.minimum(s * gp + g, npages - 1)], 0, 0)

    new_spec = pl.BlockSpec((None, page, w), lambda b, s, pt: (b, 0, 0))
    in_specs = ([pl.BlockSpec((None, rows, w), lambda b, s, pt: (b, 0, 0)),
                 pl.BlockSpec((None, rows, gp * page), lambda b, s, pt: (b, 0, s))]
                + [pl.BlockSpec((None, page, w), page_map(g)) for g in range(gp)] * 2
                + [new_spec, new_spec])
    args = [q_rows, addmask] + [cache_k] * gp + [cache_v] * gp + [k_new, v_new]
    if nm == 2:
        in_specs += [pl.BlockSpec(memory_space=pltpu.SMEM), pl.BlockSpec((1, dv), lambda b, s, pt: (0, 0))]
        args += [lam, subg]
    return pl.pallas_call(
        functools.partial(_paged_attn_kernel, gp=gp, page=page, nh=nh, nm=nm, dv=dv, tp=tp, lam_init=lam_init),
        grid_spec=pltpu.PrefetchScalarGridSpec(
            num_scalar_prefetch=1,
            grid=(bn, nsteps),
            in_specs=in_specs,
            out_specs=pl.BlockSpec((None, tp, nh * dv), lambda b, s, pt: (b, 0, 0)),
            scratch_shapes=[pltpu.VMEM((rows, 1), F32), pltpu.VMEM((rows, 1), F32), pltpu.VMEM((rows, w), F32)]),
        out_shape=jax.ShapeDtypeStruct((bn, tp, nh * dv), F32),
        compiler_params=_cparams(("arbitrary", "arbitrary")),
        name="paged_attn_dsa" if nm == 1 else "paged_attn_diff",
    )(page_table, *args)


def _gdn_kernel(xc_ref, prev_ref, buf_ref, cz_ref, tail_ref, s0_ref, cw_ref, alog_ref, dtb_ref, nw_ref,
                o_ref, sfin_ref, s_scr, *, n_valid):
    ci = pl.program_id(1)
    bg, c, ch = xc_ref.shape
    dk = C_DK
    dv = (ch - 2 * C_HEADS * dk) // C_HEADS

    @pl.when(ci == 0)
    def _():
        s_scr[...] = s0_ref[...]

    row = lax.broadcasted_iota(I32, (c, c), 0)
    colm = lax.broadcasted_iota(I32, (c, c), 1)
    incl = row >= colm
    tril = incl.astype(F32)
    eye = (row == colm).astype(F32)

    units = []
    for b in range(bg):
        hist = jnp.where(ci == 0, buf_ref[b], prev_ref[b])
        nh_rows = hist.shape[0]
        xfull = jnp.concatenate([hist, xc_ref[b]], axis=0)
        y = cw_ref[C_CONV - 1:C_CONV, :] * xfull[nh_rows:, :]
        for j in range(1, C_CONV):
            y = y + cw_ref[C_CONV - 1 - j:C_CONV - j, :] * pltpu.roll(xfull, j, 0)[nh_rows:, :]
        y = _silu(y)

        tail = tail_ref[b]
        beta_all = jax.nn.sigmoid(tail)
        sp_in = tail + dtb_ref[...]
        g_all = -jnp.exp(alog_ref[...]) * (jnp.maximum(sp_in, 0.0) + jnp.log1p(jnp.exp(-jnp.abs(sp_in))))
        if n_valid < c:
            valid = lax.broadcasted_iota(I32, tail.shape, 0) < n_valid
            beta_all = jnp.where(valid, beta_all, 0.0)
            g_all = jnp.where(valid, g_all, 0.0)
        gc_all = _dot_hi(tril, g_all)
        gc_rows = jnp.transpose(gc_all)

        for h in range(C_HEADS):
            q = y[:, h * dk:(h + 1) * dk]
            k = y[:, (C_HEADS + h) * dk:(C_HEADS + h + 1) * dk]
            v = y[:, 2 * C_HEADS * dk + h * dv:2 * C_HEADS * dk + (h + 1) * dv]
            q = q * lax.rsqrt(jnp.sum(q * q, axis=-1, keepdims=True) + EPS) * (dk ** -0.5)
            k = k * lax.rsqrt(jnp.sum(k * k, axis=-1, keepdims=True) + EPS)
            beta = beta_all[:, h:h + 1]
            gc = gc_all[:, C_HEADS + h:C_HEADS + h + 1]
            decay = jnp.exp(jnp.where(incl, gc - gc_rows[C_HEADS + h:C_HEADS + h + 1, :], -jnp.inf))
            units.append(dict(b=b, h=h, q=q, k=k, kb=k * beta, vb=v * beta, gc=gc, decay=decay))

    def stage(fn):
        return [fn(un) for un in units]

    def put(name, vals):
        for un, val in zip(units, vals):
            un[name] = val

    put('a', stage(lambda un: _dot_nt(un['kb'], un['k']) * un['decay'] * (1.0 - eye)))
    put('inv', stage(lambda un: eye - un['a']))
    put('pw', stage(lambda un: un['a']))
    for _ in range(max(0, int(c - 1).bit_length() - 1)):
        put('pw', stage(lambda un: _dot_x3(un['pw'], un['pw'])))
        put('inv', stage(lambda un: un['inv'] + _dot_x3(un['inv'], un['pw'])))
    put('u', stage(lambda un: _dot_x3(un['inv'], un['vb'])))
    put('w', stage(lambda un: _dot_x3(un['inv'], un['kb'] * jnp.exp(un['gc']))))
    put('s', stage(lambda un: s_scr[un['b'], un['h']]))
    put('v_new', stage(lambda un: un['u'] - _dot(un['w'], un['s'])))
    put('intra', stage(lambda un: _dot_nt(un['q'], un['k']) * un['decay']))
    put('o', stage(lambda un: _dot(un['q'] * jnp.exp(un['gc']), un['s']) + _dot(un['intra'], un['v_new'])))
    for un in units:
        gl = un['gc'][c - 1:c, :]
        s_scr[un['b'], un['h']] = un['s'] * jnp.exp(gl) + _dot_tn(un['k'] * jnp.exp(gl - un['gc']), un['v_new'])
    for un in units:
        b, h, o = un['b'], un['h'], un['o']
        o = o * lax.rsqrt(jnp.mean(o * o, axis=-1, keepdims=True) + EPS) * nw_ref[...]
        o_ref[b, :, h * dv:(h + 1) * dv] = (o * _silu(cz_ref[b, :, h * dv:(h + 1) * dv])).astype(o_ref.dtype)

    @pl.when(ci == pl.num_programs(1) - 1)
    def _():
        sfin_ref[...] = s_scr[...]


def _gdn(xc, buf, cz, tail, s0, conv_w, a_log, dt_bias, norm_w, chunk, n_valid):
    bn, t, ch = xc.shape
    hw = cz.shape[2]
    nck = t // chunk
    dvh = hw // C_HEADS
    hist = SUBLANES
    per_chunk = chunk // hist
    bg = 2 if bn % 2 == 0 else 1
    gate_lanes = lambda p: jnp.pad(p.astype(F32), (C_HEADS, LANES - 2 * C_HEADS)).reshape(1, LANES)
    return pl.pallas_call(
        functools.partial(_gdn_kernel, n_valid=n_valid),
        grid=(bn // bg, nck),
        in_specs=[pl.BlockSpec((bg, chunk, ch), lambda b, i: (b, i, 0)),
                  pl.BlockSpec((bg, hist, ch), lambda b, i: (b, jnp.maximum(i * per_chunk - 1, 0), 0)),
                  pl.BlockSpec((bg, hist, ch), lambda b, i: (b, 0, 0)),
                  pl.BlockSpec((bg, chunk, hw), lambda b, i: (b, i, 0)),
                  pl.BlockSpec((bg, chunk, LANES), lambda b, i: (b, i, 0)),
                  pl.BlockSpec((bg, C_HEADS, C_DK, dvh), lambda b, i: (b, 0, 0, 0)),
                  pl.BlockSpec((C_CONV, ch), lambda b, i: (0, 0)),
                  pl.BlockSpec((1, LANES), lambda b, i: (0, 0)),
                  pl.BlockSpec((1, LANES), lambda b, i: (0, 0)),
                  pl.BlockSpec((1, dvh), lambda b, i: (0, 0))],
        out_specs=[pl.BlockSpec((bg, chunk, hw), lambda b, i: (b, i, 0)),
                   pl.BlockSpec((bg, C_HEADS, C_DK, dvh), lambda b, i: (b, 0, 0, 0))],
        out_shape=[jax.ShapeDtypeStruct((bn, t, hw), BF16), jax.ShapeDtypeStruct((bn, C_HEADS, C_DK, dvh), F32)],
        scratch_shapes=[pltpu.VMEM((bg, C_HEADS, C_DK, dvh), F32)],
        compiler_params=_cparams(("arbitrary", "arbitrary")),
        name="gated_delta",
    )(xc, xc, buf, cz, tail, s0, conv_w, gate_lanes(a_log), gate_lanes(dt_bias), norm_w.reshape(1, dvh))


def _moe_kernel(be_ref, x_ref, gate_ref, w1_ref, b1_ref, w2_ref, b2_ref, o_ref, w1_scr, w2_scr):
    i = pl.program_id(0)
    dff = w2_ref.shape[0]
    changed = jnp.logical_or(i == 0, be_ref[i] != be_ref[jnp.maximum(i - 1, 0)])

    @pl.when(changed)
    def _():
        w1_scr[...] = w1_ref[...].astype(BF16)
        w2_scr[...] = w2_ref[...].astype(BF16)

    hmid = jnp.dot(x_ref[...], w1_scr[...], preferred_element_type=F32) + b1_ref[...]
    gate = jnp.minimum(hmid[:, :dff], SWIGLU_LIMIT)
    up = jnp.clip(hmid[:, dff:], -SWIGLU_LIMIT, SWIGLU_LIMIT)
    act = (up + 1.0) * gate * jax.nn.sigmoid(SWIGLU_ALPHA * gate)
    y = jnp.dot(act.astype(BF16), w2_scr[...], preferred_element_type=F32) + b2_ref[...]
    o_ref[...] = y * gate_ref[...]


def _moe_ffn(xs, slot_gate, block_e, layer, w1, b1, w2, b2, bm):
    n_slots, d = xs.shape
    _, ne, _, d2 = w1.shape
    dff = w2.shape[2]
    return pl.pallas_call(
        _moe_kernel,
        grid_spec=pltpu.PrefetchScalarGridSpec(
            num_scalar_prefetch=1,
            grid=(n_slots // bm,),
            in_specs=[pl.BlockSpec((bm, d), lambda i, be: (i, 0)),
                      pl.BlockSpec((bm, 1), lambda i, be: (i, 0)),
                      pl.BlockSpec((None, None, d, d2), lambda i, be: (layer, be[i], 0, 0)),
                      pl.BlockSpec((None, None, 1, d2), lambda i, be: (layer, be[i], 0, 0)),
                      pl.BlockSpec((None, None, dff, d), lambda i, be: (layer, be[i], 0, 0)),
                      pl.BlockSpec((None, None, 1, d), lambda i, be: (layer, be[i], 0, 0))],
            out_specs=pl.BlockSpec((bm, d), lambda i, be: (i, 0)),
            scratch_shapes=[pltpu.VMEM((d, d2), BF16), pltpu.VMEM((dff, d), BF16)]),
        out_shape=jax.ShapeDtypeStruct((n_slots, d), F32),
        compiler_params=_cparams(("arbitrary",)),
        name="moe_ffn",
    )(block_e, xs, slot_gate.reshape(n_slots, 1), w1, b1.reshape(b1.shape[0], ne, 1, d2), w2, b2.reshape(b2.shape[0], ne, 1, d))


def _route_kernel(lg_ref, e_ref, g_ref, r_ref, cnt_ref, run_scr, *, ne):
    i = pl.program_id(0)
    tm = lg_ref.shape[0]

    @pl.when(i == 0)
    def _():
        run_scr[...] = jnp.zeros(run_scr.shape, F32)

    lane = lax.broadcasted_iota(I32, (tm, LANES), 1)
    lane_f = lane.astype(F32)
    x = jnp.where(lane < ne, lg_ref[...], -jnp.inf)
    vals, hots = [], []
    e_out = jnp.zeros((tm, LANES), F32)
    for j in range(TOP_K):
        m = jnp.max(x, axis=-1, keepdims=True)
        idx = jnp.min(jnp.where(x == m, lane_f, float(LANES)), axis=-1, keepdims=True)
        hot = lane_f == idx
        vals.append(m)
        hots.append(hot)
        e_out = jnp.where(lane == j, idx, e_out)
        x = jnp.where(hot, -jnp.inf, x)
    ex = [jnp.exp(v - vals[0]) for v in vals]
    denom = ex[0]
    for j in range(1, TOP_K):
        denom = denom + ex[j]
    chosen = jnp.where(hots[0], 1.0, 0.0)
    for j in range(1, TOP_K):
        chosen = chosen + jnp.where(hots[j], 1.0, 0.0)
    before = (lax.broadcasted_iota(I32, (tm, tm), 0) > lax.broadcasted_iota(I32, (tm, tm), 1)).astype(BF16)
    base = run_scr[...] + jnp.dot(before, chosen.astype(BF16), preferred_element_type=F32)
    g_out = jnp.zeros((tm, LANES), F32)
    r_out = jnp.zeros((tm, LANES), F32)
    for j in range(TOP_K):
        g_out = jnp.where(lane == j, ex[j] / denom, g_out)
        r_out = jnp.where(lane == j, jnp.sum(jnp.where(hots[j], base, 0.0), axis=-1, keepdims=True), r_out)
    e_ref[...] = e_out[:, :TOP_K].astype(I32)
    g_ref[...] = g_out[:, :TOP_K]
    r_ref[...] = r_out[:, :TOP_K].astype(I32)
    run_scr[...] = run_scr[...] + jnp.sum(chosen, axis=0, keepdims=True)

    @pl.when(i == pl.num_programs(0) - 1)
    def _():
        cnt_ref[...] = run_scr[...]


def _route(logits, ne):
    n = logits.shape[0]
    tm = next(t for t in (512, 384, 256, 128, 64, 32, 16, 8) if n % t == 0)
    small = lambda dt: jax.ShapeDtypeStruct((n, TOP_K), dt)
    return pl.pallas_call(
        functools.partial(_route_kernel, ne=ne),
        grid=(n // tm,),
        in_specs=[pl.BlockSpec((tm, LANES), lambda i: (i, 0))],
        out_specs=[pl.BlockSpec((tm, TOP_K), lambda i: (i, 0))] * 3 + [pl.BlockSpec((1, LANES), lambda i: (0, 0))],
        out_shape=[small(I32), small(F32), small(I32), jax.ShapeDtypeStruct((1, LANES), F32)],
        scratch_shapes=[pltpu.VMEM((1, LANES), F32)],
        compiler_params=_cparams(("arbitrary",)),
        name="moe_route",
    )(logits)


def _moe(h2, logits, layer, w1, b1, w2, b2, bm):
    n_tok = h2.shape[0]
    ne = w1.shape[1]
    top_idx, gates, rank, counts = _route(logits, ne)
    counts = counts[0, :ne].astype(I32)
    padded = (counts + bm - 1) // bm * bm
    pad_end = jnp.cumsum(padded)
    pad_start = pad_end - padded
    n_assign = n_tok * TOP_K
    n_blocks = -(-(n_assign + ne * (bm - 1)) // bm)
    n_slots = n_blocks * bm
    hot = top_idx[..., None] == jnp.arange(ne, dtype=I32)
    dest = (jnp.sum(jnp.where(hot, pad_start, 0), axis=-1) + rank).reshape(-1)
    block_e = jnp.minimum(jnp.sum(pad_end[None, :] <= (jnp.arange(n_blocks, dtype=I32) * bm)[:, None], axis=1),
                          ne - 1).astype(I32)
    payload = jnp.stack([jnp.repeat(jnp.arange(n_tok, dtype=I32), TOP_K),
                         lax.bitcast_convert_type(gates.reshape(-1), I32)], axis=-1)
    slots = jnp.zeros((n_slots, 2), I32).at[dest].set(payload)
    slot_gate = lax.bitcast_convert_type(slots[:, 1], F32)
    ys = _moe_ffn(h2[slots[:, 0]], slot_gate, block_e, layer, w1, b1, w2, b2, bm)
    return jnp.sum(ys[dest].reshape(n_tok, TOP_K, -1), axis=1)


def _rel_bias_sample(rel_bias, past_len, t):
    dist = (past_len + np.arange(t))[:, None] - np.arange(past_len + t)[None, :]
    n = np.maximum(dist, 0)
    bucket = np.where(n < MAX_DISTANCE, _bucket_table()[np.minimum(n, MAX_DISTANCE - 1)], N_BUCKETS - 1)
    return jnp.transpose(rel_bias[jnp.asarray(bucket)].astype(F32), (2, 0, 1)), dist >= 0


def _head_rows(x, tp):
    bn, t, g, w = x.shape
    xp = jnp.pad(x, ((0, 0), (0, tp - t), (0, 0), (0, 0)))
    eye = jnp.eye(g, dtype=x.dtype)
    return jnp.einsum('btgw,gk->bgtkw', xp, eye).reshape(bn, g * tp, g * w)


def kernel(x_prompt, x_sample, cache_b_k, cache_b_v, cache_b_idx, state_c_rec, state_c_conv, cache_d_k, cache_d_v,
           page_table, c_prompt, c_sample, rel_bias, ada_w, ada_b, even_w_in, even_w_out, a_ln_g, a_ln_b, a_w_sp,
           a_b_sp, b_q_norm, b_k_norm, b_idx_norm, odd_w_in, odd_w_out, c_conv_w, c_a_log, c_dt_bias, c_norm_w,
           d_q_norm, d_k_norm, d_lambda, d_subln, router_w, router_b, moe_w1, moe_b1, moe_w2, moe_b2):
    bp, sp, d = x_prompt.shape
    bs, ts, _ = x_sample.shape
    depth = ada_w.shape[0]
    hw = d // 2
    page = cache_b_k.shape[2]
    past_len = page_table.shape[1] * page
    n_p, n_s = bp * sp, bs * ts
    tm_p = 256
    tq = 256
    tp = SUBLANES
    gp = 8 if page_table.shape[1] % 8 == 0 else 1
    assert sp % tm_p == 0 and n_s % SUBLANES == 0 and ts <= tp and ts >= C_CONV - 1

    n_c = bp + bs
    c_all = jnp.pad(jnp.concatenate([c_prompt, c_sample], axis=0), ((0, -n_c % SUBLANES), (0, 0)))
    mod_all = _ada_mod(c_all, ada_w, ada_b)

    def mods(layer):
        m = mod_all[layer].reshape(-1, 6, d)
        mp = [m[:bp, i][:, None, :] for i in range(6)]
        ms = [jnp.repeat(m[bp:n_c, i], ts, axis=0) for i in range(6)]
        return mp, ms

    xp = x_prompt.reshape(n_p, d)
    xs = x_sample.reshape(n_s, d)
    tpb = sp // tm_p
    tbl = _bias_tables(rel_bias, tq)
    outs_p, outs_s = {}, {}

    for layer in range(depth):
        li = layer // 2
        mp, ms = mods(layer)
        if layer % 2 == 0:
            ws_p = jnp.where(np.tril(np.ones((A_CHUNK, A_CHUNK), bool)), a_w_sp[li], 0.0).astype(BF16)
            bsp_p = a_b_sp[li].T
            (aout, _, q, k32, kbf, v32, vbf, iq, ik32, ikbf, iw) = _even_in(
                xp, mp[0], mp[1], False, tpb, tm_p, even_w_in[li], a_ln_g[li], a_ln_b[li], ws_p, bsp_p,
                b_q_norm[li], b_k_norm[li], b_idx_norm[li])
            ksel = min(TOPK_MAX, sp // 4)
            kc = 512 if sp % 512 == 0 else sp
            scores = _idx_scores_prompt(iq.reshape(bp, sp, hw), iw.reshape(bp, sp, IDX_HEADS),
                                        ikbf.reshape(bp, sp, IDX_DIM), tq, kc)
            lim = jnp.tile(jnp.arange(sp, dtype=I32), bp).reshape(n_p, 1)
            nch = jnp.tile((jnp.arange(sp // 128, dtype=I32) * 128 + 127) // kc + 1, bp)
            madd = _select_mask(scores.reshape(n_p, sp), lim, nch, ksel, kc, BF16).reshape(bp, sp, sp)
            b_out = _prompt_attn(q.reshape(bp, sp, hw), kbf.reshape(bp, sp, hw), vbf.reshape(bp, sp, -1), tbl, madd,
                                 None, None, nh=B_HEADS, nm=1, dqk=hw // B_HEADS, dv=hw // B_HEADS, tq=tq, lam_init=0.0)
            outs_p.setdefault('b_k', []).append(k32.reshape(bp, sp, B_HEADS, -1))
            outs_p.setdefault('b_v', []).append(v32.reshape(bp, sp, B_HEADS, -1))
            outs_p.setdefault('b_idx', []).append(ik32.reshape(bp, sp, IDX_DIM))
            mix_p = (aout, b_out.reshape(n_p, hw))

            cs = min(ts, A_CHUNK)
            ws_s = jnp.where(np.tril(np.ones((cs, cs), bool)), a_w_sp[li][:, :cs, :cs], 0.0)
            ws_s = jnp.einsum('ab,gts->gatbs', jnp.eye(n_s // cs, dtype=F32), ws_s).reshape(A_GROUPS, n_s, n_s).astype(BF16)
            bsp_s = jnp.tile(a_b_sp[li][:, :cs].T, (n_s // cs, 1))
            (aout, av, q, k32, kbf, v32, vbf, iq, ik32, ikbf, iw) = _even_in(
                xs, ms[0], ms[1], True, 1, n_s, even_w_in[li], a_ln_g[li], a_ln_b[li], ws_s, bsp_s,
                b_q_norm[li], b_k_norm[li], b_idx_norm[li])
            ltot = past_len + ts
            ksel = min(TOPK_MAX, ltot // 4)
            iq_rows = jnp.pad(iq.reshape(bs, ts, IDX_HEADS, IDX_DIM), ((0, 0), (0, tp - ts), (0, 0), (0, 0)))
            iq_rows = jnp.transpose(iq_rows, (0, 2, 1, 3)).reshape(bs, IDX_HEADS * tp, IDX_DIM)
            iw_rows = jnp.pad(iw.reshape(bs, ts, IDX_HEADS), ((0, 0), (0, tp - ts), (0, 0)))
            iw_rows = jnp.transpose(iw_rows, (0, 2, 1)).reshape(bs, IDX_HEADS * tp, 1)
            pad_new = lambda a: jnp.pad(a.reshape(bs, ts, -1), ((0, 0), (0, page - ts), (0, 0)))
            sc_s = _paged_scores(page_table, iq_rows, iw_rows, cache_b_idx[li], pad_new(ikbf), gp)
            width = sc_s.shape[2]
            kc_s = gp * page
            lim_s = jnp.where(np.arange(tp) < ts, past_len + np.arange(tp), -1).astype(I32)
            lim_s = jnp.tile(lim_s, bs).reshape(bs * tp, 1)
            nch_s = jnp.full((bs * tp // 128,), width // kc_s, I32)
            sel_s = _select_mask(sc_s.reshape(bs * tp, width), lim_s, nch_s, ksel, kc_s, F32).reshape(bs, tp, width)
            bias_s, _ = _rel_bias_sample(rel_bias, past_len, ts)
            bias_s = jnp.pad(bias_s, ((0, 0), (0, tp - ts), (0, width - ltot)))
            am = (sel_s[:, None] + bias_s[None]).reshape(bs, B_HEADS * tp, width)
            q_rows = _head_rows(q.reshape(bs, ts, B_HEADS, -1), tp)
            cshape = (cache_b_k.shape[1], page, hw)
            flat = lambda cache: cache[li].reshape(cshape).astype(BF16)
            b_out = _paged_attn(page_table, q_rows, am, flat(cache_b_k), flat(cache_b_v),
                                pad_new(kbf), pad_new(v32), None, None, gp=gp, nh=B_HEADS, nm=1, dv=hw // B_HEADS, tp=tp,
                                lam_init=0.0)
            outs_s.setdefault('a_v', []).append(av.reshape(bs, ts, hw))
            outs_s.setdefault('b_k', []).append(k32.reshape(bs, ts, B_HEADS, -1))
            outs_s.setdefault('b_v', []).append(v32.reshape(bs, ts, B_HEADS, -1))
            outs_s.setdefault('b_idx', []).append(ik32.reshape(bs, ts, IDX_DIM))
            mix_s = (aout, b_out[:, :ts].reshape(n_s, hw).astype(BF16))
            w_out = even_w_out[li]
        else:
            lam_init = 0.8 - 0.6 * math.exp(-0.3 * layer)
            lp = d_lambda[li].astype(F32)
            lam = (jnp.exp(jnp.sum(lp[0] * lp[1])) - jnp.exp(jnp.sum(lp[2] * lp[3])) + lam_init).reshape(1)
            subg = d_subln[li].reshape(1, -1)
            dvh = hw // D_HEADS
            xc, cz, tail, q2, k2_32, k2bf, v2_32, v2bf = _odd_in(xp, mp[0], mp[1], False, tpb, tm_p, odd_w_in[li],
                                                                d_q_norm[li], d_k_norm[li])
            ch = xc.shape[1]
            xc3 = xc.reshape(bp, sp, ch)
            zbuf = jnp.zeros((bp, SUBLANES, ch), F32)
            s0 = jnp.zeros((bp, C_HEADS, C_DK, hw // C_HEADS), F32)
            chunk = C_CHUNK if sp % C_CHUNK == 0 else sp
            c_out, c_rec = _gdn(xc3, zbuf, cz.reshape(bp, sp, hw), tail.reshape(bp, sp, LANES), s0, c_conv_w[li],
                                c_a_log[li], c_dt_bias[li], c_norm_w[li], chunk, chunk)
            d_out = _prompt_attn(q2.reshape(bp, sp, hw), k2bf.reshape(bp, sp, hw), v2bf.reshape(bp, sp, -1), tbl, None,
                                 lam, subg, nh=D_HEADS, nm=2, dqk=D_QK_DIM, dv=dvh, tq=tq, lam_init=lam_init)
            outs_p.setdefault('c_rec', []).append(c_rec)
            outs_p.setdefault('c_conv', []).append(xc3[:, sp - (C_CONV - 1):])
            outs_p.setdefault('d_k', []).append(k2_32.reshape(bp, sp, D_HEADS, -1))
            outs_p.setdefault('d_v', []).append(v2_32.reshape(bp, sp, D_HEADS, -1))
            mix_p = (c_out.reshape(n_p, hw), d_out.reshape(n_p, hw))

            xc, cz, tail, q2, k2_32, k2bf, v2_32, v2bf = _odd_in(xs, ms[0], ms[1], True, 1, n_s, odd_w_in[li],
                                                                d_q_norm[li], d_k_norm[li])
            pad_t = lambda a: jnp.pad(a.reshape(bs, ts, -1), ((0, 0), (0, C_CHUNK - ts), (0, 0)))
            buf = state_c_conv[li]
            buf8 = jnp.pad(buf, ((0, 0), (SUBLANES - (C_CONV - 1), 0), (0, 0)))
            c_out, c_rec = _gdn(pad_t(xc), buf8, pad_t(cz), pad_t(tail), state_c_rec[li], c_conv_w[li], c_a_log[li],
                                c_dt_bias[li], c_norm_w[li], C_CHUNK, ts)
            ltot = past_len + ts
            bias_s, allowed = _rel_bias_sample(rel_bias, past_len, ts)
            width = (page_table.shape[1] // gp + 1) * gp * page
            am = jnp.where(jnp.asarray(allowed)[None], bias_s, NEG)
            am = jnp.pad(am, ((0, 0), (0, tp - ts), (0, 0)))
            am = jnp.pad(am, ((0, 0), (0, 0), (0, width - ltot)), constant_values=NEG)
            am = jnp.broadcast_to(am[None, :, None], (bs, D_HEADS, 2, tp, width)).reshape(bs, D_HEADS * 2 * tp, width)
            q_rows = _head_rows(q2.reshape(bs, ts, 2 * D_HEADS, D_QK_DIM), tp)
            pad_new = lambda a: jnp.pad(a.reshape(bs, ts, -1), ((0, 0), (0, page - ts), (0, 0)))
            cshape = (cache_d_k.shape[1], page, hw)
            flat = lambda cache: cache[li].reshape(cshape).astype(BF16)
            d_out = _paged_attn(page_table, q_rows, am, flat(cache_d_k), flat(cache_d_v),
                                pad_new(k2bf), pad_new(v2_32), lam, subg, gp=gp, nh=D_HEADS, nm=2, dv=dvh, tp=tp,
                                lam_init=lam_init)
            outs_s.setdefault('c_rec', []).append(c_rec)
            outs_s.setdefault('c_conv', []).append(
                jnp.concatenate([buf, xc.reshape(bs, ts, ch)], axis=1)[:, ts:])
            outs_s.setdefault('d_k', []).append(k2_32.reshape(bs, ts, D_HEADS, -1))
            outs_s.setdefault('d_v', []).append(v2_32.reshape(bs, ts, D_HEADS, -1))
            mix_s = (c_out[:, :ts].reshape(n_s, hw), d_out[:, :ts].reshape(n_s, hw).astype(BF16))
            w_out = odd_w_out[li]

        x1p, h2p, lgp = _out_proj(mix_p[0], mix_p[1], xp, mp[2], mp[3], mp[4], False, tpb, tm_p, w_out,
                                  router_w[layer], router_b[layer])
        x1s, h2s, lgs = _out_proj(mix_s[0], mix_s[1], xs, ms[2], ms[3], ms[4], True, 1, n_s, w_out,
                                  router_w[layer], router_b[layer])
        moe_out = _moe(jnp.concatenate([h2p, h2s], axis=0), jnp.concatenate([lgp, lgs], axis=0), layer,
                       moe_w1, moe_b1, moe_w2, moe_b2, 256)
        xp = (x1p.reshape(bp, sp, d) + mp[5] * moe_out[:n_p].reshape(bp, sp, d)).reshape(n_p, d)
        xs = x1s + ms[5] * moe_out[n_p:]

    st = lambda name, src: jnp.stack(src[name])
    return (xp.reshape(bp, sp, d), xs.reshape(bs, ts, d),
            st('b_k', outs_p), st('b_v', outs_p), st('b_idx', outs_p), st('c_rec', outs_p), st('c_conv', outs_p),
            st('d_k', outs_p), st('d_v', outs_p),
            st('a_v', outs_s), st('b_k', outs_s), st('b_v', outs_s), st('b_idx', outs_s), st('c_rec', outs_s),
            st('c_conv', outs_s), st('d_k', outs_s), st('d_v', outs_s))
```

```python
import functools
import math

import numpy as np
import jax
import jax.numpy as jnp
from jax import lax
from jax.experimental import pallas as pl
from jax.experimental.pallas import tpu as pltpu

F32 = jnp.float32
BF16 = jnp.bfloat16
I32 = jnp.int32
HIGHEST = lax.Precision.HIGHEST

A_GROUPS = 4
A_CHUNK = 128
B_HEADS = 8
IDX_HEADS = 8
IDX_DIM = 64
TOPK_MAX = 256
C_HEADS = 4
C_DK = 128
C_CONV = 4
C_CHUNK = 64
D_HEADS = 8
D_QK_DIM = 32
N_BUCKETS = 32
MAX_DISTANCE = 128
TOP_K = 4
SWIGLU_LIMIT = 7.0
SWIGLU_ALPHA = 1.702
EPS = 1e-6

LANES = 128
SUBLANES = 8
VMEM_LIMIT_BYTES = 56 * 1024 * 1024

NEG = -1e30
INT_MIN = -2 ** 31


def _cparams(sem):
    return pltpu.CompilerParams(dimension_semantics=sem, vmem_limit_bytes=VMEM_LIMIT_BYTES)


def _dot(a, b):
    return jnp.dot(a.astype(BF16), b.astype(BF16), preferred_element_type=F32)


def _dot_nt(a, b):
    return lax.dot_general(a.astype(BF16), b.astype(BF16), (((1,), (1,)), ((), ())), preferred_element_type=F32)


def _dot_hi(a, b):
    return jnp.dot(a, b, preferred_element_type=F32, precision=HIGHEST)


def _dot_tn(a, b):
    return lax.dot_general(a.astype(BF16), b.astype(BF16), (((0,), (0,)), ((), ())), preferred_element_type=F32)


def _dot_x3(a, b):
    ah = a.astype(BF16)
    al = (a - ah.astype(F32)).astype(BF16)
    bh = b.astype(BF16)
    bl = (b - bh.astype(F32)).astype(BF16)
    dot = lambda x, y: jnp.dot(x, y, preferred_element_type=F32)
    return dot(ah, bh) + (dot(ah, bl) + dot(al, bh))


def _silu(x):
    return x * jax.nn.sigmoid(x)


def _gelu_tanh(x):
    return 0.5 * x * (1.0 + jnp.tanh(math.sqrt(2.0 / math.pi) * (x + 0.044715 * (x * x * x))))


def _group_rms(x, ones_ref, gsize):
    xx = x * x
    hi = xx.astype(BF16)
    lo = (xx - hi.astype(F32)).astype(BF16)
    e = ones_ref[...]
    ss = jnp.dot(hi, e, preferred_element_type=F32) + jnp.dot(lo, e, preferred_element_type=F32)
    return x * lax.rsqrt(ss * (1.0 / gsize) + EPS)


def _store_v_with_ones(vx_ref, v):
    tm, w = v.shape
    nh = vx_ref.shape[1] // LANES
    dv = w // nh
    tail = jnp.where(lax.broadcasted_iota(I32, (tm, LANES - dv), 1) == 0, 1.0, 0.0)
    for h in range(nh):
        vx_ref[:, h * LANES:(h + 1) * LANES] = jnp.concatenate([v[:, h * dv:(h + 1) * dv], tail], axis=1).astype(BF16)


def _block_ones(width, gsize):
    g = np.arange(width) // gsize
    return jnp.asarray(g[:, None] == g[None, :], BF16)


def _bucket_table():
    n = np.arange(MAX_DISTANCE)
    exact = N_BUCKETS // 2
    scaled = np.log(np.maximum(n, 1).astype(np.float32) / np.float32(exact)) / np.float32(math.log(MAX_DISTANCE / exact))
    large = np.minimum(exact + (scaled.astype(np.float32) * (N_BUCKETS - exact)).astype(np.int32), N_BUCKETS - 1)
    return np.where(n < exact, n, large).astype(np.int32)


def _bucket_of(dist):
    n = np.maximum(dist, 0)
    return np.where(n < MAX_DISTANCE, _bucket_table()[np.minimum(n, MAX_DISTANCE - 1)], N_BUCKETS - 1)


def _bias_by_bucket(rel_bias, bucket):
    out = jnp.zeros((rel_bias.shape[1],) + bucket.shape, F32)
    for b in np.unique(bucket):
        out = jnp.where(jnp.asarray(bucket == b)[None], rel_bias[b].astype(F32).reshape((-1,) + (1,) * bucket.ndim), out)
    return out


def _ada_kernel(c_ref, w_ref, b_ref, o_ref):
    o_ref[...] = _dot(_silu(c_ref[...]), w_ref[...]) + b_ref[...]


def _ada_mod(c_all, ada_w, ada_b):
    depth, d, n6 = ada_w.shape
    rows = c_all.shape[0]
    tn = 1536 if n6 % 1536 == 0 else n6
    return pl.pallas_call(
        _ada_kernel,
        grid=(depth, n6 // tn),
        in_specs=[pl.BlockSpec((rows, d), lambda l, j: (0, 0)),
                  pl.BlockSpec((None, d, tn), lambda l, j: (l, 0, j)),
                  pl.BlockSpec((None, 1, tn), lambda l, j: (l, 0, j))],
        out_specs=pl.BlockSpec((None, rows, tn), lambda l, j: (l, 0, j)),
        out_shape=jax.ShapeDtypeStruct((depth, rows, n6), F32),
        compiler_params=_cparams(("arbitrary", "arbitrary")),
        name="ada_mod",
    )(c_all, ada_w, ada_b.reshape(depth, 1, n6))


def _mod_specs(per_token, tm, d, tiles_per_batch):
    if per_token:
        return pl.BlockSpec((tm, d), lambda i: (i, 0))
    return pl.BlockSpec((None, 1, d), lambda i: (i // tiles_per_batch, 0, 0))


def _modulated_rms(x, sh, sc):
    xn = x * lax.rsqrt(jnp.mean(x * x, axis=-1, keepdims=True) + EPS)
    return xn * (1.0 + sc) + sh


def _even_in_kernel(x_ref, sh_ref, sc_ref, wm_ref, wt_ref, lng_ref, lnb_ref, msp_ref, bsp_ref,
                    qg_ref, kg_ref, ikg_ref, e_ref,
                    aout_ref, av_ref, q_ref, k32_ref, kbf_ref, v32_ref, vx_ref, iq_ref, ik32_ref, ikbf_ref, iw_ref):
    tm = x_ref.shape[0]
    aw = aout_ref.shape[1]
    gd = aw // A_GROUPS
    h = _modulated_rms(x_ref[...], sh_ref[...], sc_ref[...]).astype(BF16)

    def seg(i):
        return jnp.dot(h, wm_ref[:, i * aw:(i + 1) * aw], preferred_element_type=F32)

    au = _gelu_tanh(seg(0))
    av = _gelu_tanh(seg(1))
    avc = av - jnp.mean(av, axis=-1, keepdims=True)
    vn = avc * lax.rsqrt(jnp.mean(avc * avc, axis=-1, keepdims=True) + EPS) * lng_ref[...] + lnb_ref[...]
    av_ref[...] = vn
    vnb = vn.astype(BF16)
    cr = msp_ref.shape[1]
    for c in range(tm // cr):
        r0 = c * cr
        for g in range(A_GROUPS):
            mixed = jnp.dot(msp_ref[g], vnb[r0:r0 + cr, g * gd:(g + 1) * gd], preferred_element_type=F32)
            mixed = mixed + bsp_ref[:, g:g + 1]
            aout_ref[r0:r0 + cr, g * gd:(g + 1) * gd] = (au[r0:r0 + cr, g * gd:(g + 1) * gd] * mixed).astype(BF16)

    hd = aw // B_HEADS
    q = _group_rms(seg(2), e_ref, hd) * qg_ref[...]
    q_ref[...] = (q * (hd ** -0.5)).astype(BF16)
    k = _group_rms(seg(3), e_ref, hd) * kg_ref[...]
    k32_ref[...] = k
    kbf_ref[...] = k.astype(BF16)
    v = seg(4)
    v32_ref[...] = v
    _store_v_with_ones(vx_ref, v)
    iq_ref[...] = (seg(5) * (IDX_DIM ** -0.5)).astype(BF16)
    tail = jnp.dot(h, wt_ref[...], preferred_element_type=F32)
    ik = tail[:, :IDX_DIM]
    ik = ik * lax.rsqrt(jnp.mean(ik * ik, axis=-1, keepdims=True) + EPS) * ikg_ref[...]
    ik32_ref[...] = ik
    ikbf_ref[...] = ik.astype(BF16)
    iw_ref[...] = tail[:, IDX_DIM:IDX_DIM + IDX_HEADS] * (IDX_HEADS ** -0.5)


def _even_in(x, sh, sc, per_token, tiles_per_batch, tm, w_in, ln_g, ln_b, msp, bsp, q_g, k_g, ik_g):
    n, d = x.shape
    aw = d // 2
    wm = w_in[:, :6 * aw].astype(BF16)
    wt = jnp.pad(w_in[:, 6 * aw:], ((0, 0), (0, LANES - (IDX_DIM + IDX_HEADS)))).astype(BF16)
    full = lambda shape: pl.BlockSpec(shape, lambda i: (0,) * len(shape))
    row = lambda w: pl.BlockSpec((tm, w), lambda i: (i, 0))
    mod = _mod_specs(per_token, tm, d, tiles_per_batch)
    outs = [(aw, BF16), (aw, F32), (aw, BF16), (aw, F32), (aw, BF16), (aw, F32), (B_HEADS * LANES, BF16), (aw, BF16),
            (IDX_DIM, F32), (IDX_DIM, BF16), (IDX_HEADS, F32)]
    return pl.pallas_call(
        _even_in_kernel,
        grid=(n // tm,),
        in_specs=[row(d), mod, mod, full(wm.shape), full(wt.shape), full((1, aw)), full((1, aw)),
                  full(msp.shape), full(bsp.shape), full((1, aw)), full((1, aw)), full((1, IDX_DIM)), full((aw, aw))],
        out_specs=[row(w) for w, _ in outs],
        out_shape=[jax.ShapeDtypeStruct((n, w), dt) for w, dt in outs],
        compiler_params=_cparams(("arbitrary",)),
        name="even_in",
    )(x, sh, sc, wm, wt, ln_g.reshape(1, aw), ln_b.reshape(1, aw), msp, bsp,
      jnp.tile(q_g, B_HEADS).reshape(1, aw), jnp.tile(k_g, B_HEADS).reshape(1, aw), ik_g.reshape(1, IDX_DIM),
      _block_ones(aw, aw // B_HEADS))


def _odd_in_kernel(x_ref, sh_ref, sc_ref, wm_ref, wt_ref, qg_ref, kg_ref, e_ref,
                   xc_ref, cz_ref, tail_ref, q_ref, k32_ref, kbf_ref, v32_ref, vx_ref):
    hw = cz_ref.shape[1]
    h = _modulated_rms(x_ref[...], sh_ref[...], sc_ref[...]).astype(BF16)

    def seg(i):
        return jnp.dot(h, wm_ref[:, i * hw:(i + 1) * hw], preferred_element_type=F32)

    for i in range(3):
        xc_ref[:, i * hw:(i + 1) * hw] = seg(i)
    cz_ref[...] = seg(3)
    tail_ref[...] = jnp.dot(h, wt_ref[...], preferred_element_type=F32)
    q = _group_rms(seg(4), e_ref, D_QK_DIM) * qg_ref[...]
    q_ref[...] = (q * (D_QK_DIM ** -0.5)).astype(BF16)
    k = _group_rms(seg(5), e_ref, D_QK_DIM) * kg_ref[...]
    k32_ref[...] = k
    kbf_ref[...] = k.astype(BF16)
    v = seg(6)
    v32_ref[...] = v
    _store_v_with_ones(vx_ref, v)


def _odd_in(x, sh, sc, per_token, tiles_per_batch, tm, w_in, q_g, k_g):
    n, d = x.shape
    hw = d // 2
    ng = 2 * C_HEADS
    wm = jnp.concatenate([w_in[:, :4 * hw], w_in[:, 4 * hw + ng:]], axis=1).astype(BF16)
    wt = jnp.pad(w_in[:, 4 * hw:4 * hw + ng], ((0, 0), (0, LANES - ng))).astype(BF16)
    full = lambda shape: pl.BlockSpec(shape, lambda i: (0,) * len(shape))
    row = lambda w: pl.BlockSpec((tm, w), lambda i: (i, 0))
    mod = _mod_specs(per_token, tm, d, tiles_per_batch)
    outs = [(3 * hw, F32), (hw, F32), (LANES, F32), (hw, BF16), (hw, F32), (hw, BF16), (hw, F32), (D_HEADS * LANES, BF16)]
    reps = hw // D_QK_DIM
    return pl.pallas_call(
        _odd_in_kernel,
        grid=(n // tm,),
        in_specs=[row(d), mod, mod, full(wm.shape), full(wt.shape), full((1, hw)), full((1, hw)), full((hw, hw))],
        out_specs=[row(w) for w, _ in outs],
        out_shape=[jax.ShapeDtypeStruct((n, w), dt) for w, dt in outs],
        compiler_params=_cparams(("arbitrary",)),
        name="odd_in",
    )(x, sh, sc, wm, wt, jnp.tile(q_g, reps).reshape(1, hw), jnp.tile(k_g, reps).reshape(1, hw),
      _block_ones(hw, D_QK_DIM))


def _out_proj_kernel(a_ref, b_ref, x_ref, g1_ref, sh_ref, sc_ref, w_ref, rw_ref, rb_ref, x1_ref, h2_ref, lg_ref):
    hw = a_ref.shape[1]
    y = jnp.dot(a_ref[...], w_ref[:hw, :], preferred_element_type=F32)
    y = y + jnp.dot(b_ref[...], w_ref[hw:, :], preferred_element_type=F32)
    x1 = x_ref[...] + g1_ref[...] * y
    x1_ref[...] = x1
    h2 = _modulated_rms(x1, sh_ref[...], sc_ref[...])
    h2_ref[...] = h2
    lg_ref[...] = _dot_hi(h2, rw_ref[...]) + rb_ref[...]


def _out_proj(a, b, x, g1, sh, sc, per_token, tiles_per_batch, tm, w_out, router_w, router_b):
    n, d = x.shape
    hw = a.shape[1]
    ne = router_w.shape[1]
    rw = jnp.pad(router_w, ((0, 0), (0, LANES - ne)))
    rb = jnp.pad(router_b, (0, LANES - ne)).reshape(1, LANES)
    full = lambda shape: pl.BlockSpec(shape, lambda i: (0,) * len(shape))
    row = lambda w: pl.BlockSpec((tm, w), lambda i: (i, 0))
    mod = _mod_specs(per_token, tm, d, tiles_per_batch)
    x1, h2, lg = pl.pallas_call(
        _out_proj_kernel,
        grid=(n // tm,),
        in_specs=[row(hw), row(hw), row(d), mod, mod, mod, full((d, d)), full((d, LANES)), full((1, LANES))],
        out_specs=[row(d), row(d), row(LANES)],
        out_shape=[jax.ShapeDtypeStruct((n, d), F32), jax.ShapeDtypeStruct((n, d), F32),
                   jax.ShapeDtypeStruct((n, LANES), F32)],
        compiler_params=_cparams(("arbitrary",)),
        name="out_proj",
    )(a, b, x, g1, sh, sc, w_out.astype(BF16), rw, rb)
    return x1, h2, lg


def _idx_scores_kernel(iq_ref, iw_ref, ik_ref, o_ref, *, tq, tk):
    i = pl.program_id(1)
    j = pl.program_id(2)

    @pl.when(j * tk <= i * tq + tq - 1)
    def _():
        ik = ik_ref[...]
        acc = jnp.zeros((tq, tk), F32)
        for h in range(IDX_HEADS):
            d = _dot_nt(iq_ref[:, h * IDX_DIM:(h + 1) * IDX_DIM], ik)
            acc = acc + iw_ref[:, h:h + 1] * jnp.maximum(d, 0.0)
        o_ref[...] = acc

    @pl.when(j * tk > i * tq + tq - 1)
    def _():
        o_ref[...] = jnp.zeros((tq, tk), F32)


def _idx_scores_prompt(iq, iw, ik, tq, tk):
    bn, s, _ = iq.shape
    last = lambda i: (i * tq + tq - 1) // tk
    return pl.pallas_call(
        functools.partial(_idx_scores_kernel, tq=tq, tk=tk),
        grid=(bn, s // tq, s // tk),
        in_specs=[pl.BlockSpec((None, tq, IDX_HEADS * IDX_DIM), lambda b, i, j: (b, i, 0)),
                  pl.BlockSpec((None, tq, IDX_HEADS), lambda b, i, j: (b, i, 0)),
                  pl.BlockSpec((None, tk, IDX_DIM), lambda b, i, j: (b, jnp.minimum(j, last(i)), 0))],
        out_specs=pl.BlockSpec((None, tq, tk), lambda b, i, j: (b, i, j)),
        out_shape=jax.ShapeDtypeStruct((bn, s, s), F32),
        compiler_params=_cparams(("arbitrary", "arbitrary", "arbitrary")),
        name="idx_scores_prompt",
    )(iq, iw, ik)


def _select_kernel(nch_ref, s_ref, lim_ref, m_ref, key_ref, *, ksel, kc):
    rows, width = s_ref.shape
    nch = nch_ref[pl.program_id(0)]
    lim = lim_ref[...]
    fold = kc // LANES
    col_bits = max(1, int(width - 1).bit_length())

    def cols(c):
        return c * kc + lax.broadcasted_iota(I32, (rows, kc), 1)

    def fill(c, carry):
        off = pl.multiple_of(c * kc, kc)
        bits = lax.bitcast_convert_type(s_ref[:, pl.ds(off, kc)] + 0.0, I32)
        key = jnp.where(bits < 0, bits ^ jnp.int32(0x7FFFFFFF), bits)
        key_ref[:, pl.ds(off, kc)] = jnp.where(cols(c) <= lim, key, jnp.int32(INT_MIN))
        return carry

    lax.fori_loop(0, nch, fill, 0)

    def count(pred):
        def body(c, acc):
            off = pl.multiple_of(c * kc, kc)
            hit = jnp.where(pred(c, key_ref[:, pl.ds(off, kc)]), 1.0, 0.0)
            part = hit[:, :LANES]
            for f in range(1, fold):
                part = part + hit[:, f * LANES:(f + 1) * LANES]
            return acc + part
        acc = lax.fori_loop(0, nch, body, jnp.zeros((rows, LANES), F32))
        return jnp.sum(acc, axis=1, keepdims=True)

    def thr_bit(b, carry):
        prefix, n_ge = carry
        cand = prefix + (jnp.int32(1) << (31 - b))
        cnt = count(lambda c, key: key >= cand)
        take = cnt >= ksel
        return jnp.where(take, cand, prefix), jnp.where(take, cnt, n_ge)

    thr, n_ge = lax.fori_loop(0, 32, thr_bit, (jnp.full((rows, 1), INT_MIN, I32),
                                               jnp.full((rows, 1), 1.0, F32) * (nch * kc).astype(F32)))

    def tie_break():
        need = ksel - count(lambda c, key: key > thr)

        def tie_bit(b, pos):
            cand = pos + (jnp.int32(1) << (col_bits - 1 - b))
            cnt = count(lambda c, key: (key == thr) & (cols(c) < cand))
            return jnp.where(cnt < need, cand, pos)

        return lax.fori_loop(0, col_bits, tie_bit, jnp.zeros((rows, 1), I32))

    surplus = jnp.where((n_ge > ksel) & (thr > jnp.int32(INT_MIN)), 1.0, 0.0)
    last = lax.cond(jnp.max(surplus) > 0.0, tie_break, lambda: jnp.full((rows, 1), width, I32))

    def emit(c, carry):
        off = pl.multiple_of(c * kc, kc)
        key = key_ref[:, pl.ds(off, kc)]
        col = cols(c)
        sel = ((key > thr) | ((key == thr) & (col <= last))) & (col <= lim)
        m_ref[:, pl.ds(off, kc)] = jnp.where(sel, 0.0, NEG).astype(m_ref.dtype)
        return carry

    lax.fori_loop(0, nch, emit, 0)

    def blank(c, carry):
        off = pl.multiple_of(c * kc, kc)
        m_ref[:, pl.ds(off, kc)] = jnp.full((rows, kc), NEG, m_ref.dtype)
        return carry

    lax.fori_loop(nch, width // kc, blank, 0)


def _select_mask(scores, lim, nch, ksel, kc, out_dtype):
    r, width = scores.shape
    tr = 128
    return pl.pallas_call(
        functools.partial(_select_kernel, ksel=ksel, kc=kc),
        grid_spec=pltpu.PrefetchScalarGridSpec(
            num_scalar_prefetch=1,
            grid=(r // tr,),
            in_specs=[pl.BlockSpec((tr, width), lambda i, n: (i, 0)),
                      pl.BlockSpec((tr, 1), lambda i, n: (i, 0))],
            out_specs=pl.BlockSpec((tr, width), lambda i, n: (i, 0)),
            scratch_shapes=[pltpu.VMEM((tr, width), I32)]),
        out_shape=jax.ShapeDtypeStruct((r, width), out_dtype),
        compiler_params=_cparams(("arbitrary",)),
        name="select_mask",
    )(nch, scores, lim)


def _prompt_attn_kernel(*refs, nh, nm, dqk, dv, tq, has_mask, lam_init):
    refs = list(refs)
    q_ref, k_ref, vx_ref, tbl_ref = refs[:4]
    pos = 4
    madd_ref = None
    if has_mask:
        madd_ref = refs[pos]
        pos += 1
    lam_ref = subg_ref = None
    if nm == 2:
        lam_ref, subg_ref = refs[pos], refs[pos + 1]
        pos += 2
    o_ref, mx_scr, acc_scr, qz_scr = refs[pos:pos + 4]
    ma_scr = refs[pos + 4] if has_mask else None
    nu = nh * nm
    nf = tq // LANES
    gu = LANES // dqk
    qi = pl.program_id(1)
    mx_scr[...] = jnp.full(mx_scr.shape, -jnp.inf, F32)
    acc_scr[...] = jnp.zeros(acc_scr.shape, F32)

    lane = lax.broadcasted_iota(I32, (tq, LANES), 1)
    for g in range(nu // gu):
        qt = q_ref[:, g * LANES:(g + 1) * LANES]
        for i in range(gu):
            keep = (lane >= i * dqk) & (lane < (i + 1) * dqk)
            qz_scr[g, i * tq:(i + 1) * tq, :] = jnp.where(keep, qt, jnp.zeros_like(qt))

    def sweep(j, back, second):
        koff = pl.multiple_of(j * tq, tq)
        if has_mask:
            ma_scr[...] = madd_ref[:, pl.ds(koff, tq)].astype(F32)
        for g in range(nu // gu):
            stacked = _dot_nt(qz_scr[g], k_ref[pl.ds(koff, tq), g * LANES:(g + 1) * LANES])
            ps = []
            for i in range(gu):
                u = g * gu + i
                s = stacked[i * tq:(i + 1) * tq]
                if back is not None:
                    s = s + tbl_ref[u // nm, back]
                if has_mask:
                    s = s + ma_scr[...]
                if not second:
                    r = s[:, :LANES]
                    for f in range(1, nf):
                        r = jnp.maximum(r, s[:, f * LANES:(f + 1) * LANES])
                    mx_scr[u] = jnp.maximum(mx_scr[u], r)
                else:
                    mx = mx_scr[u]
                    p = jnp.concatenate([jnp.exp(s[:, f * LANES:(f + 1) * LANES] - mx) for f in range(nf)], axis=1)
                    ps.append(p.astype(BF16))
            if second:
                for i0 in range(0, gu, nm):
                    u0 = g * gu + i0
                    h = u0 // nm
                    stack = ps[i0] if nm == 1 else jnp.concatenate(ps[i0:i0 + nm], axis=0)
                    pv = jnp.dot(stack, vx_ref[pl.ds(koff, tq), h * LANES:(h + 1) * LANES], preferred_element_type=F32)
                    for m in range(nm):
                        acc_scr[u0 + m] += pv[m * tq:(m + 1) * tq]

    def all_chunks(second):
        def far(j, carry):
            sweep(j, None, second)
            return carry

        lax.fori_loop(0, jnp.maximum(qi - 1, 0), far, 0)

        @pl.when(qi >= 1)
        def _():
            sweep(qi - 1, 1, second)

        sweep(qi, 0, second)

    all_chunks(False)
    for u in range(nu):
        mx_scr[u] = jnp.broadcast_to(jnp.max(mx_scr[u], axis=-1, keepdims=True), (tq, LANES))
    all_chunks(True)

    def normalised(u):
        a = acc_scr[u]
        return a[:, :dv] / a[:, dv:dv + 1]

    for h in range(nh):
        if nm == 1:
            o_ref[:, h * dv:(h + 1) * dv] = normalised(h).astype(o_ref.dtype)
        else:
            att = normalised(2 * h) - lam_ref[0] * normalised(2 * h + 1)
            att = att * lax.rsqrt(jnp.mean(att * att, axis=-1, keepdims=True) + EPS)
            o_ref[:, h * dv:(h + 1) * dv] = (att * subg_ref[...] * (1.0 - lam_init)).astype(o_ref.dtype)


def _bias_tables(rel_bias, tq):
    far = rel_bias[N_BUCKETS - 1].astype(F32)[:, None, None]
    r = np.arange(tq)[:, None]
    c = np.arange(tq)[None, :]
    tabs = []
    for d in range(2):
        dist = d * tq + r - c
        vals = _bias_by_bucket(rel_bias, _bucket_of(dist)) - far
        tabs.append(jnp.where(jnp.asarray(dist >= 0)[None], vals, NEG))
    return jnp.stack(tabs, axis=1)


def _prompt_attn(q, k, vx, tbl, madd, lam, subg, *, nh, nm, dqk, dv, tq, lam_init):
    bn, s, w = q.shape
    assert MAX_DISTANCE <= tq and s % tq == 0 and tq % LANES == 0 and dv < LANES
    has_mask = madd is not None
    once = lambda width: pl.BlockSpec((None, s, width), lambda b, i: (b, 0, 0), pipeline_mode=pl.Buffered(1))
    in_specs = [pl.BlockSpec((None, tq, w), lambda b, i: (b, i, 0)), once(w), once(nh * LANES),
                pl.BlockSpec(tbl.shape, lambda b, i: (0, 0, 0, 0), pipeline_mode=pl.Buffered(1))]
    args = [q, k, vx, tbl]
    if has_mask:
        in_specs.append(pl.BlockSpec((None, tq, s), lambda b, i: (b, i, 0)))
        args.append(madd)
    if nm == 2:
        in_specs += [pl.BlockSpec(memory_space=pltpu.SMEM), pl.BlockSpec((1, dv), lambda b, i: (0, 0))]
        args += [lam, subg]
    return pl.pallas_call(
        functools.partial(_prompt_attn_kernel, nh=nh, nm=nm, dqk=dqk, dv=dv, tq=tq, has_mask=has_mask, lam_init=lam_init),
        grid=(bn, s // tq),
        in_specs=in_specs,
        out_specs=pl.BlockSpec((None, tq, nh * dv), lambda b, i: (b, i, 0)),
        out_shape=jax.ShapeDtypeStruct((bn, s, nh * dv), BF16),
        scratch_shapes=[pltpu.VMEM((nh * nm, tq, LANES), F32), pltpu.VMEM((nh * nm, tq, LANES), F32),
                        pltpu.VMEM((nh * nm * dqk // LANES, LANES // dqk * tq, LANES), BF16)]
                       + ([pltpu.VMEM((tq, tq), F32)] if has_mask else []),
        compiler_params=_cparams(("arbitrary", "arbitrary")),
        name="prompt_attn_dsa" if nm == 1 else "prompt_attn_diff",
    )(*args)


def _paged_scores_kernel(pt_ref, iq_ref, iw_ref, *refs, gp, page):
    pages = refs[:gp]
    new_ref, o_ref = refs[gp], refs[gp + 1]
    s_id = pl.program_id(1)
    last = pl.num_programs(1) - 1
    rows = iq_ref.shape[0]
    tp = rows // IDX_HEADS

    def score(ik):
        d = jnp.maximum(_dot_nt(iq_ref[...], ik), 0.0) * iw_ref[...]
        acc = d[:tp]
        for h in range(1, IDX_HEADS):
            acc = acc + d[h * tp:(h + 1) * tp]
        return acc

    @pl.when(s_id < last)
    def _():
        for g in range(gp):
            o_ref[:, g * page:(g + 1) * page] = score(pages[g][...])

    @pl.when(s_id == last)
    def _():
        o_ref[...] = jnp.zeros(o_ref.shape, F32)
        o_ref[:, :page] = score(new_ref[...])


def _paged_scores(page_table, iq_rows, iw_rows, cache_idx, ik_new, gp):
    bn, rows, _ = iq_rows.shape
    tp = rows // IDX_HEADS
    page = cache_idx.shape[1]
    npages = page_table.shape[1]
    nsteps = npages // gp + 1
    width = nsteps * gp * page

    def page_map(g):
        return lambda b, s, pt: (pt[b, jnp.minimum(s * gp + g, npages - 1)], 0, 0)

    return pl.pallas_call(
        functools.partial(_paged_scores_kernel, gp=gp, page=page),
        grid_spec=pltpu.PrefetchScalarGridSpec(
            num_scalar_prefetch=1,
            grid=(bn, nsteps),
            in_specs=[pl.BlockSpec((None, rows, IDX_DIM), lambda b, s, pt: (b, 0, 0)),
                      pl.BlockSpec((None, rows, 1), lambda b, s, pt: (b, 0, 0))]
                     + [pl.BlockSpec((None, page, IDX_DIM), page_map(g)) for g in range(gp)]
                     + [pl.BlockSpec((None, page, IDX_DIM), lambda b, s, pt: (b, 0, 0))],
            out_specs=pl.BlockSpec((None, tp, gp * page), lambda b, s, pt: (b, 0, s))),
        out_shape=jax.ShapeDtypeStruct((bn, tp, width), F32),
        compiler_params=_cparams(("arbitrary", "arbitrary")),
        name="paged_idx_scores",
    )(page_table, iq_rows, iw_rows, *([cache_idx] * gp), ik_new)


def _paged_attn_kernel(pt_ref, q_ref, am_ref, *refs, gp, page, nh, nm, dv, tp, lam_init):
    kpages = refs[:gp]
    vpages = refs[gp:2 * gp]
    knew_ref, vnew_ref = refs[2 * gp], refs[2 * gp + 1]
    pos = 2 * gp + 2
    lam_ref = subg_ref = None
    if nm == 2:
        lam_ref, subg_ref = refs[pos], refs[pos + 1]
        pos += 2
    o_ref, m_scr, l_scr, acc_scr = refs[pos:pos + 4]
    s_id = pl.program_id(1)
    last = pl.num_programs(1) - 1

    @pl.when(s_id == 0)
    def _():
        m_scr[...] = jnp.full(m_scr.shape, -jnp.inf, F32)
        l_scr[...] = jnp.zeros(l_scr.shape, F32)
        acc_scr[...] = jnp.zeros(acc_scr.shape, F32)

    def blocks(kvs):
        ss = [_dot_nt(q_ref[...], kp[...]) + am_ref[:, g * page:(g + 1) * page] for g, (kp, _) in enumerate(kvs)]
        mx = ss[0]
        for s in ss[1:]:
            mx = jnp.maximum(mx, s)
        m_prev = m_scr[...]
        m_new = jnp.maximum(m_prev, jnp.max(mx, axis=-1, keepdims=True))
        alpha = jnp.exp(m_prev - m_new)
        ps = [jnp.exp(s - m_new) for s in ss]
        psum = ps[0]
        for p in ps[1:]:
            psum = psum + p
        pv = [_dot(p, vp[...]) for p, (_, vp) in zip(ps, kvs)]
        tot = pv[0]
        for x in pv[1:]:
            tot = tot + x
        l_scr[...] = alpha * l_scr[...] + jnp.sum(psum, axis=-1, keepdims=True)
        acc_scr[...] = alpha * acc_scr[...] + tot
        m_scr[...] = m_new

    @pl.when(s_id < last)
    def _():
        blocks(list(zip(kpages, vpages)))

    @pl.when(s_id == last)
    def _():
        blocks([(knew_ref, vnew_ref)])
        accn = acc_scr[...] / l_scr[...]
        for h in range(nh):
            cols = slice(h * dv, (h + 1) * dv)
            if nm == 1:
                o_ref[:, cols] = accn[h * tp:(h + 1) * tp, cols]
            else:
                att = accn[2 * h * tp:(2 * h + 1) * tp, cols] - lam_ref[0] * accn[(2 * h + 1) * tp:(2 * h + 2) * tp, cols]
                att = att * lax.rsqrt(jnp.mean(att * att, axis=-1, keepdims=True) + EPS)
                o_ref[:, cols] = att * subg_ref[...] * (1.0 - lam_init)


def _paged_attn(page_table, q_rows, addmask, cache_k, cache_v, k_new, v_new, lam, subg, *, gp, nh, nm, dv, tp, lam_init):
    bn, rows, w = q_rows.shape
    page = cache_k.shape[1]
    npages = page_table.shape[1]
    nsteps = npages // gp + 1

    def page_map(g):
        return lambda b, s, pt: (pt[b, jnp.minimum(s * gp + g, npages - 1)], 0, 0)

    new_spec = pl.BlockSpec((None, page, w), lambda b, s, pt: (b, 0, 0))
    in_specs = ([pl.BlockSpec((None, rows, w), lambda b, s, pt: (b, 0, 0)),
                 pl.BlockSpec((None, rows, gp * page), lambda b, s, pt: (b, 0, s))]
                + [pl.BlockSpec((None, page, w), page_map(g)) for g in range(gp)] * 2
                + [new_spec, new_spec])
    args = [q_rows, addmask] + [cache_k] * gp + [cache_v] * gp + [k_new, v_new]
    if nm == 2:
        in_specs += [pl.BlockSpec(memory_space=pltpu.SMEM), pl.BlockSpec((1, dv), lambda b, s, pt: (0, 0))]
        args += [lam, subg]
    return pl.pallas_call(
        functools.partial(_paged_attn_kernel, gp=gp, page=page, nh=nh, nm=nm, dv=dv, tp=tp, lam_init=lam_init),
        grid_spec=pltpu.PrefetchScalarGridSpec(
            num_scalar_prefetch=1,
            grid=(bn, nsteps),
            in_specs=in_specs,
            out_specs=pl.BlockSpec((None, tp, nh * dv), lambda b, s, pt: (b, 0, 0)),
            scratch_shapes=[pltpu.VMEM((rows, 1), F32), pltpu.VMEM((rows, 1), F32), pltpu.VMEM((rows, w), F32)]),
        out_shape=jax.ShapeDtypeStruct((bn, tp, nh * dv), F32),
        compiler_params=_cparams(("arbitrary", "arbitrary")),
        name="paged_attn_dsa" if nm == 1 else "paged_attn_diff",
    )(page_table, *args)


def _gdn_kernel(xc_ref, prev_ref, buf_ref, cz_ref, tail_ref, s0_ref, cw_ref, alog_ref, dtb_ref, nw_ref,
                o_ref, sfin_ref, s_scr, *, n_valid):
    ci = pl.program_id(1)
    bg, c, ch = xc_ref.shape
    dk = C_DK
    dv = (ch - 2 * C_HEADS * dk) // C_HEADS

    @pl.when(ci == 0)
    def _():
        s_scr[...] = s0_ref[...]

    row = lax.broadcasted_iota(I32, (c, c), 0)
    colm = lax.broadcasted_iota(I32, (c, c), 1)
    incl = row >= colm
    tril = incl.astype(F32)
    eye = (row == colm).astype(F32)

    units = []
    for b in range(bg):
        hist = jnp.where(ci == 0, buf_ref[b], prev_ref[b])
        nh_rows = hist.shape[0]
        xfull = jnp.concatenate([hist, xc_ref[b]], axis=0)
        y = cw_ref[C_CONV - 1:C_CONV, :] * xfull[nh_rows:, :]
        for j in range(1, C_CONV):
            y = y + cw_ref[C_CONV - 1 - j:C_CONV - j, :] * pltpu.roll(xfull, j, 0)[nh_rows:, :]
        y = _silu(y)

        tail = tail_ref[b]
        beta_all = jax.nn.sigmoid(tail)
        sp_in = tail + dtb_ref[...]
        g_all = -jnp.exp(alog_ref[...]) * (jnp.maximum(sp_in, 0.0) + jnp.log1p(jnp.exp(-jnp.abs(sp_in))))
        if n_valid < c:
            valid = lax.broadcasted_iota(I32, tail.shape, 0) < n_valid
            beta_all = jnp.where(valid, beta_all, 0.0)
            g_all = jnp.where(valid, g_all, 0.0)
        gc_all = _dot_hi(tril, g_all)
        gc_rows = jnp.transpose(gc_all)

        for h in range(C_HEADS):
            q = y[:, h * dk:(h + 1) * dk]
            k = y[:, (C_HEADS + h) * dk:(C_HEADS + h + 1) * dk]
            v = y[:, 2 * C_HEADS * dk + h * dv:2 * C_HEADS * dk + (h + 1) * dv]
            q = q * lax.rsqrt(jnp.sum(q * q, axis=-1, keepdims=True) + EPS) * (dk ** -0.5)
            k = k * lax.rsqrt(jnp.sum(k * k, axis=-1, keepdims=True) + EPS)
            beta = beta_all[:, h:h + 1]
            gc = gc_all[:, C_HEADS + h:C_HEADS + h + 1]
            decay = jnp.exp(jnp.where(incl, gc - gc_rows[C_HEADS + h:C_HEADS + h + 1, :], -jnp.inf))
            units.append(dict(b=b, h=h, q=q, k=k, kb=k * beta, vb=v * beta, gc=gc, decay=decay))

    def stage(fn):
        return [fn(un) for un in units]

    def put(name, vals):
        for un, val in zip(units, vals):
            un[name] = val

    put('a', stage(lambda un: _dot_nt(un['kb'], un['k']) * un['decay'] * (1.0 - eye)))
    put('inv', stage(lambda un: eye - un['a']))
    put('pw', stage(lambda un: un['a']))
    for _ in range(max(0, int(c - 1).bit_length() - 1)):
        put('pw', stage(lambda un: _dot_x3(un['pw'], un['pw'])))
        put('inv', stage(lambda un: un['inv'] + _dot_x3(un['inv'], un['pw'])))
    put('u', stage(lambda un: _dot_x3(un['inv'], un['vb'])))
    put('w', stage(lambda un: _dot_x3(un['inv'], un['kb'] * jnp.exp(un['gc']))))
    put('s', stage(lambda un: s_scr[un['b'], un['h']]))
    put('v_new', stage(lambda un: un['u'] - _dot(un['w'], un['s'])))
    put('intra', stage(lambda un: _dot_nt(un['q'], un['k']) * un['decay']))
    put('o', stage(lambda un: _dot(un['q'] * jnp.exp(un['gc']), un['s']) + _dot(un['intra'], un['v_new'])))
    for un in units:
        gl = un['gc'][c - 1:c, :]
        s_scr[un['b'], un['h']] = un['s'] * jnp.exp(gl) + _dot_tn(un['k'] * jnp.exp(gl - un['gc']), un['v_new'])
    for un in units:
        b, h, o = un['b'], un['h'], un['o']
        o = o * lax.rsqrt(jnp.mean(o * o, axis=-1, keepdims=True) + EPS) * nw_ref[...]
        o_ref[b, :, h * dv:(h + 1) * dv] = (o * _silu(cz_ref[b, :, h * dv:(h + 1) * dv])).astype(o_ref.dtype)

    @pl.when(ci == pl.num_programs(1) - 1)
    def _():
        sfin_ref[...] = s_scr[...]


def _gdn(xc, buf, cz, tail, s0, conv_w, a_log, dt_bias, norm_w, chunk, n_valid):
    bn, t, ch = xc.shape
    hw = cz.shape[2]
    nck = t // chunk
    dvh = hw // C_HEADS
    hist = SUBLANES
    per_chunk = chunk // hist
    bg = 2 if bn % 2 == 0 else 1
    gate_lanes = lambda p: jnp.pad(p.astype(F32), (C_HEADS, LANES - 2 * C_HEADS)).reshape(1, LANES)
    return pl.pallas_call(
        functools.partial(_gdn_kernel, n_valid=n_valid),
        grid=(bn // bg, nck),
        in_specs=[pl.BlockSpec((bg, chunk, ch), lambda b, i: (b, i, 0)),
                  pl.BlockSpec((bg, hist, ch), lambda b, i: (b, jnp.maximum(i * per_chunk - 1, 0), 0)),
                  pl.BlockSpec((bg, hist, ch), lambda b, i: (b, 0, 0)),
                  pl.BlockSpec((bg, chunk, hw), lambda b, i: (b, i, 0)),
                  pl.BlockSpec((bg, chunk, LANES), lambda b, i: (b, i, 0)),
                  pl.BlockSpec((bg, C_HEADS, C_DK, dvh), lambda b, i: (b, 0, 0, 0)),
                  pl.BlockSpec((C_CONV, ch), lambda b, i: (0, 0)),
                  pl.BlockSpec((1, LANES), lambda b, i: (0, 0)),
                  pl.BlockSpec((1, LANES), lambda b, i: (0, 0)),
                  pl.BlockSpec((1, dvh), lambda b, i: (0, 0))],
        out_specs=[pl.BlockSpec((bg, chunk, hw), lambda b, i: (b, i, 0)),
                   pl.BlockSpec((bg, C_HEADS, C_DK, dvh), lambda b, i: (b, 0, 0, 0))],
        out_shape=[jax.ShapeDtypeStruct((bn, t, hw), BF16), jax.ShapeDtypeStruct((bn, C_HEADS, C_DK, dvh), F32)],
        scratch_shapes=[pltpu.VMEM((bg, C_HEADS, C_DK, dvh), F32)],
        compiler_params=_cparams(("arbitrary", "arbitrary")),
        name="gated_delta",
    )(xc, xc, buf, cz, tail, s0, conv_w, gate_lanes(a_log), gate_lanes(dt_bias), norm_w.reshape(1, dvh))


def _moe_kernel(be_ref, x_ref, gate_ref, w1_ref, b1_ref, w2_ref, b2_ref, o_ref, w1_scr, w2_scr):
    i = pl.program_id(0)
    dff = w2_ref.shape[0]
    changed = jnp.logical_or(i == 0, be_ref[i] != be_ref[jnp.maximum(i - 1, 0)])

    @pl.when(changed)
    def _():
        w1_scr[...] = w1_ref[...].astype(BF16)
        w2_scr[...] = w2_ref[...].astype(BF16)

    hmid = jnp.dot(x_ref[...].astype(BF16), w1_scr[...], preferred_element_type=F32) + b1_ref[...]
    gate = jnp.minimum(hmid[:, :dff], SWIGLU_LIMIT)
    up = jnp.clip(hmid[:, dff:], -SWIGLU_LIMIT, SWIGLU_LIMIT)
    act = (up + 1.0) * gate * jax.nn.sigmoid(SWIGLU_ALPHA * gate)
    y = jnp.dot(act.astype(BF16), w2_scr[...], preferred_element_type=F32) + b2_ref[...]
    o_ref[...] = y * gate_ref[...]


def _moe_ffn(xs, slot_gate, block_e, layer, w1, b1, w2, b2, bm):
    n_slots, d = xs.shape
    _, ne, _, d2 = w1.shape
    dff = w2.shape[2]
    return pl.pallas_call(
        _moe_kernel,
        grid_spec=pltpu.PrefetchScalarGridSpec(
            num_scalar_prefetch=1,
            grid=(n_slots // bm,),
            in_specs=[pl.BlockSpec((bm, d), lambda i, be: (i, 0)),
                      pl.BlockSpec((bm, 1), lambda i, be: (i, 0)),
                      pl.BlockSpec((None, None, d, d2), lambda i, be: (layer, be[i], 0, 0)),
                      pl.BlockSpec((None, None, 1, d2), lambda i, be: (layer, be[i], 0, 0)),
                      pl.BlockSpec((None, None, dff, d), lambda i, be: (layer, be[i], 0, 0)),
                      pl.BlockSpec((None, None, 1, d), lambda i, be: (layer, be[i], 0, 0))],
            out_specs=pl.BlockSpec((bm, d), lambda i, be: (i, 0)),
            scratch_shapes=[pltpu.VMEM((d, d2), BF16), pltpu.VMEM((dff, d), BF16)]),
        out_shape=jax.ShapeDtypeStruct((n_slots, d), F32),
        compiler_params=_cparams(("arbitrary",)),
        name="moe_ffn",
    )(block_e, xs, slot_gate.reshape(n_slots, 1), w1, b1.reshape(b1.shape[0], ne, 1, d2), w2, b2.reshape(b2.shape[0], ne, 1, d))


def _route_kernel(lg_ref, e_ref, g_ref, r_ref, cnt_ref, run_scr, *, ne):
    i = pl.program_id(0)
    tm = lg_ref.shape[0]

    @pl.when(i == 0)
    def _():
        run_scr[...] = jnp.zeros(run_scr.shape, F32)

    lane = lax.broadcasted_iota(I32, (tm, LANES), 1)
    lane_f = lane.astype(F32)
    x = jnp.where(lane < ne, lg_ref[...], -jnp.inf)
    vals, hots = [], []
    e_out = jnp.zeros((tm, LANES), F32)
    for j in range(TOP_K):
        m = jnp.max(x, axis=-1, keepdims=True)
        idx = jnp.min(jnp.where(x == m, lane_f, float(LANES)), axis=-1, keepdims=True)
        hot = lane_f == idx
        vals.append(m)
        hots.append(hot)
        e_out = jnp.where(lane == j, idx, e_out)
        x = jnp.where(hot, -jnp.inf, x)
    ex = [jnp.exp(v - vals[0]) for v in vals]
    denom = ex[0]
    for j in range(1, TOP_K):
        denom = denom + ex[j]
    chosen = jnp.where(hots[0], 1.0, 0.0)
    for j in range(1, TOP_K):
        chosen = chosen + jnp.where(hots[j], 1.0, 0.0)
    before = (lax.broadcasted_iota(I32, (tm, tm), 0) > lax.broadcasted_iota(I32, (tm, tm), 1)).astype(BF16)
    base = run_scr[...] + jnp.dot(before, chosen.astype(BF16), preferred_element_type=F32)
    g_out = jnp.zeros((tm, LANES), F32)
    r_out = jnp.zeros((tm, LANES), F32)
    for j in range(TOP_K):
        g_out = jnp.where(lane == j, ex[j] / denom, g_out)
        r_out = jnp.where(lane == j, jnp.sum(jnp.where(hots[j], base, 0.0), axis=-1, keepdims=True), r_out)
    e_ref[...] = e_out[:, :TOP_K].astype(I32)
    g_ref[...] = g_out[:, :TOP_K]
    r_ref[...] = r_out[:, :TOP_K].astype(I32)
    run_scr[...] = run_scr[...] + jnp.sum(chosen, axis=0, keepdims=True)

    @pl.when(i == pl.num_programs(0) - 1)
    def _():
        cnt_ref[...] = run_scr[...]


def _route(logits, ne):
    n = logits.shape[0]
    tm = next(t for t in (512, 384, 256, 128, 64, 32, 16, 8) if n % t == 0)
    small = lambda dt: jax.ShapeDtypeStruct((n, TOP_K), dt)
    return pl.pallas_call(
        functools.partial(_route_kernel, ne=ne),
        grid=(n // tm,),
        in_specs=[pl.BlockSpec((tm, LANES), lambda i: (i, 0))],
        out_specs=[pl.BlockSpec((tm, TOP_K), lambda i: (i, 0))] * 3 + [pl.BlockSpec((1, LANES), lambda i: (0, 0))],
        out_shape=[small(I32), small(F32), small(I32), jax.ShapeDtypeStruct((1, LANES), F32)],
        scratch_shapes=[pltpu.VMEM((1, LANES), F32)],
        compiler_params=_cparams(("arbitrary",)),
        name="moe_route",
    )(logits)


def _moe(h2, logits, layer, w1, b1, w2, b2, bm):
    n_tok = h2.shape[0]
    ne = w1.shape[1]
    top_idx, gates, rank, counts = _route(logits, ne)
    counts = counts[0, :ne].astype(I32)
    padded = (counts + bm - 1) // bm * bm
    pad_end = jnp.cumsum(padded)
    pad_start = pad_end - padded
    n_assign = n_tok * TOP_K
    n_blocks = -(-(n_assign + ne * (bm - 1)) // bm)
    n_slots = n_blocks * bm
    hot = top_idx[..., None] == jnp.arange(ne, dtype=I32)
    dest = (jnp.sum(jnp.where(hot, pad_start, 0), axis=-1) + rank).reshape(-1)
    block_e = jnp.minimum(jnp.sum(pad_end[None, :] <= (jnp.arange(n_blocks, dtype=I32) * bm)[:, None], axis=1),
                          ne - 1).astype(I32)
    payload = jnp.stack([jnp.repeat(jnp.arange(n_tok, dtype=I32), TOP_K),
                         lax.bitcast_convert_type(gates.reshape(-1), I32)], axis=-1)
    slots = jnp.zeros((n_slots, 2), I32).at[dest].set(payload)
    slot_gate = lax.bitcast_convert_type(slots[:, 1], F32)
    ys = _moe_ffn(h2[slots[:, 0]], slot_gate, block_e, layer, w1, b1, w2, b2, bm)
    return jnp.sum(ys[dest.reshape(n_tok, TOP_K).T], axis=0)


def _rel_bias_sample(rel_bias, past_len, t):
    dist = (past_len + np.arange(t))[:, None] - np.arange(past_len + t)[None, :]
    return _bias_by_bucket(rel_bias, _bucket_of(dist)), dist >= 0


def _head_rows(x, tp):
    bn, t, g, w = x.shape
    xp = jnp.pad(x, ((0, 0), (0, tp - t), (0, 0), (0, 0)))
    eye = jnp.eye(g, dtype=x.dtype)
    return jnp.einsum('btgw,gk->bgtkw', xp, eye).reshape(bn, g * tp, g * w)


def kernel(x_prompt, x_sample, cache_b_k, cache_b_v, cache_b_idx, state_c_rec, state_c_conv, cache_d_k, cache_d_v,
           page_table, c_prompt, c_sample, rel_bias, ada_w, ada_b, even_w_in, even_w_out, a_ln_g, a_ln_b, a_w_sp,
           a_b_sp, b_q_norm, b_k_norm, b_idx_norm, odd_w_in, odd_w_out, c_conv_w, c_a_log, c_dt_bias, c_norm_w,
           d_q_norm, d_k_norm, d_lambda, d_subln, router_w, router_b, moe_w1, moe_b1, moe_w2, moe_b2):
    bp, sp, d = x_prompt.shape
    bs, ts, _ = x_sample.shape
    depth = ada_w.shape[0]
    hw = d // 2
    page = cache_b_k.shape[2]
    past_len = page_table.shape[1] * page
    n_p, n_s = bp * sp, bs * ts
    tm_p = 256
    tq = 256
    tp = SUBLANES
    gp = 8 if page_table.shape[1] % 8 == 0 else 1
    assert sp % tm_p == 0 and n_s % SUBLANES == 0 and ts <= tp and ts >= C_CONV - 1

    n_c = bp + bs
    c_all = jnp.pad(jnp.concatenate([c_prompt, c_sample], axis=0), ((0, -n_c % SUBLANES), (0, 0)))
    mod_all = _ada_mod(c_all, ada_w, ada_b)

    def mods(layer):
        m = mod_all[layer].reshape(-1, 6, d)
        mp = [m[:bp, i][:, None, :] for i in range(6)]
        ms = [jnp.repeat(m[bp:n_c, i], ts, axis=0) for i in range(6)]
        return mp, ms

    xp = x_prompt.reshape(n_p, d)
    xs = x_sample.reshape(n_s, d)
    tpb = sp // tm_p
    tbl = _bias_tables(rel_bias, tq)
    outs_p, outs_s = {}, {}

    for layer in range(depth):
        li = layer // 2
        mp, ms = mods(layer)
        if layer % 2 == 0:
            ws_p = jnp.where(np.tril(np.ones((A_CHUNK, A_CHUNK), bool)), a_w_sp[li], 0.0).astype(BF16)
            bsp_p = a_b_sp[li].T
            (aout, _, q, k32, kbf, v32, vbf, iq, ik32, ikbf, iw) = _even_in(
                xp, mp[0], mp[1], False, tpb, tm_p, even_w_in[li], a_ln_g[li], a_ln_b[li], ws_p, bsp_p,
                b_q_norm[li], b_k_norm[li], b_idx_norm[li])
            ksel = min(TOPK_MAX, sp // 4)
            kc = 512 if sp % 512 == 0 else sp
            scores = _idx_scores_prompt(iq.reshape(bp, sp, hw), iw.reshape(bp, sp, IDX_HEADS),
                                        ikbf.reshape(bp, sp, IDX_DIM), tq, kc)
            lim = jnp.tile(jnp.arange(sp, dtype=I32), bp).reshape(n_p, 1)
            nch = jnp.tile((jnp.arange(sp // 128, dtype=I32) * 128 + 127) // kc + 1, bp)
            madd = _select_mask(scores.reshape(n_p, sp), lim, nch, ksel, kc, BF16).reshape(bp, sp, sp)
            b_out = _prompt_attn(q.reshape(bp, sp, hw), kbf.reshape(bp, sp, hw), vbf.reshape(bp, sp, -1), tbl, madd,
                                 None, None, nh=B_HEADS, nm=1, dqk=hw // B_HEADS, dv=hw // B_HEADS, tq=tq, lam_init=0.0)
            outs_p.setdefault('b_k', []).append(k32.reshape(bp, sp, B_HEADS, -1))
            outs_p.setdefault('b_v', []).append(v32.reshape(bp, sp, B_HEADS, -1))
            outs_p.setdefault('b_idx', []).append(ik32.reshape(bp, sp, IDX_DIM))
            mix_p = (aout, b_out.reshape(n_p, hw))

            cs = min(ts, A_CHUNK)
            ws_s = jnp.where(np.tril(np.ones((cs, cs), bool)), a_w_sp[li][:, :cs, :cs], 0.0)
            ws_s = jnp.einsum('ab,gts->gatbs', jnp.eye(n_s // cs, dtype=F32), ws_s).reshape(A_GROUPS, n_s, n_s).astype(BF16)
            bsp_s = jnp.tile(a_b_sp[li][:, :cs].T, (n_s // cs, 1))
            (aout, av, q, k32, kbf, v32, vbf, iq, ik32, ikbf, iw) = _even_in(
                xs, ms[0], ms[1], True, 1, n_s, even_w_in[li], a_ln_g[li], a_ln_b[li], ws_s, bsp_s,
                b_q_norm[li], b_k_norm[li], b_idx_norm[li])
            ltot = past_len + ts
            ksel = min(TOPK_MAX, ltot // 4)
            iq_rows = jnp.pad(iq.reshape(bs, ts, IDX_HEADS, IDX_DIM), ((0, 0), (0, tp - ts), (0, 0), (0, 0)))
            iq_rows = jnp.transpose(iq_rows, (0, 2, 1, 3)).reshape(bs, IDX_HEADS * tp, IDX_DIM)
            iw_rows = jnp.pad(iw.reshape(bs, ts, IDX_HEADS), ((0, 0), (0, tp - ts), (0, 0)))
            iw_rows = jnp.transpose(iw_rows, (0, 2, 1)).reshape(bs, IDX_HEADS * tp, 1)
            pad_new = lambda a: jnp.pad(a.reshape(bs, ts, -1), ((0, 0), (0, page - ts), (0, 0)))
            sc_s = _paged_scores(page_table, iq_rows, iw_rows, cache_b_idx[li], pad_new(ikbf), gp)
            width = sc_s.shape[2]
            kc_s = gp * page
            lim_s = jnp.where(np.arange(tp) < ts, past_len + np.arange(tp), -1).astype(I32)
            lim_s = jnp.tile(lim_s, bs).reshape(bs * tp, 1)
            nch_s = jnp.full((bs * tp // 128,), width // kc_s, I32)
            sel_s = _select_mask(sc_s.reshape(bs * tp, width), lim_s, nch_s, ksel, kc_s, F32).reshape(bs, tp, width)
            bias_s, _ = _rel_bias_sample(rel_bias, past_len, ts)
            bias_s = jnp.pad(bias_s, ((0, 0), (0, tp - ts), (0, width - ltot)))
            am = (sel_s[:, None] + bias_s[None]).reshape(bs, B_HEADS * tp, width)
            q_rows = _head_rows(q.reshape(bs, ts, B_HEADS, -1), tp)
            cshape = (cache_b_k.shape[1], page, hw)
            flat = lambda cache: cache[li].reshape(cshape)
            b_out = _paged_attn(page_table, q_rows, am, flat(cache_b_k), flat(cache_b_v),
                                pad_new(kbf), pad_new(v32), None, None, gp=gp, nh=B_HEADS, nm=1, dv=hw // B_HEADS, tp=tp,
                                lam_init=0.0)
            outs_s.setdefault('a_v', []).append(av.reshape(bs, ts, hw))
            outs_s.setdefault('b_k', []).append(k32.reshape(bs, ts, B_HEADS, -1))
            outs_s.setdefault('b_v', []).append(v32.reshape(bs, ts, B_HEADS, -1))
            outs_s.setdefault('b_idx', []).append(ik32.reshape(bs, ts, IDX_DIM))
            mix_s = (aout, b_out[:, :ts].reshape(n_s, hw).astype(BF16))
            w_out = even_w_out[li]
        else:
            lam_init = 0.8 - 0.6 * math.exp(-0.3 * layer)
            lp = d_lambda[li].astype(F32)
            lam = (jnp.exp(jnp.sum(lp[0] * lp[1])) - jnp.exp(jnp.sum(lp[2] * lp[3])) + lam_init).reshape(1)
            subg = d_subln[li].reshape(1, -1)
            dvh = hw // D_HEADS
            xc, cz, tail, q2, k2_32, k2bf, v2_32, v2bf = _odd_in(xp, mp[0], mp[1], False, tpb, tm_p, odd_w_in[li],
                                                                d_q_norm[li], d_k_norm[li])
            ch = xc.shape[1]
            xc3 = xc.reshape(bp, sp, ch)
            zbuf = jnp.zeros((bp, SUBLANES, ch), F32)
            s0 = jnp.zeros((bp, C_HEADS, C_DK, hw // C_HEADS), F32)
            chunk = C_CHUNK if sp % C_CHUNK == 0 else sp
            c_out, c_rec = _gdn(xc3, zbuf, cz.reshape(bp, sp, hw), tail.reshape(bp, sp, LANES), s0, c_conv_w[li],
                                c_a_log[li], c_dt_bias[li], c_norm_w[li], chunk, chunk)
            d_out = _prompt_attn(q2.reshape(bp, sp, hw), k2bf.reshape(bp, sp, hw), v2bf.reshape(bp, sp, -1), tbl, None,
                                 lam, subg, nh=D_HEADS, nm=2, dqk=D_QK_DIM, dv=dvh, tq=tq, lam_init=lam_init)
            outs_p.setdefault('c_rec', []).append(c_rec)
            outs_p.setdefault('c_conv', []).append(xc3[:, sp - (C_CONV - 1):])
            outs_p.setdefault('d_k', []).append(k2_32.reshape(bp, sp, D_HEADS, -1))
            outs_p.setdefault('d_v', []).append(v2_32.reshape(bp, sp, D_HEADS, -1))
            mix_p = (c_out.reshape(n_p, hw), d_out.reshape(n_p, hw))

            xc, cz, tail, q2, k2_32, k2bf, v2_32, v2bf = _odd_in(xs, ms[0], ms[1], True, 1, n_s, odd_w_in[li],
                                                                d_q_norm[li], d_k_norm[li])
            pad_t = lambda a: jnp.pad(a.reshape(bs, ts, -1), ((0, 0), (0, C_CHUNK - ts), (0, 0)))
            buf = state_c_conv[li]
            buf8 = jnp.pad(buf, ((0, 0), (SUBLANES - (C_CONV - 1), 0), (0, 0)))
            c_out, c_rec = _gdn(pad_t(xc), buf8, pad_t(cz), pad_t(tail), state_c_rec[li], c_conv_w[li], c_a_log[li],
                                c_dt_bias[li], c_norm_w[li], C_CHUNK, ts)
            ltot = past_len + ts
            bias_s, allowed = _rel_bias_sample(rel_bias, past_len, ts)
            width = (page_table.shape[1] // gp + 1) * gp * page
            am = jnp.where(jnp.asarray(allowed)[None], bias_s, NEG)
            am = jnp.pad(am, ((0, 0), (0, tp - ts), (0, 0)))
            am = jnp.pad(am, ((0, 0), (0, 0), (0, width - ltot)), constant_values=NEG)
            am = jnp.broadcast_to(am[None, :, None], (bs, D_HEADS, 2, tp, width)).reshape(bs, D_HEADS * 2 * tp, width)
            q_rows = _head_rows(q2.reshape(bs, ts, 2 * D_HEADS, D_QK_DIM), tp)
            pad_new = lambda a: jnp.pad(a.reshape(bs, ts, -1), ((0, 0), (0, page - ts), (0, 0)))
            cshape = (cache_d_k.shape[1], page, hw)
            flat = lambda cache: cache[li].reshape(cshape)
            d_out = _paged_attn(page_table, q_rows, am, flat(cache_d_k), flat(cache_d_v),
                                pad_new(k2bf), pad_new(v2_32), lam, subg, gp=gp, nh=D_HEADS, nm=2, dv=dvh, tp=tp,
                                lam_init=lam_init)
            outs_s.setdefault('c_rec', []).append(c_rec)
            outs_s.setdefault('c_conv', []).append(
                jnp.concatenate([buf, xc.reshape(bs, ts, ch)], axis=1)[:, ts:])
            outs_s.setdefault('d_k', []).append(k2_32.reshape(bs, ts, D_HEADS, -1))
            outs_s.setdefault('d_v', []).append(v2_32.reshape(bs, ts, D_HEADS, -1))
            mix_s = (c_out[:, :ts].reshape(n_s, hw), d_out[:, :ts].reshape(n_s, hw).astype(BF16))
            w_out = odd_w_out[li]

        x1p, h2p, lgp = _out_proj(mix_p[0], mix_p[1], xp, mp[2], mp[3], mp[4], False, tpb, tm_p, w_out,
                                  router_w[layer], router_b[layer])
        x1s, h2s, lgs = _out_proj(mix_s[0], mix_s[1], xs, ms[2], ms[3], ms[4], True, 1, n_s, w_out,
                                  router_w[layer], router_b[layer])
        moe_out = _moe(jnp.concatenate([h2p, h2s], axis=0), jnp.concatenate([lgp, lgs], axis=0), layer,
                       moe_w1, moe_b1, moe_w2, moe_b2, 256)
        xp = (x1p.reshape(bp, sp, d) + mp[5] * moe_out[:n_p].reshape(bp, sp, d)).reshape(n_p, d)
        xs = x1s + ms[5] * moe_out[n_p:]

    st = lambda name, src: jnp.stack(src[name])
    return (xp.reshape(bp, sp, d), xs.reshape(bs, ts, d),
            st('b_k', outs_p), st('b_v', outs_p), st('b_idx', outs_p), st('c_rec', outs_p), st('c_conv', outs_p),
            st('d_k', outs_p), st('d_v', outs_p),
            st('a_v', outs_s), st('b_k', outs_s), st('b_v', outs_s), st('b_idx', outs_s), st('c_rec', outs_s),
            st('c_conv', outs_s), st('d_k', outs_s), st('d_v', outs_s))
```

```python
import functools
import math

import numpy as np
import jax
import jax.numpy as jnp
from jax import lax
from jax.experimental import pallas as pl
from jax.experimental.pallas import tpu as pltpu

F32 = jnp.float32
BF16 = jnp.bfloat16
I32 = jnp.int32
HIGHEST = lax.Precision.HIGHEST

A_GROUPS = 4
A_CHUNK = 128
B_HEADS = 8
IDX_HEADS = 8
IDX_DIM = 64
TOPK_MAX = 256
C_HEADS = 4
C_DK = 128
C_CONV = 4
C_CHUNK = 64
D_HEADS = 8
D_QK_DIM = 32
N_BUCKETS = 32
MAX_DISTANCE = 128
TOP_K = 4
SWIGLU_LIMIT = 7.0
SWIGLU_ALPHA = 1.702
EPS = 1e-6

LANES = 128
SUBLANES = 8
VMEM_LIMIT_BYTES = 56 * 1024 * 1024

NEG = -1e30
INT_MIN = -2 ** 31


def _cparams(sem):
    return pltpu.CompilerParams(dimension_semantics=sem, vmem_limit_bytes=VMEM_LIMIT_BYTES)


def _dot(a, b):
    return jnp.dot(a.astype(BF16), b.astype(BF16), preferred_element_type=F32)


def _dot_nt(a, b):
    return lax.dot_general(a.astype(BF16), b.astype(BF16), (((1,), (1,)), ((), ())), preferred_element_type=F32)


def _dot_hi(a, b):
    return jnp.dot(a, b, preferred_element_type=F32, precision=HIGHEST)


def _dot_tn(a, b):
    return lax.dot_general(a.astype(BF16), b.astype(BF16), (((0,), (0,)), ((), ())), preferred_element_type=F32)


def _dot_x3(a, b):
    ah = a.astype(BF16)
    al = (a - ah.astype(F32)).astype(BF16)
    bh = b.astype(BF16)
    bl = (b - bh.astype(F32)).astype(BF16)
    dot = lambda x, y: jnp.dot(x, y, preferred_element_type=F32)
    return dot(ah, bh) + (dot(ah, bl) + dot(al, bh))


def _silu(x):
    return x * jax.nn.sigmoid(x)


def _gelu_tanh(x):
    return 0.5 * x * (1.0 + jnp.tanh(math.sqrt(2.0 / math.pi) * (x + 0.044715 * (x * x * x))))


def _group_rms(x, ones_ref, gsize):
    xx = x * x
    hi = xx.astype(BF16)
    lo = (xx - hi.astype(F32)).astype(BF16)
    e = ones_ref[...]
    ss = jnp.dot(hi, e, preferred_element_type=F32) + jnp.dot(lo, e, preferred_element_type=F32)
    return x * lax.rsqrt(ss * (1.0 / gsize) + EPS)


def _store_v_with_ones(vx_ref, v):
    tm, w = v.shape
    nh = vx_ref.shape[1] // LANES
    dv = w // nh
    tail = jnp.where(lax.broadcasted_iota(I32, (tm, LANES - dv), 1) == 0, 1.0, 0.0)
    for h in range(nh):
        vx_ref[:, h * LANES:(h + 1) * LANES] = jnp.concatenate([v[:, h * dv:(h + 1) * dv], tail], axis=1).astype(BF16)


def _block_ones(width, gsize):
    g = np.arange(width) // gsize
    return jnp.asarray(g[:, None] == g[None, :], BF16)


def _bucket_table():
    n = np.arange(MAX_DISTANCE)
    exact = N_BUCKETS // 2
    scaled = np.log(np.maximum(n, 1).astype(np.float32) / np.float32(exact)) / np.float32(math.log(MAX_DISTANCE / exact))
    large = np.minimum(exact + (scaled.astype(np.float32) * (N_BUCKETS - exact)).astype(np.int32), N_BUCKETS - 1)
    return np.where(n < exact, n, large).astype(np.int32)


def _bucket_of(dist):
    n = np.maximum(dist, 0)
    return np.where(n < MAX_DISTANCE, _bucket_table()[np.minimum(n, MAX_DISTANCE - 1)], N_BUCKETS - 1)


def _bias_by_bucket(rel_bias, bucket):
    out = jnp.zeros((rel_bias.shape[1],) + bucket.shape, F32)
    for b in np.unique(bucket):
        out = jnp.where(jnp.asarray(bucket == b)[None], rel_bias[b].astype(F32).reshape((-1,) + (1,) * bucket.ndim), out)
    return out


def _ada_kernel(c_ref, w_ref, b_ref, o_ref):
    o_ref[...] = _dot(_silu(c_ref[...]), w_ref[...]) + b_ref[...]


def _ada_mod(c_all, ada_w, ada_b):
    depth, d, n6 = ada_w.shape
    rows = c_all.shape[0]
    tn = 1536 if n6 % 1536 == 0 else n6
    return pl.pallas_call(
        _ada_kernel,
        grid=(depth, n6 // tn),
        in_specs=[pl.BlockSpec((rows, d), lambda l, j: (0, 0)),
                  pl.BlockSpec((None, d, tn), lambda l, j: (l, 0, j)),
                  pl.BlockSpec((None, 1, tn), lambda l, j: (l, 0, j))],
        out_specs=pl.BlockSpec((None, rows, tn), lambda l, j: (l, 0, j)),
        out_shape=jax.ShapeDtypeStruct((depth, rows, n6), F32),
        compiler_params=_cparams(("arbitrary", "arbitrary")),
        name="ada_mod",
    )(c_all, ada_w, ada_b.reshape(depth, 1, n6))


def _row_spec(tm, w):
    return pl.BlockSpec((tm, w), lambda i: (i, 0))


def _row_shape(n, w, dt):
    return jax.ShapeDtypeStruct((n, w), dt)


def _mod_specs(per_token, tm, d, tiles_per_batch):
    if per_token:
        return pl.BlockSpec((tm, d), lambda i: (i, 0))
    return pl.BlockSpec((None, 1, d), lambda i: (i // tiles_per_batch, 0, 0))


def _modulated_rms(x, sh, sc):
    xn = x * lax.rsqrt(jnp.mean(x * x, axis=-1, keepdims=True) + EPS)
    return xn * (1.0 + sc) + sh


def _even_in_kernel(x_ref, sh_ref, sc_ref, wm_ref, wt_ref, lng_ref, lnb_ref, msp_ref, bsp_ref,
                    qg_ref, kg_ref, ikg_ref, e_ref,
                    aout_ref, av_ref, q_ref, k32_ref, kbf_ref, v32_ref, vx_ref, iq_ref, ik32_ref, ikbf_ref, iw_ref):
    tm = x_ref.shape[0]
    aw = aout_ref.shape[1]
    gd = aw // A_GROUPS
    h = _modulated_rms(x_ref[...], sh_ref[...], sc_ref[...]).astype(BF16)

    def seg(i):
        return jnp.dot(h, wm_ref[:, i * aw:(i + 1) * aw], preferred_element_type=F32)

    au = _gelu_tanh(seg(0))
    av = _gelu_tanh(seg(1))
    avc = av - jnp.mean(av, axis=-1, keepdims=True)
    vn = avc * lax.rsqrt(jnp.mean(avc * avc, axis=-1, keepdims=True) + EPS) * lng_ref[...] + lnb_ref[...]
    av_ref[...] = vn
    vnb = vn.astype(BF16)
    cr = msp_ref.shape[1]
    for c in range(tm // cr):
        r0 = c * cr
        for g in range(A_GROUPS):
            mixed = jnp.dot(msp_ref[g], vnb[r0:r0 + cr, g * gd:(g + 1) * gd], preferred_element_type=F32)
            mixed = mixed + bsp_ref[:, g:g + 1]
            aout_ref[r0:r0 + cr, g * gd:(g + 1) * gd] = (au[r0:r0 + cr, g * gd:(g + 1) * gd] * mixed).astype(BF16)

    hd = aw // B_HEADS
    q = _group_rms(seg(2), e_ref, hd) * qg_ref[...]
    q_ref[...] = (q * (hd ** -0.5)).astype(BF16)
    k = _group_rms(seg(3), e_ref, hd) * kg_ref[...]
    k32_ref[...] = k
    kbf_ref[...] = k.astype(BF16)
    v = seg(4)
    v32_ref[...] = v
    _store_v_with_ones(vx_ref, v)
    iq_ref[...] = (seg(5) * (IDX_DIM ** -0.5)).astype(BF16)
    tail = jnp.dot(h, wt_ref[...], preferred_element_type=F32)
    ik = tail[:, :IDX_DIM]
    ik = ik * lax.rsqrt(jnp.mean(ik * ik, axis=-1, keepdims=True) + EPS) * ikg_ref[...]
    ik32_ref[...] = ik
    ikbf_ref[...] = ik.astype(BF16)
    iw_ref[...] = tail[:, IDX_DIM:IDX_DIM + IDX_HEADS] * (IDX_HEADS ** -0.5)


def _even_in(x, sh, sc, per_token, tiles_per_batch, tm, w_in, ln_g, ln_b, msp, bsp, q_g, k_g, ik_g):
    n, d = x.shape
    aw = d // 2
    wm = w_in[:, :6 * aw].astype(BF16)
    wt = jnp.pad(w_in[:, 6 * aw:], ((0, 0), (0, LANES - (IDX_DIM + IDX_HEADS)))).astype(BF16)
    full = lambda shape: pl.BlockSpec(shape, lambda i: (0,) * len(shape))
    row = lambda w: pl.BlockSpec((tm, w), lambda i: (i, 0))
    mod = _mod_specs(per_token, tm, d, tiles_per_batch)
    outs = [(aw, BF16), (aw, F32), (aw, BF16), (aw, F32), (aw, BF16), (aw, F32), (B_HEADS * LANES, BF16),
            (aw, BF16), (IDX_DIM, F32), (IDX_DIM, BF16), (IDX_HEADS, F32)]
    return pl.pallas_call(
        _even_in_kernel,
        grid=(n // tm,),
        in_specs=[row(d), mod, mod, full(wm.shape), full(wt.shape), full((1, aw)), full((1, aw)),
                  full(msp.shape), full(bsp.shape), full((1, aw)), full((1, aw)), full((1, IDX_DIM)), full((aw, aw))],
        out_specs=[_row_spec(tm, w) for w, _ in outs],
        out_shape=[_row_shape(n, w, dt) for w, dt in outs],
        compiler_params=_cparams(("arbitrary",)),
        name="even_in",
    )(x, sh, sc, wm, wt, ln_g.reshape(1, aw), ln_b.reshape(1, aw), msp, bsp,
      jnp.tile(q_g, B_HEADS).reshape(1, aw), jnp.tile(k_g, B_HEADS).reshape(1, aw), ik_g.reshape(1, IDX_DIM),
      _block_ones(aw, aw // B_HEADS))


def _odd_in_kernel(x_ref, sh_ref, sc_ref, wm_ref, wt_ref, qg_ref, kg_ref, e_ref,
                   xc_ref, cz_ref, tail_ref, q_ref, k32_ref, kbf_ref, v32_ref, vx_ref):
    hw = cz_ref.shape[1]
    h = _modulated_rms(x_ref[...], sh_ref[...], sc_ref[...]).astype(BF16)

    def seg(i):
        return jnp.dot(h, wm_ref[:, i * hw:(i + 1) * hw], preferred_element_type=F32)

    for i in range(3):
        xc_ref[:, i * hw:(i + 1) * hw] = seg(i)
    cz_ref[...] = seg(3)
    tail_ref[...] = jnp.dot(h, wt_ref[...], preferred_element_type=F32)
    q = _group_rms(seg(4), e_ref, D_QK_DIM) * qg_ref[...]
    q_ref[...] = (q * (D_QK_DIM ** -0.5)).astype(BF16)
    k = _group_rms(seg(5), e_ref, D_QK_DIM) * kg_ref[...]
    k32_ref[...] = k
    kbf_ref[...] = k.astype(BF16)
    v = seg(6)
    v32_ref[...] = v
    _store_v_with_ones(vx_ref, v)


def _odd_in(x, sh, sc, per_token, tiles_per_batch, tm, w_in, q_g, k_g):
    n, d = x.shape
    hw = d // 2
    ng = 2 * C_HEADS
    wm = jnp.concatenate([w_in[:, :4 * hw], w_in[:, 4 * hw + ng:]], axis=1).astype(BF16)
    wt = jnp.pad(w_in[:, 4 * hw:4 * hw + ng], ((0, 0), (0, LANES - ng))).astype(BF16)
    full = lambda shape: pl.BlockSpec(shape, lambda i: (0,) * len(shape))
    row = lambda w: pl.BlockSpec((tm, w), lambda i: (i, 0))
    mod = _mod_specs(per_token, tm, d, tiles_per_batch)
    outs = [(3 * hw, F32), (hw, F32), (LANES, F32), (hw, BF16), (hw, F32), (hw, BF16), (hw, F32),
            (D_HEADS * LANES, BF16)]
    reps = hw // D_QK_DIM
    return pl.pallas_call(
        _odd_in_kernel,
        grid=(n // tm,),
        in_specs=[row(d), mod, mod, full(wm.shape), full(wt.shape), full((1, hw)), full((1, hw)), full((hw, hw))],
        out_specs=[_row_spec(tm, w) for w, _ in outs],
        out_shape=[_row_shape(n, w, dt) for w, dt in outs],
        compiler_params=_cparams(("arbitrary",)),
        name="odd_in",
    )(x, sh, sc, wm, wt, jnp.tile(q_g, reps).reshape(1, hw), jnp.tile(k_g, reps).reshape(1, hw),
      _block_ones(hw, D_QK_DIM))


def _out_proj_kernel(a_ref, b_ref, x_ref, g1_ref, sh_ref, sc_ref, w_ref, rw_ref, rb_ref, x1_ref, h2_ref, lg_ref):
    hw = a_ref.shape[1]
    y = jnp.dot(a_ref[...], w_ref[:hw, :], preferred_element_type=F32)
    y = y + jnp.dot(b_ref[...], w_ref[hw:, :], preferred_element_type=F32)
    x1 = x_ref[...] + g1_ref[...] * y
    x1_ref[...] = x1
    h2 = _modulated_rms(x1, sh_ref[...], sc_ref[...])
    h2_ref[...] = h2
    lg_ref[...] = _dot(h2, rw_ref[...]) + rb_ref[...]


def _out_proj(a, b, x, g1, sh, sc, per_token, tiles_per_batch, tm, w_out, router_w, router_b):
    n, d = x.shape
    hw = a.shape[1]
    ne = router_w.shape[1]
    rw = jnp.pad(router_w, ((0, 0), (0, LANES - ne)))
    rb = jnp.pad(router_b, (0, LANES - ne)).reshape(1, LANES)
    full = lambda shape: pl.BlockSpec(shape, lambda i: (0,) * len(shape))
    row = lambda w: pl.BlockSpec((tm, w), lambda i: (i, 0))
    mod = _mod_specs(per_token, tm, d, tiles_per_batch)
    x1, h2, lg = pl.pallas_call(
        _out_proj_kernel,
        grid=(n // tm,),
        in_specs=[row(hw), row(hw), row(d), mod, mod, mod, full((d, d)), full((d, LANES)), full((1, LANES))],
        out_specs=[row(d), row(d), row(LANES)],
        out_shape=[jax.ShapeDtypeStruct((n, d), F32), jax.ShapeDtypeStruct((n, d), F32),
                   jax.ShapeDtypeStruct((n, LANES), F32)],
        compiler_params=_cparams(("arbitrary",)),
        name="out_proj",
    )(a, b, x, g1, sh, sc, w_out.astype(BF16), rw, rb)
    return x1, h2, lg


def _idx_scores_kernel(iq_ref, iw_ref, ik_ref, o_ref, *, tq, tk):
    i = pl.program_id(1)
    j = pl.program_id(2)

    @pl.when(j * tk <= i * tq + tq - 1)
    def _():
        ik = ik_ref[...]
        acc = jnp.zeros((tq, tk), F32)
        for h in range(IDX_HEADS):
            d = _dot_nt(iq_ref[:, h * IDX_DIM:(h + 1) * IDX_DIM], ik)
            acc = acc + iw_ref[:, h:h + 1] * jnp.maximum(d, 0.0)
        o_ref[...] = acc

    @pl.when(j * tk > i * tq + tq - 1)
    def _():
        o_ref[...] = jnp.zeros((tq, tk), F32)


def _idx_scores_prompt(iq, iw, ik, tq, tk):
    bn, s, _ = iq.shape
    last = lambda i: (i * tq + tq - 1) // tk
    return pl.pallas_call(
        functools.partial(_idx_scores_kernel, tq=tq, tk=tk),
        grid=(bn, s // tq, s // tk),
        in_specs=[pl.BlockSpec((None, tq, IDX_HEADS * IDX_DIM), lambda b, i, j: (b, i, 0)),
                  pl.BlockSpec((None, tq, IDX_HEADS), lambda b, i, j: (b, i, 0)),
                  pl.BlockSpec((None, tk, IDX_DIM), lambda b, i, j: (b, jnp.minimum(j, last(i)), 0))],
        out_specs=pl.BlockSpec((None, tq, tk), lambda b, i, j: (b, i, j)),
        out_shape=jax.ShapeDtypeStruct((bn, s, s), F32),
        compiler_params=_cparams(("arbitrary", "arbitrary", "arbitrary")),
        name="idx_scores_prompt",
    )(iq, iw, ik)


def _select_kernel(nch_ref, s_ref, lim_ref, m_ref, key_ref, *, ksel, kc):
    rows, width = s_ref.shape
    nch = nch_ref[pl.program_id(0)]
    lim = lim_ref[...]
    fold = kc // LANES
    col_bits = max(1, int(width - 1).bit_length())

    def cols(c):
        return c * kc + lax.broadcasted_iota(I32, (rows, kc), 1)

    def fill(c, carry):
        off = pl.multiple_of(c * kc, kc)
        bits = lax.bitcast_convert_type(s_ref[:, pl.ds(off, kc)] + 0.0, I32)
        key = jnp.where(bits < 0, bits ^ jnp.int32(0x7FFFFFFF), bits)
        key_ref[:, pl.ds(off, kc)] = jnp.where(cols(c) <= lim, key, jnp.int32(INT_MIN))
        return carry

    lax.fori_loop(0, nch, fill, 0)

    def count(pred):
        def body(c, acc):
            off = pl.multiple_of(c * kc, kc)
            hit = jnp.where(pred(c, key_ref[:, pl.ds(off, kc)]), 1.0, 0.0)
            part = hit[:, :LANES]
            for f in range(1, fold):
                part = part + hit[:, f * LANES:(f + 1) * LANES]
            return acc + part
        acc = lax.fori_loop(0, nch, body, jnp.zeros((rows, LANES), F32))
        return jnp.sum(acc, axis=1, keepdims=True)

    def thr_bit(b, carry):
        prefix, n_ge = carry
        cand = prefix + (jnp.int32(1) << (31 - b))
        cnt = count(lambda c, key: key >= cand)
        take = cnt >= ksel
        return jnp.where(take, cand, prefix), jnp.where(take, cnt, n_ge)

    thr, n_ge = lax.fori_loop(0, 32, thr_bit, (jnp.full((rows, 1), INT_MIN, I32),
                                               jnp.full((rows, 1), 1.0, F32) * (nch * kc).astype(F32)))

    def tie_break():
        need = ksel - count(lambda c, key: key > thr)

        def tie_bit(b, pos):
            cand = pos + (jnp.int32(1) << (col_bits - 1 - b))
            cnt = count(lambda c, key: (key == thr) & (cols(c) < cand))
            return jnp.where(cnt < need, cand, pos)

        return lax.fori_loop(0, col_bits, tie_bit, jnp.zeros((rows, 1), I32))

    surplus = jnp.where((n_ge > ksel) & (thr > jnp.int32(INT_MIN)), 1.0, 0.0)
    last = lax.cond(jnp.max(surplus) > 0.0, tie_break, lambda: jnp.full((rows, 1), width, I32))

    def emit(c, carry):
        off = pl.multiple_of(c * kc, kc)
        key = key_ref[:, pl.ds(off, kc)]
        col = cols(c)
        sel = ((key > thr) | ((key == thr) & (col <= last))) & (col <= lim)
        m_ref[:, pl.ds(off, kc)] = jnp.where(sel, 0.0, NEG).astype(m_ref.dtype)
        return carry

    lax.fori_loop(0, nch, emit, 0)

    def blank(c, carry):
        off = pl.multiple_of(c * kc, kc)
        m_ref[:, pl.ds(off, kc)] = jnp.full((rows, kc), NEG, m_ref.dtype)
        return carry

    lax.fori_loop(nch, width // kc, blank, 0)


def _select_mask(scores, lim, nch, ksel, kc, out_dtype):
    r, width = scores.shape
    tr = 128
    return pl.pallas_call(
        functools.partial(_select_kernel, ksel=ksel, kc=kc),
        grid_spec=pltpu.PrefetchScalarGridSpec(
            num_scalar_prefetch=1,
            grid=(r // tr,),
            in_specs=[pl.BlockSpec((tr, width), lambda i, n: (i, 0)),
                      pl.BlockSpec((tr, 1), lambda i, n: (i, 0))],
            out_specs=pl.BlockSpec((tr, width), lambda i, n: (i, 0)),
            scratch_shapes=[pltpu.VMEM((tr, width), I32)]),
        out_shape=jax.ShapeDtypeStruct((r, width), out_dtype),
        compiler_params=_cparams(("arbitrary",)),
        name="select_mask",
    )(nch, scores, lim)


def _prompt_attn_kernel(*refs, nh, nm, dqk, dv, tq, has_mask, lam_init):
    refs = list(refs)
    q_ref, k_ref, vx_ref, tbl_ref = refs[:4]
    pos = 4
    madd_ref = None
    if has_mask:
        madd_ref = refs[pos]
        pos += 1
    lam_ref = subg_ref = None
    if nm == 2:
        lam_ref, subg_ref = refs[pos], refs[pos + 1]
        pos += 2
    o_ref, mx_scr, acc_scr, qz_scr = refs[pos:pos + 4]
    ma_scr = refs[pos + 4] if has_mask else None
    nu = nh * nm
    nf = tq // LANES
    gu = LANES // dqk
    qi = pl.program_id(1)
    mx_scr[...] = jnp.full(mx_scr.shape, -jnp.inf, F32)
    acc_scr[...] = jnp.zeros(acc_scr.shape, F32)

    lane = lax.broadcasted_iota(I32, (tq, LANES), 1)
    for g in range(nu // gu):
        qt = q_ref[:, g * LANES:(g + 1) * LANES]
        for i in range(gu):
            keep = (lane >= i * dqk) & (lane < (i + 1) * dqk)
            qz_scr[g, i * tq:(i + 1) * tq, :] = jnp.where(keep, qt, jnp.zeros_like(qt))

    def sweep(j, back, second):
        koff = pl.multiple_of(j * tq, tq)
        if has_mask:
            ma_scr[...] = madd_ref[:, pl.ds(koff, tq)].astype(F32)
        for g in range(nu // gu):
            stacked = _dot_nt(qz_scr[g], k_ref[pl.ds(koff, tq), g * LANES:(g + 1) * LANES])
            ps = []
            for i in range(gu):
                u = g * gu + i
                s = stacked[i * tq:(i + 1) * tq]
                if back is not None:
                    s = s + tbl_ref[u // nm, back]
                if has_mask:
                    s = s + ma_scr[...]
                if not second:
                    r = s[:, :LANES]
                    for f in range(1, nf):
                        r = jnp.maximum(r, s[:, f * LANES:(f + 1) * LANES])
                    mx_scr[u] = jnp.maximum(mx_scr[u], r)
                else:
                    mx = mx_scr[u]
                    p = jnp.concatenate([jnp.exp(s[:, f * LANES:(f + 1) * LANES] - mx) for f in range(nf)], axis=1)
                    ps.append(p.astype(BF16))
            if second:
                for i0 in range(0, gu, nm):
                    u0 = g * gu + i0
                    h = u0 // nm
                    stack = ps[i0] if nm == 1 else jnp.concatenate(ps[i0:i0 + nm], axis=0)
                    pv = jnp.dot(stack, vx_ref[pl.ds(koff, tq), h * LANES:(h + 1) * LANES], preferred_element_type=F32)
                    for m in range(nm):
                        acc_scr[u0 + m] += pv[m * tq:(m + 1) * tq]

    def all_chunks(second):
        def far(j, carry):
            sweep(j, None, second)
            return carry

        lax.fori_loop(0, jnp.maximum(qi - 1, 0), far, 0)

        @pl.when(qi >= 1)
        def _():
            sweep(qi - 1, 1, second)

        sweep(qi, 0, second)

    all_chunks(False)
    for u in range(nu):
        mx_scr[u] = jnp.broadcast_to(jnp.max(mx_scr[u], axis=-1, keepdims=True), (tq, LANES))
    all_chunks(True)

    def normalised(u):
        a = acc_scr[u]
        return a[:, :dv] / a[:, dv:dv + 1]

    for h in range(nh):
        if nm == 1:
            o_ref[:, h * dv:(h + 1) * dv] = normalised(h).astype(o_ref.dtype)
        else:
            att = normalised(2 * h) - lam_ref[0] * normalised(2 * h + 1)
            att = att * lax.rsqrt(jnp.mean(att * att, axis=-1, keepdims=True) + EPS)
            o_ref[:, h * dv:(h + 1) * dv] = (att * subg_ref[...] * (1.0 - lam_init)).astype(o_ref.dtype)


def _bias_tables(rel_bias, tq):
    far = rel_bias[N_BUCKETS - 1].astype(F32)[:, None, None]
    r = np.arange(tq)[:, None]
    c = np.arange(tq)[None, :]
    tabs = []
    for d in range(2):
        dist = d * tq + r - c
        vals = _bias_by_bucket(rel_bias, _bucket_of(dist)) - far
        tabs.append(jnp.where(jnp.asarray(dist >= 0)[None], vals, NEG))
    return jnp.stack(tabs, axis=1)


def _prompt_attn(q, k, vx, tbl, madd, lam, subg, *, nh, nm, dqk, dv, tq, lam_init):
    bn, s, w = q.shape
    assert MAX_DISTANCE <= tq and s % tq == 0 and tq % LANES == 0 and dv < LANES
    has_mask = madd is not None
    once = lambda width: pl.BlockSpec((None, s, width), lambda b, i: (b, 0, 0), pipeline_mode=pl.Buffered(1))
    in_specs = [pl.BlockSpec((None, tq, w), lambda b, i: (b, i, 0)), once(w), once(nh * LANES),
                pl.BlockSpec(tbl.shape, lambda b, i: (0, 0, 0, 0), pipeline_mode=pl.Buffered(1))]
    args = [q, k, vx, tbl]
    if has_mask:
        in_specs.append(pl.BlockSpec((None, tq, s), lambda b, i: (b, i, 0)))
        args.append(madd)
    if nm == 2:
        in_specs += [pl.BlockSpec(memory_space=pltpu.SMEM), pl.BlockSpec((1, dv), lambda b, i: (0, 0))]
        args += [lam, subg]
    return pl.pallas_call(
        functools.partial(_prompt_attn_kernel, nh=nh, nm=nm, dqk=dqk, dv=dv, tq=tq, has_mask=has_mask, lam_init=lam_init),
        grid=(bn, s // tq),
        in_specs=in_specs,
        out_specs=pl.BlockSpec((None, tq, nh * dv), lambda b, i: (b, i, 0)),
        out_shape=jax.ShapeDtypeStruct((bn, s, nh * dv), BF16),
        scratch_shapes=[pltpu.VMEM((nh * nm, tq, LANES), F32), pltpu.VMEM((nh * nm, tq, LANES), F32),
                        pltpu.VMEM((nh * nm * dqk // LANES, LANES // dqk * tq, LANES), BF16)]
                       + ([pltpu.VMEM((tq, tq), F32)] if has_mask else []),
        compiler_params=_cparams(("arbitrary", "arbitrary")),
        name="prompt_attn_dsa" if nm == 1 else "prompt_attn_diff",
    )(*args)


def _paged_scores_kernel(pt_ref, iq_ref, iw_ref, *refs, gp, page):
    pages = refs[:gp]
    new_ref, o_ref = refs[gp], refs[gp + 1]
    s_id = pl.program_id(1)
    last = pl.num_programs(1) - 1
    rows = iq_ref.shape[0]
    tp = rows // IDX_HEADS

    def score(ik):
        d = jnp.maximum(_dot_nt(iq_ref[...], ik), 0.0) * iw_ref[...]
        acc = d[:tp]
        for h in range(1, IDX_HEADS):
            acc = acc + d[h * tp:(h + 1) * tp]
        return acc

    @pl.when(s_id < last)
    def _():
        for g in range(gp):
            o_ref[:, g * page:(g + 1) * page] = score(pages[g][...])

    @pl.when(s_id == last)
    def _():
        o_ref[...] = jnp.zeros(o_ref.shape, F32)
        o_ref[:, :page] = score(new_ref[...])


def _paged_scores(page_table, iq_rows, iw_rows, cache_idx, ik_new, gp):
    bn, rows, _ = iq_rows.shape
    tp = rows // IDX_HEADS
    page = cache_idx.shape[1]
    npages = page_table.shape[1]
    nsteps = npages // gp + 1
    width = nsteps * gp * page

    def page_map(g):
        return lambda b, s, pt: (pt[b, jnp.minimum(s * gp + g, npages - 1)], 0, 0)

    return pl.pallas_call(
        functools.partial(_paged_scores_kernel, gp=gp, page=page),
        grid_spec=pltpu.PrefetchScalarGridSpec(
            num_scalar_prefetch=1,
            grid=(bn, nsteps),
            in_specs=[pl.BlockSpec((None, rows, IDX_DIM), lambda b, s, pt: (b, 0, 0)),
                      pl.BlockSpec((None, rows, 1), lambda b, s, pt: (b, 0, 0))]
                     + [pl.BlockSpec((None, page, IDX_DIM), page_map(g)) for g in range(gp)]
                     + [pl.BlockSpec((None, page, IDX_DIM), lambda b, s, pt: (b, 0, 0))],
            out_specs=pl.BlockSpec((None, tp, gp * page), lambda b, s, pt: (b, 0, s))),
        out_shape=jax.ShapeDtypeStruct((bn, tp, width), F32),
        compiler_params=_cparams(("arbitrary", "arbitrary")),
        name="paged_idx_scores",
    )(page_table, iq_rows, iw_rows, *([cache_idx] * gp), ik_new)


def _paged_rows_kernel(pt_ref, w_ref, *refs, gp, nh, nm, tp, has_sel, lam_init):
    refs = list(refs)
    pos = 0
    selp_ref = e_ref = None
    if has_sel:
        selp_ref, e_ref = refs[0], refs[1]
        pos = 2
    dmask_ref, blast_ref, bnew_ref = refs[pos:pos + 3]
    pos += 3
    kpages = refs[pos:pos + gp]
    vpages = refs[pos + gp:pos + 2 * gp]
    pos += 2 * gp
    knew_ref, vnew_ref = refs[pos], refs[pos + 1]
    pos += 2
    lam_ref = subg_ref = None
    if nm == 2:
        lam_ref, subg_ref = refs[pos], refs[pos + 1]
        pos += 2
    o_ref, m_scr, l_scr, acc_scr = refs[pos:pos + 4]
    page, heads, hd = knew_ref.shape
    dv = vnew_ref.shape[2]
    rows = page * heads
    s_id = pl.program_id(1)
    last = pl.num_programs(1) - 1

    @pl.when(s_id == 0)
    def _():
        m_scr[...] = jnp.full(m_scr.shape, -jnp.inf, F32)
        l_scr[...] = jnp.zeros(l_scr.shape, F32)
        acc_scr[...] = jnp.zeros(acc_scr.shape, F32)

    def update(blocks):
        ss = []
        for k_ref, _, sel, extra in blocks:
            s = _dot(k_ref[...].reshape(rows, hd), w_ref[...]) + dmask_ref[...]
            if sel is not None:
                s = s + jnp.dot(e_ref[...], sel, preferred_element_type=F32)
            if extra is not None:
                s = s + extra
            ss.append(s)
        mx = ss[0]
        for s in ss[1:]:
            mx = jnp.maximum(mx, s)
        m_prev = m_scr[...]
        m_new = jnp.maximum(m_prev, jnp.max(mx, axis=0, keepdims=True))
        alpha = jnp.exp(m_prev - m_new)
        ps = [jnp.exp(s - m_new) for s in ss]
        psum = ps[0]
        for p in ps[1:]:
            psum = psum + p
        tot = None
        for p, (_, v_ref, _, _) in zip(ps, blocks):
            pv = _dot_tn(v_ref[...].reshape(rows, dv), p)
            tot = pv if tot is None else tot + pv
        l_scr[...] = alpha * l_scr[...] + jnp.sum(psum, axis=0, keepdims=True)
        acc_scr[...] = alpha * acc_scr[...] + tot
        m_scr[...] = m_new

    @pl.when(s_id < last)
    def _():
        near = blast_ref[...] * jnp.where(s_id == last - 1, 1.0, 0.0)
        update([(kpages[g], vpages[g],
                 selp_ref[g * page:(g + 1) * page, :] if has_sel else None,
                 near if g == gp - 1 else None) for g in range(gp)])

    @pl.when(s_id == last)
    def _():
        update([(knew_ref, vnew_ref, selp_ref[:page, :] if has_sel else None, bnew_ref[...])])
        accn = jnp.transpose(acc_scr[...] / l_scr[...])
        for h in range(nh):
            cols = slice(h * dv, (h + 1) * dv)
            if nm == 1:
                o_ref[:, cols] = accn[h * tp:(h + 1) * tp]
            else:
                second = nh * tp
                att = accn[h * tp:(h + 1) * tp] - lam_ref[0] * accn[second + h * tp:second + (h + 1) * tp]
                att = att * lax.rsqrt(jnp.mean(att * att, axis=-1, keepdims=True) + EPS)
                o_ref[:, cols] = att * subg_ref[...] * (1.0 - lam_init)


def _paged_rows_attn(page_table, li, q, sel, bias, allowed, cache_k, cache_v, k_new, v_new, lam, subg, *,
                     gp, nm, tp, lam_init):
    bn, t, nh, _, dq = q.shape
    page, hd = cache_k.shape[2], cache_k.shape[4]
    dv = cache_v.shape[4]
    npages = page_table.shape[1]
    nsteps = npages // gp + 1
    past = npages * page
    ncol = nm * nh * tp
    rows = page * nh
    has_sel = sel is not None
    assert nm * dq == hd and npages % gp == 0

    qp = jnp.pad(q, ((0, 0), (0, tp - t), (0, 0), (0, 0), (0, 0)))
    w = jnp.einsum('bthmd,mn->bmdnht', qp, jnp.eye(nm, dtype=q.dtype)).reshape(bn, hd, ncol)

    def expand(x):
        x = jnp.transpose(x, (2, 0, 1))[:, None, None]
        return jnp.broadcast_to(x, (page, nh, nm, nh, tp)).reshape(rows, ncol)

    far = bias[:, :1, :1]
    assert past - 1 + 1 >= MAX_DISTANCE
    pad_t = lambda x, v: jnp.pad(x, ((0, 0), (0, tp - t), (0, 0)), constant_values=v)
    near = pad_t(bias[:, :, past - page:past] - far, 0.0)
    new = jnp.where(jnp.asarray(allowed[:, past:])[None], bias[:, :, past:] - far, NEG)
    new = jnp.pad(pad_t(new, 0.0), ((0, 0), (0, 0), (0, page - t)), constant_values=NEG)
    r = np.arange(rows)[:, None] % nh
    c = (np.arange(ncol)[None, :] // tp) % nh
    dmask = jnp.asarray(np.where(r == c, 0.0, NEG), F32)
    pad_new = lambda a: jnp.pad(a, ((0, 0), (0, page - t), (0, 0), (0, 0)))

    def page_map(g):
        return lambda b, s, pt: (li, pt[b, jnp.minimum(s * gp + g, npages - 1)], 0, 0, 0)

    const = lambda shape: pl.BlockSpec(shape, lambda b, s, pt: (0,) * len(shape))
    in_specs = [pl.BlockSpec((None, hd, ncol), lambda b, s, pt: (b, 0, 0))]
    args = [w]
    if has_sel:
        selp = jnp.broadcast_to(jnp.transpose(sel, (0, 2, 1))[:, :, None, :], (bn, sel.shape[2], nh, tp))
        expander = jnp.asarray(np.arange(rows)[:, None] // nh == np.arange(page)[None, :], BF16)
        in_specs += [pl.BlockSpec((None, gp * page, ncol), lambda b, s, pt: (b, s, 0)), const((rows, page))]
        args += [selp.reshape(bn, sel.shape[2], ncol).astype(BF16), expander]
    in_specs += [const((rows, ncol))] * 3
    args += [dmask, expand(near), expand(new)]
    in_specs += [pl.BlockSpec((None, None, page, nh, hd), page_map(g)) for g in range(gp)]
    in_specs += [pl.BlockSpec((None, None, page, nh, dv), page_map(g)) for g in range(gp)]
    args += [cache_k] * gp + [cache_v] * gp
    in_specs += [pl.BlockSpec((None, page, nh, hd), lambda b, s, pt: (b, 0, 0, 0)),
                 pl.BlockSpec((None, page, nh, dv), lambda b, s, pt: (b, 0, 0, 0))]
    args += [pad_new(k_new), pad_new(v_new)]
    if nm == 2:
        in_specs += [pl.BlockSpec(memory_space=pltpu.SMEM), pl.BlockSpec((1, dv), lambda b, s, pt: (0, 0))]
        args += [lam, subg]
    return pl.pallas_call(
        functools.partial(_paged_rows_kernel, gp=gp, nh=nh, nm=nm, tp=tp, has_sel=has_sel, lam_init=lam_init),
        grid_spec=pltpu.PrefetchScalarGridSpec(
            num_scalar_prefetch=1,
            grid=(bn, nsteps),
            in_specs=in_specs,
            out_specs=pl.BlockSpec((None, tp, nh * dv), lambda b, s, pt: (b, 0, 0)),
            scratch_shapes=[pltpu.VMEM((1, ncol), F32), pltpu.VMEM((1, ncol), F32), pltpu.VMEM((dv, ncol), F32)]),
        out_shape=jax.ShapeDtypeStruct((bn, tp, nh * dv), F32),
        compiler_params=_cparams(("arbitrary", "arbitrary")),
        name="paged_attn_dsa" if nm == 1 else "paged_attn_diff",
    )(page_table, *args)


def _gdn_kernel(xc_ref, prev_ref, buf_ref, cz_ref, tail_ref, s0_ref, cw_ref, alog_ref, dtb_ref, nw_ref,
                o_ref, sfin_ref, s_scr, *, n_valid):
    ci = pl.program_id(1)
    bg, c, ch = xc_ref.shape
    dk = C_DK
    dv = (ch - 2 * C_HEADS * dk) // C_HEADS

    @pl.when(ci == 0)
    def _():
        s_scr[...] = s0_ref[...]

    row = lax.broadcasted_iota(I32, (c, c), 0)
    colm = lax.broadcasted_iota(I32, (c, c), 1)
    incl = row >= colm
    tril = incl.astype(F32)
    eye = (row == colm).astype(F32)

    units = []
    for b in range(bg):
        hist = jnp.where(ci == 0, buf_ref[b], prev_ref[b])
        nh_rows = hist.shape[0]
        xfull = jnp.concatenate([hist, xc_ref[b]], axis=0)
        y = cw_ref[C_CONV - 1:C_CONV, :] * xfull[nh_rows:, :]
        for j in range(1, C_CONV):
            y = y + cw_ref[C_CONV - 1 - j:C_CONV - j, :] * pltpu.roll(xfull, j, 0)[nh_rows:, :]
        y = _silu(y)

        tail = tail_ref[b]
        beta_all = jax.nn.sigmoid(tail)
        sp_in = tail + dtb_ref[...]
        g_all = -jnp.exp(alog_ref[...]) * (jnp.maximum(sp_in, 0.0) + jnp.log1p(jnp.exp(-jnp.abs(sp_in))))
        if n_valid < c:
            valid = lax.broadcasted_iota(I32, tail.shape, 0) < n_valid
            beta_all = jnp.where(valid, beta_all, 0.0)
            g_all = jnp.where(valid, g_all, 0.0)
        gc_all = _dot_hi(tril, g_all)
        gc_rows = jnp.transpose(gc_all)

        for h in range(C_HEADS):
            q = y[:, h * dk:(h + 1) * dk]
            k = y[:, (C_HEADS + h) * dk:(C_HEADS + h + 1) * dk]
            v = y[:, 2 * C_HEADS * dk + h * dv:2 * C_HEADS * dk + (h + 1) * dv]
            q = q * lax.rsqrt(jnp.sum(q * q, axis=-1, keepdims=True) + EPS) * (dk ** -0.5)
            k = k * lax.rsqrt(jnp.sum(k * k, axis=-1, keepdims=True) + EPS)
            beta = beta_all[:, h:h + 1]
            gc = gc_all[:, C_HEADS + h:C_HEADS + h + 1]
            decay = jnp.exp(jnp.where(incl, gc - gc_rows[C_HEADS + h:C_HEADS + h + 1, :], -jnp.inf))
            units.append(dict(b=b, h=h, q=q, k=k, kb=k * beta, vb=v * beta, gc=gc, decay=decay))

    def stage(fn):
        return [fn(un) for un in units]

    def put(name, vals):
        for un, val in zip(units, vals):
            un[name] = val

    put('a', stage(lambda un: _dot_nt(un['kb'], un['k']) * un['decay'] * (1.0 - eye)))
    put('inv', stage(lambda un: eye - un['a']))
    put('pw', stage(lambda un: un['a']))
    for _ in range(max(0, int(c - 1).bit_length() - 1)):
        put('pw', stage(lambda un: _dot_x3(un['pw'], un['pw'])))
        put('inv', stage(lambda un: un['inv'] + _dot_x3(un['inv'], un['pw'])))
    put('u', stage(lambda un: _dot_x3(un['inv'], un['vb'])))
    put('w', stage(lambda un: _dot_x3(un['inv'], un['kb'] * jnp.exp(un['gc']))))
    put('s', stage(lambda un: s_scr[un['b'], un['h']]))
    put('v_new', stage(lambda un: un['u'] - _dot(un['w'], un['s'])))
    put('intra', stage(lambda un: _dot_nt(un['q'], un['k']) * un['decay']))
    put('o', stage(lambda un: _dot(un['q'] * jnp.exp(un['gc']), un['s']) + _dot(un['intra'], un['v_new'])))
    for un in units:
        gl = un['gc'][c - 1:c, :]
        s_scr[un['b'], un['h']] = un['s'] * jnp.exp(gl) + _dot_tn(un['k'] * jnp.exp(gl - un['gc']), un['v_new'])
    for un in units:
        b, h, o = un['b'], un['h'], un['o']
        o = o * lax.rsqrt(jnp.mean(o * o, axis=-1, keepdims=True) + EPS) * nw_ref[...]
        o_ref[b, :, h * dv:(h + 1) * dv] = (o * _silu(cz_ref[b, :, h * dv:(h + 1) * dv])).astype(o_ref.dtype)

    @pl.when(ci == pl.num_programs(1) - 1)
    def _():
        sfin_ref[...] = s_scr[...]


def _gdn(xc, buf, cz, tail, s0, conv_w, a_log, dt_bias, norm_w, chunk, n_valid):
    bn, t, ch = xc.shape
    hw = cz.shape[2]
    nck = t // chunk
    dvh = hw // C_HEADS
    hist = SUBLANES
    per_chunk = chunk // hist
    bg = 2 if bn % 2 == 0 else 1
    gate_lanes = lambda p: jnp.pad(p.astype(F32), (C_HEADS, LANES - 2 * C_HEADS)).reshape(1, LANES)
    return pl.pallas_call(
        functools.partial(_gdn_kernel, n_valid=n_valid),
        grid=(bn // bg, nck),
        in_specs=[pl.BlockSpec((bg, chunk, ch), lambda b, i: (b, i, 0)),
                  pl.BlockSpec((bg, hist, ch), lambda b, i: (b, jnp.maximum(i * per_chunk - 1, 0), 0)),
                  pl.BlockSpec((bg, hist, ch), lambda b, i: (b, 0, 0)),
                  pl.BlockSpec((bg, chunk, hw), lambda b, i: (b, i, 0)),
                  pl.BlockSpec((bg, chunk, LANES), lambda b, i: (b, i, 0)),
                  pl.BlockSpec((bg, C_HEADS, C_DK, dvh), lambda b, i: (b, 0, 0, 0)),
                  pl.BlockSpec((C_CONV, ch), lambda b, i: (0, 0)),
                  pl.BlockSpec((1, LANES), lambda b, i: (0, 0)),
                  pl.BlockSpec((1, LANES), lambda b, i: (0, 0)),
                  pl.BlockSpec((1, dvh), lambda b, i: (0, 0))],
        out_specs=[pl.BlockSpec((bg, chunk, hw), lambda b, i: (b, i, 0)),
                   pl.BlockSpec((bg, C_HEADS, C_DK, dvh), lambda b, i: (b, 0, 0, 0))],
        out_shape=[jax.ShapeDtypeStruct((bn, t, hw), BF16), jax.ShapeDtypeStruct((bn, C_HEADS, C_DK, dvh), F32)],
        scratch_shapes=[pltpu.VMEM((bg, C_HEADS, C_DK, dvh), F32)],
        compiler_params=_cparams(("arbitrary", "arbitrary")),
        name="gated_delta",
    )(xc, xc, buf, cz, tail, s0, conv_w, gate_lanes(a_log), gate_lanes(dt_bias), norm_w.reshape(1, dvh))


def _moe_kernel(be_ref, x_ref, gate_ref, w1_ref, b1_ref, w2_ref, b2_ref, o_ref, w1_scr, w2_scr):
    i = pl.program_id(0)
    dff = w2_ref.shape[0]
    changed = jnp.logical_or(i == 0, be_ref[i] != be_ref[jnp.maximum(i - 1, 0)])

    @pl.when(changed)
    def _():
        w1_scr[...] = w1_ref[...].astype(BF16)
        w2_scr[...] = w2_ref[...].astype(BF16)

    hmid = jnp.dot(x_ref[...].astype(BF16), w1_scr[...], preferred_element_type=F32) + b1_ref[...]
    gate = jnp.minimum(hmid[:, :dff], SWIGLU_LIMIT)
    up = jnp.clip(hmid[:, dff:], -SWIGLU_LIMIT, SWIGLU_LIMIT)
    act = (up + 1.0) * gate * jax.nn.sigmoid(SWIGLU_ALPHA * gate)
    y = jnp.dot(act.astype(BF16), w2_scr[...], preferred_element_type=F32) + b2_ref[...]
    o_ref[...] = y * gate_ref[...]


def _moe_ffn(xs, slot_gate, block_e, layer, w1, b1, w2, b2, bm):
    n_slots, d = xs.shape
    _, ne, _, d2 = w1.shape
    dff = w2.shape[2]
    return pl.pallas_call(
        _moe_kernel,
        grid_spec=pltpu.PrefetchScalarGridSpec(
            num_scalar_prefetch=1,
            grid=(n_slots // bm,),
            in_specs=[pl.BlockSpec((bm, d), lambda i, be: (i, 0)),
                      pl.BlockSpec((bm, 1), lambda i, be: (i, 0)),
                      pl.BlockSpec((None, None, d, d2), lambda i, be: (layer, be[i], 0, 0)),
                      pl.BlockSpec((None, None, 1, d2), lambda i, be: (layer, be[i], 0, 0)),
                      pl.BlockSpec((None, None, dff, d), lambda i, be: (layer, be[i], 0, 0)),
                      pl.BlockSpec((None, None, 1, d), lambda i, be: (layer, be[i], 0, 0))],
            out_specs=pl.BlockSpec((bm, d), lambda i, be: (i, 0)),
            scratch_shapes=[pltpu.VMEM((d, d2), BF16), pltpu.VMEM((dff, d), BF16)]),
        out_shape=jax.ShapeDtypeStruct((n_slots, d), F32),
        compiler_params=_cparams(("arbitrary",)),
        name="moe_ffn",
    )(block_e, xs, slot_gate.reshape(n_slots, 1), w1, b1.reshape(b1.shape[0], ne, 1, d2), w2, b2.reshape(b2.shape[0], ne, 1, d))


def _route_kernel(lg_ref, e_ref, g_ref, r_ref, cnt_ref, run_scr, *, ne):
    i = pl.program_id(0)
    tm = lg_ref.shape[0]

    @pl.when(i == 0)
    def _():
        run_scr[...] = jnp.zeros(run_scr.shape, F32)

    lane = lax.broadcasted_iota(I32, (tm, LANES), 1)
    lane_f = lane.astype(F32)
    x = jnp.where(lane < ne, lg_ref[...], -jnp.inf)
    vals, hots = [], []
    e_out = jnp.zeros((tm, LANES), F32)
    for j in range(TOP_K):
        m = jnp.max(x, axis=-1, keepdims=True)
        idx = jnp.min(jnp.where(x == m, lane_f, float(LANES)), axis=-1, keepdims=True)
        hot = lane_f == idx
        vals.append(m)
        hots.append(hot)
        e_out = jnp.where(lane == j, idx, e_out)
        x = jnp.where(hot, -jnp.inf, x)
    ex = [jnp.exp(v - vals[0]) for v in vals]
    denom = ex[0]
    for j in range(1, TOP_K):
        denom = denom + ex[j]
    chosen = jnp.where(hots[0], 1.0, 0.0)
    for j in range(1, TOP_K):
        chosen = chosen + jnp.where(hots[j], 1.0, 0.0)
    before = (lax.broadcasted_iota(I32, (tm, tm), 0) > lax.broadcasted_iota(I32, (tm, tm), 1)).astype(BF16)
    base = run_scr[...] + jnp.dot(before, chosen.astype(BF16), preferred_element_type=F32)
    g_out = jnp.zeros((tm, LANES), F32)
    r_out = jnp.zeros((tm, LANES), F32)
    for j in range(TOP_K):
        g_out = jnp.where(lane == j, ex[j] / denom, g_out)
        r_out = jnp.where(lane == j, jnp.sum(jnp.where(hots[j], base, 0.0), axis=-1, keepdims=True), r_out)
    e_ref[...] = e_out[:, :TOP_K].astype(I32)
    g_ref[...] = g_out[:, :TOP_K]
    r_ref[...] = r_out[:, :TOP_K].astype(I32)
    run_scr[...] = run_scr[...] + jnp.sum(chosen, axis=0, keepdims=True)

    @pl.when(i == pl.num_programs(0) - 1)
    def _():
        cnt_ref[...] = run_scr[...]


def _route(logits, ne):
    n = logits.shape[0]
    tm = next(t for t in (512, 384, 256, 128, 64, 32, 16, 8) if n % t == 0)
    small = lambda dt: jax.ShapeDtypeStruct((n, TOP_K), dt)
    return pl.pallas_call(
        functools.partial(_route_kernel, ne=ne),
        grid=(n // tm,),
        in_specs=[pl.BlockSpec((tm, LANES), lambda i: (i, 0))],
        out_specs=[pl.BlockSpec((tm, TOP_K), lambda i: (i, 0))] * 3 + [pl.BlockSpec((1, LANES), lambda i: (0, 0))],
        out_shape=[small(I32), small(F32), small(I32), jax.ShapeDtypeStruct((1, LANES), F32)],
        scratch_shapes=[pltpu.VMEM((1, LANES), F32)],
        compiler_params=_cparams(("arbitrary",)),
        name="moe_route",
    )(logits)


def _moe(h2, logits, layer, w1, b1, w2, b2, bm):
    n_tok = h2.shape[0]
    ne = w1.shape[1]
    top_idx, gates, rank, counts = _route(logits, ne)
    counts = counts[0, :ne].astype(I32)
    padded = (counts + bm - 1) // bm * bm
    pad_end = jnp.cumsum(padded)
    pad_start = pad_end - padded
    n_assign = n_tok * TOP_K
    n_blocks = -(-(n_assign + ne * (bm - 1)) // bm)
    n_slots = n_blocks * bm
    hot = top_idx[..., None] == jnp.arange(ne, dtype=I32)
    dest = (jnp.sum(jnp.where(hot, pad_start, 0), axis=-1) + rank).reshape(-1)
    block_e = jnp.minimum(jnp.sum(pad_end[None, :] <= (jnp.arange(n_blocks, dtype=I32) * bm)[:, None], axis=1),
                          ne - 1).astype(I32)
    payload = jnp.stack([jnp.repeat(jnp.arange(n_tok, dtype=I32), TOP_K),
                         lax.bitcast_convert_type(gates.reshape(-1), I32)], axis=-1)
    slots = jnp.zeros((n_slots, 2), I32).at[dest].set(payload)
    slot_gate = lax.bitcast_convert_type(slots[:, 1], F32)
    ys = _moe_ffn(h2[slots[:, 0]], slot_gate, block_e, layer, w1, b1, w2, b2, bm)
    return jnp.sum(ys[dest.reshape(n_tok, TOP_K).T], axis=0)


def _rel_bias_sample(rel_bias, past_len, t):
    dist = (past_len + np.arange(t))[:, None] - np.arange(past_len + t)[None, :]
    return _bias_by_bucket(rel_bias, _bucket_of(dist)), dist >= 0


def kernel(x_prompt, x_sample, cache_b_k, cache_b_v, cache_b_idx, state_c_rec, state_c_conv, cache_d_k, cache_d_v,
           page_table, c_prompt, c_sample, rel_bias, ada_w, ada_b, even_w_in, even_w_out, a_ln_g, a_ln_b, a_w_sp,
           a_b_sp, b_q_norm, b_k_norm, b_idx_norm, odd_w_in, odd_w_out, c_conv_w, c_a_log, c_dt_bias, c_norm_w,
           d_q_norm, d_k_norm, d_lambda, d_subln, router_w, router_b, moe_w1, moe_b1, moe_w2, moe_b2):
    bp, sp, d = x_prompt.shape
    bs, ts, _ = x_sample.shape
    depth = ada_w.shape[0]
    hw = d // 2
    page = cache_b_k.shape[2]
    past_len = page_table.shape[1] * page
    n_p, n_s = bp * sp, bs * ts
    tm_p = 256
    tq = 256
    tp = SUBLANES
    gp = 8 if page_table.shape[1] % 8 == 0 else 1
    assert sp % tm_p == 0 and n_s % SUBLANES == 0 and ts <= tp and ts >= C_CONV - 1

    n_c = bp + bs
    c_all = jnp.pad(jnp.concatenate([c_prompt, c_sample], axis=0), ((0, -n_c % SUBLANES), (0, 0)))
    mod_all = _ada_mod(c_all, ada_w, ada_b)

    def mods(layer):
        m = mod_all[layer].reshape(-1, 6, d)
        mp = [m[:bp, i][:, None, :] for i in range(6)]
        ms = [jnp.repeat(m[bp:n_c, i], ts, axis=0) for i in range(6)]
        return mp, ms

    xp = x_prompt.reshape(n_p, d)
    xs = x_sample.reshape(n_s, d)
    tpb = sp // tm_p
    tbl = _bias_tables(rel_bias, tq)
    outs_p, outs_s = {}, {}

    for layer in range(depth):
        li = layer // 2
        mp, ms = mods(layer)
        if layer % 2 == 0:
            ws_p = jnp.where(np.tril(np.ones((A_CHUNK, A_CHUNK), bool)), a_w_sp[li], 0.0).astype(BF16)
            bsp_p = a_b_sp[li].T
            (aout, _, q, k32, kbf, v32, vbf, iq, ik32, ikbf, iw) = _even_in(
                xp, mp[0], mp[1], False, tpb, tm_p, even_w_in[li], a_ln_g[li], a_ln_b[li], ws_p, bsp_p,
                b_q_norm[li], b_k_norm[li], b_idx_norm[li])
            ksel = min(TOPK_MAX, sp // 4)
            kc = 512 if sp % 512 == 0 else sp
            scores = _idx_scores_prompt(iq.reshape(bp, sp, hw), iw.reshape(bp, sp, IDX_HEADS),
                                        ikbf.reshape(bp, sp, IDX_DIM), tq, kc)
            lim = jnp.tile(jnp.arange(sp, dtype=I32), bp).reshape(n_p, 1)
            nch = jnp.tile((jnp.arange(sp // 128, dtype=I32) * 128 + 127) // kc + 1, bp)
            madd = _select_mask(scores.reshape(n_p, sp), lim, nch, ksel, kc, BF16).reshape(bp, sp, sp)
            b_out = _prompt_attn(q.reshape(bp, sp, hw), kbf.reshape(bp, sp, hw), vbf.reshape(bp, sp, -1), tbl, madd,
                                 None, None, nh=B_HEADS, nm=1, dqk=hw // B_HEADS, dv=hw // B_HEADS, tq=tq, lam_init=0.0)
            outs_p.setdefault('b_k', []).append(k32.reshape(bp, sp, B_HEADS, -1))
            outs_p.setdefault('b_v', []).append(v32.reshape(bp, sp, B_HEADS, -1))
            outs_p.setdefault('b_idx', []).append(ik32.reshape(bp, sp, IDX_DIM))
            mix_p = (aout, b_out.reshape(n_p, hw))

            cs = min(ts, A_CHUNK)
            ws_s = jnp.where(np.tril(np.ones((cs, cs), bool)), a_w_sp[li][:, :cs, :cs], 0.0)
            ws_s = jnp.einsum('ab,gts->gatbs', jnp.eye(n_s // cs, dtype=F32), ws_s).reshape(A_GROUPS, n_s, n_s).astype(BF16)
            bsp_s = jnp.tile(a_b_sp[li][:, :cs].T, (n_s // cs, 1))
            (aout, av, q, k32, kbf, v32, vbf, iq, ik32, ikbf, iw) = _even_in(
                xs, ms[0], ms[1], True, 1, n_s, even_w_in[li], a_ln_g[li], a_ln_b[li], ws_s, bsp_s,
                b_q_norm[li], b_k_norm[li], b_idx_norm[li])
            ltot = past_len + ts
            ksel = min(TOPK_MAX, ltot // 4)
            iq_rows = jnp.pad(iq.reshape(bs, ts, IDX_HEADS, IDX_DIM), ((0, 0), (0, tp - ts), (0, 0), (0, 0)))
            iq_rows = jnp.transpose(iq_rows, (0, 2, 1, 3)).reshape(bs, IDX_HEADS * tp, IDX_DIM)
            iw_rows = jnp.pad(iw.reshape(bs, ts, IDX_HEADS), ((0, 0), (0, tp - ts), (0, 0)))
            iw_rows = jnp.transpose(iw_rows, (0, 2, 1)).reshape(bs, IDX_HEADS * tp, 1)
            pad_new = lambda a: jnp.pad(a.reshape(bs, ts, -1), ((0, 0), (0, page - ts), (0, 0)))
            sc_s = _paged_scores(page_table, iq_rows, iw_rows, cache_b_idx[li], pad_new(ikbf), gp)
            width = sc_s.shape[2]
            kc_s = gp * page
            lim_s = jnp.where(np.arange(tp) < ts, past_len + np.arange(tp), -1).astype(I32)
            lim_s = jnp.tile(lim_s, bs).reshape(bs * tp, 1)
            nch_s = jnp.full((bs * tp // 128,), width // kc_s, I32)
            sel_s = _select_mask(sc_s.reshape(bs * tp, width), lim_s, nch_s, ksel, kc_s, F32).reshape(bs, tp, width)
            bias_s, allowed = _rel_bias_sample(rel_bias, past_len, ts)
            heads = lambda a: a.reshape(bs, ts, B_HEADS, -1)
            b_out = _paged_rows_attn(page_table, li, q.reshape(bs, ts, B_HEADS, 1, -1), sel_s, bias_s, allowed,
                                     cache_b_k, cache_b_v, heads(k32), heads(v32), None, None,
                                     gp=gp, nm=1, tp=tp, lam_init=0.0)
            outs_s.setdefault('a_v', []).append(av.reshape(bs, ts, hw))
            outs_s.setdefault('b_k', []).append(k32.reshape(bs, ts, B_HEADS, -1))
            outs_s.setdefault('b_v', []).append(v32.reshape(bs, ts, B_HEADS, -1))
            outs_s.setdefault('b_idx', []).append(ik32.reshape(bs, ts, IDX_DIM))
            mix_s = (aout, b_out[:, :ts].reshape(n_s, hw).astype(BF16))
            w_out = even_w_out[li]
        else:
            lam_init = 0.8 - 0.6 * math.exp(-0.3 * layer)
            lp = d_lambda[li].astype(F32)
            lam = (jnp.exp(jnp.sum(lp[0] * lp[1])) - jnp.exp(jnp.sum(lp[2] * lp[3])) + lam_init).reshape(1)
            subg = d_subln[li].reshape(1, -1)
            dvh = hw // D_HEADS
            xc, cz, tail, q2, k2_32, k2bf, v2_32, v2bf = _odd_in(xp, mp[0], mp[1], False, tpb, tm_p, odd_w_in[li],
                                                                d_q_norm[li], d_k_norm[li])
            ch = xc.shape[1]
            xc3 = xc.reshape(bp, sp, ch)
            zbuf = jnp.zeros((bp, SUBLANES, ch), F32)
            s0 = jnp.zeros((bp, C_HEADS, C_DK, hw // C_HEADS), F32)
            chunk = C_CHUNK if sp % C_CHUNK == 0 else sp
            c_out, c_rec = _gdn(xc3, zbuf, cz.reshape(bp, sp, hw), tail.reshape(bp, sp, LANES), s0, c_conv_w[li],
                                c_a_log[li], c_dt_bias[li], c_norm_w[li], chunk, chunk)
            d_out = _prompt_attn(q2.reshape(bp, sp, hw), k2bf.reshape(bp, sp, hw), v2bf.reshape(bp, sp, -1), tbl, None,
                                 lam, subg, nh=D_HEADS, nm=2, dqk=D_QK_DIM, dv=dvh, tq=tq, lam_init=lam_init)
            outs_p.setdefault('c_rec', []).append(c_rec)
            outs_p.setdefault('c_conv', []).append(xc3[:, sp - (C_CONV - 1):])
            outs_p.setdefault('d_k', []).append(k2_32.reshape(bp, sp, D_HEADS, -1))
            outs_p.setdefault('d_v', []).append(v2_32.reshape(bp, sp, D_HEADS, -1))
            mix_p = (c_out.reshape(n_p, hw), d_out.reshape(n_p, hw))

            xc, cz, tail, q2, k2_32, k2bf, v2_32, v2bf = _odd_in(xs, ms[0], ms[1], True, 1, n_s, odd_w_in[li],
                                                                d_q_norm[li], d_k_norm[li])
            pad_t = lambda a: jnp.pad(a.reshape(bs, ts, -1), ((0, 0), (0, C_CHUNK - ts), (0, 0)))
            buf = state_c_conv[li]
            buf8 = jnp.pad(buf, ((0, 0), (SUBLANES - (C_CONV - 1), 0), (0, 0)))
            c_out, c_rec = _gdn(pad_t(xc), buf8, pad_t(cz), pad_t(tail), state_c_rec[li], c_conv_w[li], c_a_log[li],
                                c_dt_bias[li], c_norm_w[li], C_CHUNK, ts)
            ltot = past_len + ts
            bias_s, allowed = _rel_bias_sample(rel_bias, past_len, ts)
            heads = lambda a: a.reshape(bs, ts, D_HEADS, -1)
            d_out = _paged_rows_attn(page_table, li, q2.reshape(bs, ts, D_HEADS, 2, D_QK_DIM), None, bias_s, allowed,
                                     cache_d_k, cache_d_v, heads(k2_32), heads(v2_32), lam, subg,
                                     gp=gp, nm=2, tp=tp, lam_init=lam_init)
            outs_s.setdefault('c_rec', []).append(c_rec)
            outs_s.setdefault('c_conv', []).append(
                jnp.concatenate([buf, xc.reshape(bs, ts, ch)], axis=1)[:, ts:])
            outs_s.setdefault('d_k', []).append(k2_32.reshape(bs, ts, D_HEADS, -1))
            outs_s.setdefault('d_v', []).append(v2_32.reshape(bs, ts, D_HEADS, -1))
            mix_s = (c_out[:, :ts].reshape(n_s, hw), d_out[:, :ts].reshape(n_s, hw).astype(BF16))
            w_out = odd_w_out[li]

        x1p, h2p, lgp = _out_proj(mix_p[0], mix_p[1], xp, mp[2], mp[3], mp[4], False, tpb, tm_p, w_out,
                                  router_w[layer], router_b[layer])
        x1s, h2s, lgs = _out_proj(mix_s[0], mix_s[1], xs, ms[2], ms[3], ms[4], True, 1, n_s, w_out,
                                  router_w[layer], router_b[layer])
        moe_out = _moe(jnp.concatenate([h2p, h2s], axis=0), jnp.concatenate([lgp, lgs], axis=0), layer,
                       moe_w1, moe_b1, moe_w2, moe_b2, 256)
        xp = (x1p.reshape(bp, sp, d) + mp[5] * moe_out[:n_p].reshape(bp, sp, d)).reshape(n_p, d)
        xs = x1s + ms[5] * moe_out[n_p:]

    st = lambda name, src: jnp.stack(src[name])
    return (xp.reshape(bp, sp, d), xs.reshape(bs, ts, d),
            st('b_k', outs_p), st('b_v', outs_p), st('b_idx', outs_p), st('c_rec', outs_p), st('c_conv', outs_p),
            st('d_k', outs_p), st('d_v', outs_p),
            st('a_v', outs_s), st('b_k', outs_s), st('b_v', outs_s), st('b_idx', outs_s), st('c_rec', outs_s),
            st('c_conv', outs_s), st('d_k', outs_s), st('d_v', outs_s))
```

```python
import functools
import math

import numpy as np
import jax
import jax.numpy as jnp
from jax import lax
from jax.experimental import pallas as pl
from jax.experimental.pallas import tpu as pltpu

F32 = jnp.float32
BF16 = jnp.bfloat16
I32 = jnp.int32
HIGHEST = lax.Precision.HIGHEST

A_GROUPS = 4
A_CHUNK = 128
B_HEADS = 8
IDX_HEADS = 8
IDX_DIM = 64
TOPK_MAX = 256
C_HEADS = 4
C_DK = 128
C_CONV = 4
C_CHUNK = 64
D_HEADS = 8
D_QK_DIM = 32
N_BUCKETS = 32
MAX_DISTANCE = 128
TOP_K = 4
SWIGLU_LIMIT = 7.0
SWIGLU_ALPHA = 1.702
EPS = 1e-6

LANES = 128
SUBLANES = 8
VMEM_LIMIT_BYTES = 56 * 1024 * 1024

NEG = -1e30
INT_MIN = -2 ** 31


def _cparams(sem):
    return pltpu.CompilerParams(dimension_semantics=sem, vmem_limit_bytes=VMEM_LIMIT_BYTES)


def _dot(a, b):
    return jnp.dot(a.astype(BF16), b.astype(BF16), preferred_element_type=F32)


def _dot_nt(a, b):
    return lax.dot_general(a.astype(BF16), b.astype(BF16), (((1,), (1,)), ((), ())), preferred_element_type=F32)


def _dot_hi(a, b):
    return jnp.dot(a, b, preferred_element_type=F32, precision=HIGHEST)


def _dot_tn(a, b):
    return lax.dot_general(a.astype(BF16), b.astype(BF16), (((0,), (0,)), ((), ())), preferred_element_type=F32)


def _dot_x3(a, b):
    ah = a.astype(BF16)
    al = (a - ah.astype(F32)).astype(BF16)
    bh = b.astype(BF16)
    bl = (b - bh.astype(F32)).astype(BF16)
    dot = lambda x, y: jnp.dot(x, y, preferred_element_type=F32)
    return dot(ah, bh) + (dot(ah, bl) + dot(al, bh))


def _silu(x):
    return x * jax.nn.sigmoid(x)


def _gelu_tanh(x):
    return 0.5 * x * (1.0 + jnp.tanh(math.sqrt(2.0 / math.pi) * (x + 0.044715 * (x * x * x))))


def _group_rms(x, ones_ref, gsize):
    xx = x * x
    hi = xx.astype(BF16)
    lo = (xx - hi.astype(F32)).astype(BF16)
    e = ones_ref[...]
    ss = jnp.dot(hi, e, preferred_element_type=F32) + jnp.dot(lo, e, preferred_element_type=F32)
    return x * lax.rsqrt(ss * (1.0 / gsize) + EPS)


def _store_v_with_ones(vx_ref, v):
    tm, w = v.shape
    nh = vx_ref.shape[1] // LANES
    dv = w // nh
    tail = jnp.where(lax.broadcasted_iota(I32, (tm, LANES - dv), 1) == 0, 1.0, 0.0)
    for h in range(nh):
        vx_ref[:, h * LANES:(h + 1) * LANES] = jnp.concatenate([v[:, h * dv:(h + 1) * dv], tail], axis=1).astype(BF16)


def _block_ones(width, gsize):
    g = np.arange(width) // gsize
    return jnp.asarray(g[:, None] == g[None, :], BF16)


def _bucket_table():
    n = np.arange(MAX_DISTANCE)
    exact = N_BUCKETS // 2
    scaled = np.log(np.maximum(n, 1).astype(np.float32) / np.float32(exact)) / np.float32(math.log(MAX_DISTANCE / exact))
    large = np.minimum(exact + (scaled.astype(np.float32) * (N_BUCKETS - exact)).astype(np.int32), N_BUCKETS - 1)
    return np.where(n < exact, n, large).astype(np.int32)


def _bucket_of(dist):
    n = np.maximum(dist, 0)
    return np.where(n < MAX_DISTANCE, _bucket_table()[np.minimum(n, MAX_DISTANCE - 1)], N_BUCKETS - 1)


def _bias_by_bucket(rel_bias, bucket):
    out = jnp.zeros((rel_bias.shape[1],) + bucket.shape, F32)
    for b in np.unique(bucket):
        out = jnp.where(jnp.asarray(bucket == b)[None], rel_bias[b].astype(F32).reshape((-1,) + (1,) * bucket.ndim), out)
    return out


def _ada_kernel(c_ref, w_ref, b_ref, o_ref):
    o_ref[...] = _dot(_silu(c_ref[...]), w_ref[...]) + b_ref[...]


def _ada_mod(c_all, ada_w, ada_b):
    depth, d, n6 = ada_w.shape
    rows = c_all.shape[0]
    tn = 1536 if n6 % 1536 == 0 else n6
    return pl.pallas_call(
        _ada_kernel,
        grid=(depth, n6 // tn),
        in_specs=[pl.BlockSpec((rows, d), lambda l, j: (0, 0)),
                  pl.BlockSpec((None, d, tn), lambda l, j: (l, 0, j)),
                  pl.BlockSpec((None, 1, tn), lambda l, j: (l, 0, j))],
        out_specs=pl.BlockSpec((None, rows, tn), lambda l, j: (l, 0, j)),
        out_shape=jax.ShapeDtypeStruct((depth, rows, n6), F32),
        compiler_params=_cparams(("arbitrary", "arbitrary")),
        name="ada_mod",
    )(c_all, ada_w, ada_b.reshape(depth, 1, n6))


def _row_spec(tm, w):
    return pl.BlockSpec((tm, w), lambda i: (i, 0))


def _row_shape(n, w, dt):
    return jax.ShapeDtypeStruct((n, w), dt)


def _mod_specs(per_token, tm, d, tiles_per_batch):
    if per_token:
        return pl.BlockSpec((tm, d), lambda i: (i, 0))
    return pl.BlockSpec((None, 1, d), lambda i: (i // tiles_per_batch, 0, 0))


def _modulated_rms(x, sh, sc):
    xn = x * lax.rsqrt(jnp.mean(x * x, axis=-1, keepdims=True) + EPS)
    return xn * (1.0 + sc) + sh


def _even_in_kernel(x_ref, sh_ref, sc_ref, wm_ref, wt_ref, lng_ref, lnb_ref, msp_ref, bsp_ref,
                    qg_ref, kg_ref, ikg_ref, e_ref,
                    aout_ref, av_ref, q_ref, k32_ref, kbf_ref, v32_ref, vx_ref, iq_ref, ik32_ref, ikbf_ref, iw_ref):
    tm = x_ref.shape[0]
    aw = aout_ref.shape[1]
    gd = aw // A_GROUPS
    h = _modulated_rms(x_ref[...], sh_ref[...], sc_ref[...]).astype(BF16)

    def seg(i):
        return jnp.dot(h, wm_ref[:, i * aw:(i + 1) * aw], preferred_element_type=F32)

    au = _gelu_tanh(seg(0))
    av = _gelu_tanh(seg(1))
    avc = av - jnp.mean(av, axis=-1, keepdims=True)
    vn = avc * lax.rsqrt(jnp.mean(avc * avc, axis=-1, keepdims=True) + EPS) * lng_ref[...] + lnb_ref[...]
    av_ref[...] = vn
    vnb = vn.astype(BF16)
    cr = msp_ref.shape[1]
    for c in range(tm // cr):
        r0 = c * cr
        for g in range(A_GROUPS):
            mixed = jnp.dot(msp_ref[g], vnb[r0:r0 + cr, g * gd:(g + 1) * gd], preferred_element_type=F32)
            mixed = mixed + bsp_ref[:, g:g + 1]
            aout_ref[r0:r0 + cr, g * gd:(g + 1) * gd] = (au[r0:r0 + cr, g * gd:(g + 1) * gd] * mixed).astype(BF16)

    hd = aw // B_HEADS
    q = _group_rms(seg(2), e_ref, hd) * qg_ref[...]
    q_ref[...] = (q * (hd ** -0.5)).astype(BF16)
    k = _group_rms(seg(3), e_ref, hd) * kg_ref[...]
    k32_ref[...] = k
    kbf_ref[...] = k.astype(BF16)
    v = seg(4)
    v32_ref[...] = v
    _store_v_with_ones(vx_ref, v)
    iq_ref[...] = (seg(5) * (IDX_DIM ** -0.5)).astype(BF16)
    tail = jnp.dot(h, wt_ref[...], preferred_element_type=F32)
    ik = tail[:, :IDX_DIM]
    ik = ik * lax.rsqrt(jnp.mean(ik * ik, axis=-1, keepdims=True) + EPS) * ikg_ref[...]
    ik32_ref[...] = ik
    ikbf_ref[...] = ik.astype(BF16)
    iw_ref[...] = tail[:, IDX_DIM:IDX_DIM + IDX_HEADS] * (IDX_HEADS ** -0.5)


def _even_in(x, sh, sc, per_token, tiles_per_batch, tm, w_in, ln_g, ln_b, msp, bsp, q_g, k_g, ik_g):
    n, d = x.shape
    aw = d // 2
    wm = w_in[:, :6 * aw].astype(BF16)
    wt = jnp.pad(w_in[:, 6 * aw:], ((0, 0), (0, LANES - (IDX_DIM + IDX_HEADS)))).astype(BF16)
    full = lambda shape: pl.BlockSpec(shape, lambda i: (0,) * len(shape))
    row = lambda w: pl.BlockSpec((tm, w), lambda i: (i, 0))
    mod = _mod_specs(per_token, tm, d, tiles_per_batch)
    outs = [(aw, BF16), (aw, F32), (aw, BF16), (aw, F32), (aw, BF16), (aw, F32), (B_HEADS * LANES, BF16),
            (aw, BF16), (IDX_DIM, F32), (IDX_DIM, BF16), (IDX_HEADS, F32)]
    return pl.pallas_call(
        _even_in_kernel,
        grid=(n // tm,),
        in_specs=[row(d), mod, mod, full(wm.shape), full(wt.shape), full((1, aw)), full((1, aw)),
                  full(msp.shape), full(bsp.shape), full((1, aw)), full((1, aw)), full((1, IDX_DIM)), full((aw, aw))],
        out_specs=[_row_spec(tm, w) for w, _ in outs],
        out_shape=[_row_shape(n, w, dt) for w, dt in outs],
        compiler_params=_cparams(("arbitrary",)),
        name="even_in",
    )(x, sh, sc, wm, wt, ln_g.reshape(1, aw), ln_b.reshape(1, aw), msp, bsp,
      jnp.tile(q_g, B_HEADS).reshape(1, aw), jnp.tile(k_g, B_HEADS).reshape(1, aw), ik_g.reshape(1, IDX_DIM),
      _block_ones(aw, aw // B_HEADS))


def _odd_in_kernel(x_ref, sh_ref, sc_ref, wm_ref, wt_ref, qg_ref, kg_ref, e_ref,
                   xc_ref, cz_ref, tail_ref, q_ref, k32_ref, kbf_ref, v32_ref, vx_ref):
    hw = cz_ref.shape[1]
    h = _modulated_rms(x_ref[...], sh_ref[...], sc_ref[...]).astype(BF16)

    def seg(i):
        return jnp.dot(h, wm_ref[:, i * hw:(i + 1) * hw], preferred_element_type=F32)

    for i in range(3):
        xc_ref[:, i * hw:(i + 1) * hw] = seg(i)
    cz_ref[...] = seg(3)
    tail_ref[...] = jnp.dot(h, wt_ref[...], preferred_element_type=F32)
    q = _group_rms(seg(4), e_ref, D_QK_DIM) * qg_ref[...]
    q_ref[...] = (q * (D_QK_DIM ** -0.5)).astype(BF16)
    k = _group_rms(seg(5), e_ref, D_QK_DIM) * kg_ref[...]
    k32_ref[...] = k
    kbf_ref[...] = k.astype(BF16)
    v = seg(6)
    v32_ref[...] = v
    _store_v_with_ones(vx_ref, v)


def _odd_in(x, sh, sc, per_token, tiles_per_batch, tm, w_in, q_g, k_g):
    n, d = x.shape
    hw = d // 2
    ng = 2 * C_HEADS
    wm = jnp.concatenate([w_in[:, :4 * hw], w_in[:, 4 * hw + ng:]], axis=1).astype(BF16)
    wt = jnp.pad(w_in[:, 4 * hw:4 * hw + ng], ((0, 0), (0, LANES - ng))).astype(BF16)
    full = lambda shape: pl.BlockSpec(shape, lambda i: (0,) * len(shape))
    row = lambda w: pl.BlockSpec((tm, w), lambda i: (i, 0))
    mod = _mod_specs(per_token, tm, d, tiles_per_batch)
    outs = [(3 * hw, F32), (hw, F32), (LANES, F32), (hw, BF16), (hw, F32), (hw, BF16), (hw, F32),
            (D_HEADS * LANES, BF16)]
    reps = hw // D_QK_DIM
    return pl.pallas_call(
        _odd_in_kernel,
        grid=(n // tm,),
        in_specs=[row(d), mod, mod, full(wm.shape), full(wt.shape), full((1, hw)), full((1, hw)), full((hw, hw))],
        out_specs=[_row_spec(tm, w) for w, _ in outs],
        out_shape=[_row_shape(n, w, dt) for w, dt in outs],
        compiler_params=_cparams(("arbitrary",)),
        name="odd_in",
    )(x, sh, sc, wm, wt, jnp.tile(q_g, reps).reshape(1, hw), jnp.tile(k_g, reps).reshape(1, hw),
      _block_ones(hw, D_QK_DIM))


def _out_proj_kernel(a_ref, b_ref, x_ref, g1_ref, sh_ref, sc_ref, w_ref, rw_ref, rb_ref, x1_ref, h2_ref, lg_ref):
    hw = a_ref.shape[1]
    y = jnp.dot(a_ref[...], w_ref[:hw, :], preferred_element_type=F32)
    y = y + jnp.dot(b_ref[...], w_ref[hw:, :], preferred_element_type=F32)
    x1 = x_ref[...] + g1_ref[...] * y
    x1_ref[...] = x1
    h2 = _modulated_rms(x1, sh_ref[...], sc_ref[...])
    h2_ref[...] = h2
    lg_ref[...] = _dot(h2, rw_ref[...]) + rb_ref[...]


def _out_proj(a, b, x, g1, sh, sc, per_token, tiles_per_batch, tm, w_out, router_w, router_b):
    n, d = x.shape
    hw = a.shape[1]
    ne = router_w.shape[1]
    rw = jnp.pad(router_w, ((0, 0), (0, LANES - ne)))
    rb = jnp.pad(router_b, (0, LANES - ne)).reshape(1, LANES)
    full = lambda shape: pl.BlockSpec(shape, lambda i: (0,) * len(shape))
    row = lambda w: pl.BlockSpec((tm, w), lambda i: (i, 0))
    mod = _mod_specs(per_token, tm, d, tiles_per_batch)
    x1, h2, lg = pl.pallas_call(
        _out_proj_kernel,
        grid=(n // tm,),
        in_specs=[row(hw), row(hw), row(d), mod, mod, mod, full((d, d)), full((d, LANES)), full((1, LANES))],
        out_specs=[row(d), row(d), row(LANES)],
        out_shape=[jax.ShapeDtypeStruct((n, d), F32), jax.ShapeDtypeStruct((n, d), F32),
                   jax.ShapeDtypeStruct((n, LANES), F32)],
        compiler_params=_cparams(("arbitrary",)),
        name="out_proj",
    )(a, b, x, g1, sh, sc, w_out.astype(BF16), rw, rb)
    return x1, h2, lg


def _idx_scores_kernel(iq_ref, iw_ref, ik_ref, o_ref, *, tq, tk):
    i = pl.program_id(1)
    j = pl.program_id(2)

    @pl.when(j * tk <= i * tq + tq - 1)
    def _():
        ik = ik_ref[...]
        acc = jnp.zeros((tq, tk), F32)
        for h in range(IDX_HEADS):
            d = _dot_nt(iq_ref[:, h * IDX_DIM:(h + 1) * IDX_DIM], ik)
            acc = acc + iw_ref[:, h:h + 1] * jnp.maximum(d, 0.0)
        o_ref[...] = acc

    @pl.when(j * tk > i * tq + tq - 1)
    def _():
        o_ref[...] = jnp.zeros((tq, tk), F32)


def _idx_scores_prompt(iq, iw, ik, tq, tk):
    bn, s, _ = iq.shape
    last = lambda i: (i * tq + tq - 1) // tk
    return pl.pallas_call(
        functools.partial(_idx_scores_kernel, tq=tq, tk=tk),
        grid=(bn, s // tq, s // tk),
        in_specs=[pl.BlockSpec((None, tq, IDX_HEADS * IDX_DIM), lambda b, i, j: (b, i, 0)),
                  pl.BlockSpec((None, tq, IDX_HEADS), lambda b, i, j: (b, i, 0)),
                  pl.BlockSpec((None, tk, IDX_DIM), lambda b, i, j: (b, jnp.minimum(j, last(i)), 0))],
        out_specs=pl.BlockSpec((None, tq, tk), lambda b, i, j: (b, i, j)),
        out_shape=jax.ShapeDtypeStruct((bn, s, s), F32),
        compiler_params=_cparams(("arbitrary", "arbitrary", "arbitrary")),
        name="idx_scores_prompt",
    )(iq, iw, ik)


def _select_kernel(nch_ref, s_ref, lim_ref, m_ref, key_ref, *, ksel, kc):
    rows, width = s_ref.shape
    nch = nch_ref[pl.program_id(0)]
    lim = lim_ref[...]
    fold = kc // LANES
    col_bits = max(1, int(width - 1).bit_length())

    def cols(c):
        return c * kc + lax.broadcasted_iota(I32, (rows, kc), 1)

    def fill(c, carry):
        off = pl.multiple_of(c * kc, kc)
        bits = lax.bitcast_convert_type(s_ref[:, pl.ds(off, kc)] + 0.0, I32)
        key = jnp.where(bits < 0, bits ^ jnp.int32(0x7FFFFFFF), bits)
        key_ref[:, pl.ds(off, kc)] = jnp.where(cols(c) <= lim, key, jnp.int32(INT_MIN))
        return carry

    lax.fori_loop(0, nch, fill, 0)

    def count(pred):
        def body(c, acc):
            off = pl.multiple_of(c * kc, kc)
            hit = jnp.where(pred(c, key_ref[:, pl.ds(off, kc)]), 1.0, 0.0)
            part = hit[:, :LANES]
            for f in range(1, fold):
                part = part + hit[:, f * LANES:(f + 1) * LANES]
            return acc + part
        acc = lax.fori_loop(0, nch, body, jnp.zeros((rows, LANES), F32))
        return jnp.sum(acc, axis=1, keepdims=True)

    def thr_bit(b, carry):
        prefix, n_ge = carry
        cand = prefix + (jnp.int32(1) << (31 - b))
        cnt = count(lambda c, key: key >= cand)
        take = cnt >= ksel
        return jnp.where(take, cand, prefix), jnp.where(take, cnt, n_ge)

    thr, n_ge = lax.fori_loop(0, 32, thr_bit, (jnp.full((rows, 1), INT_MIN, I32),
                                               jnp.full((rows, 1), 1.0, F32) * (nch * kc).astype(F32)))

    def tie_break():
        need = ksel - count(lambda c, key: key > thr)

        def tie_bit(b, pos):
            cand = pos + (jnp.int32(1) << (col_bits - 1 - b))
            cnt = count(lambda c, key: (key == thr) & (cols(c) < cand))
            return jnp.where(cnt < need, cand, pos)

        return lax.fori_loop(0, col_bits, tie_bit, jnp.zeros((rows, 1), I32))

    surplus = jnp.where((n_ge > ksel) & (thr > jnp.int32(INT_MIN)), 1.0, 0.0)
    last = lax.cond(jnp.max(surplus) > 0.0, tie_break, lambda: jnp.full((rows, 1), width, I32))

    def emit(c, carry):
        off = pl.multiple_of(c * kc, kc)
        key = key_ref[:, pl.ds(off, kc)]
        col = cols(c)
        sel = ((key > thr) | ((key == thr) & (col <= last))) & (col <= lim)
        m_ref[:, pl.ds(off, kc)] = jnp.where(sel, 0.0, NEG).astype(m_ref.dtype)
        return carry

    lax.fori_loop(0, nch, emit, 0)

    def blank(c, carry):
        off = pl.multiple_of(c * kc, kc)
        m_ref[:, pl.ds(off, kc)] = jnp.full((rows, kc), NEG, m_ref.dtype)
        return carry

    lax.fori_loop(nch, width // kc, blank, 0)


def _select_mask(scores, lim, nch, ksel, kc, out_dtype):
    r, width = scores.shape
    tr = 128
    return pl.pallas_call(
        functools.partial(_select_kernel, ksel=ksel, kc=kc),
        grid_spec=pltpu.PrefetchScalarGridSpec(
            num_scalar_prefetch=1,
            grid=(r // tr,),
            in_specs=[pl.BlockSpec((tr, width), lambda i, n: (i, 0)),
                      pl.BlockSpec((tr, 1), lambda i, n: (i, 0))],
            out_specs=pl.BlockSpec((tr, width), lambda i, n: (i, 0)),
            scratch_shapes=[pltpu.VMEM((tr, width), I32)]),
        out_shape=jax.ShapeDtypeStruct((r, width), out_dtype),
        compiler_params=_cparams(("arbitrary",)),
        name="select_mask",
    )(nch, scores, lim)


def _prompt_attn_kernel(*refs, nh, nm, dqk, dv, tq, has_mask, lam_init):
    refs = list(refs)
    q_ref, k_ref, vx_ref, tbl_ref = refs[:4]
    pos = 4
    madd_ref = None
    if has_mask:
        madd_ref = refs[pos]
        pos += 1
    lam_ref = subg_ref = None
    if nm == 2:
        lam_ref, subg_ref = refs[pos], refs[pos + 1]
        pos += 2
    o_ref, mx_scr, acc_scr, qz_scr = refs[pos:pos + 4]
    ma_scr = refs[pos + 4] if has_mask else None
    nu = nh * nm
    nf = tq // LANES
    gu = LANES // dqk
    qi = pl.program_id(1)
    mx_scr[...] = jnp.full(mx_scr.shape, -jnp.inf, F32)
    acc_scr[...] = jnp.zeros(acc_scr.shape, F32)

    lane = lax.broadcasted_iota(I32, (tq, LANES), 1)
    for g in range(nu // gu):
        qt = q_ref[:, g * LANES:(g + 1) * LANES]
        for i in range(gu):
            keep = (lane >= i * dqk) & (lane < (i + 1) * dqk)
            qz_scr[g, i * tq:(i + 1) * tq, :] = jnp.where(keep, qt, jnp.zeros_like(qt))

    def sweep(j, back, second):
        koff = pl.multiple_of(j * tq, tq)
        if has_mask:
            ma_scr[...] = madd_ref[:, pl.ds(koff, tq)].astype(F32)
        for g in range(nu // gu):
            stacked = _dot_nt(qz_scr[g], k_ref[pl.ds(koff, tq), g * LANES:(g + 1) * LANES])
            ps = []
            for i in range(gu):
                u = g * gu + i
                s = stacked[i * tq:(i + 1) * tq]
                if back is not None:
                    s = s + tbl_ref[u // nm, back]
                if has_mask:
                    s = s + ma_scr[...]
                if not second:
                    r = s[:, :LANES]
                    for f in range(1, nf):
                        r = jnp.maximum(r, s[:, f * LANES:(f + 1) * LANES])
                    mx_scr[u] = jnp.maximum(mx_scr[u], r)
                else:
                    mx = mx_scr[u]
                    p = jnp.concatenate([jnp.exp(s[:, f * LANES:(f + 1) * LANES] - mx) for f in range(nf)], axis=1)
                    ps.append(p.astype(BF16))
            if second:
                for i0 in range(0, gu, nm):
                    u0 = g * gu + i0
                    h = u0 // nm
                    stack = ps[i0] if nm == 1 else jnp.concatenate(ps[i0:i0 + nm], axis=0)
                    pv = jnp.dot(stack, vx_ref[pl.ds(koff, tq), h * LANES:(h + 1) * LANES], preferred_element_type=F32)
                    for m in range(nm):
                        acc_scr[u0 + m] += pv[m * tq:(m + 1) * tq]

    def all_chunks(second):
        def far(j, carry):
            sweep(j, None, second)
            return carry

        lax.fori_loop(0, jnp.maximum(qi - 1, 0), far, 0)

        @pl.when(qi >= 1)
        def _():
            sweep(qi - 1, 1, second)

        sweep(qi, 0, second)

    all_chunks(False)
    for u in range(nu):
        mx_scr[u] = jnp.broadcast_to(jnp.max(mx_scr[u], axis=-1, keepdims=True), (tq, LANES))
    all_chunks(True)

    def normalised(u):
        a = acc_scr[u]
        return a[:, :dv] / a[:, dv:dv + 1]

    for h in range(nh):
        if nm == 1:
            o_ref[:, h * dv:(h + 1) * dv] = normalised(h).astype(o_ref.dtype)
        else:
            att = normalised(2 * h) - lam_ref[0] * normalised(2 * h + 1)
            att = att * lax.rsqrt(jnp.mean(att * att, axis=-1, keepdims=True) + EPS)
            o_ref[:, h * dv:(h + 1) * dv] = (att * subg_ref[...] * (1.0 - lam_init)).astype(o_ref.dtype)


def _bias_tables(rel_bias, tq):
    far = rel_bias[N_BUCKETS - 1].astype(F32)[:, None, None]
    r = np.arange(tq)[:, None]
    c = np.arange(tq)[None, :]
    tabs = []
    for d in range(2):
        dist = d * tq + r - c
        vals = _bias_by_bucket(rel_bias, _bucket_of(dist)) - far
        tabs.append(jnp.where(jnp.asarray(dist >= 0)[None], vals, NEG))
    return jnp.stack(tabs, axis=1)


def _prompt_attn(q, k, vx, tbl, madd, lam, subg, *, nh, nm, dqk, dv, tq, lam_init):
    bn, s, w = q.shape
    assert MAX_DISTANCE <= tq and s % tq == 0 and tq % LANES == 0 and dv < LANES
    has_mask = madd is not None
    once = lambda width: pl.BlockSpec((None, s, width), lambda b, i: (b, 0, 0), pipeline_mode=pl.Buffered(1))
    in_specs = [pl.BlockSpec((None, tq, w), lambda b, i: (b, i, 0)), once(w), once(nh * LANES),
                pl.BlockSpec(tbl.shape, lambda b, i: (0, 0, 0, 0), pipeline_mode=pl.Buffered(1))]
    args = [q, k, vx, tbl]
    if has_mask:
        in_specs.append(pl.BlockSpec((None, tq, s), lambda b, i: (b, i, 0)))
        args.append(madd)
    if nm == 2:
        in_specs += [pl.BlockSpec(memory_space=pltpu.SMEM), pl.BlockSpec((1, dv), lambda b, i: (0, 0))]
        args += [lam, subg]
    return pl.pallas_call(
        functools.partial(_prompt_attn_kernel, nh=nh, nm=nm, dqk=dqk, dv=dv, tq=tq, has_mask=has_mask, lam_init=lam_init),
        grid=(bn, s // tq),
        in_specs=in_specs,
        out_specs=pl.BlockSpec((None, tq, nh * dv), lambda b, i: (b, i, 0)),
        out_shape=jax.ShapeDtypeStruct((bn, s, nh * dv), BF16),
        scratch_shapes=[pltpu.VMEM((nh * nm, tq, LANES), F32), pltpu.VMEM((nh * nm, tq, LANES), F32),
                        pltpu.VMEM((nh * nm * dqk // LANES, LANES // dqk * tq, LANES), BF16)]
                       + ([pltpu.VMEM((tq, tq), F32)] if has_mask else []),
        compiler_params=_cparams(("arbitrary", "arbitrary")),
        name="prompt_attn_dsa" if nm == 1 else "prompt_attn_diff",
    )(*args)


def _paged_scores_kernel(pt_ref, iq_ref, iw_ref, *refs, gp, page):
    pages = refs[:gp]
    new_ref, o_ref = refs[gp], refs[gp + 1]
    s_id = pl.program_id(1)
    last = pl.num_programs(1) - 1
    rows = iq_ref.shape[0]
    tp = rows // IDX_HEADS

    def score(ik_t):
        d = jnp.maximum(_dot(iq_ref[...], ik_t), 0.0) * iw_ref[...]
        acc = d[:tp]
        for h in range(1, IDX_HEADS):
            acc = acc + d[h * tp:(h + 1) * tp]
        return acc

    @pl.when(s_id < last)
    def _():
        for g in range(gp):
            o_ref[:, g * page:(g + 1) * page] = score(pages[g][...])

    @pl.when(s_id == last)
    def _():
        o_ref[...] = jnp.zeros(o_ref.shape, F32)
        o_ref[:, :page] = score(new_ref[...])


def _paged_scores(page_table, li, iq_rows, iw_rows, cache_idx, ik_new, gp):
    bn, rows, _ = iq_rows.shape
    tp = rows // IDX_HEADS
    page = cache_idx.shape[2]
    npages = page_table.shape[1]
    nsteps = npages // gp + 1
    width = nsteps * gp * page

    def page_map(g):
        return lambda b, s, pt: (li, pt[b, jnp.minimum(s * gp + g, npages - 1)], 0, 0)

    return pl.pallas_call(
        functools.partial(_paged_scores_kernel, gp=gp, page=page),
        grid_spec=pltpu.PrefetchScalarGridSpec(
            num_scalar_prefetch=1,
            grid=(bn, nsteps),
            in_specs=[pl.BlockSpec((None, rows, IDX_DIM), lambda b, s, pt: (b, 0, 0)),
                      pl.BlockSpec((None, rows, 1), lambda b, s, pt: (b, 0, 0))]
                     + [pl.BlockSpec((None, None, IDX_DIM, page), page_map(g)) for g in range(gp)]
                     + [pl.BlockSpec((None, IDX_DIM, page), lambda b, s, pt: (b, 0, 0))],
            out_specs=pl.BlockSpec((None, tp, gp * page), lambda b, s, pt: (b, 0, s))),
        out_shape=jax.ShapeDtypeStruct((bn, tp, width), F32),
        compiler_params=_cparams(("arbitrary", "arbitrary")),
        name="paged_idx_scores",
    )(page_table, iq_rows, iw_rows, *([jnp.transpose(cache_idx, (0, 1, 3, 2))] * gp), jnp.transpose(ik_new, (0, 2, 1)))


def _paged_attn_kernel(pt_ref, q_ref, am_ref, *refs, gp, nh, nm, tp, lam_init):
    kpages = refs[:gp]
    vpages = refs[gp:2 * gp]
    knew_ref, vnew_ref = refs[2 * gp], refs[2 * gp + 1]
    pos = 2 * gp + 2
    lam_ref = subg_ref = None
    if nm == 2:
        lam_ref, subg_ref = refs[pos], refs[pos + 1]
        pos += 2
    o_ref, m_scr, l_scr, acc_scr = refs[pos:pos + 4]
    _, hd, page = knew_ref.shape
    dv = vnew_ref.shape[1]
    s_id = pl.program_id(1)
    last = pl.num_programs(1) - 1

    @pl.when(s_id == 0)
    def _():
        m_scr[...] = jnp.full(m_scr.shape, -jnp.inf, F32)
        l_scr[...] = jnp.zeros(l_scr.shape, F32)
        acc_scr[...] = jnp.zeros(acc_scr.shape, F32)

    def blocks(kvs):
        ss = [_dot(q_ref[...], kp[...].reshape(nh * hd, page)) + am_ref[:, g * page:(g + 1) * page]
              for g, (kp, _) in enumerate(kvs)]
        mx = ss[0]
        for s in ss[1:]:
            mx = jnp.maximum(mx, s)
        m_prev = m_scr[...]
        m_new = jnp.maximum(m_prev, jnp.max(mx, axis=-1, keepdims=True))
        alpha = jnp.exp(m_prev - m_new)
        ps = [jnp.exp(s - m_new) for s in ss]
        psum = ps[0]
        for p in ps[1:]:
            psum = psum + p
        pv = [_dot_nt(p, vp[...].reshape(nh * dv, page)) for p, (_, vp) in zip(ps, kvs)]
        tot = pv[0]
        for x in pv[1:]:
            tot = tot + x
        l_scr[...] = alpha * l_scr[...] + jnp.sum(psum, axis=-1, keepdims=True)
        acc_scr[...] = alpha * acc_scr[...] + tot
        m_scr[...] = m_new

    @pl.when(s_id < last)
    def _():
        blocks(list(zip(kpages, vpages)))

    @pl.when(s_id == last)
    def _():
        blocks([(knew_ref, vnew_ref)])
        accn = acc_scr[...] / l_scr[...]
        for h in range(nh):
            cols = slice(h * dv, (h + 1) * dv)
            if nm == 1:
                o_ref[:, cols] = accn[h * tp:(h + 1) * tp, cols]
            else:
                att = accn[2 * h * tp:(2 * h + 1) * tp, cols] - lam_ref[0] * accn[(2 * h + 1) * tp:(2 * h + 2) * tp, cols]
                att = att * lax.rsqrt(jnp.mean(att * att, axis=-1, keepdims=True) + EPS)
                o_ref[:, cols] = att * subg_ref[...] * (1.0 - lam_init)


def _pages_first(cache):
    return jnp.transpose(cache, (0, 1, 3, 4, 2))


def _head_rows(x, tp):
    bn, t, g, w = x.shape
    xp = jnp.pad(x, ((0, 0), (0, tp - t), (0, 0), (0, 0)))
    eye = jnp.eye(g, dtype=x.dtype)
    return jnp.einsum('btgw,gk->bgtkw', xp, eye).reshape(bn, g * tp, g * w)


def _paged_attn(page_table, li, q_rows, addmask, cache_k, cache_v, k_new, v_new, lam, subg, *, gp, nh, nm, tp, lam_init):
    bn, rows, w = q_rows.shape
    page, hd, dv = cache_k.shape[2], cache_k.shape[4], cache_v.shape[4]
    t = k_new.shape[1]
    npages = page_table.shape[1]
    nsteps = npages // gp + 1

    def page_map(g):
        return lambda b, s, pt: (li, pt[b, jnp.minimum(s * gp + g, npages - 1)], 0, 0, 0)

    new_t = lambda a: jnp.transpose(jnp.pad(a, ((0, 0), (0, page - t), (0, 0), (0, 0))), (0, 2, 3, 1))
    in_specs = ([pl.BlockSpec((None, rows, w), lambda b, s, pt: (b, 0, 0)),
                 pl.BlockSpec((None, rows, gp * page), lambda b, s, pt: (b, 0, s))]
                + [pl.BlockSpec((None, None, nh, hd, page), page_map(g)) for g in range(gp)]
                + [pl.BlockSpec((None, None, nh, dv, page), page_map(g)) for g in range(gp)]
                + [pl.BlockSpec((None, nh, hd, page), lambda b, s, pt: (b, 0, 0, 0)),
                   pl.BlockSpec((None, nh, dv, page), lambda b, s, pt: (b, 0, 0, 0))])
    args = [q_rows, addmask] + [_pages_first(cache_k)] * gp + [_pages_first(cache_v)] * gp + [new_t(k_new), new_t(v_new)]
    if nm == 2:
        in_specs += [pl.BlockSpec(memory_space=pltpu.SMEM), pl.BlockSpec((1, dv), lambda b, s, pt: (0, 0))]
        args += [lam, subg]
    return pl.pallas_call(
        functools.partial(_paged_attn_kernel, gp=gp, nh=nh, nm=nm, tp=tp, lam_init=lam_init),
        grid_spec=pltpu.PrefetchScalarGridSpec(
            num_scalar_prefetch=1,
            grid=(bn, nsteps),
            in_specs=in_specs,
            out_specs=pl.BlockSpec((None, tp, nh * dv), lambda b, s, pt: (b, 0, 0)),
            scratch_shapes=[pltpu.VMEM((rows, 1), F32), pltpu.VMEM((rows, 1), F32), pltpu.VMEM((rows, nh * dv), F32)]),
        out_shape=jax.ShapeDtypeStruct((bn, tp, nh * dv), F32),
        compiler_params=_cparams(("arbitrary", "arbitrary")),
        name="paged_attn_dsa" if nm == 1 else "paged_attn_diff",
    )(page_table, *args)


def _gdn_kernel(xc_ref, prev_ref, buf_ref, cz_ref, tail_ref, s0_ref, cw_ref, alog_ref, dtb_ref, nw_ref,
                o_ref, sfin_ref, s_scr, *, n_valid):
    ci = pl.program_id(1)
    bg, c, ch = xc_ref.shape
    dk = C_DK
    dv = (ch - 2 * C_HEADS * dk) // C_HEADS

    @pl.when(ci == 0)
    def _():
        s_scr[...] = s0_ref[...]

    row = lax.broadcasted_iota(I32, (c, c), 0)
    colm = lax.broadcasted_iota(I32, (c, c), 1)
    incl = row >= colm
    tril = incl.astype(F32)
    eye = (row == colm).astype(F32)

    units = []
    for b in range(bg):
        hist = jnp.where(ci == 0, buf_ref[b], prev_ref[b])
        nh_rows = hist.shape[0]
        xfull = jnp.concatenate([hist, xc_ref[b]], axis=0)
        y = cw_ref[C_CONV - 1:C_CONV, :] * xfull[nh_rows:, :]
        for j in range(1, C_CONV):
            y = y + cw_ref[C_CONV - 1 - j:C_CONV - j, :] * pltpu.roll(xfull, j, 0)[nh_rows:, :]
        y = _silu(y)

        tail = tail_ref[b]
        beta_all = jax.nn.sigmoid(tail)
        sp_in = tail + dtb_ref[...]
        g_all = -jnp.exp(alog_ref[...]) * (jnp.maximum(sp_in, 0.0) + jnp.log1p(jnp.exp(-jnp.abs(sp_in))))
        if n_valid < c:
            valid = lax.broadcasted_iota(I32, tail.shape, 0) < n_valid
            beta_all = jnp.where(valid, beta_all, 0.0)
            g_all = jnp.where(valid, g_all, 0.0)
        gc_all = _dot_hi(tril, g_all)
        gc_rows = jnp.transpose(gc_all)

        for h in range(C_HEADS):
            q = y[:, h * dk:(h + 1) * dk]
            k = y[:, (C_HEADS + h) * dk:(C_HEADS + h + 1) * dk]
            v = y[:, 2 * C_HEADS * dk + h * dv:2 * C_HEADS * dk + (h + 1) * dv]
            q = q * lax.rsqrt(jnp.sum(q * q, axis=-1, keepdims=True) + EPS) * (dk ** -0.5)
            k = k * lax.rsqrt(jnp.sum(k * k, axis=-1, keepdims=True) + EPS)
            beta = beta_all[:, h:h + 1]
            gc = gc_all[:, C_HEADS + h:C_HEADS + h + 1]
            decay = jnp.exp(jnp.where(incl, gc - gc_rows[C_HEADS + h:C_HEADS + h + 1, :], -jnp.inf))
            units.append(dict(b=b, h=h, q=q, k=k, kb=k * beta, vb=v * beta, gc=gc, decay=decay))

    def stage(fn):
        return [fn(un) for un in units]

    def put(name, vals):
        for un, val in zip(units, vals):
            un[name] = val

    put('a', stage(lambda un: _dot_nt(un['kb'], un['k']) * un['decay'] * (1.0 - eye)))
    put('inv', stage(lambda un: eye - un['a']))
    put('pw', stage(lambda un: un['a']))
    for _ in range(max(0, int(c - 1).bit_length() - 1)):
        put('pw', stage(lambda un: _dot_x3(un['pw'], un['pw'])))
        put('inv', stage(lambda un: un['inv'] + _dot_x3(un['inv'], un['pw'])))
    put('u', stage(lambda un: _dot_x3(un['inv'], un['vb'])))
    put('w', stage(lambda un: _dot_x3(un['inv'], un['kb'] * jnp.exp(un['gc']))))
    put('s', stage(lambda un: s_scr[un['b'], un['h']]))
    put('v_new', stage(lambda un: un['u'] - _dot(un['w'], un['s'])))
    put('intra', stage(lambda un: _dot_nt(un['q'], un['k']) * un['decay']))
    put('o', stage(lambda un: _dot(un['q'] * jnp.exp(un['gc']), un['s']) + _dot(un['intra'], un['v_new'])))
    for un in units:
        gl = un['gc'][c - 1:c, :]
        s_scr[un['b'], un['h']] = un['s'] * jnp.exp(gl) + _dot_tn(un['k'] * jnp.exp(gl - un['gc']), un['v_new'])
    for un in units:
        b, h, o = un['b'], un['h'], un['o']
        o = o * lax.rsqrt(jnp.mean(o * o, axis=-1, keepdims=True) + EPS) * nw_ref[...]
        o_ref[b, :, h * dv:(h + 1) * dv] = (o * _silu(cz_ref[b, :, h * dv:(h + 1) * dv])).astype(o_ref.dtype)

    @pl.when(ci == pl.num_programs(1) - 1)
    def _():
        sfin_ref[...] = s_scr[...]


def _gdn(xc, buf, cz, tail, s0, conv_w, a_log, dt_bias, norm_w, chunk, n_valid):
    bn, t, ch = xc.shape
    hw = cz.shape[2]
    nck = t // chunk
    dvh = hw // C_HEADS
    hist = SUBLANES
    per_chunk = chunk // hist
    bg = 2 if bn % 2 == 0 else 1
    gate_lanes = lambda p: jnp.pad(p.astype(F32), (C_HEADS, LANES - 2 * C_HEADS)).reshape(1, LANES)
    return pl.pallas_call(
        functools.partial(_gdn_kernel, n_valid=n_valid),
        grid=(bn // bg, nck),
        in_specs=[pl.BlockSpec((bg, chunk, ch), lambda b, i: (b, i, 0)),
                  pl.BlockSpec((bg, hist, ch), lambda b, i: (b, jnp.maximum(i * per_chunk - 1, 0), 0)),
                  pl.BlockSpec((bg, hist, ch), lambda b, i: (b, 0, 0)),
                  pl.BlockSpec((bg, chunk, hw), lambda b, i: (b, i, 0)),
                  pl.BlockSpec((bg, chunk, LANES), lambda b, i: (b, i, 0)),
                  pl.BlockSpec((bg, C_HEADS, C_DK, dvh), lambda b, i: (b, 0, 0, 0)),
                  pl.BlockSpec((C_CONV, ch), lambda b, i: (0, 0)),
                  pl.BlockSpec((1, LANES), lambda b, i: (0, 0)),
                  pl.BlockSpec((1, LANES), lambda b, i: (0, 0)),
                  pl.BlockSpec((1, dvh), lambda b, i: (0, 0))],
        out_specs=[pl.BlockSpec((bg, chunk, hw), lambda b, i: (b, i, 0)),
                   pl.BlockSpec((bg, C_HEADS, C_DK, dvh), lambda b, i: (b, 0, 0, 0))],
        out_shape=[jax.ShapeDtypeStruct((bn, t, hw), BF16), jax.ShapeDtypeStruct((bn, C_HEADS, C_DK, dvh), F32)],
        scratch_shapes=[pltpu.VMEM((bg, C_HEADS, C_DK, dvh), F32)],
        compiler_params=_cparams(("arbitrary", "arbitrary")),
        name="gated_delta",
    )(xc, xc, buf, cz, tail, s0, conv_w, gate_lanes(a_log), gate_lanes(dt_bias), norm_w.reshape(1, dvh))


def _moe_kernel(be_ref, x_ref, gate_ref, w1_ref, b1_ref, w2_ref, b2_ref, o_ref, w1_scr, w2_scr):
    i = pl.program_id(0)
    dff = w2_ref.shape[0]
    changed = jnp.logical_or(i == 0, be_ref[i] != be_ref[jnp.maximum(i - 1, 0)])

    @pl.when(changed)
    def _():
        w1_scr[...] = w1_ref[...].astype(BF16)
        w2_scr[...] = w2_ref[...].astype(BF16)

    hmid = jnp.dot(x_ref[...].astype(BF16), w1_scr[...], preferred_element_type=F32) + b1_ref[...]
    gate = jnp.minimum(hmid[:, :dff], SWIGLU_LIMIT)
    up = jnp.clip(hmid[:, dff:], -SWIGLU_LIMIT, SWIGLU_LIMIT)
    act = (up + 1.0) * gate * jax.nn.sigmoid(SWIGLU_ALPHA * gate)
    y = jnp.dot(act.astype(BF16), w2_scr[...], preferred_element_type=F32) + b2_ref[...]
    o_ref[...] = y * gate_ref[...]


def _moe_ffn(xs, slot_gate, block_e, layer, w1, b1, w2, b2, bm):
    n_slots, d = xs.shape
    _, ne, _, d2 = w1.shape
    dff = w2.shape[2]
    return pl.pallas_call(
        _moe_kernel,
        grid_spec=pltpu.PrefetchScalarGridSpec(
            num_scalar_prefetch=1,
            grid=(n_slots // bm,),
            in_specs=[pl.BlockSpec((bm, d), lambda i, be: (i, 0)),
                      pl.BlockSpec((bm, 1), lambda i, be: (i, 0)),
                      pl.BlockSpec((None, None, d, d2), lambda i, be: (layer, be[i], 0, 0)),
                      pl.BlockSpec((None, None, 1, d2), lambda i, be: (layer, be[i], 0, 0)),
                      pl.BlockSpec((None, None, dff, d), lambda i, be: (layer, be[i], 0, 0)),
                      pl.BlockSpec((None, None, 1, d), lambda i, be: (layer, be[i], 0, 0))],
            out_specs=pl.BlockSpec((bm, d), lambda i, be: (i, 0)),
            scratch_shapes=[pltpu.VMEM((d, d2), BF16), pltpu.VMEM((dff, d), BF16)]),
        out_shape=jax.ShapeDtypeStruct((n_slots, d), F32),
        compiler_params=_cparams(("arbitrary",)),
        name="moe_ffn",
    )(block_e, xs, slot_gate.reshape(n_slots, 1), w1, b1.reshape(b1.shape[0], ne, 1, d2), w2, b2.reshape(b2.shape[0], ne, 1, d))


def _route_kernel(lg_ref, e_ref, g_ref, r_ref, cnt_ref, run_scr, *, ne):
    i = pl.program_id(0)
    tm = lg_ref.shape[0]

    @pl.when(i == 0)
    def _():
        run_scr[...] = jnp.zeros(run_scr.shape, F32)

    lane = lax.broadcasted_iota(I32, (tm, LANES), 1)
    lane_f = lane.astype(F32)
    x = jnp.where(lane < ne, lg_ref[...], -jnp.inf)
    vals, hots = [], []
    e_out = jnp.zeros((tm, LANES), F32)
    for j in range(TOP_K):
        m = jnp.max(x, axis=-1, keepdims=True)
        idx = jnp.min(jnp.where(x == m, lane_f, float(LANES)), axis=-1, keepdims=True)
        hot = lane_f == idx
        vals.append(m)
        hots.append(hot)
        e_out = jnp.where(lane == j, idx, e_out)
        x = jnp.where(hot, -jnp.inf, x)
    ex = [jnp.exp(v - vals[0]) for v in vals]
    denom = ex[0]
    for j in range(1, TOP_K):
        denom = denom + ex[j]
    chosen = jnp.where(hots[0], 1.0, 0.0)
    for j in range(1, TOP_K):
        chosen = chosen + jnp.where(hots[j], 1.0, 0.0)
    before = (lax.broadcasted_iota(I32, (tm, tm), 0) > lax.broadcasted_iota(I32, (tm, tm), 1)).astype(BF16)
    base = run_scr[...] + jnp.dot(before, chosen.astype(BF16), preferred_element_type=F32)
    g_out = jnp.zeros((tm, LANES), F32)
    r_out = jnp.zeros((tm, LANES), F32)
    for j in range(TOP_K):
        g_out = jnp.where(lane == j, ex[j] / denom, g_out)
        r_out = jnp.where(lane == j, jnp.sum(jnp.where(hots[j], base, 0.0), axis=-1, keepdims=True), r_out)
    e_ref[...] = e_out[:, :TOP_K].astype(I32)
    g_ref[...] = g_out[:, :TOP_K]
    r_ref[...] = r_out[:, :TOP_K].astype(I32)
    run_scr[...] = run_scr[...] + jnp.sum(chosen, axis=0, keepdims=True)

    @pl.when(i == pl.num_programs(0) - 1)
    def _():
        cnt_ref[...] = run_scr[...]


def _route(logits, ne):
    n = logits.shape[0]
    tm = next(t for t in (512, 384, 256, 128, 64, 32, 16, 8) if n % t == 0)
    small = lambda dt: jax.ShapeDtypeStruct((n, TOP_K), dt)
    return pl.pallas_call(
        functools.partial(_route_kernel, ne=ne),
        grid=(n // tm,),
        in_specs=[pl.BlockSpec((tm, LANES), lambda i: (i, 0))],
        out_specs=[pl.BlockSpec((tm, TOP_K), lambda i: (i, 0))] * 3 + [pl.BlockSpec((1, LANES), lambda i: (0, 0))],
        out_shape=[small(I32), small(F32), small(I32), jax.ShapeDtypeStruct((1, LANES), F32)],
        scratch_shapes=[pltpu.VMEM((1, LANES), F32)],
        compiler_params=_cparams(("arbitrary",)),
        name="moe_route",
    )(logits)


def _moe(h2, logits, layer, w1, b1, w2, b2, bm):
    n_tok = h2.shape[0]
    ne = w1.shape[1]
    top_idx, gates, rank, counts = _route(logits, ne)
    counts = counts[0, :ne].astype(I32)
    padded = (counts + bm - 1) // bm * bm
    pad_end = jnp.cumsum(padded)
    pad_start = pad_end - padded
    n_assign = n_tok * TOP_K
    n_blocks = -(-(n_assign + ne * (bm - 1)) // bm)
    n_slots = n_blocks * bm
    hot = top_idx[..., None] == jnp.arange(ne, dtype=I32)
    dest = (jnp.sum(jnp.where(hot, pad_start, 0), axis=-1) + rank).reshape(-1)
    block_e = jnp.minimum(jnp.sum(pad_end[None, :] <= (jnp.arange(n_blocks, dtype=I32) * bm)[:, None], axis=1),
                          ne - 1).astype(I32)
    payload = jnp.stack([jnp.repeat(jnp.arange(n_tok, dtype=I32), TOP_K),
                         lax.bitcast_convert_type(gates.reshape(-1), I32)], axis=-1)
    slots = jnp.zeros((n_slots, 2), I32).at[dest].set(payload)
    slot_gate = lax.bitcast_convert_type(slots[:, 1], F32)
    ys = _moe_ffn(h2[slots[:, 0]], slot_gate, block_e, layer, w1, b1, w2, b2, bm)
    return jnp.sum(ys[dest.reshape(n_tok, TOP_K).T], axis=0)


def _rel_bias_sample(rel_bias, past_len, t):
    dist = (past_len + np.arange(t))[:, None] - np.arange(past_len + t)[None, :]
    return _bias_by_bucket(rel_bias, _bucket_of(dist)), dist >= 0


def kernel(x_prompt, x_sample, cache_b_k, cache_b_v, cache_b_idx, state_c_rec, state_c_conv, cache_d_k, cache_d_v,
           page_table, c_prompt, c_sample, rel_bias, ada_w, ada_b, even_w_in, even_w_out, a_ln_g, a_ln_b, a_w_sp,
           a_b_sp, b_q_norm, b_k_norm, b_idx_norm, odd_w_in, odd_w_out, c_conv_w, c_a_log, c_dt_bias, c_norm_w,
           d_q_norm, d_k_norm, d_lambda, d_subln, router_w, router_b, moe_w1, moe_b1, moe_w2, moe_b2):
    bp, sp, d = x_prompt.shape
    bs, ts, _ = x_sample.shape
    depth = ada_w.shape[0]
    hw = d // 2
    page = cache_b_k.shape[2]
    past_len = page_table.shape[1] * page
    n_p, n_s = bp * sp, bs * ts
    tm_p = 256
    tq = 256
    tp = SUBLANES
    gp = 8 if page_table.shape[1] % 8 == 0 else 1
    assert sp % tm_p == 0 and n_s % SUBLANES == 0 and ts <= tp and ts >= C_CONV - 1

    n_c = bp + bs
    c_all = jnp.pad(jnp.concatenate([c_prompt, c_sample], axis=0), ((0, -n_c % SUBLANES), (0, 0)))
    mod_all = _ada_mod(c_all, ada_w, ada_b)

    def mods(layer):
        m = mod_all[layer].reshape(-1, 6, d)
        mp = [m[:bp, i][:, None, :] for i in range(6)]
        ms = [jnp.repeat(m[bp:n_c, i], ts, axis=0) for i in range(6)]
        return mp, ms

    xp = x_prompt.reshape(n_p, d)
    xs = x_sample.reshape(n_s, d)
    tpb = sp // tm_p
    tbl = _bias_tables(rel_bias, tq)
    outs_p, outs_s = {}, {}

    for layer in range(depth):
        li = layer // 2
        mp, ms = mods(layer)
        if layer % 2 == 0:
            ws_p = jnp.where(np.tril(np.ones((A_CHUNK, A_CHUNK), bool)), a_w_sp[li], 0.0).astype(BF16)
            bsp_p = a_b_sp[li].T
            (aout, _, q, k32, kbf, v32, vbf, iq, ik32, ikbf, iw) = _even_in(
                xp, mp[0], mp[1], False, tpb, tm_p, even_w_in[li], a_ln_g[li], a_ln_b[li], ws_p, bsp_p,
                b_q_norm[li], b_k_norm[li], b_idx_norm[li])
            ksel = min(TOPK_MAX, sp // 4)
            kc = 512 if sp % 512 == 0 else sp
            scores = _idx_scores_prompt(iq.reshape(bp, sp, hw), iw.reshape(bp, sp, IDX_HEADS),
                                        ikbf.reshape(bp, sp, IDX_DIM), tq, kc)
            lim = jnp.tile(jnp.arange(sp, dtype=I32), bp).reshape(n_p, 1)
            nch = jnp.tile((jnp.arange(sp // 128, dtype=I32) * 128 + 127) // kc + 1, bp)
            madd = _select_mask(scores.reshape(n_p, sp), lim, nch, ksel, kc, BF16).reshape(bp, sp, sp)
            b_out = _prompt_attn(q.reshape(bp, sp, hw), kbf.reshape(bp, sp, hw), vbf.reshape(bp, sp, -1), tbl, madd,
                                 None, None, nh=B_HEADS, nm=1, dqk=hw // B_HEADS, dv=hw // B_HEADS, tq=tq, lam_init=0.0)
            outs_p.setdefault('b_k', []).append(k32.reshape(bp, sp, B_HEADS, -1))
            outs_p.setdefault('b_v', []).append(v32.reshape(bp, sp, B_HEADS, -1))
            outs_p.setdefault('b_idx', []).append(ik32.reshape(bp, sp, IDX_DIM))
            mix_p = (aout, b_out.reshape(n_p, hw))

            cs = min(ts, A_CHUNK)
            ws_s = jnp.where(np.tril(np.ones((cs, cs), bool)), a_w_sp[li][:, :cs, :cs], 0.0)
            ws_s = jnp.einsum('ab,gts->gatbs', jnp.eye(n_s // cs, dtype=F32), ws_s).reshape(A_GROUPS, n_s, n_s).astype(BF16)
            bsp_s = jnp.tile(a_b_sp[li][:, :cs].T, (n_s // cs, 1))
            (aout, av, q, k32, kbf, v32, vbf, iq, ik32, ikbf, iw) = _even_in(
                xs, ms[0], ms[1], True, 1, n_s, even_w_in[li], a_ln_g[li], a_ln_b[li], ws_s, bsp_s,
                b_q_norm[li], b_k_norm[li], b_idx_norm[li])
            ltot = past_len + ts
            ksel = min(TOPK_MAX, ltot // 4)
            iq_rows = jnp.pad(iq.reshape(bs, ts, IDX_HEADS, IDX_DIM), ((0, 0), (0, tp - ts), (0, 0), (0, 0)))
            iq_rows = jnp.transpose(iq_rows, (0, 2, 1, 3)).reshape(bs, IDX_HEADS * tp, IDX_DIM)
            iw_rows = jnp.pad(iw.reshape(bs, ts, IDX_HEADS), ((0, 0), (0, tp - ts), (0, 0)))
            iw_rows = jnp.transpose(iw_rows, (0, 2, 1)).reshape(bs, IDX_HEADS * tp, 1)
            pad_new = lambda a: jnp.pad(a.reshape(bs, ts, -1), ((0, 0), (0, page - ts), (0, 0)))
            sc_s = _paged_scores(page_table, li, iq_rows, iw_rows, cache_b_idx, pad_new(ikbf), gp)
            width = sc_s.shape[2]
            kc_s = gp * page
            lim_s = jnp.where(np.arange(tp) < ts, past_len + np.arange(tp), -1).astype(I32)
            lim_s = jnp.tile(lim_s, bs).reshape(bs * tp, 1)
            nch_s = jnp.full((bs * tp // 128,), width // kc_s, I32)
            sel_s = _select_mask(sc_s.reshape(bs * tp, width), lim_s, nch_s, ksel, kc_s, F32).reshape(bs, tp, width)
            bias_s, _ = _rel_bias_sample(rel_bias, past_len, ts)
            bias_s = jnp.pad(bias_s, ((0, 0), (0, tp - ts), (0, width - ltot)))
            am = (sel_s[:, None] + bias_s[None]).reshape(bs, B_HEADS * tp, width)
            heads = lambda a: a.reshape(bs, ts, B_HEADS, -1)
            b_out = _paged_attn(page_table, li, _head_rows(heads(q), tp), am, cache_b_k, cache_b_v, heads(k32), heads(v32),
                                None, None, gp=gp, nh=B_HEADS, nm=1, tp=tp, lam_init=0.0)
            outs_s.setdefault('a_v', []).append(av.reshape(bs, ts, hw))
            outs_s.setdefault('b_k', []).append(k32.reshape(bs, ts, B_HEADS, -1))
            outs_s.setdefault('b_v', []).append(v32.reshape(bs, ts, B_HEADS, -1))
            outs_s.setdefault('b_idx', []).append(ik32.reshape(bs, ts, IDX_DIM))
            mix_s = (aout, b_out[:, :ts].reshape(n_s, hw).astype(BF16))
            w_out = even_w_out[li]
        else:
            lam_init = 0.8 - 0.6 * math.exp(-0.3 * layer)
            lp = d_lambda[li].astype(F32)
            lam = (jnp.exp(jnp.sum(lp[0] * lp[1])) - jnp.exp(jnp.sum(lp[2] * lp[3])) + lam_init).reshape(1)
            subg = d_subln[li].reshape(1, -1)
            dvh = hw // D_HEADS
            xc, cz, tail, q2, k2_32, k2bf, v2_32, v2bf = _odd_in(xp, mp[0], mp[1], False, tpb, tm_p, odd_w_in[li],
                                                                d_q_norm[li], d_k_norm[li])
            ch = xc.shape[1]
            xc3 = xc.reshape(bp, sp, ch)
            zbuf = jnp.zeros((bp, SUBLANES, ch), F32)
            s0 = jnp.zeros((bp, C_HEADS, C_DK, hw // C_HEADS), F32)
            chunk = C_CHUNK if sp % C_CHUNK == 0 else sp
            c_out, c_rec = _gdn(xc3, zbuf, cz.reshape(bp, sp, hw), tail.reshape(bp, sp, LANES), s0, c_conv_w[li],
                                c_a_log[li], c_dt_bias[li], c_norm_w[li], chunk, chunk)
            d_out = _prompt_attn(q2.reshape(bp, sp, hw), k2bf.reshape(bp, sp, hw), v2bf.reshape(bp, sp, -1), tbl, None,
                                 lam, subg, nh=D_HEADS, nm=2, dqk=D_QK_DIM, dv=dvh, tq=tq, lam_init=lam_init)
            outs_p.setdefault('c_rec', []).append(c_rec)
            outs_p.setdefault('c_conv', []).append(xc3[:, sp - (C_CONV - 1):])
            outs_p.setdefault('d_k', []).append(k2_32.reshape(bp, sp, D_HEADS, -1))
            outs_p.setdefault('d_v', []).append(v2_32.reshape(bp, sp, D_HEADS, -1))
            mix_p = (c_out.reshape(n_p, hw), d_out.reshape(n_p, hw))

            xc, cz, tail, q2, k2_32, k2bf, v2_32, v2bf = _odd_in(xs, ms[0], ms[1], True, 1, n_s, odd_w_in[li],
                                                                d_q_norm[li], d_k_norm[li])
            pad_t = lambda a: jnp.pad(a.reshape(bs, ts, -1), ((0, 0), (0, C_CHUNK - ts), (0, 0)))
            buf = state_c_conv[li]
            buf8 = jnp.pad(buf, ((0, 0), (SUBLANES - (C_CONV - 1), 0), (0, 0)))
            c_out, c_rec = _gdn(pad_t(xc), buf8, pad_t(cz), pad_t(tail), state_c_rec[li], c_conv_w[li], c_a_log[li],
                                c_dt_bias[li], c_norm_w[li], C_CHUNK, ts)
            ltot = past_len + ts
            bias_s, allowed = _rel_bias_sample(rel_bias, past_len, ts)
            width = (page_table.shape[1] // gp + 1) * gp * page
            am = jnp.where(jnp.asarray(allowed)[None], bias_s, NEG)
            am = jnp.pad(am, ((0, 0), (0, tp - ts), (0, 0)))
            am = jnp.pad(am, ((0, 0), (0, 0), (0, width - ltot)), constant_values=NEG)
            am = jnp.broadcast_to(am[None, :, None], (bs, D_HEADS, 2, tp, width)).reshape(bs, D_HEADS * 2 * tp, width)
            q_rows = _head_rows(q2.reshape(bs, ts, 2 * D_HEADS, D_QK_DIM), tp)
            heads = lambda a: a.reshape(bs, ts, D_HEADS, -1)
            d_out = _paged_attn(page_table, li, q_rows, am, cache_d_k, cache_d_v, heads(k2_32), heads(v2_32), lam, subg,
                                gp=gp, nh=D_HEADS, nm=2, tp=tp, lam_init=lam_init)
            outs_s.setdefault('c_rec', []).append(c_rec)
            outs_s.setdefault('c_conv', []).append(
                jnp.concatenate([buf, xc.reshape(bs, ts, ch)], axis=1)[:, ts:])
            outs_s.setdefault('d_k', []).append(k2_32.reshape(bs, ts, D_HEADS, -1))
            outs_s.setdefault('d_v', []).append(v2_32.reshape(bs, ts, D_HEADS, -1))
            mix_s = (c_out[:, :ts].reshape(n_s, hw), d_out[:, :ts].reshape(n_s, hw).astype(BF16))
            w_out = odd_w_out[li]

        x1p, h2p, lgp = _out_proj(mix_p[0], mix_p[1], xp, mp[2], mp[3], mp[4], False, tpb, tm_p, w_out,
                                  router_w[layer], router_b[layer])
        x1s, h2s, lgs = _out_proj(mix_s[0], mix_s[1], xs, ms[2], ms[3], ms[4], True, 1, n_s, w_out,
                                  router_w[layer], router_b[layer])
        moe_out = _moe(jnp.concatenate([h2p, h2s], axis=0), jnp.concatenate([lgp, lgs], axis=0), layer,
                       moe_w1, moe_b1, moe_w2, moe_b2, 256)
        xp = (x1p.reshape(bp, sp, d) + mp[5] * moe_out[:n_p].reshape(bp, sp, d)).reshape(n_p, d)
        xs = x1s + ms[5] * moe_out[n_p:]

    st = lambda name, src: jnp.stack(src[name])
    return (xp.reshape(bp, sp, d), xs.reshape(bs, ts, d),
            st('b_k', outs_p), st('b_v', outs_p), st('b_idx', outs_p), st('c_rec', outs_p), st('c_conv', outs_p),
            st('d_k', outs_p), st('d_v', outs_p),
            st('a_v', outs_s), st('b_k', outs_s), st('b_v', outs_s), st('b_idx', outs_s), st('c_rec', outs_s),
            st('c_conv', outs_s), st('d_k', outs_s), st('d_v', outs_s))
```

```python
import functools
import math

import numpy as np
import jax
import jax.numpy as jnp
from jax import lax
from jax.experimental import pallas as pl
from jax.experimental.pallas import tpu as pltpu

F32 = jnp.float32
BF16 = jnp.bfloat16
I32 = jnp.int32
HIGHEST = lax.Precision.HIGHEST

A_GROUPS = 4
A_CHUNK = 128
B_HEADS = 8
IDX_HEADS = 8
IDX_DIM = 64
TOPK_MAX = 256
C_HEADS = 4
C_DK = 128
C_CONV = 4
C_CHUNK = 64
D_HEADS = 8
D_QK_DIM = 32
N_BUCKETS = 32
MAX_DISTANCE = 128
TOP_K = 4
SWIGLU_LIMIT = 7.0
SWIGLU_ALPHA = 1.702
EPS = 1e-6

LANES = 128
SUBLANES = 8
VMEM_LIMIT_BYTES = 56 * 1024 * 1024

TOKEN_TILE = 256
ATTN_TILE = 256
SELECT_ROWS = 128
SELECT_CHUNK = 512
MOE_BLOCK_ROWS = 256
PAGES_PER_STEP = 8

NEG = -1e30
INT_MIN = -2 ** 31


def _cparams(sem):
    return pltpu.CompilerParams(dimension_semantics=sem, vmem_limit_bytes=VMEM_LIMIT_BYTES)


def _dot(a, b):
    return jnp.dot(a.astype(BF16), b.astype(BF16), preferred_element_type=F32)


def _dot_nt(a, b):
    return lax.dot_general(a.astype(BF16), b.astype(BF16), (((1,), (1,)), ((), ())), preferred_element_type=F32)


def _dot_hi(a, b):
    return jnp.dot(a, b, preferred_element_type=F32, precision=HIGHEST)


def _dot_tn(a, b):
    return lax.dot_general(a.astype(BF16), b.astype(BF16), (((0,), (0,)), ((), ())), preferred_element_type=F32)


def _dot_x3(a, b):
    ah = a.astype(BF16)
    al = (a - ah.astype(F32)).astype(BF16)
    bh = b.astype(BF16)
    bl = (b - bh.astype(F32)).astype(BF16)
    dot = lambda x, y: jnp.dot(x, y, preferred_element_type=F32)
    return dot(ah, bh) + (dot(ah, bl) + dot(al, bh))


def _silu(x):
    return x * jax.nn.sigmoid(x)


def _gelu_tanh(x):
    return 0.5 * x * (1.0 + jnp.tanh(math.sqrt(2.0 / math.pi) * (x + 0.044715 * (x * x * x))))


def _group_rms(x, ones_ref, gsize):
    xx = x * x
    hi = xx.astype(BF16)
    lo = (xx - hi.astype(F32)).astype(BF16)
    e = ones_ref[...]
    ss = jnp.dot(hi, e, preferred_element_type=F32) + jnp.dot(lo, e, preferred_element_type=F32)
    return x * lax.rsqrt(ss * (1.0 / gsize) + EPS)


def _store_v_with_ones(vx_ref, v):
    tm, w = v.shape
    nh = vx_ref.shape[1] // LANES
    dv = w // nh
    tail = jnp.where(lax.broadcasted_iota(I32, (tm, LANES - dv), 1) == 0, 1.0, 0.0)
    for h in range(nh):
        vx_ref[:, h * LANES:(h + 1) * LANES] = jnp.concatenate([v[:, h * dv:(h + 1) * dv], tail], axis=1).astype(BF16)


def _block_ones(width, gsize):
    g = np.arange(width) // gsize
    return jnp.asarray(g[:, None] == g[None, :], BF16)


def _bucket_table():
    n = np.arange(MAX_DISTANCE)
    exact = N_BUCKETS // 2
    scaled = np.log(np.maximum(n, 1).astype(np.float32) / np.float32(exact)) / np.float32(math.log(MAX_DISTANCE / exact))
    large = np.minimum(exact + (scaled.astype(np.float32) * (N_BUCKETS - exact)).astype(np.int32), N_BUCKETS - 1)
    return np.where(n < exact, n, large).astype(np.int32)


def _bucket_of(dist):
    n = np.maximum(dist, 0)
    return np.where(n < MAX_DISTANCE, _bucket_table()[np.minimum(n, MAX_DISTANCE - 1)], N_BUCKETS - 1)


def _bias_by_bucket(rel_bias, bucket):
    out = jnp.zeros((rel_bias.shape[1],) + bucket.shape, F32)
    for b in np.unique(bucket):
        out = jnp.where(jnp.asarray(bucket == b)[None], rel_bias[b].astype(F32).reshape((-1,) + (1,) * bucket.ndim), out)
    return out


def _ada_kernel(c_ref, w_ref, b_ref, o_ref):
    o_ref[...] = _dot(_silu(c_ref[...]), w_ref[...]) + b_ref[...]


def _ada_mod(c_all, ada_w, ada_b):
    depth, d, n6 = ada_w.shape
    rows = c_all.shape[0]
    tn = 1536 if n6 % 1536 == 0 else n6
    return pl.pallas_call(
        _ada_kernel,
        grid=(depth, n6 // tn),
        in_specs=[pl.BlockSpec((rows, d), lambda l, j: (0, 0)),
                  pl.BlockSpec((None, d, tn), lambda l, j: (l, 0, j)),
                  pl.BlockSpec((None, 1, tn), lambda l, j: (l, 0, j))],
        out_specs=pl.BlockSpec((None, rows, tn), lambda l, j: (l, 0, j)),
        out_shape=jax.ShapeDtypeStruct((depth, rows, n6), F32),
        compiler_params=_cparams(("arbitrary", "arbitrary")),
        name="ada_mod",
    )(c_all, ada_w, ada_b.reshape(depth, 1, n6))


def _row_spec(tm, w):
    return pl.BlockSpec((tm, w), lambda i: (i, 0))


def _row_shape(n, w, dt):
    return jax.ShapeDtypeStruct((n, w), dt)


def _mod_specs(per_token, tm, d, tiles_per_batch):
    if per_token:
        return pl.BlockSpec((tm, d), lambda i: (i, 0))
    return pl.BlockSpec((None, 1, d), lambda i: (i // tiles_per_batch, 0, 0))


def _modulated_rms(x, sh, sc):
    xn = x * lax.rsqrt(jnp.mean(x * x, axis=-1, keepdims=True) + EPS)
    return xn * (1.0 + sc) + sh


def _even_in_kernel(x_ref, sh_ref, sc_ref, wm_ref, wt_ref, lng_ref, lnb_ref, msp_ref, bsp_ref,
                    qg_ref, kg_ref, ikg_ref, e_ref,
                    aout_ref, av_ref, q_ref, k32_ref, kbf_ref, v32_ref, vx_ref, iq_ref, ik32_ref, ikbf_ref, iw_ref):
    tm = x_ref.shape[0]
    aw = aout_ref.shape[1]
    gd = aw // A_GROUPS
    h = _modulated_rms(x_ref[...], sh_ref[...], sc_ref[...]).astype(BF16)

    def seg(i):
        return jnp.dot(h, wm_ref[:, i * aw:(i + 1) * aw], preferred_element_type=F32)

    au = _gelu_tanh(seg(0))
    av = _gelu_tanh(seg(1))
    avc = av - jnp.mean(av, axis=-1, keepdims=True)
    vn = avc * lax.rsqrt(jnp.mean(avc * avc, axis=-1, keepdims=True) + EPS) * lng_ref[...] + lnb_ref[...]
    av_ref[...] = vn
    vnb = vn.astype(BF16)
    cr = msp_ref.shape[1]
    for c in range(tm // cr):
        r0 = c * cr
        for g in range(A_GROUPS):
            mixed = jnp.dot(msp_ref[g], vnb[r0:r0 + cr, g * gd:(g + 1) * gd], preferred_element_type=F32)
            mixed = mixed + bsp_ref[:, g:g + 1]
            aout_ref[r0:r0 + cr, g * gd:(g + 1) * gd] = (au[r0:r0 + cr, g * gd:(g + 1) * gd] * mixed).astype(BF16)

    hd = aw // B_HEADS
    q = _group_rms(seg(2), e_ref, hd) * qg_ref[...]
    q_ref[...] = (q * (hd ** -0.5)).astype(BF16)
    k = _group_rms(seg(3), e_ref, hd) * kg_ref[...]
    k32_ref[...] = k
    kbf_ref[...] = k.astype(BF16)
    v = seg(4)
    v32_ref[...] = v
    _store_v_with_ones(vx_ref, v)
    iq_ref[...] = (seg(5) * (IDX_DIM ** -0.5)).astype(BF16)
    tail = jnp.dot(h, wt_ref[...], preferred_element_type=F32)
    ik = tail[:, :IDX_DIM]
    ik = ik * lax.rsqrt(jnp.mean(ik * ik, axis=-1, keepdims=True) + EPS) * ikg_ref[...]
    ik32_ref[...] = ik
    ikbf_ref[...] = ik.astype(BF16)
    iw_ref[...] = tail[:, IDX_DIM:IDX_DIM + IDX_HEADS] * (IDX_HEADS ** -0.5)


def _even_in(x, sh, sc, per_token, tiles_per_batch, tm, w_in, ln_g, ln_b, msp, bsp, q_g, k_g, ik_g):
    n, d = x.shape
    aw = d // 2
    wm = w_in[:, :6 * aw].astype(BF16)
    wt = jnp.pad(w_in[:, 6 * aw:], ((0, 0), (0, LANES - (IDX_DIM + IDX_HEADS)))).astype(BF16)
    full = lambda shape: pl.BlockSpec(shape, lambda i: (0,) * len(shape))
    row = lambda w: pl.BlockSpec((tm, w), lambda i: (i, 0))
    mod = _mod_specs(per_token, tm, d, tiles_per_batch)
    outs = [(aw, BF16), (aw, F32), (aw, BF16), (aw, F32), (aw, BF16), (aw, F32), (B_HEADS * LANES, BF16),
            (aw, BF16), (IDX_DIM, F32), (IDX_DIM, BF16), (IDX_HEADS, F32)]
    return pl.pallas_call(
        _even_in_kernel,
        grid=(n // tm,),
        in_specs=[row(d), mod, mod, full(wm.shape), full(wt.shape), full((1, aw)), full((1, aw)),
                  full(msp.shape), full(bsp.shape), full((1, aw)), full((1, aw)), full((1, IDX_DIM)), full((aw, aw))],
        out_specs=[_row_spec(tm, w) for w, _ in outs],
        out_shape=[_row_shape(n, w, dt) for w, dt in outs],
        compiler_params=_cparams(("arbitrary",)),
        name="even_in",
    )(x, sh, sc, wm, wt, ln_g.reshape(1, aw), ln_b.reshape(1, aw), msp, bsp,
      jnp.tile(q_g, B_HEADS).reshape(1, aw), jnp.tile(k_g, B_HEADS).reshape(1, aw), ik_g.reshape(1, IDX_DIM),
      _block_ones(aw, aw // B_HEADS))


def _odd_in_kernel(x_ref, sh_ref, sc_ref, wm_ref, wt_ref, qg_ref, kg_ref, e_ref,
                   xc_ref, cz_ref, tail_ref, q_ref, k32_ref, kbf_ref, v32_ref, vx_ref):
    hw = cz_ref.shape[1]
    h = _modulated_rms(x_ref[...], sh_ref[...], sc_ref[...]).astype(BF16)

    def seg(i):
        return jnp.dot(h, wm_ref[:, i * hw:(i + 1) * hw], preferred_element_type=F32)

    for i in range(3):
        xc_ref[:, i * hw:(i + 1) * hw] = seg(i)
    cz_ref[...] = seg(3)
    tail_ref[...] = jnp.dot(h, wt_ref[...], preferred_element_type=F32)
    q = _group_rms(seg(4), e_ref, D_QK_DIM) * qg_ref[...]
    q_ref[...] = (q * (D_QK_DIM ** -0.5)).astype(BF16)
    k = _group_rms(seg(5), e_ref, D_QK_DIM) * kg_ref[...]
    k32_ref[...] = k
    kbf_ref[...] = k.astype(BF16)
    v = seg(6)
    v32_ref[...] = v
    _store_v_with_ones(vx_ref, v)


def _odd_in(x, sh, sc, per_token, tiles_per_batch, tm, w_in, q_g, k_g):
    n, d = x.shape
    hw = d // 2
    ng = 2 * C_HEADS
    wm = jnp.concatenate([w_in[:, :4 * hw], w_in[:, 4 * hw + ng:]], axis=1).astype(BF16)
    wt = jnp.pad(w_in[:, 4 * hw:4 * hw + ng], ((0, 0), (0, LANES - ng))).astype(BF16)
    full = lambda shape: pl.BlockSpec(shape, lambda i: (0,) * len(shape))
    row = lambda w: pl.BlockSpec((tm, w), lambda i: (i, 0))
    mod = _mod_specs(per_token, tm, d, tiles_per_batch)
    outs = [(3 * hw, F32), (hw, F32), (LANES, F32), (hw, BF16), (hw, F32), (hw, BF16), (hw, F32),
            (D_HEADS * LANES, BF16)]
    reps = hw // D_QK_DIM
    return pl.pallas_call(
        _odd_in_kernel,
        grid=(n // tm,),
        in_specs=[row(d), mod, mod, full(wm.shape), full(wt.shape), full((1, hw)), full((1, hw)), full((hw, hw))],
        out_specs=[_row_spec(tm, w) for w, _ in outs],
        out_shape=[_row_shape(n, w, dt) for w, dt in outs],
        compiler_params=_cparams(("arbitrary",)),
        name="odd_in",
    )(x, sh, sc, wm, wt, jnp.tile(q_g, reps).reshape(1, hw), jnp.tile(k_g, reps).reshape(1, hw),
      _block_ones(hw, D_QK_DIM))


def _out_proj_kernel(a_ref, b_ref, x_ref, g1_ref, sh_ref, sc_ref, w_ref, rw_ref, rb_ref, x1_ref, h2_ref, lg_ref):
    hw = a_ref.shape[1]
    y = jnp.dot(a_ref[...], w_ref[:hw, :], preferred_element_type=F32)
    y = y + jnp.dot(b_ref[...], w_ref[hw:, :], preferred_element_type=F32)
    x1 = x_ref[...] + g1_ref[...] * y
    x1_ref[...] = x1
    h2 = _modulated_rms(x1, sh_ref[...], sc_ref[...])
    h2_ref[...] = h2
    lg_ref[...] = _dot(h2, rw_ref[...]) + rb_ref[...]


def _out_proj(a, b, x, g1, sh, sc, per_token, tiles_per_batch, tm, w_out, router_w, router_b):
    n, d = x.shape
    hw = a.shape[1]
    ne = router_w.shape[1]
    rw = jnp.pad(router_w, ((0, 0), (0, LANES - ne)))
    rb = jnp.pad(router_b, (0, LANES - ne)).reshape(1, LANES)
    full = lambda shape: pl.BlockSpec(shape, lambda i: (0,) * len(shape))
    row = lambda w: pl.BlockSpec((tm, w), lambda i: (i, 0))
    mod = _mod_specs(per_token, tm, d, tiles_per_batch)
    x1, h2, lg = pl.pallas_call(
        _out_proj_kernel,
        grid=(n // tm,),
        in_specs=[row(hw), row(hw), row(d), mod, mod, mod, full((d, d)), full((d, LANES)), full((1, LANES))],
        out_specs=[row(d), row(d), row(LANES)],
        out_shape=[jax.ShapeDtypeStruct((n, d), F32), jax.ShapeDtypeStruct((n, d), F32),
                   jax.ShapeDtypeStruct((n, LANES), F32)],
        compiler_params=_cparams(("arbitrary",)),
        name="out_proj",
    )(a, b, x, g1, sh, sc, w_out.astype(BF16), rw, rb)
    return x1, h2, lg


def _select_kernel(nch_ref, s_ref, lim_ref, m_ref, key_ref, *, ksel, kc):
    _select_rows(lambda off: s_ref[:, pl.ds(off, kc)], nch_ref[pl.program_id(0)], lim_ref[...], m_ref, key_ref, ksel, kc)


def _select_prompt_kernel(iq_ref, iw_ref, ik_ref, m_ref, key_ref, *, ksel, kc):
    rows = iq_ref.shape[0]
    first = pl.program_id(1) * rows

    iq_heads = jnp.concatenate([iq_ref[:, h * IDX_DIM:(h + 1) * IDX_DIM] for h in range(IDX_HEADS)], axis=0)

    def scores(off):
        d = _dot_nt(iq_heads, ik_ref[pl.ds(off, kc), :])
        acc = jnp.zeros((rows, kc), F32)
        for h in range(IDX_HEADS):
            acc = acc + iw_ref[:, h:h + 1] * jnp.maximum(d[h * rows:(h + 1) * rows], 0.0)
        return acc

    lim = first + lax.broadcasted_iota(I32, (rows, 1), 0)
    _select_rows(scores, (first + rows - 1) // kc + 1, lim, m_ref, key_ref, ksel, kc)


def _select_rows(score_chunk, nch, lim, m_ref, key_ref, ksel, kc):
    rows, width = m_ref.shape
    fold = kc // LANES
    col_bits = max(1, int(width - 1).bit_length())

    def cols(c):
        return c * kc + lax.broadcasted_iota(I32, (rows, kc), 1)

    def fill(c, carry):
        off = pl.multiple_of(c * kc, kc)
        bits = lax.bitcast_convert_type(score_chunk(off) + 0.0, I32)
        key = jnp.where(bits < 0, bits ^ jnp.int32(0x7FFFFFFF), bits)
        key_ref[:, pl.ds(off, kc)] = jnp.where(cols(c) <= lim, key, jnp.int32(INT_MIN))
        return carry

    lax.fori_loop(0, nch, fill, 0)

    def count(pred):
        def body(c, acc):
            off = pl.multiple_of(c * kc, kc)
            hit = jnp.where(pred(c, key_ref[:, pl.ds(off, kc)]), 1.0, 0.0)
            part = hit[:, :LANES]
            for f in range(1, fold):
                part = part + hit[:, f * LANES:(f + 1) * LANES]
            return acc + part
        acc = lax.fori_loop(0, nch, body, jnp.zeros((rows, LANES), F32))
        return jnp.sum(acc, axis=1, keepdims=True)

    def thr_bit(b, carry):
        prefix, n_ge = carry
        cand = prefix + (jnp.int32(1) << (31 - b))
        cnt = count(lambda c, key: key >= cand)
        take = cnt >= ksel
        return jnp.where(take, cand, prefix), jnp.where(take, cnt, n_ge)

    thr, n_ge = lax.fori_loop(0, 32, thr_bit, (jnp.full((rows, 1), INT_MIN, I32),
                                               jnp.full((rows, 1), 1.0, F32) * (nch * kc).astype(F32)))

    def tie_break():
        need = ksel - count(lambda c, key: key > thr)

        def tie_bit(b, pos):
            cand = pos + (jnp.int32(1) << (col_bits - 1 - b))
            cnt = count(lambda c, key: (key == thr) & (cols(c) < cand))
            return jnp.where(cnt < need, cand, pos)

        return lax.fori_loop(0, col_bits, tie_bit, jnp.zeros((rows, 1), I32))

    surplus = jnp.where((n_ge > ksel) & (thr > jnp.int32(INT_MIN)), 1.0, 0.0)
    last = lax.cond(jnp.max(surplus) > 0.0, tie_break, lambda: jnp.full((rows, 1), width, I32))

    def emit(c, carry):
        off = pl.multiple_of(c * kc, kc)
        key = key_ref[:, pl.ds(off, kc)]
        col = cols(c)
        sel = ((key > thr) | ((key == thr) & (col <= last))) & (col <= lim)
        m_ref[:, pl.ds(off, kc)] = jnp.where(sel, 0.0, NEG).astype(m_ref.dtype)
        return carry

    lax.fori_loop(0, nch, emit, 0)

    def blank(c, carry):
        off = pl.multiple_of(c * kc, kc)
        m_ref[:, pl.ds(off, kc)] = jnp.full((rows, kc), NEG, m_ref.dtype)
        return carry

    lax.fori_loop(nch, width // kc, blank, 0)


def _select_mask(scores, lim, nch, ksel, kc, out_dtype):
    r, width = scores.shape
    tr = SELECT_ROWS
    return pl.pallas_call(
        functools.partial(_select_kernel, ksel=ksel, kc=kc),
        grid_spec=pltpu.PrefetchScalarGridSpec(
            num_scalar_prefetch=1,
            grid=(r // tr,),
            in_specs=[pl.BlockSpec((tr, width), lambda i, n: (i, 0)),
                      pl.BlockSpec((tr, 1), lambda i, n: (i, 0))],
            out_specs=pl.BlockSpec((tr, width), lambda i, n: (i, 0)),
            scratch_shapes=[pltpu.VMEM((tr, width), I32)]),
        out_shape=jax.ShapeDtypeStruct((r, width), out_dtype),
        compiler_params=_cparams(("arbitrary",)),
        name="select_mask",
    )(nch, scores, lim)


def _select_mask_prompt(iq, iw, ik, ksel, kc, tr):
    bn, s, _ = iq.shape
    assert s % tr == 0 and s % kc == 0 and kc >= ksel
    return pl.pallas_call(
        functools.partial(_select_prompt_kernel, ksel=ksel, kc=kc),
        grid=(bn, s // tr),
        in_specs=[pl.BlockSpec((None, tr, IDX_HEADS * IDX_DIM), lambda b, i: (b, i, 0)),
                  pl.BlockSpec((None, tr, IDX_HEADS), lambda b, i: (b, i, 0)),
                  pl.BlockSpec((None, s, IDX_DIM), lambda b, i: (b, 0, 0))],
        out_specs=pl.BlockSpec((None, tr, s), lambda b, i: (b, i, 0)),
        out_shape=jax.ShapeDtypeStruct((bn, s, s), BF16),
        scratch_shapes=[pltpu.VMEM((tr, s), I32)],
        compiler_params=_cparams(("arbitrary", "arbitrary")),
        name="select_mask_prompt",
    )(iq, iw, ik)


def _prompt_attn_kernel(*refs, nh, nm, dqk, dv, tq, has_mask, lam_init):
    refs = list(refs)
    q_ref, k_ref, vx_ref, tbl_ref = refs[:4]
    pos = 4
    madd_ref = None
    if has_mask:
        madd_ref = refs[pos]
        pos += 1
    lam_ref = subg_ref = None
    if nm == 2:
        lam_ref, subg_ref = refs[pos], refs[pos + 1]
        pos += 2
    o_ref, mx_scr, acc_scr, qz_scr = refs[pos:pos + 4]
    ma_scr = refs[pos + 4] if has_mask else None
    nu = nh * nm
    nf = tq // LANES
    gu = LANES // dqk
    qi = pl.program_id(1)
    mx_scr[...] = jnp.full(mx_scr.shape, -jnp.inf, F32)
    acc_scr[...] = jnp.zeros(acc_scr.shape, F32)

    lane = lax.broadcasted_iota(I32, (tq, LANES), 1)
    for g in range(nu // gu):
        qt = q_ref[:, g * LANES:(g + 1) * LANES]
        for i in range(gu):
            keep = (lane >= i * dqk) & (lane < (i + 1) * dqk)
            qz_scr[g, i * tq:(i + 1) * tq, :] = jnp.where(keep, qt, jnp.zeros_like(qt))

    def sweep(j, back, second):
        koff = pl.multiple_of(j * tq, tq)
        if has_mask:
            ma_scr[...] = madd_ref[:, pl.ds(koff, tq)].astype(F32)
        for g in range(nu // gu):
            stacked = _dot_nt(qz_scr[g], k_ref[pl.ds(koff, tq), g * LANES:(g + 1) * LANES])
            ps = []
            for i in range(gu):
                u = g * gu + i
                s = stacked[i * tq:(i + 1) * tq]
                if back is not None:
                    s = s + tbl_ref[u // nm, back]
                if has_mask:
                    s = s + ma_scr[...]
                if not second:
                    r = s[:, :LANES]
                    for f in range(1, nf):
                        r = jnp.maximum(r, s[:, f * LANES:(f + 1) * LANES])
                    mx_scr[u] = jnp.maximum(mx_scr[u], r)
                else:
                    mx = mx_scr[u]
                    p = jnp.concatenate([jnp.exp(s[:, f * LANES:(f + 1) * LANES] - mx) for f in range(nf)], axis=1)
                    ps.append(p.astype(BF16))
            if second:
                for i0 in range(0, gu, nm):
                    u0 = g * gu + i0
                    h = u0 // nm
                    stack = ps[i0] if nm == 1 else jnp.concatenate(ps[i0:i0 + nm], axis=0)
                    pv = jnp.dot(stack, vx_ref[pl.ds(koff, tq), h * LANES:(h + 1) * LANES], preferred_element_type=F32)
                    for m in range(nm):
                        acc_scr[u0 + m] += pv[m * tq:(m + 1) * tq]

    def all_chunks(second):
        def far(j, carry):
            sweep(j, None, second)
            return carry

        lax.fori_loop(0, jnp.maximum(qi - 1, 0), far, 0)

        @pl.when(qi >= 1)
        def _():
            sweep(qi - 1, 1, second)

        sweep(qi, 0, second)

    all_chunks(False)
    for u in range(nu):
        mx_scr[u] = jnp.broadcast_to(jnp.max(mx_scr[u], axis=-1, keepdims=True), (tq, LANES))
    all_chunks(True)

    def normalised(u):
        a = acc_scr[u]
        return a[:, :dv] / a[:, dv:dv + 1]

    for h in range(nh):
        if nm == 1:
            o_ref[:, h * dv:(h + 1) * dv] = normalised(h).astype(o_ref.dtype)
        else:
            att = normalised(2 * h) - lam_ref[0] * normalised(2 * h + 1)
            att = att * lax.rsqrt(jnp.mean(att * att, axis=-1, keepdims=True) + EPS)
            o_ref[:, h * dv:(h + 1) * dv] = (att * subg_ref[...] * (1.0 - lam_init)).astype(o_ref.dtype)


def _bias_tables(rel_bias, tq):
    far = rel_bias[N_BUCKETS - 1].astype(F32)[:, None, None]
    r = np.arange(tq)[:, None]
    c = np.arange(tq)[None, :]
    tabs = []
    for d in range(2):
        dist = d * tq + r - c
        vals = _bias_by_bucket(rel_bias, _bucket_of(dist)) - far
        tabs.append(jnp.where(jnp.asarray(dist >= 0)[None], vals, NEG))
    return jnp.stack(tabs, axis=1)


def _prompt_attn(q, k, vx, tbl, madd, lam, subg, *, nh, nm, dqk, dv, tq, lam_init):
    bn, s, w = q.shape
    assert MAX_DISTANCE <= tq and s % tq == 0 and tq % LANES == 0 and dv < LANES
    has_mask = madd is not None
    once = lambda width: pl.BlockSpec((None, s, width), lambda b, i: (b, 0, 0), pipeline_mode=pl.Buffered(1))
    in_specs = [pl.BlockSpec((None, tq, w), lambda b, i: (b, i, 0)), once(w), once(nh * LANES),
                pl.BlockSpec(tbl.shape, lambda b, i: (0, 0, 0, 0), pipeline_mode=pl.Buffered(1))]
    args = [q, k, vx, tbl]
    if has_mask:
        in_specs.append(pl.BlockSpec((None, tq, s), lambda b, i: (b, i, 0)))
        args.append(madd)
    if nm == 2:
        in_specs += [pl.BlockSpec(memory_space=pltpu.SMEM), pl.BlockSpec((1, dv), lambda b, i: (0, 0))]
        args += [lam, subg]
    return pl.pallas_call(
        functools.partial(_prompt_attn_kernel, nh=nh, nm=nm, dqk=dqk, dv=dv, tq=tq, has_mask=has_mask, lam_init=lam_init),
        grid=(bn, s // tq),
        in_specs=in_specs,
        out_specs=pl.BlockSpec((None, tq, nh * dv), lambda b, i: (b, i, 0)),
        out_shape=jax.ShapeDtypeStruct((bn, s, nh * dv), BF16),
        scratch_shapes=[pltpu.VMEM((nh * nm, tq, LANES), F32), pltpu.VMEM((nh * nm, tq, LANES), F32),
                        pltpu.VMEM((nh * nm * dqk // LANES, LANES // dqk * tq, LANES), BF16)]
                       + ([pltpu.VMEM((tq, tq), F32)] if has_mask else []),
        compiler_params=_cparams(("arbitrary", "arbitrary")),
        name="prompt_attn_dsa" if nm == 1 else "prompt_attn_diff",
    )(*args)


def _paged_scores_kernel(pt_ref, iq_ref, iw_ref, *refs, gp, page):
    pages = refs[:gp]
    new_ref, o_ref = refs[gp], refs[gp + 1]
    s_id = pl.program_id(1)
    last = pl.num_programs(1) - 1
    rows = iq_ref.shape[0]
    tp = rows // IDX_HEADS

    def score(ik_t):
        d = jnp.maximum(_dot(iq_ref[...], ik_t), 0.0) * iw_ref[...]
        acc = d[:tp]
        for h in range(1, IDX_HEADS):
            acc = acc + d[h * tp:(h + 1) * tp]
        return acc

    @pl.when(s_id < last)
    def _():
        for g in range(gp):
            o_ref[:, g * page:(g + 1) * page] = score(pages[g][...])

    @pl.when(s_id == last)
    def _():
        o_ref[...] = jnp.zeros(o_ref.shape, F32)
        o_ref[:, :page] = score(new_ref[...])


def _paged_scores(page_table, li, iq_rows, iw_rows, cache_idx, ik_new, gp):
    bn, rows, _ = iq_rows.shape
    tp = rows // IDX_HEADS
    page = cache_idx.shape[2]
    npages = page_table.shape[1]
    nsteps = npages // gp + 1
    width = nsteps * gp * page

    def page_map(g):
        return lambda b, s, pt: (li, pt[b, jnp.minimum(s * gp + g, npages - 1)], 0, 0)

    return pl.pallas_call(
        functools.partial(_paged_scores_kernel, gp=gp, page=page),
        grid_spec=pltpu.PrefetchScalarGridSpec(
            num_scalar_prefetch=1,
            grid=(bn, nsteps),
            in_specs=[pl.BlockSpec((None, rows, IDX_DIM), lambda b, s, pt: (b, 0, 0)),
                      pl.BlockSpec((None, rows, 1), lambda b, s, pt: (b, 0, 0))]
                     + [pl.BlockSpec((None, None, IDX_DIM, page), page_map(g)) for g in range(gp)]
                     + [pl.BlockSpec((None, IDX_DIM, page), lambda b, s, pt: (b, 0, 0))],
            out_specs=pl.BlockSpec((None, tp, gp * page), lambda b, s, pt: (b, 0, s))),
        out_shape=jax.ShapeDtypeStruct((bn, tp, width), F32),
        compiler_params=_cparams(("arbitrary", "arbitrary")),
        name="paged_idx_scores",
    )(page_table, iq_rows, iw_rows, *([jnp.transpose(cache_idx, (0, 1, 3, 2))] * gp), jnp.transpose(ik_new, (0, 2, 1)))


def _paged_attn_kernel(pt_ref, q_ref, am_ref, *refs, gp, nh, nm, tp, lam_init):
    kpages = refs[:gp]
    vpages = refs[gp:2 * gp]
    knew_ref, vnew_ref = refs[2 * gp], refs[2 * gp + 1]
    pos = 2 * gp + 2
    lam_ref = subg_ref = None
    if nm == 2:
        lam_ref, subg_ref = refs[pos], refs[pos + 1]
        pos += 2
    o_ref, m_scr, l_scr, acc_scr = refs[pos:pos + 4]
    _, hd, page = knew_ref.shape
    dv = vnew_ref.shape[1]
    s_id = pl.program_id(1)
    last = pl.num_programs(1) - 1

    @pl.when(s_id == 0)
    def _():
        m_scr[...] = jnp.full(m_scr.shape, -jnp.inf, F32)
        l_scr[...] = jnp.zeros(l_scr.shape, F32)
        acc_scr[...] = jnp.zeros(acc_scr.shape, F32)

    def blocks(kvs):
        ss = [_dot(q_ref[...], kp[...].reshape(nh * hd, page)) + am_ref[:, g * page:(g + 1) * page]
              for g, (kp, _) in enumerate(kvs)]
        mx = ss[0]
        for s in ss[1:]:
            mx = jnp.maximum(mx, s)
        m_prev = m_scr[...]
        m_new = jnp.maximum(m_prev, jnp.max(mx, axis=-1, keepdims=True))
        alpha = jnp.exp(m_prev - m_new)
        ps = [jnp.exp(s - m_new) for s in ss]
        psum = ps[0]
        for p in ps[1:]:
            psum = psum + p
        pv = [_dot_nt(p, vp[...].reshape(nh * dv, page)) for p, (_, vp) in zip(ps, kvs)]
        tot = pv[0]
        for x in pv[1:]:
            tot = tot + x
        l_scr[...] = alpha * l_scr[...] + jnp.sum(psum, axis=-1, keepdims=True)
        acc_scr[...] = alpha * acc_scr[...] + tot
        m_scr[...] = m_new

    @pl.when(s_id < last)
    def _():
        blocks(list(zip(kpages, vpages)))

    @pl.when(s_id == last)
    def _():
        blocks([(knew_ref, vnew_ref)])
        accn = acc_scr[...] / l_scr[...]
        for h in range(nh):
            cols = slice(h * dv, (h + 1) * dv)
            if nm == 1:
                o_ref[:, cols] = accn[h * tp:(h + 1) * tp, cols]
            else:
                att = accn[2 * h * tp:(2 * h + 1) * tp, cols] - lam_ref[0] * accn[(2 * h + 1) * tp:(2 * h + 2) * tp, cols]
                att = att * lax.rsqrt(jnp.mean(att * att, axis=-1, keepdims=True) + EPS)
                o_ref[:, cols] = att * subg_ref[...] * (1.0 - lam_init)


def _pages_first(cache):
    return jnp.transpose(cache, (0, 1, 3, 4, 2))


def _head_rows(x, tp):
    bn, t, g, w = x.shape
    xp = jnp.pad(x, ((0, 0), (0, tp - t), (0, 0), (0, 0)))
    eye = jnp.eye(g, dtype=x.dtype)
    return jnp.einsum('btgw,gk->bgtkw', xp, eye).reshape(bn, g * tp, g * w)


def _paged_attn(page_table, li, q_rows, addmask, cache_k, cache_v, k_new, v_new, lam, subg, *, gp, nh, nm, tp, lam_init):
    bn, rows, w = q_rows.shape
    page, hd, dv = cache_k.shape[2], cache_k.shape[4], cache_v.shape[4]
    t = k_new.shape[1]
    npages = page_table.shape[1]
    nsteps = npages // gp + 1

    def page_map(g):
        return lambda b, s, pt: (li, pt[b, jnp.minimum(s * gp + g, npages - 1)], 0, 0, 0)

    new_t = lambda a: jnp.transpose(jnp.pad(a, ((0, 0), (0, page - t), (0, 0), (0, 0))), (0, 2, 3, 1))
    in_specs = ([pl.BlockSpec((None, rows, w), lambda b, s, pt: (b, 0, 0)),
                 pl.BlockSpec((None, rows, gp * page), lambda b, s, pt: (b, 0, s))]
                + [pl.BlockSpec((None, None, nh, hd, page), page_map(g)) for g in range(gp)]
                + [pl.BlockSpec((None, None, nh, dv, page), page_map(g)) for g in range(gp)]
                + [pl.BlockSpec((None, nh, hd, page), lambda b, s, pt: (b, 0, 0, 0)),
                   pl.BlockSpec((None, nh, dv, page), lambda b, s, pt: (b, 0, 0, 0))])
    args = [q_rows, addmask] + [_pages_first(cache_k)] * gp + [_pages_first(cache_v)] * gp + [new_t(k_new), new_t(v_new)]
    if nm == 2:
        in_specs += [pl.BlockSpec(memory_space=pltpu.SMEM), pl.BlockSpec((1, dv), lambda b, s, pt: (0, 0))]
        args += [lam, subg]
    return pl.pallas_call(
        functools.partial(_paged_attn_kernel, gp=gp, nh=nh, nm=nm, tp=tp, lam_init=lam_init),
        grid_spec=pltpu.PrefetchScalarGridSpec(
            num_scalar_prefetch=1,
            grid=(bn, nsteps),
            in_specs=in_specs,
            out_specs=pl.BlockSpec((None, tp, nh * dv), lambda b, s, pt: (b, 0, 0)),
            scratch_shapes=[pltpu.VMEM((rows, 1), F32), pltpu.VMEM((rows, 1), F32), pltpu.VMEM((rows, nh * dv), F32)]),
        out_shape=jax.ShapeDtypeStruct((bn, tp, nh * dv), F32),
        compiler_params=_cparams(("arbitrary", "arbitrary")),
        name="paged_attn_dsa" if nm == 1 else "paged_attn_diff",
    )(page_table, *args)


def _gdn_kernel(xc_ref, prev_ref, buf_ref, cz_ref, tail_ref, s0_ref, cw_ref, alog_ref, dtb_ref, nw_ref,
                o_ref, sfin_ref, s_scr, *, n_valid):
    ci = pl.program_id(1)
    bg, c, ch = xc_ref.shape
    dk = C_DK
    dv = (ch - 2 * C_HEADS * dk) // C_HEADS

    @pl.when(ci == 0)
    def _():
        s_scr[...] = s0_ref[...]

    row = lax.broadcasted_iota(I32, (c, c), 0)
    colm = lax.broadcasted_iota(I32, (c, c), 1)
    incl = row >= colm
    tril = incl.astype(F32)
    eye = (row == colm).astype(F32)

    units = []
    for b in range(bg):
        hist = jnp.where(ci == 0, buf_ref[b], prev_ref[b])
        nh_rows = hist.shape[0]
        xfull = jnp.concatenate([hist, xc_ref[b]], axis=0)
        y = cw_ref[C_CONV - 1:C_CONV, :] * xfull[nh_rows:, :]
        for j in range(1, C_CONV):
            y = y + cw_ref[C_CONV - 1 - j:C_CONV - j, :] * pltpu.roll(xfull, j, 0)[nh_rows:, :]
        y = _silu(y)

        tail = tail_ref[b]
        beta_all = jax.nn.sigmoid(tail)
        sp_in = tail + dtb_ref[...]
        g_all = -jnp.exp(alog_ref[...]) * (jnp.maximum(sp_in, 0.0) + jnp.log1p(jnp.exp(-jnp.abs(sp_in))))
        if n_valid < c:
            valid = lax.broadcasted_iota(I32, tail.shape, 0) < n_valid
            beta_all = jnp.where(valid, beta_all, 0.0)
            g_all = jnp.where(valid, g_all, 0.0)
        gc_all = _dot_hi(tril, g_all)
        gc_rows = jnp.transpose(gc_all)

        for h in range(C_HEADS):
            q = y[:, h * dk:(h + 1) * dk]
            k = y[:, (C_HEADS + h) * dk:(C_HEADS + h + 1) * dk]
            v = y[:, 2 * C_HEADS * dk + h * dv:2 * C_HEADS * dk + (h + 1) * dv]
            q = q * lax.rsqrt(jnp.sum(q * q, axis=-1, keepdims=True) + EPS) * (dk ** -0.5)
            k = k * lax.rsqrt(jnp.sum(k * k, axis=-1, keepdims=True) + EPS)
            beta = beta_all[:, h:h + 1]
            gc = gc_all[:, C_HEADS + h:C_HEADS + h + 1]
            decay = jnp.exp(jnp.where(incl, gc - gc_rows[C_HEADS + h:C_HEADS + h + 1, :], -jnp.inf))
            units.append(dict(b=b, h=h, q=q, k=k, kb=k * beta, vb=v * beta, gc=gc, decay=decay))

    def stage(fn):
        return [fn(un) for un in units]

    def put(name, vals):
        for un, val in zip(units, vals):
            un[name] = val

    put('a', stage(lambda un: _dot_nt(un['kb'], un['k']) * un['decay'] * (1.0 - eye)))
    put('inv', stage(lambda un: eye - un['a']))
    put('pw', stage(lambda un: un['a']))
    for _ in range(max(0, int(c - 1).bit_length() - 1)):
        put('pw', stage(lambda un: _dot_x3(un['pw'], un['pw'])))
        put('inv', stage(lambda un: un['inv'] + _dot_x3(un['inv'], un['pw'])))
    put('u', stage(lambda un: _dot_x3(un['inv'], un['vb'])))
    put('w', stage(lambda un: _dot_x3(un['inv'], un['kb'] * jnp.exp(un['gc']))))
    put('s', stage(lambda un: s_scr[un['b'], un['h']]))
    put('v_new', stage(lambda un: un['u'] - _dot(un['w'], un['s'])))
    put('intra', stage(lambda un: _dot_nt(un['q'], un['k']) * un['decay']))
    put('o', stage(lambda un: _dot(un['q'] * jnp.exp(un['gc']), un['s']) + _dot(un['intra'], un['v_new'])))
    for un in units:
        gl = un['gc'][c - 1:c, :]
        s_scr[un['b'], un['h']] = un['s'] * jnp.exp(gl) + _dot_tn(un['k'] * jnp.exp(gl - un['gc']), un['v_new'])
    for un in units:
        b, h, o = un['b'], un['h'], un['o']
        o = o * lax.rsqrt(jnp.mean(o * o, axis=-1, keepdims=True) + EPS) * nw_ref[...]
        o_ref[b, :, h * dv:(h + 1) * dv] = (o * _silu(cz_ref[b, :, h * dv:(h + 1) * dv])).astype(o_ref.dtype)

    @pl.when(ci == pl.num_programs(1) - 1)
    def _():
        sfin_ref[...] = s_scr[...]


def _gdn(xc, buf, cz, tail, s0, conv_w, a_log, dt_bias, norm_w, chunk, n_valid):
    bn, t, ch = xc.shape
    hw = cz.shape[2]
    nck = t // chunk
    dvh = hw // C_HEADS
    hist = SUBLANES
    per_chunk = chunk // hist
    bg = 2 if bn % 2 == 0 else 1
    gate_lanes = lambda p: jnp.pad(p.astype(F32), (C_HEADS, LANES - 2 * C_HEADS)).reshape(1, LANES)
    return pl.pallas_call(
        functools.partial(_gdn_kernel, n_valid=n_valid),
        grid=(bn // bg, nck),
        in_specs=[pl.BlockSpec((bg, chunk, ch), lambda b, i: (b, i, 0)),
                  pl.BlockSpec((bg, hist, ch), lambda b, i: (b, jnp.maximum(i * per_chunk - 1, 0), 0)),
                  pl.BlockSpec((bg, hist, ch), lambda b, i: (b, 0, 0)),
                  pl.BlockSpec((bg, chunk, hw), lambda b, i: (b, i, 0)),
                  pl.BlockSpec((bg, chunk, LANES), lambda b, i: (b, i, 0)),
                  pl.BlockSpec((bg, C_HEADS, C_DK, dvh), lambda b, i: (b, 0, 0, 0)),
                  pl.BlockSpec((C_CONV, ch), lambda b, i: (0, 0)),
                  pl.BlockSpec((1, LANES), lambda b, i: (0, 0)),
                  pl.BlockSpec((1, LANES), lambda b, i: (0, 0)),
                  pl.BlockSpec((1, dvh), lambda b, i: (0, 0))],
        out_specs=[pl.BlockSpec((bg, chunk, hw), lambda b, i: (b, i, 0)),
                   pl.BlockSpec((bg, C_HEADS, C_DK, dvh), lambda b, i: (b, 0, 0, 0))],
        out_shape=[jax.ShapeDtypeStruct((bn, t, hw), BF16), jax.ShapeDtypeStruct((bn, C_HEADS, C_DK, dvh), F32)],
        scratch_shapes=[pltpu.VMEM((bg, C_HEADS, C_DK, dvh), F32)],
        compiler_params=_cparams(("arbitrary", "arbitrary")),
        name="gated_delta",
    )(xc, xc, buf, cz, tail, s0, conv_w, gate_lanes(a_log), gate_lanes(dt_bias), norm_w.reshape(1, dvh))


def _moe_kernel(be_ref, x_ref, gate_ref, w1_ref, b1_ref, w2_ref, b2_ref, o_ref, w1_scr, w2_scr):
    i = pl.program_id(0)
    dff = w2_ref.shape[0]
    changed = jnp.logical_or(i == 0, be_ref[i] != be_ref[jnp.maximum(i - 1, 0)])

    @pl.when(changed)
    def _():
        w1_scr[...] = w1_ref[...].astype(BF16)
        w2_scr[...] = w2_ref[...].astype(BF16)

    hmid = jnp.dot(x_ref[...].astype(BF16), w1_scr[...], preferred_element_type=F32) + b1_ref[...]
    gate = jnp.minimum(hmid[:, :dff], SWIGLU_LIMIT)
    up = jnp.clip(hmid[:, dff:], -SWIGLU_LIMIT, SWIGLU_LIMIT)
    act = (up + 1.0) * gate * jax.nn.sigmoid(SWIGLU_ALPHA * gate)
    y = jnp.dot(act.astype(BF16), w2_scr[...], preferred_element_type=F32) + b2_ref[...]
    o_ref[...] = y * gate_ref[...]


def _moe_ffn(xs, slot_gate, block_e, layer, w1, b1, w2, b2, bm):
    n_slots, d = xs.shape
    _, ne, _, d2 = w1.shape
    dff = w2.shape[2]
    return pl.pallas_call(
        _moe_kernel,
        grid_spec=pltpu.PrefetchScalarGridSpec(
            num_scalar_prefetch=1,
            grid=(n_slots // bm,),
            in_specs=[pl.BlockSpec((bm, d), lambda i, be: (i, 0)),
                      pl.BlockSpec((bm, 1), lambda i, be: (i, 0)),
                      pl.BlockSpec((None, None, d, d2), lambda i, be: (layer, be[i], 0, 0)),
                      pl.BlockSpec((None, None, 1, d2), lambda i, be: (layer, be[i], 0, 0)),
                      pl.BlockSpec((None, None, dff, d), lambda i, be: (layer, be[i], 0, 0)),
                      pl.BlockSpec((None, None, 1, d), lambda i, be: (layer, be[i], 0, 0))],
            out_specs=pl.BlockSpec((bm, d), lambda i, be: (i, 0)),
            scratch_shapes=[pltpu.VMEM((d, d2), BF16), pltpu.VMEM((dff, d), BF16)]),
        out_shape=jax.ShapeDtypeStruct((n_slots, d), F32),
        compiler_params=_cparams(("arbitrary",)),
        name="moe_ffn",
    )(block_e, xs, slot_gate.reshape(n_slots, 1), w1, b1.reshape(b1.shape[0], ne, 1, d2), w2, b2.reshape(b2.shape[0], ne, 1, d))


def _route_kernel(lg_ref, e_ref, g_ref, r_ref, cnt_ref, run_scr, *, ne):
    i = pl.program_id(0)
    tm = lg_ref.shape[0]

    @pl.when(i == 0)
    def _():
        run_scr[...] = jnp.zeros(run_scr.shape, F32)

    lane = lax.broadcasted_iota(I32, (tm, LANES), 1)
    lane_f = lane.astype(F32)
    x = jnp.where(lane < ne, lg_ref[...], -jnp.inf)
    vals, hots = [], []
    e_out = jnp.zeros((tm, LANES), F32)
    for j in range(TOP_K):
        m = jnp.max(x, axis=-1, keepdims=True)
        idx = jnp.min(jnp.where(x == m, lane_f, float(LANES)), axis=-1, keepdims=True)
        hot = lane_f == idx
        vals.append(m)
        hots.append(hot)
        e_out = jnp.where(lane == j, idx, e_out)
        x = jnp.where(hot, -jnp.inf, x)
    ex = [jnp.exp(v - vals[0]) for v in vals]
    denom = ex[0]
    for j in range(1, TOP_K):
        denom = denom + ex[j]
    chosen = jnp.where(hots[0], 1.0, 0.0)
    for j in range(1, TOP_K):
        chosen = chosen + jnp.where(hots[j], 1.0, 0.0)
    before = (lax.broadcasted_iota(I32, (tm, tm), 0) > lax.broadcasted_iota(I32, (tm, tm), 1)).astype(BF16)
    base = run_scr[...] + jnp.dot(before, chosen.astype(BF16), preferred_element_type=F32)
    g_out = jnp.zeros((tm, LANES), F32)
    r_out = jnp.zeros((tm, LANES), F32)
    for j in range(TOP_K):
        g_out = jnp.where(lane == j, ex[j] / denom, g_out)
        r_out = jnp.where(lane == j, jnp.sum(jnp.where(hots[j], base, 0.0), axis=-1, keepdims=True), r_out)
    e_ref[...] = e_out[:, :TOP_K].astype(I32)
    g_ref[...] = g_out[:, :TOP_K]
    r_ref[...] = r_out[:, :TOP_K].astype(I32)
    run_scr[...] = run_scr[...] + jnp.sum(chosen, axis=0, keepdims=True)

    @pl.when(i == pl.num_programs(0) - 1)
    def _():
        cnt_ref[...] = run_scr[...]


def _route(logits, ne):
    n = logits.shape[0]
    tm = next(t for t in (512, 384, 256, 128, 64, 32, 16, 8) if n % t == 0)
    small = lambda dt: jax.ShapeDtypeStruct((n, TOP_K), dt)
    return pl.pallas_call(
        functools.partial(_route_kernel, ne=ne),
        grid=(n // tm,),
        in_specs=[pl.BlockSpec((tm, LANES), lambda i: (i, 0))],
        out_specs=[pl.BlockSpec((tm, TOP_K), lambda i: (i, 0))] * 3 + [pl.BlockSpec((1, LANES), lambda i: (0, 0))],
        out_shape=[small(I32), small(F32), small(I32), jax.ShapeDtypeStruct((1, LANES), F32)],
        scratch_shapes=[pltpu.VMEM((1, LANES), F32)],
        compiler_params=_cparams(("arbitrary",)),
        name="moe_route",
    )(logits)


def _moe(h2, logits, layer, w1, b1, w2, b2, bm):
    n_tok = h2.shape[0]
    ne = w1.shape[1]
    top_idx, gates, rank, counts = _route(logits, ne)
    counts = counts[0, :ne].astype(I32)
    padded = (counts + bm - 1) // bm * bm
    pad_end = jnp.cumsum(padded)
    pad_start = pad_end - padded
    n_assign = n_tok * TOP_K
    n_blocks = -(-(n_assign + ne * (bm - 1)) // bm)
    n_slots = n_blocks * bm
    hot = top_idx[..., None] == jnp.arange(ne, dtype=I32)
    dest = (jnp.sum(jnp.where(hot, pad_start, 0), axis=-1) + rank).reshape(-1)
    block_e = jnp.minimum(jnp.sum(pad_end[None, :] <= (jnp.arange(n_blocks, dtype=I32) * bm)[:, None], axis=1),
                          ne - 1).astype(I32)
    payload = jnp.stack([jnp.repeat(jnp.arange(n_tok, dtype=I32), TOP_K),
                         lax.bitcast_convert_type(gates.reshape(-1), I32)], axis=-1)
    slots = jnp.zeros((n_slots, 2), I32).at[dest].set(payload)
    slot_gate = lax.bitcast_convert_type(slots[:, 1], F32)
    ys = _moe_ffn(h2[slots[:, 0]], slot_gate, block_e, layer, w1, b1, w2, b2, bm)
    return jnp.sum(ys[dest.reshape(n_tok, TOP_K).T], axis=0)


def _rel_bias_sample(rel_bias, past_len, t):
    dist = (past_len + np.arange(t))[:, None] - np.arange(past_len + t)[None, :]
    return _bias_by_bucket(rel_bias, _bucket_of(dist)), dist >= 0


def kernel(x_prompt, x_sample, cache_b_k, cache_b_v, cache_b_idx, state_c_rec, state_c_conv, cache_d_k, cache_d_v,
           page_table, c_prompt, c_sample, rel_bias, ada_w, ada_b, even_w_in, even_w_out, a_ln_g, a_ln_b, a_w_sp,
           a_b_sp, b_q_norm, b_k_norm, b_idx_norm, odd_w_in, odd_w_out, c_conv_w, c_a_log, c_dt_bias, c_norm_w,
           d_q_norm, d_k_norm, d_lambda, d_subln, router_w, router_b, moe_w1, moe_b1, moe_w2, moe_b2):
    bp, sp, d = x_prompt.shape
    bs, ts, _ = x_sample.shape
    depth = ada_w.shape[0]
    hw = d // 2
    page = cache_b_k.shape[2]
    past_len = page_table.shape[1] * page
    n_p, n_s = bp * sp, bs * ts
    tm_p = TOKEN_TILE
    tq = ATTN_TILE
    tp = SUBLANES
    gp = PAGES_PER_STEP if page_table.shape[1] % PAGES_PER_STEP == 0 else 1
    assert sp % tm_p == 0 and n_s % SUBLANES == 0 and ts <= tp and ts >= C_CONV - 1

    n_c = bp + bs
    c_all = jnp.pad(jnp.concatenate([c_prompt, c_sample], axis=0), ((0, -n_c % SUBLANES), (0, 0)))
    mod_all = _ada_mod(c_all, ada_w, ada_b)

    def mods(layer):
        m = mod_all[layer].reshape(-1, 6, d)
        mp = [m[:bp, i][:, None, :] for i in range(6)]
        ms = [jnp.repeat(m[bp:n_c, i], ts, axis=0) for i in range(6)]
        return mp, ms

    xp = x_prompt.reshape(n_p, d)
    xs = x_sample.reshape(n_s, d)
    tpb = sp // tm_p
    tbl = _bias_tables(rel_bias, tq)
    outs_p, outs_s = {}, {}

    for layer in range(depth):
        li = layer // 2
        mp, ms = mods(layer)
        if layer % 2 == 0:
            ws_p = jnp.where(np.tril(np.ones((A_CHUNK, A_CHUNK), bool)), a_w_sp[li], 0.0).astype(BF16)
            bsp_p = a_b_sp[li].T
            (aout, _, q, k32, kbf, v32, vbf, iq, ik32, ikbf, iw) = _even_in(
                xp, mp[0], mp[1], False, tpb, tm_p, even_w_in[li], a_ln_g[li], a_ln_b[li], ws_p, bsp_p,
                b_q_norm[li], b_k_norm[li], b_idx_norm[li])
            ksel = min(TOPK_MAX, sp // 4)
            kc = SELECT_CHUNK if sp % SELECT_CHUNK == 0 else sp
            madd = _select_mask_prompt(iq.reshape(bp, sp, hw), iw.reshape(bp, sp, IDX_HEADS),
                                       ikbf.reshape(bp, sp, IDX_DIM), ksel, kc, SELECT_ROWS)
            b_out = _prompt_attn(q.reshape(bp, sp, hw), kbf.reshape(bp, sp, hw), vbf.reshape(bp, sp, -1), tbl, madd,
                                 None, None, nh=B_HEADS, nm=1, dqk=hw // B_HEADS, dv=hw // B_HEADS, tq=tq, lam_init=0.0)
            outs_p.setdefault('b_k', []).append(k32.reshape(bp, sp, B_HEADS, -1))
            outs_p.setdefault('b_v', []).append(v32.reshape(bp, sp, B_HEADS, -1))
            outs_p.setdefault('b_idx', []).append(ik32.reshape(bp, sp, IDX_DIM))
            mix_p = (aout, b_out.reshape(n_p, hw))

            cs = min(ts, A_CHUNK)
            ws_s = jnp.where(np.tril(np.ones((cs, cs), bool)), a_w_sp[li][:, :cs, :cs], 0.0)
            ws_s = jnp.einsum('ab,gts->gatbs', jnp.eye(n_s // cs, dtype=F32), ws_s).reshape(A_GROUPS, n_s, n_s).astype(BF16)
            bsp_s = jnp.tile(a_b_sp[li][:, :cs].T, (n_s // cs, 1))
            (aout, av, q, k32, kbf, v32, vbf, iq, ik32, ikbf, iw) = _even_in(
                xs, ms[0], ms[1], True, 1, n_s, even_w_in[li], a_ln_g[li], a_ln_b[li], ws_s, bsp_s,
                b_q_norm[li], b_k_norm[li], b_idx_norm[li])
            ltot = past_len + ts
            ksel = min(TOPK_MAX, ltot // 4)
            iq_rows = jnp.pad(iq.reshape(bs, ts, IDX_HEADS, IDX_DIM), ((0, 0), (0, tp - ts), (0, 0), (0, 0)))
            iq_rows = jnp.transpose(iq_rows, (0, 2, 1, 3)).reshape(bs, IDX_HEADS * tp, IDX_DIM)
            iw_rows = jnp.pad(iw.reshape(bs, ts, IDX_HEADS), ((0, 0), (0, tp - ts), (0, 0)))
            iw_rows = jnp.transpose(iw_rows, (0, 2, 1)).reshape(bs, IDX_HEADS * tp, 1)
            pad_new = lambda a: jnp.pad(a.reshape(bs, ts, -1), ((0, 0), (0, page - ts), (0, 0)))
            sc_s = _paged_scores(page_table, li, iq_rows, iw_rows, cache_b_idx, pad_new(ikbf), gp)
            width = sc_s.shape[2]
            kc_s = gp * page
            lim_s = jnp.where(np.arange(tp) < ts, past_len + np.arange(tp), -1).astype(I32)
            lim_s = jnp.tile(lim_s, bs).reshape(bs * tp, 1)
            nch_s = jnp.full((bs * tp // 128,), width // kc_s, I32)
            sel_s = _select_mask(sc_s.reshape(bs * tp, width), lim_s, nch_s, ksel, kc_s, F32).reshape(bs, tp, width)
            bias_s, _ = _rel_bias_sample(rel_bias, past_len, ts)
            bias_s = jnp.pad(bias_s, ((0, 0), (0, tp - ts), (0, width - ltot)))
            am = (sel_s[:, None] + bias_s[None]).reshape(bs, B_HEADS * tp, width)
            heads = lambda a: a.reshape(bs, ts, B_HEADS, -1)
            b_out = _paged_attn(page_table, li, _head_rows(heads(q), tp), am, cache_b_k, cache_b_v, heads(k32), heads(v32),
                                None, None, gp=gp, nh=B_HEADS, nm=1, tp=tp, lam_init=0.0)
            outs_s.setdefault('a_v', []).append(av.reshape(bs, ts, hw))
            outs_s.setdefault('b_k', []).append(k32.reshape(bs, ts, B_HEADS, -1))
            outs_s.setdefault('b_v', []).append(v32.reshape(bs, ts, B_HEADS, -1))
            outs_s.setdefault('b_idx', []).append(ik32.reshape(bs, ts, IDX_DIM))
            mix_s = (aout, b_out[:, :ts].reshape(n_s, hw).astype(BF16))
            w_out = even_w_out[li]
        else:
            lam_init = 0.8 - 0.6 * math.exp(-0.3 * layer)
            lp = d_lambda[li].astype(F32)
            lam = (jnp.exp(jnp.sum(lp[0] * lp[1])) - jnp.exp(jnp.sum(lp[2] * lp[3])) + lam_init).reshape(1)
            subg = d_subln[li].reshape(1, -1)
            dvh = hw // D_HEADS
            xc, cz, tail, q2, k2_32, k2bf, v2_32, v2bf = _odd_in(xp, mp[0], mp[1], False, tpb, tm_p, odd_w_in[li],
                                                                d_q_norm[li], d_k_norm[li])
            ch = xc.shape[1]
            xc3 = xc.reshape(bp, sp, ch)
            zbuf = jnp.zeros((bp, SUBLANES, ch), F32)
            s0 = jnp.zeros((bp, C_HEADS, C_DK, hw // C_HEADS), F32)
            chunk = C_CHUNK if sp % C_CHUNK == 0 else sp
            c_out, c_rec = _gdn(xc3, zbuf, cz.reshape(bp, sp, hw), tail.reshape(bp, sp, LANES), s0, c_conv_w[li],
                                c_a_log[li], c_dt_bias[li], c_norm_w[li], chunk, chunk)
            d_out = _prompt_attn(q2.reshape(bp, sp, hw), k2bf.reshape(bp, sp, hw), v2bf.reshape(bp, sp, -1), tbl, None,
                                 lam, subg, nh=D_HEADS, nm=2, dqk=D_QK_DIM, dv=dvh, tq=tq, lam_init=lam_init)
            outs_p.setdefault('c_rec', []).append(c_rec)
            outs_p.setdefault('c_conv', []).append(xc3[:, sp - (C_CONV - 1):])
            outs_p.setdefault('d_k', []).append(k2_32.reshape(bp, sp, D_HEADS, -1))
            outs_p.setdefault('d_v', []).append(v2_32.reshape(bp, sp, D_HEADS, -1))
            mix_p = (c_out.reshape(n_p, hw), d_out.reshape(n_p, hw))

            xc, cz, tail, q2, k2_32, k2bf, v2_32, v2bf = _odd_in(xs, ms[0], ms[1], True, 1, n_s, odd_w_in[li],
                                                                d_q_norm[li], d_k_norm[li])
            pad_t = lambda a: jnp.pad(a.reshape(bs, ts, -1), ((0, 0), (0, C_CHUNK - ts), (0, 0)))
            buf = state_c_conv[li]
            buf8 = jnp.pad(buf, ((0, 0), (SUBLANES - (C_CONV - 1), 0), (0, 0)))
            c_out, c_rec = _gdn(pad_t(xc), buf8, pad_t(cz), pad_t(tail), state_c_rec[li], c_conv_w[li], c_a_log[li],
                                c_dt_bias[li], c_norm_w[li], C_CHUNK, ts)
            ltot = past_len + ts
            bias_s, allowed = _rel_bias_sample(rel_bias, past_len, ts)
            width = (page_table.shape[1] // gp + 1) * gp * page
            am = jnp.where(jnp.asarray(allowed)[None], bias_s, NEG)
            am = jnp.pad(am, ((0, 0), (0, tp - ts), (0, 0)))
            am = jnp.pad(am, ((0, 0), (0, 0), (0, width - ltot)), constant_values=NEG)
            am = jnp.broadcast_to(am[None, :, None], (bs, D_HEADS, 2, tp, width)).reshape(bs, D_HEADS * 2 * tp, width)
            q_rows = _head_rows(q2.reshape(bs, ts, 2 * D_HEADS, D_QK_DIM), tp)
            heads = lambda a: a.reshape(bs, ts, D_HEADS, -1)
            d_out = _paged_attn(page_table, li, q_rows, am, cache_d_k, cache_d_v, heads(k2_32), heads(v2_32), lam, subg,
                                gp=gp, nh=D_HEADS, nm=2, tp=tp, lam_init=lam_init)
            outs_s.setdefault('c_rec', []).append(c_rec)
            outs_s.setdefault('c_conv', []).append(
                jnp.concatenate([buf, xc.reshape(bs, ts, ch)], axis=1)[:, ts:])
            outs_s.setdefault('d_k', []).append(k2_32.reshape(bs, ts, D_HEADS, -1))
            outs_s.setdefault('d_v', []).append(v2_32.reshape(bs, ts, D_HEADS, -1))
            mix_s = (c_out[:, :ts].reshape(n_s, hw), d_out[:, :ts].reshape(n_s, hw).astype(BF16))
            w_out = odd_w_out[li]

        x1p, h2p, lgp = _out_proj(mix_p[0], mix_p[1], xp, mp[2], mp[3], mp[4], False, tpb, tm_p, w_out,
                                  router_w[layer], router_b[layer])
        x1s, h2s, lgs = _out_proj(mix_s[0], mix_s[1], xs, ms[2], ms[3], ms[4], True, 1, n_s, w_out,
                                  router_w[layer], router_b[layer])
        moe_out = _moe(jnp.concatenate([h2p, h2s], axis=0), jnp.concatenate([lgp, lgs], axis=0), layer,
                       moe_w1, moe_b1, moe_w2, moe_b2, MOE_BLOCK_ROWS)
        xp = (x1p.reshape(bp, sp, d) + mp[5] * moe_out[:n_p].reshape(bp, sp, d)).reshape(n_p, d)
        xs = x1s + ms[5] * moe_out[n_p:]

    st = lambda name, src: jnp.stack(src[name])
    return (xp.reshape(bp, sp, d), xs.reshape(bs, ts, d),
            st('b_k', outs_p), st('b_v', outs_p), st('b_idx', outs_p), st('c_rec', outs_p), st('c_conv', outs_p),
            st('d_k', outs_p), st('d_v', outs_p),
            st('a_v', outs_s), st('b_k', outs_s), st('b_v', outs_s), st('b_idx', outs_s), st('c_rec', outs_s),
            st('c_conv', outs_s), st('d_k', outs_s), st('d_v', outs_s))
```

```python
import functools
import math

import numpy as np
import jax
import jax.numpy as jnp
from jax import lax
from jax.experimental import pallas as pl
from jax.experimental.pallas import tpu as pltpu

F32 = jnp.float32
BF16 = jnp.bfloat16
I32 = jnp.int32
HIGHEST = lax.Precision.HIGHEST

A_GROUPS = 4
A_CHUNK = 128
B_HEADS = 8
IDX_HEADS = 8
IDX_DIM = 64
TOPK_MAX = 256
C_HEADS = 4
C_DK = 128
C_CONV = 4
C_CHUNK = 64
D_HEADS = 8
D_QK_DIM = 32
N_BUCKETS = 32
MAX_DISTANCE = 128
TOP_K = 4
SWIGLU_LIMIT = 7.0
SWIGLU_ALPHA = 1.702
EPS = 1e-6

LANES = 128
SUBLANES = 8
VMEM_LIMIT_BYTES = 56 * 1024 * 1024

TOKEN_TILE = 256
ATTN_TILE = 256
SELECT_ROWS = 128
SELECT_CHUNK = 512
MOE_BLOCK_ROWS = 256
PAGES_PER_STEP = 8

NEG = -1e30
SHIFT_MARGIN = 16.0
INT_MIN = -2 ** 31


def _cparams(sem):
    return pltpu.CompilerParams(dimension_semantics=sem, vmem_limit_bytes=VMEM_LIMIT_BYTES)


def _dot(a, b):
    return jnp.dot(a.astype(BF16), b.astype(BF16), preferred_element_type=F32)


def _dot_nt(a, b):
    return lax.dot_general(a.astype(BF16), b.astype(BF16), (((1,), (1,)), ((), ())), preferred_element_type=F32)


def _dot_hi(a, b):
    return jnp.dot(a, b, preferred_element_type=F32, precision=HIGHEST)


def _dot_tn(a, b):
    return lax.dot_general(a.astype(BF16), b.astype(BF16), (((0,), (0,)), ((), ())), preferred_element_type=F32)


def _dot_x3(a, b):
    ah = a.astype(BF16)
    al = (a - ah.astype(F32)).astype(BF16)
    bh = b.astype(BF16)
    bl = (b - bh.astype(F32)).astype(BF16)
    dot = lambda x, y: jnp.dot(x, y, preferred_element_type=F32)
    return dot(ah, bh) + (dot(ah, bl) + dot(al, bh))


def _silu(x):
    return x * jax.nn.sigmoid(x)


def _gelu_tanh(x):
    return 0.5 * x * (1.0 + jnp.tanh(math.sqrt(2.0 / math.pi) * (x + 0.044715 * (x * x * x))))


def _group_rms(x, ones_ref, gsize):
    xx = x * x
    hi = xx.astype(BF16)
    lo = (xx - hi.astype(F32)).astype(BF16)
    e = ones_ref[...]
    ss = jnp.dot(hi, e, preferred_element_type=F32) + jnp.dot(lo, e, preferred_element_type=F32)
    return x * lax.rsqrt(ss * (1.0 / gsize) + EPS)


def _store_v_with_ones(vx_ref, v):
    tm, w = v.shape
    nh = vx_ref.shape[1] // LANES
    dv = w // nh
    tail = jnp.where(lax.broadcasted_iota(I32, (tm, LANES - dv), 1) == 0, 1.0, 0.0)
    for h in range(nh):
        vx_ref[:, h * LANES:(h + 1) * LANES] = jnp.concatenate([v[:, h * dv:(h + 1) * dv], tail], axis=1).astype(BF16)


def _block_ones(width, gsize):
    g = np.arange(width) // gsize
    return jnp.asarray(g[:, None] == g[None, :], BF16)


def _bucket_table():
    n = np.arange(MAX_DISTANCE)
    exact = N_BUCKETS // 2
    scaled = np.log(np.maximum(n, 1).astype(np.float32) / np.float32(exact)) / np.float32(math.log(MAX_DISTANCE / exact))
    large = np.minimum(exact + (scaled.astype(np.float32) * (N_BUCKETS - exact)).astype(np.int32), N_BUCKETS - 1)
    return np.where(n < exact, n, large).astype(np.int32)


def _bucket_of(dist):
    n = np.maximum(dist, 0)
    return np.where(n < MAX_DISTANCE, _bucket_table()[np.minimum(n, MAX_DISTANCE - 1)], N_BUCKETS - 1)


def _bias_by_bucket(rel_bias, bucket):
    out = jnp.zeros((rel_bias.shape[1],) + bucket.shape, F32)
    for b in np.unique(bucket):
        out = jnp.where(jnp.asarray(bucket == b)[None], rel_bias[b].astype(F32).reshape((-1,) + (1,) * bucket.ndim), out)
    return out


def _ada_kernel(c_ref, w_ref, b_ref, o_ref):
    o_ref[...] = _dot(_silu(c_ref[...]), w_ref[...]) + b_ref[...]


def _ada_mod(c_all, ada_w, ada_b):
    depth, d, n6 = ada_w.shape
    rows = c_all.shape[0]
    tn = 1536 if n6 % 1536 == 0 else n6
    return pl.pallas_call(
        _ada_kernel,
        grid=(depth, n6 // tn),
        in_specs=[pl.BlockSpec((rows, d), lambda l, j: (0, 0)),
                  pl.BlockSpec((None, d, tn), lambda l, j: (l, 0, j)),
                  pl.BlockSpec((None, 1, tn), lambda l, j: (l, 0, j))],
        out_specs=pl.BlockSpec((None, rows, tn), lambda l, j: (l, 0, j)),
        out_shape=jax.ShapeDtypeStruct((depth, rows, n6), F32),
        compiler_params=_cparams(("arbitrary", "arbitrary")),
        name="ada_mod",
    )(c_all, ada_w, ada_b.reshape(depth, 1, n6))


def _row_spec(tm, w):
    return pl.BlockSpec((tm, w), lambda i: (i, 0))


def _row_shape(n, w, dt):
    return jax.ShapeDtypeStruct((n, w), dt)


def _mod_specs(per_token, tm, d, tiles_per_batch):
    if per_token:
        return pl.BlockSpec((tm, d), lambda i: (i, 0))
    return pl.BlockSpec((None, 1, d), lambda i: (i // tiles_per_batch, 0, 0))


def _modulated_rms(x, sh, sc):
    xn = x * lax.rsqrt(jnp.mean(x * x, axis=-1, keepdims=True) + EPS)
    return xn * (1.0 + sc) + sh


def _even_in_kernel(x_ref, sh_ref, sc_ref, wm_ref, wt_ref, lng_ref, lnb_ref, msp_ref, bsp_ref,
                    qg_ref, kg_ref, ikg_ref, e_ref,
                    aout_ref, av_ref, q_ref, k32_ref, kbf_ref, v32_ref, vx_ref, iq_ref, ik32_ref, ikbf_ref, iw_ref):
    tm = x_ref.shape[0]
    aw = aout_ref.shape[1]
    gd = aw // A_GROUPS
    h = _modulated_rms(x_ref[...], sh_ref[...], sc_ref[...]).astype(BF16)

    def seg(i):
        return jnp.dot(h, wm_ref[:, i * aw:(i + 1) * aw], preferred_element_type=F32)

    au = _gelu_tanh(seg(0))
    av = _gelu_tanh(seg(1))
    avc = av - jnp.mean(av, axis=-1, keepdims=True)
    vn = avc * lax.rsqrt(jnp.mean(avc * avc, axis=-1, keepdims=True) + EPS) * lng_ref[...] + lnb_ref[...]
    av_ref[...] = vn
    vnb = vn.astype(BF16)
    cr = msp_ref.shape[1]
    for c in range(tm // cr):
        r0 = c * cr
        for g in range(A_GROUPS):
            mixed = jnp.dot(msp_ref[g], vnb[r0:r0 + cr, g * gd:(g + 1) * gd], preferred_element_type=F32)
            mixed = mixed + bsp_ref[:, g:g + 1]
            aout_ref[r0:r0 + cr, g * gd:(g + 1) * gd] = (au[r0:r0 + cr, g * gd:(g + 1) * gd] * mixed).astype(BF16)

    hd = aw // B_HEADS
    q = _group_rms(seg(2), e_ref, hd) * qg_ref[...]
    q_ref[...] = (q * (hd ** -0.5)).astype(BF16)
    k = _group_rms(seg(3), e_ref, hd) * kg_ref[...]
    k32_ref[...] = k
    kbf_ref[...] = k.astype(BF16)
    v = seg(4)
    v32_ref[...] = v
    _store_v_with_ones(vx_ref, v)
    iq_ref[...] = (seg(5) * (IDX_DIM ** -0.5)).astype(BF16)
    tail = jnp.dot(h, wt_ref[...], preferred_element_type=F32)
    ik = tail[:, :IDX_DIM]
    ik = ik * lax.rsqrt(jnp.mean(ik * ik, axis=-1, keepdims=True) + EPS) * ikg_ref[...]
    ik32_ref[...] = ik
    ikbf_ref[...] = ik.astype(BF16)
    iw_ref[...] = tail[:, IDX_DIM:IDX_DIM + IDX_HEADS] * (IDX_HEADS ** -0.5)


def _even_in(x, sh, sc, per_token, tiles_per_batch, tm, w_in, ln_g, ln_b, msp, bsp, q_g, k_g, ik_g):
    n, d = x.shape
    aw = d // 2
    wm = w_in[:, :6 * aw].astype(BF16)
    wt = jnp.pad(w_in[:, 6 * aw:], ((0, 0), (0, LANES - (IDX_DIM + IDX_HEADS)))).astype(BF16)
    full = lambda shape: pl.BlockSpec(shape, lambda i: (0,) * len(shape))
    row = lambda w: pl.BlockSpec((tm, w), lambda i: (i, 0))
    mod = _mod_specs(per_token, tm, d, tiles_per_batch)
    outs = [(aw, BF16), (aw, F32), (aw, BF16), (aw, F32), (aw, BF16), (aw, F32), (B_HEADS * LANES, BF16),
            (aw, BF16), (IDX_DIM, F32), (IDX_DIM, BF16), (IDX_HEADS, F32)]
    return pl.pallas_call(
        _even_in_kernel,
        grid=(n // tm,),
        in_specs=[row(d), mod, mod, full(wm.shape), full(wt.shape), full((1, aw)), full((1, aw)),
                  full(msp.shape), full(bsp.shape), full((1, aw)), full((1, aw)), full((1, IDX_DIM)), full((aw, aw))],
        out_specs=[_row_spec(tm, w) for w, _ in outs],
        out_shape=[_row_shape(n, w, dt) for w, dt in outs],
        compiler_params=_cparams(("arbitrary",)),
        name="even_in",
    )(x, sh, sc, wm, wt, ln_g.reshape(1, aw), ln_b.reshape(1, aw), msp, bsp,
      jnp.tile(q_g, B_HEADS).reshape(1, aw), jnp.tile(k_g, B_HEADS).reshape(1, aw), ik_g.reshape(1, IDX_DIM),
      _block_ones(aw, aw // B_HEADS))


def _odd_in_kernel(x_ref, sh_ref, sc_ref, wm_ref, wt_ref, qg_ref, kg_ref, e_ref,
                   xc_ref, cz_ref, tail_ref, q_ref, k32_ref, kbf_ref, v32_ref, vx_ref):
    hw = cz_ref.shape[1]
    h = _modulated_rms(x_ref[...], sh_ref[...], sc_ref[...]).astype(BF16)

    def seg(i):
        return jnp.dot(h, wm_ref[:, i * hw:(i + 1) * hw], preferred_element_type=F32)

    for i in range(3):
        xc_ref[:, i * hw:(i + 1) * hw] = seg(i)
    cz_ref[...] = seg(3)
    tail_ref[...] = jnp.dot(h, wt_ref[...], preferred_element_type=F32)
    q = _group_rms(seg(4), e_ref, D_QK_DIM) * qg_ref[...]
    q_ref[...] = (q * (D_QK_DIM ** -0.5)).astype(BF16)
    k = _group_rms(seg(5), e_ref, D_QK_DIM) * kg_ref[...]
    k32_ref[...] = k
    kbf_ref[...] = k.astype(BF16)
    v = seg(6)
    v32_ref[...] = v
    _store_v_with_ones(vx_ref, v)


def _odd_in(x, sh, sc, per_token, tiles_per_batch, tm, w_in, q_g, k_g):
    n, d = x.shape
    hw = d // 2
    ng = 2 * C_HEADS
    wm = jnp.concatenate([w_in[:, :4 * hw], w_in[:, 4 * hw + ng:]], axis=1).astype(BF16)
    wt = jnp.pad(w_in[:, 4 * hw:4 * hw + ng], ((0, 0), (0, LANES - ng))).astype(BF16)
    full = lambda shape: pl.BlockSpec(shape, lambda i: (0,) * len(shape))
    row = lambda w: pl.BlockSpec((tm, w), lambda i: (i, 0))
    mod = _mod_specs(per_token, tm, d, tiles_per_batch)
    outs = [(3 * hw, F32), (hw, F32), (LANES, F32), (hw, BF16), (hw, F32), (hw, BF16), (hw, F32),
            (D_HEADS * LANES, BF16)]
    reps = hw // D_QK_DIM
    return pl.pallas_call(
        _odd_in_kernel,
        grid=(n // tm,),
        in_specs=[row(d), mod, mod, full(wm.shape), full(wt.shape), full((1, hw)), full((1, hw)), full((hw, hw))],
        out_specs=[_row_spec(tm, w) for w, _ in outs],
        out_shape=[_row_shape(n, w, dt) for w, dt in outs],
        compiler_params=_cparams(("arbitrary",)),
        name="odd_in",
    )(x, sh, sc, wm, wt, jnp.tile(q_g, reps).reshape(1, hw), jnp.tile(k_g, reps).reshape(1, hw),
      _block_ones(hw, D_QK_DIM))


def _out_proj_kernel(a_ref, b_ref, x_ref, g1_ref, sh_ref, sc_ref, w_ref, rw_ref, rb_ref, x1_ref, h2_ref, lg_ref):
    hw = a_ref.shape[1]
    y = jnp.dot(a_ref[...], w_ref[:hw, :], preferred_element_type=F32)
    y = y + jnp.dot(b_ref[...], w_ref[hw:, :], preferred_element_type=F32)
    x1 = x_ref[...] + g1_ref[...] * y
    x1_ref[...] = x1
    h2 = _modulated_rms(x1, sh_ref[...], sc_ref[...])
    h2_ref[...] = h2
    lg_ref[...] = _dot(h2, rw_ref[...]) + rb_ref[...]


def _out_proj(a, b, x, g1, sh, sc, per_token, tiles_per_batch, tm, w_out, router_w, router_b):
    n, d = x.shape
    hw = a.shape[1]
    ne = router_w.shape[1]
    rw = jnp.pad(router_w, ((0, 0), (0, LANES - ne)))
    rb = jnp.pad(router_b, (0, LANES - ne)).reshape(1, LANES)
    full = lambda shape: pl.BlockSpec(shape, lambda i: (0,) * len(shape))
    row = lambda w: pl.BlockSpec((tm, w), lambda i: (i, 0))
    mod = _mod_specs(per_token, tm, d, tiles_per_batch)
    x1, h2, lg = pl.pallas_call(
        _out_proj_kernel,
        grid=(n // tm,),
        in_specs=[row(hw), row(hw), row(d), mod, mod, mod, full((d, d)), full((d, LANES)), full((1, LANES))],
        out_specs=[row(d), row(d), row(LANES)],
        out_shape=[jax.ShapeDtypeStruct((n, d), F32), jax.ShapeDtypeStruct((n, d), F32),
                   jax.ShapeDtypeStruct((n, LANES), F32)],
        compiler_params=_cparams(("arbitrary",)),
        name="out_proj",
    )(a, b, x, g1, sh, sc, w_out.astype(BF16), rw, rb)
    return x1, h2, lg


def _select_kernel(nch_ref, s_ref, lim_ref, m_ref, key_ref, *, ksel, kc):
    _select_rows(lambda off: s_ref[:, pl.ds(off, kc)], nch_ref[pl.program_id(0)], lim_ref[...], m_ref, key_ref, ksel, kc)


def _select_prompt_kernel(iq_ref, iw_ref, ik_ref, m_ref, key_ref, *, ksel, kc):
    rows = iq_ref.shape[0]
    first = pl.program_id(1) * rows

    iq_heads = jnp.concatenate([iq_ref[:, h * IDX_DIM:(h + 1) * IDX_DIM] for h in range(IDX_HEADS)], axis=0)

    def scores(off):
        d = _dot_nt(iq_heads, ik_ref[pl.ds(off, kc), :])
        acc = jnp.zeros((rows, kc), F32)
        for h in range(IDX_HEADS):
            acc = acc + iw_ref[:, h:h + 1] * jnp.maximum(d[h * rows:(h + 1) * rows], 0.0)
        return acc

    lim = first + lax.broadcasted_iota(I32, (rows, 1), 0)
    _select_rows(scores, (first + rows - 1) // kc + 1, lim, m_ref, key_ref, ksel, kc)


def _select_rows(score_chunk, nch, lim, m_ref, key_ref, ksel, kc):
    rows, width = m_ref.shape
    fold = kc // LANES
    col_bits = max(1, int(width - 1).bit_length())

    def cols(c):
        return c * kc + lax.broadcasted_iota(I32, (rows, kc), 1)

    def fill(c, carry):
        off = pl.multiple_of(c * kc, kc)
        bits = lax.bitcast_convert_type(score_chunk(off) + 0.0, I32)
        key = jnp.where(bits < 0, bits ^ jnp.int32(0x7FFFFFFF), bits)
        key_ref[:, pl.ds(off, kc)] = jnp.where(cols(c) <= lim, key, jnp.int32(INT_MIN))
        return carry

    lax.fori_loop(0, nch, fill, 0)

    def count(pred):
        def body(c, acc):
            off = pl.multiple_of(c * kc, kc)
            hit = jnp.where(pred(c, key_ref[:, pl.ds(off, kc)]), 1.0, 0.0)
            part = hit[:, :LANES]
            for f in range(1, fold):
                part = part + hit[:, f * LANES:(f + 1) * LANES]
            return acc + part
        acc = lax.fori_loop(0, nch, body, jnp.zeros((rows, LANES), F32))
        return jnp.sum(acc, axis=1, keepdims=True)

    def thr_bit(b, carry):
        prefix, n_ge = carry
        cand = prefix + (jnp.int32(1) << (31 - b))
        cnt = count(lambda c, key: key >= cand)
        take = cnt >= ksel
        return jnp.where(take, cand, prefix), jnp.where(take, cnt, n_ge)

    thr, n_ge = lax.fori_loop(0, 32, thr_bit, (jnp.full((rows, 1), INT_MIN, I32),
                                               jnp.full((rows, 1), 1.0, F32) * (nch * kc).astype(F32)))

    def tie_break():
        need = ksel - count(lambda c, key: key > thr)

        def tie_bit(b, pos):
            cand = pos + (jnp.int32(1) << (col_bits - 1 - b))
            cnt = count(lambda c, key: (key == thr) & (cols(c) < cand))
            return jnp.where(cnt < need, cand, pos)

        return lax.fori_loop(0, col_bits, tie_bit, jnp.zeros((rows, 1), I32))

    surplus = jnp.where((n_ge > ksel) & (thr > jnp.int32(INT_MIN)), 1.0, 0.0)
    last = lax.cond(jnp.max(surplus) > 0.0, tie_break, lambda: jnp.full((rows, 1), width, I32))

    def emit(c, carry):
        off = pl.multiple_of(c * kc, kc)
        key = key_ref[:, pl.ds(off, kc)]
        col = cols(c)
        sel = ((key > thr) | ((key == thr) & (col <= last))) & (col <= lim)
        m_ref[:, pl.ds(off, kc)] = jnp.where(sel, 0.0, NEG).astype(m_ref.dtype)
        return carry

    lax.fori_loop(0, nch, emit, 0)

    def blank(c, carry):
        off = pl.multiple_of(c * kc, kc)
        m_ref[:, pl.ds(off, kc)] = jnp.full((rows, kc), NEG, m_ref.dtype)
        return carry

    lax.fori_loop(nch, width // kc, blank, 0)


def _select_mask(scores, lim, nch, ksel, kc, out_dtype):
    r, width = scores.shape
    tr = SELECT_ROWS
    return pl.pallas_call(
        functools.partial(_select_kernel, ksel=ksel, kc=kc),
        grid_spec=pltpu.PrefetchScalarGridSpec(
            num_scalar_prefetch=1,
            grid=(r // tr,),
            in_specs=[pl.BlockSpec((tr, width), lambda i, n: (i, 0)),
                      pl.BlockSpec((tr, 1), lambda i, n: (i, 0))],
            out_specs=pl.BlockSpec((tr, width), lambda i, n: (i, 0)),
            scratch_shapes=[pltpu.VMEM((tr, width), I32)]),
        out_shape=jax.ShapeDtypeStruct((r, width), out_dtype),
        compiler_params=_cparams(("arbitrary",)),
        name="select_mask",
    )(nch, scores, lim)


def _select_mask_prompt(iq, iw, ik, ksel, kc, tr):
    bn, s, _ = iq.shape
    assert s % tr == 0 and s % kc == 0 and kc >= ksel
    return pl.pallas_call(
        functools.partial(_select_prompt_kernel, ksel=ksel, kc=kc),
        grid=(bn, s // tr),
        in_specs=[pl.BlockSpec((None, tr, IDX_HEADS * IDX_DIM), lambda b, i: (b, i, 0)),
                  pl.BlockSpec((None, tr, IDX_HEADS), lambda b, i: (b, i, 0)),
                  pl.BlockSpec((None, s, IDX_DIM), lambda b, i: (b, 0, 0))],
        out_specs=pl.BlockSpec((None, tr, s), lambda b, i: (b, i, 0)),
        out_shape=jax.ShapeDtypeStruct((bn, s, s), BF16),
        scratch_shapes=[pltpu.VMEM((tr, s), I32)],
        compiler_params=_cparams(("arbitrary", "arbitrary")),
        name="select_mask_prompt",
    )(iq, iw, ik)


def _prompt_attn_kernel(*refs, nh, nm, dqk, dv, tq, has_mask, lam_init):
    refs = list(refs)
    q_ref, k_ref, vx_ref, tbl_ref = refs[:4]
    pos = 4
    madd_ref = None
    if has_mask:
        madd_ref = refs[pos]
        pos += 1
    lam_ref = subg_ref = None
    if nm == 2:
        lam_ref, subg_ref = refs[pos], refs[pos + 1]
        pos += 2
    o_ref, mx_scr, acc_scr, qz_scr, s_scr, top_scr = refs[pos:pos + 6]
    ma_scr = refs[pos + 6] if has_mask else None
    nu = nh * nm
    nf = tq // LANES
    gu = LANES // dqk
    qi = pl.program_id(1)
    mx_scr[...] = jnp.full(mx_scr.shape, -jnp.inf, F32)
    acc_scr[...] = jnp.zeros(acc_scr.shape, F32)

    lane = lax.broadcasted_iota(I32, (tq, LANES), 1)
    for g in range(nu // gu):
        qt = q_ref[:, g * LANES:(g + 1) * LANES]
        for i in range(gu):
            keep = (lane >= i * dqk) & (lane < (i + 1) * dqk)
            qz_scr[g, i * tq:(i + 1) * tq, :] = jnp.where(keep, qt, jnp.zeros_like(qt))

    def chunk(j, back):
        koff = pl.multiple_of(j * tq, tq)
        if has_mask:
            ma_scr[...] = madd_ref[:, pl.ds(koff, tq)].astype(F32)
        above = jnp.zeros((tq, LANES), F32)
        for g in range(nu // gu):
            stacked = _dot_nt(qz_scr[g], k_ref[pl.ds(koff, tq), g * LANES:(g + 1) * LANES])
            for i in range(gu):
                u = g * gu + i
                s = stacked[i * tq:(i + 1) * tq]
                if back is not None:
                    s = s + tbl_ref[u // nm, back]
                if has_mask:
                    s = s + ma_scr[...]
                s_scr[u] = s
                r = s[:, :LANES]
                for f in range(1, nf):
                    r = jnp.maximum(r, s[:, f * LANES:(f + 1) * LANES])
                top_scr[u] = r
                above = jnp.maximum(above, jnp.where(r > mx_scr[u] + SHIFT_MARGIN, 1.0, 0.0))

        @pl.when(jnp.max(above) > 0.0)
        def _():
            for u in range(nu):
                old = mx_scr[u]
                new = jnp.maximum(old, jnp.broadcast_to(jnp.max(top_scr[u], axis=-1, keepdims=True), (tq, LANES)))
                acc_scr[u] = acc_scr[u] * jnp.exp(old - new)
                mx_scr[u] = new

        for g in range(nu // gu):
            ps = []
            for i in range(gu):
                u = g * gu + i
                mx = mx_scr[u]
                p = jnp.concatenate([jnp.exp(s_scr[u, :, f * LANES:(f + 1) * LANES] - mx) for f in range(nf)], axis=1)
                ps.append(p.astype(BF16))
            for i0 in range(0, gu, nm):
                u0 = g * gu + i0
                h = u0 // nm
                stack = ps[i0] if nm == 1 else jnp.concatenate(ps[i0:i0 + nm], axis=0)
                pv = jnp.dot(stack, vx_ref[pl.ds(koff, tq), h * LANES:(h + 1) * LANES], preferred_element_type=F32)
                for m in range(nm):
                    acc_scr[u0 + m] += pv[m * tq:(m + 1) * tq]

    def far(j, carry):
        chunk(j, None)
        return carry

    lax.fori_loop(0, jnp.maximum(qi - 1, 0), far, 0)

    @pl.when(qi >= 1)
    def _():
        chunk(qi - 1, 1)

    chunk(qi, 0)

    def normalised(u):
        a = acc_scr[u]
        return a[:, :dv] / a[:, dv:dv + 1]

    for h in range(nh):
        if nm == 1:
            o_ref[:, h * dv:(h + 1) * dv] = normalised(h).astype(o_ref.dtype)
        else:
            att = normalised(2 * h) - lam_ref[0] * normalised(2 * h + 1)
            att = att * lax.rsqrt(jnp.mean(att * att, axis=-1, keepdims=True) + EPS)
            o_ref[:, h * dv:(h + 1) * dv] = (att * subg_ref[...] * (1.0 - lam_init)).astype(o_ref.dtype)


def _bias_tables(rel_bias, tq):
    far = rel_bias[N_BUCKETS - 1].astype(F32)[:, None, None]
    r = np.arange(tq)[:, None]
    c = np.arange(tq)[None, :]
    tabs = []
    for d in range(2):
        dist = d * tq + r - c
        vals = _bias_by_bucket(rel_bias, _bucket_of(dist)) - far
        tabs.append(jnp.where(jnp.asarray(dist >= 0)[None], vals, NEG))
    return jnp.stack(tabs, axis=1)


def _prompt_attn(q, k, vx, tbl, madd, lam, subg, *, nh, nm, dqk, dv, tq, lam_init):
    bn, s, w = q.shape
    assert MAX_DISTANCE <= tq and s % tq == 0 and tq % LANES == 0 and dv < LANES
    has_mask = madd is not None
    once = lambda width: pl.BlockSpec((None, s, width), lambda b, i: (b, 0, 0), pipeline_mode=pl.Buffered(1))
    in_specs = [pl.BlockSpec((None, tq, w), lambda b, i: (b, i, 0)), once(w), once(nh * LANES),
                pl.BlockSpec(tbl.shape, lambda b, i: (0, 0, 0, 0), pipeline_mode=pl.Buffered(1))]
    args = [q, k, vx, tbl]
    if has_mask:
        in_specs.append(pl.BlockSpec((None, tq, s), lambda b, i: (b, i, 0)))
        args.append(madd)
    if nm == 2:
        in_specs += [pl.BlockSpec(memory_space=pltpu.SMEM), pl.BlockSpec((1, dv), lambda b, i: (0, 0))]
        args += [lam, subg]
    return pl.pallas_call(
        functools.partial(_prompt_attn_kernel, nh=nh, nm=nm, dqk=dqk, dv=dv, tq=tq, has_mask=has_mask, lam_init=lam_init),
        grid=(bn, s // tq),
        in_specs=in_specs,
        out_specs=pl.BlockSpec((None, tq, nh * dv), lambda b, i: (b, i, 0)),
        out_shape=jax.ShapeDtypeStruct((bn, s, nh * dv), BF16),
        scratch_shapes=[pltpu.VMEM((nh * nm, tq, LANES), F32), pltpu.VMEM((nh * nm, tq, LANES), F32),
                        pltpu.VMEM((nh * nm * dqk // LANES, LANES // dqk * tq, LANES), BF16),
                        pltpu.VMEM((nh * nm, tq, tq), F32), pltpu.VMEM((nh * nm, tq, LANES), F32)]
                       + ([pltpu.VMEM((tq, tq), F32)] if has_mask else []),
        compiler_params=_cparams(("arbitrary", "arbitrary")),
        name="prompt_attn_dsa" if nm == 1 else "prompt_attn_diff",
    )(*args)


def _paged_scores_kernel(pt_ref, iq_ref, iw_ref, *refs, gp, page):
    pages = refs[:gp]
    new_ref, o_ref = refs[gp], refs[gp + 1]
    s_id = pl.program_id(1)
    last = pl.num_programs(1) - 1
    rows = iq_ref.shape[0]
    tp = rows // IDX_HEADS

    def score(ik_t):
        d = jnp.maximum(_dot(iq_ref[...], ik_t), 0.0) * iw_ref[...]
        acc = d[:tp]
        for h in range(1, IDX_HEADS):
            acc = acc + d[h * tp:(h + 1) * tp]
        return acc

    @pl.when(s_id < last)
    def _():
        for g in range(gp):
            o_ref[:, g * page:(g + 1) * page] = score(pages[g][...])

    @pl.when(s_id == last)
    def _():
        o_ref[...] = jnp.zeros(o_ref.shape, F32)
        o_ref[:, :page] = score(new_ref[...])


def _paged_scores(page_table, li, iq_rows, iw_rows, cache_idx, ik_new, gp):
    bn, rows, _ = iq_rows.shape
    tp = rows // IDX_HEADS
    page = cache_idx.shape[2]
    npages = page_table.shape[1]
    nsteps = npages // gp + 1
    width = nsteps * gp * page

    def page_map(g):
        return lambda b, s, pt: (li, pt[b, jnp.minimum(s * gp + g, npages - 1)], 0, 0)

    return pl.pallas_call(
        functools.partial(_paged_scores_kernel, gp=gp, page=page),
        grid_spec=pltpu.PrefetchScalarGridSpec(
            num_scalar_prefetch=1,
            grid=(bn, nsteps),
            in_specs=[pl.BlockSpec((None, rows, IDX_DIM), lambda b, s, pt: (b, 0, 0)),
                      pl.BlockSpec((None, rows, 1), lambda b, s, pt: (b, 0, 0))]
                     + [pl.BlockSpec((None, None, IDX_DIM, page), page_map(g)) for g in range(gp)]
                     + [pl.BlockSpec((None, IDX_DIM, page), lambda b, s, pt: (b, 0, 0))],
            out_specs=pl.BlockSpec((None, tp, gp * page), lambda b, s, pt: (b, 0, s))),
        out_shape=jax.ShapeDtypeStruct((bn, tp, width), F32),
        compiler_params=_cparams(("arbitrary", "arbitrary")),
        name="paged_idx_scores",
    )(page_table, iq_rows, iw_rows, *([jnp.transpose(cache_idx, (0, 1, 3, 2))] * gp), jnp.transpose(ik_new, (0, 2, 1)))


def _paged_attn_kernel(pt_ref, q_ref, am_ref, *refs, gp, nh, nm, tp, lam_init):
    kpages = refs[:gp]
    vpages = refs[gp:2 * gp]
    knew_ref, vnew_ref = refs[2 * gp], refs[2 * gp + 1]
    pos = 2 * gp + 2
    lam_ref = subg_ref = None
    if nm == 2:
        lam_ref, subg_ref = refs[pos], refs[pos + 1]
        pos += 2
    o_ref, m_scr, l_scr, acc_scr = refs[pos:pos + 4]
    _, hd, page = knew_ref.shape
    dv = vnew_ref.shape[1]
    s_id = pl.program_id(1)
    last = pl.num_programs(1) - 1

    @pl.when(s_id == 0)
    def _():
        m_scr[...] = jnp.full(m_scr.shape, -jnp.inf, F32)
        l_scr[...] = jnp.zeros(l_scr.shape, F32)
        acc_scr[...] = jnp.zeros(acc_scr.shape, F32)

    def blocks(kvs):
        ss = [_dot(q_ref[...], kp[...].reshape(nh * hd, page)) + am_ref[:, g * page:(g + 1) * page]
              for g, (kp, _) in enumerate(kvs)]
        mx = ss[0]
        for s in ss[1:]:
            mx = jnp.maximum(mx, s)
        m_prev = m_scr[...]
        m_new = jnp.maximum(m_prev, jnp.max(mx, axis=-1, keepdims=True))
        alpha = jnp.exp(m_prev - m_new)
        ps = [jnp.exp(s - m_new) for s in ss]
        psum = ps[0]
        for p in ps[1:]:
            psum = psum + p
        pv = [_dot_nt(p, vp[...].reshape(nh * dv, page)) for p, (_, vp) in zip(ps, kvs)]
        tot = pv[0]
        for x in pv[1:]:
            tot = tot + x
        l_scr[...] = alpha * l_scr[...] + jnp.sum(psum, axis=-1, keepdims=True)
        acc_scr[...] = alpha * acc_scr[...] + tot
        m_scr[...] = m_new

    @pl.when(s_id < last)
    def _():
        blocks(list(zip(kpages, vpages)))

    @pl.when(s_id == last)
    def _():
        blocks([(knew_ref, vnew_ref)])
        accn = acc_scr[...] / l_scr[...]
        for h in range(nh):
            cols = slice(h * dv, (h + 1) * dv)
            if nm == 1:
                o_ref[:, cols] = accn[h * tp:(h + 1) * tp, cols]
            else:
                att = accn[2 * h * tp:(2 * h + 1) * tp, cols] - lam_ref[0] * accn[(2 * h + 1) * tp:(2 * h + 2) * tp, cols]
                att = att * lax.rsqrt(jnp.mean(att * att, axis=-1, keepdims=True) + EPS)
                o_ref[:, cols] = att * subg_ref[...] * (1.0 - lam_init)


def _pages_first(cache):
    return jnp.transpose(cache, (0, 1, 3, 4, 2))


def _head_rows(x, tp):
    bn, t, g, w = x.shape
    xp = jnp.pad(x, ((0, 0), (0, tp - t), (0, 0), (0, 0)))
    eye = jnp.eye(g, dtype=x.dtype)
    return jnp.einsum('btgw,gk->bgtkw', xp, eye).reshape(bn, g * tp, g * w)


def _paged_attn(page_table, li, q_rows, addmask, cache_k, cache_v, k_new, v_new, lam, subg, *, gp, nh, nm, tp, lam_init):
    bn, rows, w = q_rows.shape
    page, hd, dv = cache_k.shape[2], cache_k.shape[4], cache_v.shape[4]
    t = k_new.shape[1]
    npages = page_table.shape[1]
    nsteps = npages // gp + 1

    def page_map(g):
        return lambda b, s, pt: (li, pt[b, jnp.minimum(s * gp + g, npages - 1)], 0, 0, 0)

    new_t = lambda a: jnp.transpose(jnp.pad(a, ((0, 0), (0, page - t), (0, 0), (0, 0))), (0, 2, 3, 1))
    in_specs = ([pl.BlockSpec((None, rows, w), lambda b, s, pt: (b, 0, 0)),
                 pl.BlockSpec((None, rows, gp * page), lambda b, s, pt: (b, 0, s))]
                + [pl.BlockSpec((None, None, nh, hd, page), page_map(g)) for g in range(gp)]
                + [pl.BlockSpec((None, None, nh, dv, page), page_map(g)) for g in range(gp)]
                + [pl.BlockSpec((None, nh, hd, page), lambda b, s, pt: (b, 0, 0, 0)),
                   pl.BlockSpec((None, nh, dv, page), lambda b, s, pt: (b, 0, 0, 0))])
    args = [q_rows, addmask] + [_pages_first(cache_k)] * gp + [_pages_first(cache_v)] * gp + [new_t(k_new), new_t(v_new)]
    if nm == 2:
        in_specs += [pl.BlockSpec(memory_space=pltpu.SMEM), pl.BlockSpec((1, dv), lambda b, s, pt: (0, 0))]
        args += [lam, subg]
    return pl.pallas_call(
        functools.partial(_paged_attn_kernel, gp=gp, nh=nh, nm=nm, tp=tp, lam_init=lam_init),
        grid_spec=pltpu.PrefetchScalarGridSpec(
            num_scalar_prefetch=1,
            grid=(bn, nsteps),
            in_specs=in_specs,
            out_specs=pl.BlockSpec((None, tp, nh * dv), lambda b, s, pt: (b, 0, 0)),
            scratch_shapes=[pltpu.VMEM((rows, 1), F32), pltpu.VMEM((rows, 1), F32), pltpu.VMEM((rows, nh * dv), F32)]),
        out_shape=jax.ShapeDtypeStruct((bn, tp, nh * dv), F32),
        compiler_params=_cparams(("arbitrary", "arbitrary")),
        name="paged_attn_dsa" if nm == 1 else "paged_attn_diff",
    )(page_table, *args)


def _gdn_kernel(xc_ref, prev_ref, buf_ref, cz_ref, tail_ref, s0_ref, cw_ref, alog_ref, dtb_ref, nw_ref,
                o_ref, sfin_ref, s_scr, *, n_valid):
    ci = pl.program_id(1)
    bg, c, ch = xc_ref.shape
    dk = C_DK
    dv = (ch - 2 * C_HEADS * dk) // C_HEADS

    @pl.when(ci == 0)
    def _():
        s_scr[...] = s0_ref[...]

    row = lax.broadcasted_iota(I32, (c, c), 0)
    colm = lax.broadcasted_iota(I32, (c, c), 1)
    incl = row >= colm
    tril = incl.astype(F32)
    eye = (row == colm).astype(F32)

    units = []
    for b in range(bg):
        hist = jnp.where(ci == 0, buf_ref[b], prev_ref[b])
        nh_rows = hist.shape[0]
        xfull = jnp.concatenate([hist, xc_ref[b]], axis=0)
        y = cw_ref[C_CONV - 1:C_CONV, :] * xfull[nh_rows:, :]
        for j in range(1, C_CONV):
            y = y + cw_ref[C_CONV - 1 - j:C_CONV - j, :] * pltpu.roll(xfull, j, 0)[nh_rows:, :]
        y = _silu(y)

        tail = tail_ref[b]
        beta_all = jax.nn.sigmoid(tail)
        sp_in = tail + dtb_ref[...]
        g_all = -jnp.exp(alog_ref[...]) * (jnp.maximum(sp_in, 0.0) + jnp.log1p(jnp.exp(-jnp.abs(sp_in))))
        if n_valid < c:
            valid = lax.broadcasted_iota(I32, tail.shape, 0) < n_valid
            beta_all = jnp.where(valid, beta_all, 0.0)
            g_all = jnp.where(valid, g_all, 0.0)
        gc_all = _dot_hi(tril, g_all)
        gc_rows = jnp.transpose(gc_all)

        for h in range(C_HEADS):
            q = y[:, h * dk:(h + 1) * dk]
            k = y[:, (C_HEADS + h) * dk:(C_HEADS + h + 1) * dk]
            v = y[:, 2 * C_HEADS * dk + h * dv:2 * C_HEADS * dk + (h + 1) * dv]
            q = q * lax.rsqrt(jnp.sum(q * q, axis=-1, keepdims=True) + EPS) * (dk ** -0.5)
            k = k * lax.rsqrt(jnp.sum(k * k, axis=-1, keepdims=True) + EPS)
            beta = beta_all[:, h:h + 1]
            gc = gc_all[:, C_HEADS + h:C_HEADS + h + 1]
            decay = jnp.exp(jnp.where(incl, gc - gc_rows[C_HEADS + h:C_HEADS + h + 1, :], -jnp.inf))
            units.append(dict(b=b, h=h, q=q, k=k, kb=k * beta, vb=v * beta, gc=gc, decay=decay))

    def stage(fn):
        return [fn(un) for un in units]

    def put(name, vals):
        for un, val in zip(units, vals):
            un[name] = val

    put('a', stage(lambda un: _dot_nt(un['kb'], un['k']) * un['decay'] * (1.0 - eye)))
    put('inv', stage(lambda un: eye - un['a']))
    put('pw', stage(lambda un: un['a']))
    for _ in range(max(0, int(c - 1).bit_length() - 1)):
        put('pw', stage(lambda un: _dot_x3(un['pw'], un['pw'])))
        put('inv', stage(lambda un: un['inv'] + _dot_x3(un['inv'], un['pw'])))
    put('u', stage(lambda un: _dot_x3(un['inv'], un['vb'])))
    put('w', stage(lambda un: _dot_x3(un['inv'], un['kb'] * jnp.exp(un['gc']))))
    put('s', stage(lambda un: s_scr[un['b'], un['h']]))
    put('v_new', stage(lambda un: un['u'] - _dot(un['w'], un['s'])))
    put('intra', stage(lambda un: _dot_nt(un['q'], un['k']) * un['decay']))
    put('o', stage(lambda un: _dot(un['q'] * jnp.exp(un['gc']), un['s']) + _dot(un['intra'], un['v_new'])))
    for un in units:
        gl = un['gc'][c - 1:c, :]
        s_scr[un['b'], un['h']] = un['s'] * jnp.exp(gl) + _dot_tn(un['k'] * jnp.exp(gl - un['gc']), un['v_new'])
    for un in units:
        b, h, o = un['b'], un['h'], un['o']
        o = o * lax.rsqrt(jnp.mean(o * o, axis=-1, keepdims=True) + EPS) * nw_ref[...]
        o_ref[b, :, h * dv:(h + 1) * dv] = (o * _silu(cz_ref[b, :, h * dv:(h + 1) * dv])).astype(o_ref.dtype)

    @pl.when(ci == pl.num_programs(1) - 1)
    def _():
        sfin_ref[...] = s_scr[...]


def _gdn(xc, buf, cz, tail, s0, conv_w, a_log, dt_bias, norm_w, chunk, n_valid):
    bn, t, ch = xc.shape
    hw = cz.shape[2]
    nck = t // chunk
    dvh = hw // C_HEADS
    hist = SUBLANES
    per_chunk = chunk // hist
    bg = 2 if bn % 2 == 0 else 1
    gate_lanes = lambda p: jnp.pad(p.astype(F32), (C_HEADS, LANES - 2 * C_HEADS)).reshape(1, LANES)
    return pl.pallas_call(
        functools.partial(_gdn_kernel, n_valid=n_valid),
        grid=(bn // bg, nck),
        in_specs=[pl.BlockSpec((bg, chunk, ch), lambda b, i: (b, i, 0)),
                  pl.BlockSpec((bg, hist, ch), lambda b, i: (b, jnp.maximum(i * per_chunk - 1, 0), 0)),
                  pl.BlockSpec((bg, hist, ch), lambda b, i: (b, 0, 0)),
                  pl.BlockSpec((bg, chunk, hw), lambda b, i: (b, i, 0)),
                  pl.BlockSpec((bg, chunk, LANES), lambda b, i: (b, i, 0)),
                  pl.BlockSpec((bg, C_HEADS, C_DK, dvh), lambda b, i: (b, 0, 0, 0)),
                  pl.BlockSpec((C_CONV, ch), lambda b, i: (0, 0)),
                  pl.BlockSpec((1, LANES), lambda b, i: (0, 0)),
                  pl.BlockSpec((1, LANES), lambda b, i: (0, 0)),
                  pl.BlockSpec((1, dvh), lambda b, i: (0, 0))],
        out_specs=[pl.BlockSpec((bg, chunk, hw), lambda b, i: (b, i, 0)),
                   pl.BlockSpec((bg, C_HEADS, C_DK, dvh), lambda b, i: (b, 0, 0, 0))],
        out_shape=[jax.ShapeDtypeStruct((bn, t, hw), BF16), jax.ShapeDtypeStruct((bn, C_HEADS, C_DK, dvh), F32)],
        scratch_shapes=[pltpu.VMEM((bg, C_HEADS, C_DK, dvh), F32)],
        compiler_params=_cparams(("arbitrary", "arbitrary")),
        name="gated_delta",
    )(xc, xc, buf, cz, tail, s0, conv_w, gate_lanes(a_log), gate_lanes(dt_bias), norm_w.reshape(1, dvh))


def _moe_kernel(be_ref, x_ref, gate_ref, w1_ref, b1_ref, w2_ref, b2_ref, o_ref, w1_scr, w2_scr):
    i = pl.program_id(0)
    dff = w2_ref.shape[0]
    changed = jnp.logical_or(i == 0, be_ref[i] != be_ref[jnp.maximum(i - 1, 0)])

    @pl.when(changed)
    def _():
        w1_scr[...] = w1_ref[...].astype(BF16)
        w2_scr[...] = w2_ref[...].astype(BF16)

    hmid = jnp.dot(x_ref[...].astype(BF16), w1_scr[...], preferred_element_type=F32) + b1_ref[...]
    gate = jnp.minimum(hmid[:, :dff], SWIGLU_LIMIT)
    up = jnp.clip(hmid[:, dff:], -SWIGLU_LIMIT, SWIGLU_LIMIT)
    act = (up + 1.0) * gate * jax.nn.sigmoid(SWIGLU_ALPHA * gate)
    y = jnp.dot(act.astype(BF16), w2_scr[...], preferred_element_type=F32) + b2_ref[...]
    o_ref[...] = y * gate_ref[...]


def _moe_ffn(xs, slot_gate, block_e, layer, w1, b1, w2, b2, bm):
    n_slots, d = xs.shape
    _, ne, _, d2 = w1.shape
    dff = w2.shape[2]
    return pl.pallas_call(
        _moe_kernel,
        grid_spec=pltpu.PrefetchScalarGridSpec(
            num_scalar_prefetch=1,
            grid=(n_slots // bm,),
            in_specs=[pl.BlockSpec((bm, d), lambda i, be: (i, 0)),
                      pl.BlockSpec((bm, 1), lambda i, be: (i, 0)),
                      pl.BlockSpec((None, None, d, d2), lambda i, be: (layer, be[i], 0, 0)),
                      pl.BlockSpec((None, None, 1, d2), lambda i, be: (layer, be[i], 0, 0)),
                      pl.BlockSpec((None, None, dff, d), lambda i, be: (layer, be[i], 0, 0)),
                      pl.BlockSpec((None, None, 1, d), lambda i, be: (layer, be[i], 0, 0))],
            out_specs=pl.BlockSpec((bm, d), lambda i, be: (i, 0)),
            scratch_shapes=[pltpu.VMEM((d, d2), BF16), pltpu.VMEM((dff, d), BF16)]),
        out_shape=jax.ShapeDtypeStruct((n_slots, d), F32),
        compiler_params=_cparams(("arbitrary",)),
        name="moe_ffn",
    )(block_e, xs, slot_gate.reshape(n_slots, 1), w1, b1.reshape(b1.shape[0], ne, 1, d2), w2, b2.reshape(b2.shape[0], ne, 1, d))


def _route_kernel(lg_ref, e_ref, g_ref, r_ref, cnt_ref, run_scr, *, ne):
    i = pl.program_id(0)
    tm = lg_ref.shape[0]

    @pl.when(i == 0)
    def _():
        run_scr[...] = jnp.zeros(run_scr.shape, F32)

    lane = lax.broadcasted_iota(I32, (tm, LANES), 1)
    lane_f = lane.astype(F32)
    x = jnp.where(lane < ne, lg_ref[...], -jnp.inf)
    vals, hots = [], []
    e_out = jnp.zeros((tm, LANES), F32)
    for j in range(TOP_K):
        m = jnp.max(x, axis=-1, keepdims=True)
        idx = jnp.min(jnp.where(x == m, lane_f, float(LANES)), axis=-1, keepdims=True)
        hot = lane_f == idx
        vals.append(m)
        hots.append(hot)
        e_out = jnp.where(lane == j, idx, e_out)
        x = jnp.where(hot, -jnp.inf, x)
    ex = [jnp.exp(v - vals[0]) for v in vals]
    denom = ex[0]
    for j in range(1, TOP_K):
        denom = denom + ex[j]
    chosen = jnp.where(hots[0], 1.0, 0.0)
    for j in range(1, TOP_K):
        chosen = chosen + jnp.where(hots[j], 1.0, 0.0)
    before = (lax.broadcasted_iota(I32, (tm, tm), 0) > lax.broadcasted_iota(I32, (tm, tm), 1)).astype(BF16)
    base = run_scr[...] + jnp.dot(before, chosen.astype(BF16), preferred_element_type=F32)
    g_out = jnp.zeros((tm, LANES), F32)
    r_out = jnp.zeros((tm, LANES), F32)
    for j in range(TOP_K):
        g_out = jnp.where(lane == j, ex[j] / denom, g_out)
        r_out = jnp.where(lane == j, jnp.sum(jnp.where(hots[j], base, 0.0), axis=-1, keepdims=True), r_out)
    e_ref[...] = e_out[:, :TOP_K].astype(I32)
    g_ref[...] = g_out[:, :TOP_K]
    r_ref[...] = r_out[:, :TOP_K].astype(I32)
    run_scr[...] = run_scr[...] + jnp.sum(chosen, axis=0, keepdims=True)

    @pl.when(i == pl.num_programs(0) - 1)
    def _():
        cnt_ref[...] = run_scr[...]


def _route(logits, ne):
    n = logits.shape[0]
    tm = next(t for t in (512, 384, 256, 128, 64, 32, 16, 8) if n % t == 0)
    small = lambda dt: jax.ShapeDtypeStruct((n, TOP_K), dt)
    return pl.pallas_call(
        functools.partial(_route_kernel, ne=ne),
        grid=(n // tm,),
        in_specs=[pl.BlockSpec((tm, LANES), lambda i: (i, 0))],
        out_specs=[pl.BlockSpec((tm, TOP_K), lambda i: (i, 0))] * 3 + [pl.BlockSpec((1, LANES), lambda i: (0, 0))],
        out_shape=[small(I32), small(F32), small(I32), jax.ShapeDtypeStruct((1, LANES), F32)],
        scratch_shapes=[pltpu.VMEM((1, LANES), F32)],
        compiler_params=_cparams(("arbitrary",)),
        name="moe_route",
    )(logits)


def _moe(h2, logits, layer, w1, b1, w2, b2, bm):
    n_tok = h2.shape[0]
    ne = w1.shape[1]
    top_idx, gates, rank, counts = _route(logits, ne)
    counts = counts[0, :ne].astype(I32)
    padded = (counts + bm - 1) // bm * bm
    pad_end = jnp.cumsum(padded)
    pad_start = pad_end - padded
    n_assign = n_tok * TOP_K
    n_blocks = -(-(n_assign + ne * (bm - 1)) // bm)
    n_slots = n_blocks * bm
    hot = top_idx[..., None] == jnp.arange(ne, dtype=I32)
    dest = (jnp.sum(jnp.where(hot, pad_start, 0), axis=-1) + rank).reshape(-1)
    block_e = jnp.minimum(jnp.sum(pad_end[None, :] <= (jnp.arange(n_blocks, dtype=I32) * bm)[:, None], axis=1),
                          ne - 1).astype(I32)
    payload = jnp.stack([jnp.repeat(jnp.arange(n_tok, dtype=I32), TOP_K),
                         lax.bitcast_convert_type(gates.reshape(-1), I32)], axis=-1)
    slots = jnp.zeros((n_slots, 2), I32).at[dest].set(payload)
    slot_gate = lax.bitcast_convert_type(slots[:, 1], F32)
    ys = _moe_ffn(h2[slots[:, 0]], slot_gate, block_e, layer, w1, b1, w2, b2, bm)
    return jnp.sum(ys[dest.reshape(n_tok, TOP_K).T], axis=0)


def _rel_bias_sample(rel_bias, past_len, t):
    dist = (past_len + np.arange(t))[:, None] - np.arange(past_len + t)[None, :]
    return _bias_by_bucket(rel_bias, _bucket_of(dist)), dist >= 0


def kernel(x_prompt, x_sample, cache_b_k, cache_b_v, cache_b_idx, state_c_rec, state_c_conv, cache_d_k, cache_d_v,
           page_table, c_prompt, c_sample, rel_bias, ada_w, ada_b, even_w_in, even_w_out, a_ln_g, a_ln_b, a_w_sp,
           a_b_sp, b_q_norm, b_k_norm, b_idx_norm, odd_w_in, odd_w_out, c_conv_w, c_a_log, c_dt_bias, c_norm_w,
           d_q_norm, d_k_norm, d_lambda, d_subln, router_w, router_b, moe_w1, moe_b1, moe_w2, moe_b2):
    bp, sp, d = x_prompt.shape
    bs, ts, _ = x_sample.shape
    depth = ada_w.shape[0]
    hw = d // 2
    page = cache_b_k.shape[2]
    past_len = page_table.shape[1] * page
    n_p, n_s = bp * sp, bs * ts
    tm_p = TOKEN_TILE
    tq = ATTN_TILE
    tp = SUBLANES
    gp = PAGES_PER_STEP if page_table.shape[1] % PAGES_PER_STEP == 0 else 1
    assert sp % tm_p == 0 and n_s % SUBLANES == 0 and ts <= tp and ts >= C_CONV - 1

    n_c = bp + bs
    c_all = jnp.pad(jnp.concatenate([c_prompt, c_sample], axis=0), ((0, -n_c % SUBLANES), (0, 0)))
    mod_all = _ada_mod(c_all, ada_w, ada_b)

    def mods(layer):
        m = mod_all[layer].reshape(-1, 6, d)
        mp = [m[:bp, i][:, None, :] for i in range(6)]
        ms = [jnp.repeat(m[bp:n_c, i], ts, axis=0) for i in range(6)]
        return mp, ms

    xp = x_prompt.reshape(n_p, d)
    xs = x_sample.reshape(n_s, d)
    tpb = sp // tm_p
    tbl = _bias_tables(rel_bias, tq)
    outs_p, outs_s = {}, {}

    for layer in range(depth):
        li = layer // 2
        mp, ms = mods(layer)
        if layer % 2 == 0:
            ws_p = jnp.where(np.tril(np.ones((A_CHUNK, A_CHUNK), bool)), a_w_sp[li], 0.0).astype(BF16)
            bsp_p = a_b_sp[li].T
            (aout, _, q, k32, kbf, v32, vbf, iq, ik32, ikbf, iw) = _even_in(
                xp, mp[0], mp[1], False, tpb, tm_p, even_w_in[li], a_ln_g[li], a_ln_b[li], ws_p, bsp_p,
                b_q_norm[li], b_k_norm[li], b_idx_norm[li])
            ksel = min(TOPK_MAX, sp // 4)
            kc = SELECT_CHUNK if sp % SELECT_CHUNK == 0 else sp
            madd = _select_mask_prompt(iq.reshape(bp, sp, hw), iw.reshape(bp, sp, IDX_HEADS),
                                       ikbf.reshape(bp, sp, IDX_DIM), ksel, kc, SELECT_ROWS)
            b_out = _prompt_attn(q.reshape(bp, sp, hw), kbf.reshape(bp, sp, hw), vbf.reshape(bp, sp, -1), tbl, madd,
                                 None, None, nh=B_HEADS, nm=1, dqk=hw // B_HEADS, dv=hw // B_HEADS, tq=tq, lam_init=0.0)
            outs_p.setdefault('b_k', []).append(k32.reshape(bp, sp, B_HEADS, -1))
            outs_p.setdefault('b_v', []).append(v32.reshape(bp, sp, B_HEADS, -1))
            outs_p.setdefault('b_idx', []).append(ik32.reshape(bp, sp, IDX_DIM))
            mix_p = (aout, b_out.reshape(n_p, hw))

            cs = min(ts, A_CHUNK)
            ws_s = jnp.where(np.tril(np.ones((cs, cs), bool)), a_w_sp[li][:, :cs, :cs], 0.0)
            ws_s = jnp.einsum('ab,gts->gatbs', jnp.eye(n_s // cs, dtype=F32), ws_s).reshape(A_GROUPS, n_s, n_s).astype(BF16)
            bsp_s = jnp.tile(a_b_sp[li][:, :cs].T, (n_s // cs, 1))
            (aout, av, q, k32, kbf, v32, vbf, iq, ik32, ikbf, iw) = _even_in(
                xs, ms[0], ms[1], True, 1, n_s, even_w_in[li], a_ln_g[li], a_ln_b[li], ws_s, bsp_s,
                b_q_norm[li], b_k_norm[li], b_idx_norm[li])
            ltot = past_len + ts
            ksel = min(TOPK_MAX, ltot // 4)
            iq_rows = jnp.pad(iq.reshape(bs, ts, IDX_HEADS, IDX_DIM), ((0, 0), (0, tp - ts), (0, 0), (0, 0)))
            iq_rows = jnp.transpose(iq_rows, (0, 2, 1, 3)).reshape(bs, IDX_HEADS * tp, IDX_DIM)
            iw_rows = jnp.pad(iw.reshape(bs, ts, IDX_HEADS), ((0, 0), (0, tp - ts), (0, 0)))
            iw_rows = jnp.transpose(iw_rows, (0, 2, 1)).reshape(bs, IDX_HEADS * tp, 1)
            pad_new = lambda a: jnp.pad(a.reshape(bs, ts, -1), ((0, 0), (0, page - ts), (0, 0)))
            sc_s = _paged_scores(page_table, li, iq_rows, iw_rows, cache_b_idx, pad_new(ikbf), gp)
            width = sc_s.shape[2]
            kc_s = gp * page
            lim_s = jnp.where(np.arange(tp) < ts, past_len + np.arange(tp), -1).astype(I32)
            lim_s = jnp.tile(lim_s, bs).reshape(bs * tp, 1)
            nch_s = jnp.full((bs * tp // 128,), width // kc_s, I32)
            sel_s = _select_mask(sc_s.reshape(bs * tp, width), lim_s, nch_s, ksel, kc_s, F32).reshape(bs, tp, width)
            bias_s, _ = _rel_bias_sample(rel_bias, past_len, ts)
            bias_s = jnp.pad(bias_s, ((0, 0), (0, tp - ts), (0, width - ltot)))
            am = (sel_s[:, None] + bias_s[None]).reshape(bs, B_HEADS * tp, width)
            heads = lambda a: a.reshape(bs, ts, B_HEADS, -1)
            b_out = _paged_attn(page_table, li, _head_rows(heads(q), tp), am, cache_b_k, cache_b_v, heads(k32), heads(v32),
                                None, None, gp=gp, nh=B_HEADS, nm=1, tp=tp, lam_init=0.0)
            outs_s.setdefault('a_v', []).append(av.reshape(bs, ts, hw))
            outs_s.setdefault('b_k', []).append(k32.reshape(bs, ts, B_HEADS, -1))
            outs_s.setdefault('b_v', []).append(v32.reshape(bs, ts, B_HEADS, -1))
            outs_s.setdefault('b_idx', []).append(ik32.reshape(bs, ts, IDX_DIM))
            mix_s = (aout, b_out[:, :ts].reshape(n_s, hw).astype(BF16))
            w_out = even_w_out[li]
        else:
            lam_init = 0.8 - 0.6 * math.exp(-0.3 * layer)
            lp = d_lambda[li].astype(F32)
            lam = (jnp.exp(jnp.sum(lp[0] * lp[1])) - jnp.exp(jnp.sum(lp[2] * lp[3])) + lam_init).reshape(1)
            subg = d_subln[li].reshape(1, -1)
            dvh = hw // D_HEADS
            xc, cz, tail, q2, k2_32, k2bf, v2_32, v2bf = _odd_in(xp, mp[0], mp[1], False, tpb, tm_p, odd_w_in[li],
                                                                d_q_norm[li], d_k_norm[li])
            ch = xc.shape[1]
            xc3 = xc.reshape(bp, sp, ch)
            zbuf = jnp.zeros((bp, SUBLANES, ch), F32)
            s0 = jnp.zeros((bp, C_HEADS, C_DK, hw // C_HEADS), F32)
            chunk = C_CHUNK if sp % C_CHUNK == 0 else sp
            c_out, c_rec = _gdn(xc3, zbuf, cz.reshape(bp, sp, hw), tail.reshape(bp, sp, LANES), s0, c_conv_w[li],
                                c_a_log[li], c_dt_bias[li], c_norm_w[li], chunk, chunk)
            d_out = _prompt_attn(q2.reshape(bp, sp, hw), k2bf.reshape(bp, sp, hw), v2bf.reshape(bp, sp, -1), tbl, None,
                                 lam, subg, nh=D_HEADS, nm=2, dqk=D_QK_DIM, dv=dvh, tq=tq, lam_init=lam_init)
            outs_p.setdefault('c_rec', []).append(c_rec)
            outs_p.setdefault('c_conv', []).append(xc3[:, sp - (C_CONV - 1):])
            outs_p.setdefault('d_k', []).append(k2_32.reshape(bp, sp, D_HEADS, -1))
            outs_p.setdefault('d_v', []).append(v2_32.reshape(bp, sp, D_HEADS, -1))
            mix_p = (c_out.reshape(n_p, hw), d_out.reshape(n_p, hw))

            xc, cz, tail, q2, k2_32, k2bf, v2_32, v2bf = _odd_in(xs, ms[0], ms[1], True, 1, n_s, odd_w_in[li],
                                                                d_q_norm[li], d_k_norm[li])
            pad_t = lambda a: jnp.pad(a.reshape(bs, ts, -1), ((0, 0), (0, C_CHUNK - ts), (0, 0)))
            buf = state_c_conv[li]
            buf8 = jnp.pad(buf, ((0, 0), (SUBLANES - (C_CONV - 1), 0), (0, 0)))
            c_out, c_rec = _gdn(pad_t(xc), buf8, pad_t(cz), pad_t(tail), state_c_rec[li], c_conv_w[li], c_a_log[li],
                                c_dt_bias[li], c_norm_w[li], C_CHUNK, ts)
            ltot = past_len + ts
            bias_s, allowed = _rel_bias_sample(rel_bias, past_len, ts)
            width = (page_table.shape[1] // gp + 1) * gp * page
            am = jnp.where(jnp.asarray(allowed)[None], bias_s, NEG)
            am = jnp.pad(am, ((0, 0), (0, tp - ts), (0, 0)))
            am = jnp.pad(am, ((0, 0), (0, 0), (0, width - ltot)), constant_values=NEG)
            am = jnp.broadcast_to(am[None, :, None], (bs, D_HEADS, 2, tp, width)).reshape(bs, D_HEADS * 2 * tp, width)
            q_rows = _head_rows(q2.reshape(bs, ts, 2 * D_HEADS, D_QK_DIM), tp)
            heads = lambda a: a.reshape(bs, ts, D_HEADS, -1)
            d_out = _paged_attn(page_table, li, q_rows, am, cache_d_k, cache_d_v, heads(k2_32), heads(v2_32), lam, subg,
                                gp=gp, nh=D_HEADS, nm=2, tp=tp, lam_init=lam_init)
            outs_s.setdefault('c_rec', []).append(c_rec)
            outs_s.setdefault('c_conv', []).append(
                jnp.concatenate([buf, xc.reshape(bs, ts, ch)], axis=1)[:, ts:])
            outs_s.setdefault('d_k', []).append(k2_32.reshape(bs, ts, D_HEADS, -1))
            outs_s.setdefault('d_v', []).append(v2_32.reshape(bs, ts, D_HEADS, -1))
            mix_s = (c_out[:, :ts].reshape(n_s, hw), d_out[:, :ts].reshape(n_s, hw).astype(BF16))
            w_out = odd_w_out[li]

        x1p, h2p, lgp = _out_proj(mix_p[0], mix_p[1], xp, mp[2], mp[3], mp[4], False, tpb, tm_p, w_out,
                                  router_w[layer], router_b[layer])
        x1s, h2s, lgs = _out_proj(mix_s[0], mix_s[1], xs, ms[2], ms[3], ms[4], True, 1, n_s, w_out,
                                  router_w[layer], router_b[layer])
        moe_out = _moe(jnp.concatenate([h2p, h2s], axis=0), jnp.concatenate([lgp, lgs], axis=0), layer,
                       moe_w1, moe_b1, moe_w2, moe_b2, MOE_BLOCK_ROWS)
        xp = (x1p.reshape(bp, sp, d) + mp[5] * moe_out[:n_p].reshape(bp, sp, d)).reshape(n_p, d)
        xs = x1s + ms[5] * moe_out[n_p:]

    st = lambda name, src: jnp.stack(src[name])
    return (xp.reshape(bp, sp, d), xs.reshape(bs, ts, d),
            st('b_k', outs_p), st('b_v', outs_p), st('b_idx', outs_p), st('c_rec', outs_p), st('c_conv', outs_p),
            st('d_k', outs_p), st('d_v', outs_p),
            st('a_v', outs_s), st('b_k', outs_s), st('b_v', outs_s), st('b_idx', outs_s), st('c_rec', outs_s),
            st('c_conv', outs_s), st('d_k', outs_s), st('d_v', outs_s))
```

```python
import functools
import math

import numpy as np
import jax
import jax.numpy as jnp
from jax import lax
from jax.experimental import pallas as pl
from jax.experimental.pallas import tpu as pltpu

F32 = jnp.float32
BF16 = jnp.bfloat16
I32 = jnp.int32
HIGHEST = lax.Precision.HIGHEST

A_GROUPS = 4
A_CHUNK = 128
B_HEADS = 8
IDX_HEADS = 8
IDX_DIM = 64
TOPK_MAX = 256
C_HEADS = 4
C_DK = 128
C_CONV = 4
C_CHUNK = 64
D_HEADS = 8
D_QK_DIM = 32
N_BUCKETS = 32
MAX_DISTANCE = 128
TOP_K = 4
SWIGLU_LIMIT = 7.0
SWIGLU_ALPHA = 1.702
EPS = 1e-6

LANES = 128
SUBLANES = 8
VMEM_LIMIT_BYTES = 56 * 1024 * 1024

TOKEN_TILE = 256
ATTN_TILE = 256
SELECT_ROWS = 128
SELECT_CHUNK = 512
MOE_BLOCK_ROWS = 256
PAGES_PER_STEP = 16

NEG = -1e30
SHIFT_MARGIN = 16.0
INT_MIN = -2 ** 31


def _cparams(sem):
    return pltpu.CompilerParams(dimension_semantics=sem, vmem_limit_bytes=VMEM_LIMIT_BYTES)


def _dot(a, b):
    return jnp.dot(a.astype(BF16), b.astype(BF16), preferred_element_type=F32)


def _dot_nt(a, b):
    return lax.dot_general(a.astype(BF16), b.astype(BF16), (((1,), (1,)), ((), ())), preferred_element_type=F32)


def _dot_hi(a, b):
    return jnp.dot(a, b, preferred_element_type=F32, precision=HIGHEST)


def _dot_tn(a, b):
    return lax.dot_general(a.astype(BF16), b.astype(BF16), (((0,), (0,)), ((), ())), preferred_element_type=F32)


def _dot_x3(a, b):
    ah = a.astype(BF16)
    al = (a - ah.astype(F32)).astype(BF16)
    bh = b.astype(BF16)
    bl = (b - bh.astype(F32)).astype(BF16)
    dot = lambda x, y: jnp.dot(x, y, preferred_element_type=F32)
    return dot(ah, bh) + (dot(ah, bl) + dot(al, bh))


def _silu(x):
    return x * jax.nn.sigmoid(x)


def _gelu_tanh(x):
    return 0.5 * x * (1.0 + jnp.tanh(math.sqrt(2.0 / math.pi) * (x + 0.044715 * (x * x * x))))


def _group_rms(x, ones_ref, gsize):
    xx = x * x
    hi = xx.astype(BF16)
    lo = (xx - hi.astype(F32)).astype(BF16)
    e = ones_ref[...]
    ss = jnp.dot(hi, e, preferred_element_type=F32) + jnp.dot(lo, e, preferred_element_type=F32)
    return x * lax.rsqrt(ss * (1.0 / gsize) + EPS)


def _store_v_with_ones(vx_ref, v):
    tm, w = v.shape
    nh = vx_ref.shape[1] // LANES
    dv = w // nh
    tail = jnp.where(lax.broadcasted_iota(I32, (tm, LANES - dv), 1) == 0, 1.0, 0.0)
    for h in range(nh):
        vx_ref[:, h * LANES:(h + 1) * LANES] = jnp.concatenate([v[:, h * dv:(h + 1) * dv], tail], axis=1).astype(BF16)


def _block_ones(width, gsize):
    g = np.arange(width) // gsize
    return jnp.asarray(g[:, None] == g[None, :], BF16)


def _bucket_table():
    n = np.arange(MAX_DISTANCE)
    exact = N_BUCKETS // 2
    scaled = np.log(np.maximum(n, 1).astype(np.float32) / np.float32(exact)) / np.float32(math.log(MAX_DISTANCE / exact))
    large = np.minimum(exact + (scaled.astype(np.float32) * (N_BUCKETS - exact)).astype(np.int32), N_BUCKETS - 1)
    return np.where(n < exact, n, large).astype(np.int32)


def _bucket_of(dist):
    n = np.maximum(dist, 0)
    return np.where(n < MAX_DISTANCE, _bucket_table()[np.minimum(n, MAX_DISTANCE - 1)], N_BUCKETS - 1)


def _bias_by_bucket(rel_bias, bucket):
    out = jnp.zeros((rel_bias.shape[1],) + bucket.shape, F32)
    for b in np.unique(bucket):
        out = jnp.where(jnp.asarray(bucket == b)[None], rel_bias[b].astype(F32).reshape((-1,) + (1,) * bucket.ndim), out)
    return out


def _ada_kernel(c_ref, w_ref, b_ref, o_ref):
    o_ref[...] = _dot(_silu(c_ref[...]), w_ref[...]) + b_ref[...]


def _ada_mod(c_all, ada_w, ada_b):
    depth, d, n6 = ada_w.shape
    rows = c_all.shape[0]
    tn = 1536 if n6 % 1536 == 0 else n6
    return pl.pallas_call(
        _ada_kernel,
        grid=(depth, n6 // tn),
        in_specs=[pl.BlockSpec((rows, d), lambda l, j: (0, 0)),
                  pl.BlockSpec((None, d, tn), lambda l, j: (l, 0, j)),
                  pl.BlockSpec((None, 1, tn), lambda l, j: (l, 0, j))],
        out_specs=pl.BlockSpec((None, rows, tn), lambda l, j: (l, 0, j)),
        out_shape=jax.ShapeDtypeStruct((depth, rows, n6), F32),
        compiler_params=_cparams(("arbitrary", "arbitrary")),
        name="ada_mod",
    )(c_all, ada_w, ada_b.reshape(depth, 1, n6))


def _row_spec(tm, w):
    return pl.BlockSpec((tm, w), lambda i: (i, 0))


def _row_shape(n, w, dt):
    return jax.ShapeDtypeStruct((n, w), dt)


def _mod_specs(per_token, tm, d, tiles_per_batch):
    if per_token:
        return pl.BlockSpec((tm, d), lambda i: (i, 0))
    return pl.BlockSpec((None, 1, d), lambda i: (i // tiles_per_batch, 0, 0))


def _modulated_rms(x, sh, sc):
    xn = x * lax.rsqrt(jnp.mean(x * x, axis=-1, keepdims=True) + EPS)
    return xn * (1.0 + sc) + sh


def _even_in_kernel(x_ref, sh_ref, sc_ref, wm_ref, wt_ref, lng_ref, lnb_ref, msp_ref, bsp_ref,
                    qg_ref, kg_ref, ikg_ref, e_ref,
                    aout_ref, av_ref, q_ref, k32_ref, kbf_ref, v32_ref, vx_ref, iq_ref, ik32_ref, ikbf_ref, iw_ref):
    tm = x_ref.shape[0]
    aw = aout_ref.shape[1]
    gd = aw // A_GROUPS
    h = _modulated_rms(x_ref[...], sh_ref[...], sc_ref[...]).astype(BF16)

    def seg(i):
        return jnp.dot(h, wm_ref[:, i * aw:(i + 1) * aw], preferred_element_type=F32)

    au = _gelu_tanh(seg(0))
    av = _gelu_tanh(seg(1))
    avc = av - jnp.mean(av, axis=-1, keepdims=True)
    vn = avc * lax.rsqrt(jnp.mean(avc * avc, axis=-1, keepdims=True) + EPS) * lng_ref[...] + lnb_ref[...]
    av_ref[...] = vn
    vnb = vn.astype(BF16)
    cr = msp_ref.shape[1]
    for c in range(tm // cr):
        r0 = c * cr
        for g in range(A_GROUPS):
            mixed = jnp.dot(msp_ref[g], vnb[r0:r0 + cr, g * gd:(g + 1) * gd], preferred_element_type=F32)
            mixed = mixed + bsp_ref[:, g:g + 1]
            aout_ref[r0:r0 + cr, g * gd:(g + 1) * gd] = (au[r0:r0 + cr, g * gd:(g + 1) * gd] * mixed).astype(BF16)

    hd = aw // B_HEADS
    q = _group_rms(seg(2), e_ref, hd) * qg_ref[...]
    q_ref[...] = (q * (hd ** -0.5)).astype(BF16)
    k = _group_rms(seg(3), e_ref, hd) * kg_ref[...]
    k32_ref[...] = k
    kbf_ref[...] = k.astype(BF16)
    v = seg(4)
    v32_ref[...] = v
    _store_v_with_ones(vx_ref, v)
    iq_ref[...] = (seg(5) * (IDX_DIM ** -0.5)).astype(BF16)
    tail = jnp.dot(h, wt_ref[...], preferred_element_type=F32)
    ik = tail[:, :IDX_DIM]
    ik = ik * lax.rsqrt(jnp.mean(ik * ik, axis=-1, keepdims=True) + EPS) * ikg_ref[...]
    ik32_ref[...] = ik
    ikbf_ref[...] = ik.astype(BF16)
    iw_ref[...] = tail[:, IDX_DIM:IDX_DIM + IDX_HEADS] * (IDX_HEADS ** -0.5)


def _even_in(x, sh, sc, per_token, tiles_per_batch, tm, w_in, ln_g, ln_b, msp, bsp, q_g, k_g, ik_g):
    n, d = x.shape
    aw = d // 2
    wm = w_in[:, :6 * aw].astype(BF16)
    wt = jnp.pad(w_in[:, 6 * aw:], ((0, 0), (0, LANES - (IDX_DIM + IDX_HEADS)))).astype(BF16)
    full = lambda shape: pl.BlockSpec(shape, lambda i: (0,) * len(shape))
    row = lambda w: pl.BlockSpec((tm, w), lambda i: (i, 0))
    mod = _mod_specs(per_token, tm, d, tiles_per_batch)
    outs = [(aw, BF16), (aw, F32), (aw, BF16), (aw, F32), (aw, BF16), (aw, F32), (B_HEADS * LANES, BF16),
            (aw, BF16), (IDX_DIM, F32), (IDX_DIM, BF16), (IDX_HEADS, F32)]
    return pl.pallas_call(
        _even_in_kernel,
        grid=(n // tm,),
        in_specs=[row(d), mod, mod, full(wm.shape), full(wt.shape), full((1, aw)), full((1, aw)),
                  full(msp.shape), full(bsp.shape), full((1, aw)), full((1, aw)), full((1, IDX_DIM)), full((aw, aw))],
        out_specs=[_row_spec(tm, w) for w, _ in outs],
        out_shape=[_row_shape(n, w, dt) for w, dt in outs],
        compiler_params=_cparams(("arbitrary",)),
        name="even_in",
    )(x, sh, sc, wm, wt, ln_g.reshape(1, aw), ln_b.reshape(1, aw), msp, bsp,
      jnp.tile(q_g, B_HEADS).reshape(1, aw), jnp.tile(k_g, B_HEADS).reshape(1, aw), ik_g.reshape(1, IDX_DIM),
      _block_ones(aw, aw // B_HEADS))


def _odd_in_kernel(x_ref, sh_ref, sc_ref, wm_ref, wt_ref, qg_ref, kg_ref, e_ref,
                   xc_ref, cz_ref, tail_ref, q_ref, k32_ref, kbf_ref, v32_ref, vx_ref):
    hw = cz_ref.shape[1]
    h = _modulated_rms(x_ref[...], sh_ref[...], sc_ref[...]).astype(BF16)

    def seg(i):
        return jnp.dot(h, wm_ref[:, i * hw:(i + 1) * hw], preferred_element_type=F32)

    for i in range(3):
        xc_ref[:, i * hw:(i + 1) * hw] = seg(i)
    cz_ref[...] = seg(3)
    tail_ref[...] = jnp.dot(h, wt_ref[...], preferred_element_type=F32)
    q = _group_rms(seg(4), e_ref, D_QK_DIM) * qg_ref[...]
    q_ref[...] = (q * (D_QK_DIM ** -0.5)).astype(BF16)
    k = _group_rms(seg(5), e_ref, D_QK_DIM) * kg_ref[...]
    k32_ref[...] = k
    kbf_ref[...] = k.astype(BF16)
    v = seg(6)
    v32_ref[...] = v
    _store_v_with_ones(vx_ref, v)


def _odd_in(x, sh, sc, per_token, tiles_per_batch, tm, w_in, q_g, k_g):
    n, d = x.shape
    hw = d // 2
    ng = 2 * C_HEADS
    wm = jnp.concatenate([w_in[:, :4 * hw], w_in[:, 4 * hw + ng:]], axis=1).astype(BF16)
    wt = jnp.pad(w_in[:, 4 * hw:4 * hw + ng], ((0, 0), (0, LANES - ng))).astype(BF16)
    full = lambda shape: pl.BlockSpec(shape, lambda i: (0,) * len(shape))
    row = lambda w: pl.BlockSpec((tm, w), lambda i: (i, 0))
    mod = _mod_specs(per_token, tm, d, tiles_per_batch)
    outs = [(3 * hw, F32), (hw, F32), (LANES, F32), (hw, BF16), (hw, F32), (hw, BF16), (hw, F32),
            (D_HEADS * LANES, BF16)]
    reps = hw // D_QK_DIM
    return pl.pallas_call(
        _odd_in_kernel,
        grid=(n // tm,),
        in_specs=[row(d), mod, mod, full(wm.shape), full(wt.shape), full((1, hw)), full((1, hw)), full((hw, hw))],
        out_specs=[_row_spec(tm, w) for w, _ in outs],
        out_shape=[_row_shape(n, w, dt) for w, dt in outs],
        compiler_params=_cparams(("arbitrary",)),
        name="odd_in",
    )(x, sh, sc, wm, wt, jnp.tile(q_g, reps).reshape(1, hw), jnp.tile(k_g, reps).reshape(1, hw),
      _block_ones(hw, D_QK_DIM))


def _out_proj_kernel(a_ref, b_ref, x_ref, g1_ref, sh_ref, sc_ref, w_ref, rw_ref, rb_ref, x1_ref, h2_ref, lg_ref):
    hw = a_ref.shape[1]
    y = jnp.dot(a_ref[...], w_ref[:hw, :], preferred_element_type=F32)
    y = y + jnp.dot(b_ref[...], w_ref[hw:, :], preferred_element_type=F32)
    x1 = x_ref[...] + g1_ref[...] * y
    x1_ref[...] = x1
    h2 = _modulated_rms(x1, sh_ref[...], sc_ref[...])
    h2_ref[...] = h2
    lg_ref[...] = _dot(h2, rw_ref[...]) + rb_ref[...]


def _out_proj(a, b, x, g1, sh, sc, per_token, tiles_per_batch, tm, w_out, router_w, router_b):
    n, d = x.shape
    hw = a.shape[1]
    ne = router_w.shape[1]
    rw = jnp.pad(router_w, ((0, 0), (0, LANES - ne)))
    rb = jnp.pad(router_b, (0, LANES - ne)).reshape(1, LANES)
    full = lambda shape: pl.BlockSpec(shape, lambda i: (0,) * len(shape))
    row = lambda w: pl.BlockSpec((tm, w), lambda i: (i, 0))
    mod = _mod_specs(per_token, tm, d, tiles_per_batch)
    x1, h2, lg = pl.pallas_call(
        _out_proj_kernel,
        grid=(n // tm,),
        in_specs=[row(hw), row(hw), row(d), mod, mod, mod, full((d, d)), full((d, LANES)), full((1, LANES))],
        out_specs=[row(d), row(d), row(LANES)],
        out_shape=[jax.ShapeDtypeStruct((n, d), F32), jax.ShapeDtypeStruct((n, d), F32),
                   jax.ShapeDtypeStruct((n, LANES), F32)],
        compiler_params=_cparams(("arbitrary",)),
        name="out_proj",
    )(a, b, x, g1, sh, sc, w_out.astype(BF16), rw, rb)
    return x1, h2, lg


def _select_kernel(nch_ref, s_ref, lim_ref, m_ref, key_ref, *, ksel, kc):
    _select_rows(lambda off: s_ref[:, pl.ds(off, kc)], nch_ref[pl.program_id(0)], lim_ref[...], m_ref, key_ref, ksel, kc)


def _select_prompt_kernel(iq_ref, iw_ref, ik_ref, m_ref, key_ref, *, ksel, kc):
    rows = iq_ref.shape[0]
    first = pl.program_id(1) * rows

    iq_heads = jnp.concatenate([iq_ref[:, h * IDX_DIM:(h + 1) * IDX_DIM] for h in range(IDX_HEADS)], axis=0)

    def scores(off):
        d = _dot_nt(iq_heads, ik_ref[pl.ds(off, kc), :])
        acc = jnp.zeros((rows, kc), F32)
        for h in range(IDX_HEADS):
            acc = acc + iw_ref[:, h:h + 1] * jnp.maximum(d[h * rows:(h + 1) * rows], 0.0)
        return acc

    lim = first + lax.broadcasted_iota(I32, (rows, 1), 0)
    _select_rows(scores, (first + rows - 1) // kc + 1, lim, m_ref, key_ref, ksel, kc)


def _select_rows(score_chunk, nch, lim, m_ref, key_ref, ksel, kc):
    rows, width = m_ref.shape
    fold = kc // LANES
    col_bits = max(1, int(width - 1).bit_length())

    def cols(c):
        return c * kc + lax.broadcasted_iota(I32, (rows, kc), 1)

    def fill(c, carry):
        off = pl.multiple_of(c * kc, kc)
        bits = lax.bitcast_convert_type(score_chunk(off) + 0.0, I32)
        key = jnp.where(bits < 0, bits ^ jnp.int32(0x7FFFFFFF), bits)
        key_ref[:, pl.ds(off, kc)] = jnp.where(cols(c) <= lim, key, jnp.int32(INT_MIN))
        return carry

    lax.fori_loop(0, nch, fill, 0)

    def count(pred):
        def body(c, acc):
            off = pl.multiple_of(c * kc, kc)
            hit = jnp.where(pred(c, key_ref[:, pl.ds(off, kc)]), 1.0, 0.0)
            part = hit[:, :LANES]
            for f in range(1, fold):
                part = part + hit[:, f * LANES:(f + 1) * LANES]
            return acc + part
        acc = lax.fori_loop(0, nch, body, jnp.zeros((rows, LANES), F32))
        return jnp.sum(acc, axis=1, keepdims=True)

    def thr_bit(b, carry):
        prefix, n_ge = carry
        cand = prefix + (jnp.int32(1) << (31 - b))
        cnt = count(lambda c, key: key >= cand)
        take = cnt >= ksel
        return jnp.where(take, cand, prefix), jnp.where(take, cnt, n_ge)

    thr, n_ge = lax.fori_loop(0, 32, thr_bit, (jnp.full((rows, 1), INT_MIN, I32),
                                               jnp.full((rows, 1), 1.0, F32) * (nch * kc).astype(F32)))

    def tie_break():
        need = ksel - count(lambda c, key: key > thr)

        def tie_bit(b, pos):
            cand = pos + (jnp.int32(1) << (col_bits - 1 - b))
            cnt = count(lambda c, key: (key == thr) & (cols(c) < cand))
            return jnp.where(cnt < need, cand, pos)

        return lax.fori_loop(0, col_bits, tie_bit, jnp.zeros((rows, 1), I32))

    surplus = jnp.where((n_ge > ksel) & (thr > jnp.int32(INT_MIN)), 1.0, 0.0)
    last = lax.cond(jnp.max(surplus) > 0.0, tie_break, lambda: jnp.full((rows, 1), width, I32))

    def emit(c, carry):
        off = pl.multiple_of(c * kc, kc)
        key = key_ref[:, pl.ds(off, kc)]
        col = cols(c)
        sel = ((key > thr) | ((key == thr) & (col <= last))) & (col <= lim)
        m_ref[:, pl.ds(off, kc)] = jnp.where(sel, 0.0, NEG).astype(m_ref.dtype)
        return carry

    lax.fori_loop(0, nch, emit, 0)

    def blank(c, carry):
        off = pl.multiple_of(c * kc, kc)
        m_ref[:, pl.ds(off, kc)] = jnp.full((rows, kc), NEG, m_ref.dtype)
        return carry

    lax.fori_loop(nch, width // kc, blank, 0)


def _select_mask(scores, lim, nch, ksel, kc, out_dtype):
    r, width = scores.shape
    tr = SELECT_ROWS
    return pl.pallas_call(
        functools.partial(_select_kernel, ksel=ksel, kc=kc),
        grid_spec=pltpu.PrefetchScalarGridSpec(
            num_scalar_prefetch=1,
            grid=(r // tr,),
            in_specs=[pl.BlockSpec((tr, width), lambda i, n: (i, 0)),
                      pl.BlockSpec((tr, 1), lambda i, n: (i, 0))],
            out_specs=pl.BlockSpec((tr, width), lambda i, n: (i, 0)),
            scratch_shapes=[pltpu.VMEM((tr, width), I32)]),
        out_shape=jax.ShapeDtypeStruct((r, width), out_dtype),
        compiler_params=_cparams(("arbitrary",)),
        name="select_mask",
    )(nch, scores, lim)


def _select_mask_prompt(iq, iw, ik, ksel, kc, tr):
    bn, s, _ = iq.shape
    assert s % tr == 0 and s % kc == 0 and kc >= ksel
    return pl.pallas_call(
        functools.partial(_select_prompt_kernel, ksel=ksel, kc=kc),
        grid=(bn, s // tr),
        in_specs=[pl.BlockSpec((None, tr, IDX_HEADS * IDX_DIM), lambda b, i: (b, i, 0)),
                  pl.BlockSpec((None, tr, IDX_HEADS), lambda b, i: (b, i, 0)),
                  pl.BlockSpec((None, s, IDX_DIM), lambda b, i: (b, 0, 0))],
        out_specs=pl.BlockSpec((None, tr, s), lambda b, i: (b, i, 0)),
        out_shape=jax.ShapeDtypeStruct((bn, s, s), BF16),
        scratch_shapes=[pltpu.VMEM((tr, s), I32)],
        compiler_params=_cparams(("arbitrary", "arbitrary")),
        name="select_mask_prompt",
    )(iq, iw, ik)


def _prompt_attn_kernel(*refs, nh, nm, dqk, dv, tq, has_mask, lam_init):
    refs = list(refs)
    q_ref, k_ref, vx_ref, tbl_ref = refs[:4]
    pos = 4
    madd_ref = None
    if has_mask:
        madd_ref = refs[pos]
        pos += 1
    lam_ref = subg_ref = None
    if nm == 2:
        lam_ref, subg_ref = refs[pos], refs[pos + 1]
        pos += 2
    o_ref, mx_scr, acc_scr, qz_scr, s_scr, top_scr = refs[pos:pos + 6]
    ma_scr = refs[pos + 6] if has_mask else None
    nu = nh * nm
    nf = tq // LANES
    gu = LANES // dqk
    qi = pl.program_id(1)
    mx_scr[...] = jnp.full(mx_scr.shape, -jnp.inf, F32)
    acc_scr[...] = jnp.zeros(acc_scr.shape, F32)

    lane = lax.broadcasted_iota(I32, (tq, LANES), 1)
    for g in range(nu // gu):
        qt = q_ref[:, g * LANES:(g + 1) * LANES]
        for i in range(gu):
            keep = (lane >= i * dqk) & (lane < (i + 1) * dqk)
            qz_scr[g, i * tq:(i + 1) * tq, :] = jnp.where(keep, qt, jnp.zeros_like(qt))

    def chunk(j, back):
        koff = pl.multiple_of(j * tq, tq)
        if has_mask:
            ma_scr[...] = madd_ref[:, pl.ds(koff, tq)].astype(F32)
        excess = jnp.full((tq, LANES), -jnp.inf, F32)
        for g in range(nu // gu):
            stacked = _dot_nt(qz_scr[g], k_ref[pl.ds(koff, tq), g * LANES:(g + 1) * LANES])
            for i in range(gu):
                u = g * gu + i
                s = stacked[i * tq:(i + 1) * tq]
                if back is not None:
                    s = s + tbl_ref[u // nm, back]
                if has_mask:
                    s = s + ma_scr[...]
                s_scr[u] = s
                r = s[:, :LANES]
                for f in range(1, nf):
                    r = jnp.maximum(r, s[:, f * LANES:(f + 1) * LANES])
                top_scr[u] = r
                excess = jnp.maximum(excess, r - mx_scr[u])

        @pl.when(jnp.max(excess) > SHIFT_MARGIN)
        def _():
            for u in range(nu):
                old = mx_scr[u]
                new = jnp.maximum(old, jnp.broadcast_to(jnp.max(top_scr[u], axis=-1, keepdims=True), (tq, LANES)))
                acc_scr[u] = acc_scr[u] * jnp.exp(old - new)
                mx_scr[u] = new

        for g in range(nu // gu):
            ps = []
            for i in range(gu):
                u = g * gu + i
                mx = mx_scr[u]
                p = jnp.concatenate([jnp.exp(s_scr[u, :, f * LANES:(f + 1) * LANES] - mx) for f in range(nf)], axis=1)
                ps.append(p.astype(BF16))
            for i0 in range(0, gu, nm):
                u0 = g * gu + i0
                h = u0 // nm
                stack = ps[i0] if nm == 1 else jnp.concatenate(ps[i0:i0 + nm], axis=0)
                pv = jnp.dot(stack, vx_ref[pl.ds(koff, tq), h * LANES:(h + 1) * LANES], preferred_element_type=F32)
                for m in range(nm):
                    acc_scr[u0 + m] += pv[m * tq:(m + 1) * tq]

    def far(j, carry):
        chunk(j, None)
        return carry

    lax.fori_loop(0, jnp.maximum(qi - 1, 0), far, 0)

    @pl.when(qi >= 1)
    def _():
        chunk(qi - 1, 1)

    chunk(qi, 0)

    def normalised(u):
        a = acc_scr[u]
        return a[:, :dv] / a[:, dv:dv + 1]

    for h in range(nh):
        if nm == 1:
            o_ref[:, h * dv:(h + 1) * dv] = normalised(h).astype(o_ref.dtype)
        else:
            att = normalised(2 * h) - lam_ref[0] * normalised(2 * h + 1)
            att = att * lax.rsqrt(jnp.mean(att * att, axis=-1, keepdims=True) + EPS)
            o_ref[:, h * dv:(h + 1) * dv] = (att * subg_ref[...] * (1.0 - lam_init)).astype(o_ref.dtype)


def _bias_tables(rel_bias, tq):
    far = rel_bias[N_BUCKETS - 1].astype(F32)[:, None, None]
    r = np.arange(tq)[:, None]
    c = np.arange(tq)[None, :]
    tabs = []
    for d in range(2):
        dist = d * tq + r - c
        vals = _bias_by_bucket(rel_bias, _bucket_of(dist)) - far
        tabs.append(jnp.where(jnp.asarray(dist >= 0)[None], vals, NEG))
    return jnp.stack(tabs, axis=1)


def _prompt_attn(q, k, vx, tbl, madd, lam, subg, *, nh, nm, dqk, dv, tq, lam_init):
    bn, s, w = q.shape
    assert MAX_DISTANCE <= tq and s % tq == 0 and tq % LANES == 0 and dv < LANES
    has_mask = madd is not None
    once = lambda width: pl.BlockSpec((None, s, width), lambda b, i: (b, 0, 0), pipeline_mode=pl.Buffered(1))
    in_specs = [pl.BlockSpec((None, tq, w), lambda b, i: (b, i, 0)), once(w), once(nh * LANES),
                pl.BlockSpec(tbl.shape, lambda b, i: (0, 0, 0, 0), pipeline_mode=pl.Buffered(1))]
    args = [q, k, vx, tbl]
    if has_mask:
        in_specs.append(pl.BlockSpec((None, tq, s), lambda b, i: (b, i, 0)))
        args.append(madd)
    if nm == 2:
        in_specs += [pl.BlockSpec(memory_space=pltpu.SMEM), pl.BlockSpec((1, dv), lambda b, i: (0, 0))]
        args += [lam, subg]
    return pl.pallas_call(
        functools.partial(_prompt_attn_kernel, nh=nh, nm=nm, dqk=dqk, dv=dv, tq=tq, has_mask=has_mask, lam_init=lam_init),
        grid=(bn, s // tq),
        in_specs=in_specs,
        out_specs=pl.BlockSpec((None, tq, nh * dv), lambda b, i: (b, i, 0)),
        out_shape=jax.ShapeDtypeStruct((bn, s, nh * dv), BF16),
        scratch_shapes=[pltpu.VMEM((nh * nm, tq, LANES), F32), pltpu.VMEM((nh * nm, tq, LANES), F32),
                        pltpu.VMEM((nh * nm * dqk // LANES, LANES // dqk * tq, LANES), BF16),
                        pltpu.VMEM((nh * nm, tq, tq), F32), pltpu.VMEM((nh * nm, tq, LANES), F32)]
                       + ([pltpu.VMEM((tq, tq), F32)] if has_mask else []),
        compiler_params=_cparams(("arbitrary", "arbitrary")),
        name="prompt_attn_dsa" if nm == 1 else "prompt_attn_diff",
    )(*args)


def _paged_scores_kernel(pt_ref, iq_ref, iw_ref, *refs, gp, page):
    pages = refs[:gp]
    new_ref, o_ref = refs[gp], refs[gp + 1]
    s_id = pl.program_id(1)
    last = pl.num_programs(1) - 1
    rows = iq_ref.shape[0]
    tp = rows // IDX_HEADS

    def score(ik_t):
        d = jnp.maximum(_dot(iq_ref[...], ik_t), 0.0) * iw_ref[...]
        acc = d[:tp]
        for h in range(1, IDX_HEADS):
            acc = acc + d[h * tp:(h + 1) * tp]
        return acc

    @pl.when(s_id < last)
    def _():
        for g in range(gp):
            o_ref[:, g * page:(g + 1) * page] = score(pages[g][...])

    @pl.when(s_id == last)
    def _():
        o_ref[...] = jnp.zeros(o_ref.shape, F32)
        o_ref[:, :page] = score(new_ref[...])


def _paged_scores(page_table, li, iq_rows, iw_rows, cache_idx, ik_new, gp):
    bn, rows, _ = iq_rows.shape
    tp = rows // IDX_HEADS
    page = cache_idx.shape[2]
    npages = page_table.shape[1]
    nsteps = npages // gp + 1
    width = nsteps * gp * page

    def page_map(g):
        return lambda b, s, pt: (li, pt[b, jnp.minimum(s * gp + g, npages - 1)], 0, 0)

    return pl.pallas_call(
        functools.partial(_paged_scores_kernel, gp=gp, page=page),
        grid_spec=pltpu.PrefetchScalarGridSpec(
            num_scalar_prefetch=1,
            grid=(bn, nsteps),
            in_specs=[pl.BlockSpec((None, rows, IDX_DIM), lambda b, s, pt: (b, 0, 0)),
                      pl.BlockSpec((None, rows, 1), lambda b, s, pt: (b, 0, 0))]
                     + [pl.BlockSpec((None, None, IDX_DIM, page), page_map(g)) for g in range(gp)]
                     + [pl.BlockSpec((None, IDX_DIM, page), lambda b, s, pt: (b, 0, 0))],
            out_specs=pl.BlockSpec((None, tp, gp * page), lambda b, s, pt: (b, 0, s))),
        out_shape=jax.ShapeDtypeStruct((bn, tp, width), F32),
        compiler_params=_cparams(("arbitrary", "arbitrary")),
        name="paged_idx_scores",
    )(page_table, iq_rows, iw_rows, *([jnp.transpose(cache_idx, (0, 1, 3, 2))] * gp), jnp.transpose(ik_new, (0, 2, 1)))


def _paged_attn_kernel(pt_ref, q_ref, am_ref, *refs, gp, nh, nm, tp, lam_init):
    kpages = refs[:gp]
    vpages = refs[gp:2 * gp]
    knew_ref, vnew_ref = refs[2 * gp], refs[2 * gp + 1]
    pos = 2 * gp + 2
    lam_ref = subg_ref = None
    if nm == 2:
        lam_ref, subg_ref = refs[pos], refs[pos + 1]
        pos += 2
    o_ref, m_scr, l_scr, acc_scr = refs[pos:pos + 4]
    _, hd, page = knew_ref.shape
    dv = vnew_ref.shape[1]
    s_id = pl.program_id(1)
    last = pl.num_programs(1) - 1

    @pl.when(s_id == 0)
    def _():
        m_scr[...] = jnp.full(m_scr.shape, -jnp.inf, F32)
        l_scr[...] = jnp.zeros(l_scr.shape, F32)
        acc_scr[...] = jnp.zeros(acc_scr.shape, F32)

    def blocks(kvs):
        ss = [_dot(q_ref[...], kp[...].reshape(nh * hd, page)) + am_ref[:, g * page:(g + 1) * page]
              for g, (kp, _) in enumerate(kvs)]
        mx = ss[0]
        for s in ss[1:]:
            mx = jnp.maximum(mx, s)
        m_prev = m_scr[...]
        m_new = jnp.maximum(m_prev, jnp.max(mx, axis=-1, keepdims=True))
        alpha = jnp.exp(m_prev - m_new)
        ps = [jnp.exp(s - m_new) for s in ss]
        psum = ps[0]
        for p in ps[1:]:
            psum = psum + p
        pv = [_dot_nt(p, vp[...].reshape(nh * dv, page)) for p, (_, vp) in zip(ps, kvs)]
        tot = pv[0]
        for x in pv[1:]:
            tot = tot + x
        l_scr[...] = alpha * l_scr[...] + jnp.sum(psum, axis=-1, keepdims=True)
        acc_scr[...] = alpha * acc_scr[...] + tot
        m_scr[...] = m_new

    @pl.when(s_id < last)
    def _():
        blocks(list(zip(kpages, vpages)))

    @pl.when(s_id == last)
    def _():
        blocks([(knew_ref, vnew_ref)])
        accn = acc_scr[...] / l_scr[...]
        for h in range(nh):
            cols = slice(h * dv, (h + 1) * dv)
            if nm == 1:
                o_ref[:, cols] = accn[h * tp:(h + 1) * tp, cols]
            else:
                att = accn[2 * h * tp:(2 * h + 1) * tp, cols] - lam_ref[0] * accn[(2 * h + 1) * tp:(2 * h + 2) * tp, cols]
                att = att * lax.rsqrt(jnp.mean(att * att, axis=-1, keepdims=True) + EPS)
                o_ref[:, cols] = att * subg_ref[...] * (1.0 - lam_init)


def _pages_first(cache):
    return jnp.transpose(cache, (0, 1, 3, 4, 2))


def _head_rows(x, tp):
    bn, t, g, w = x.shape
    xp = jnp.pad(x, ((0, 0), (0, tp - t), (0, 0), (0, 0)))
    eye = jnp.eye(g, dtype=x.dtype)
    return jnp.einsum('btgw,gk->bgtkw', xp, eye).reshape(bn, g * tp, g * w)


def _paged_attn(page_table, li, q_rows, addmask, cache_k, cache_v, k_new, v_new, lam, subg, *, gp, nh, nm, tp, lam_init):
    bn, rows, w = q_rows.shape
    page, hd, dv = cache_k.shape[2], cache_k.shape[4], cache_v.shape[4]
    t = k_new.shape[1]
    npages = page_table.shape[1]
    nsteps = npages // gp + 1

    def page_map(g):
        return lambda b, s, pt: (li, pt[b, jnp.minimum(s * gp + g, npages - 1)], 0, 0, 0)

    new_t = lambda a: jnp.transpose(jnp.pad(a, ((0, 0), (0, page - t), (0, 0), (0, 0))), (0, 2, 3, 1))
    in_specs = ([pl.BlockSpec((None, rows, w), lambda b, s, pt: (b, 0, 0)),
                 pl.BlockSpec((None, rows, gp * page), lambda b, s, pt: (b, 0, s))]
                + [pl.BlockSpec((None, None, nh, hd, page), page_map(g)) for g in range(gp)]
                + [pl.BlockSpec((None, None, nh, dv, page), page_map(g)) for g in range(gp)]
                + [pl.BlockSpec((None, nh, hd, page), lambda b, s, pt: (b, 0, 0, 0)),
                   pl.BlockSpec((None, nh, dv, page), lambda b, s, pt: (b, 0, 0, 0))])
    args = [q_rows, addmask] + [_pages_first(cache_k)] * gp + [_pages_first(cache_v)] * gp + [new_t(k_new), new_t(v_new)]
    if nm == 2:
        in_specs += [pl.BlockSpec(memory_space=pltpu.SMEM), pl.BlockSpec((1, dv), lambda b, s, pt: (0, 0))]
        args += [lam, subg]
    return pl.pallas_call(
        functools.partial(_paged_attn_kernel, gp=gp, nh=nh, nm=nm, tp=tp, lam_init=lam_init),
        grid_spec=pltpu.PrefetchScalarGridSpec(
            num_scalar_prefetch=1,
            grid=(bn, nsteps),
            in_specs=in_specs,
            out_specs=pl.BlockSpec((None, tp, nh * dv), lambda b, s, pt: (b, 0, 0)),
            scratch_shapes=[pltpu.VMEM((rows, 1), F32), pltpu.VMEM((rows, 1), F32), pltpu.VMEM((rows, nh * dv), F32)]),
        out_shape=jax.ShapeDtypeStruct((bn, tp, nh * dv), F32),
        compiler_params=_cparams(("arbitrary", "arbitrary")),
        name="paged_attn_dsa" if nm == 1 else "paged_attn_diff",
    )(page_table, *args)


def _gdn_kernel(xc_ref, prev_ref, buf_ref, cz_ref, tail_ref, s0_ref, cw_ref, alog_ref, dtb_ref, nw_ref,
                o_ref, sfin_ref, s_scr, *, n_valid):
    ci = pl.program_id(1)
    bg, c, ch = xc_ref.shape
    dk = C_DK
    dv = (ch - 2 * C_HEADS * dk) // C_HEADS

    @pl.when(ci == 0)
    def _():
        s_scr[...] = s0_ref[...]

    row = lax.broadcasted_iota(I32, (c, c), 0)
    colm = lax.broadcasted_iota(I32, (c, c), 1)
    incl = row >= colm
    tril = incl.astype(F32)
    eye = (row == colm).astype(F32)

    units = []
    for b in range(bg):
        hist = jnp.where(ci == 0, buf_ref[b], prev_ref[b])
        nh_rows = hist.shape[0]
        xfull = jnp.concatenate([hist, xc_ref[b]], axis=0)
        y = cw_ref[C_CONV - 1:C_CONV, :] * xfull[nh_rows:, :]
        for j in range(1, C_CONV):
            y = y + cw_ref[C_CONV - 1 - j:C_CONV - j, :] * pltpu.roll(xfull, j, 0)[nh_rows:, :]
        y = _silu(y)

        tail = tail_ref[b]
        beta_all = jax.nn.sigmoid(tail)
        sp_in = tail + dtb_ref[...]
        g_all = -jnp.exp(alog_ref[...]) * (jnp.maximum(sp_in, 0.0) + jnp.log1p(jnp.exp(-jnp.abs(sp_in))))
        if n_valid < c:
            valid = lax.broadcasted_iota(I32, tail.shape, 0) < n_valid
            beta_all = jnp.where(valid, beta_all, 0.0)
            g_all = jnp.where(valid, g_all, 0.0)
        gc_all = _dot_hi(tril, g_all)
        gc_rows = jnp.transpose(gc_all)

        for h in range(C_HEADS):
            q = y[:, h * dk:(h + 1) * dk]
            k = y[:, (C_HEADS + h) * dk:(C_HEADS + h + 1) * dk]
            v = y[:, 2 * C_HEADS * dk + h * dv:2 * C_HEADS * dk + (h + 1) * dv]
            q = q * lax.rsqrt(jnp.sum(q * q, axis=-1, keepdims=True) + EPS) * (dk ** -0.5)
            k = k * lax.rsqrt(jnp.sum(k * k, axis=-1, keepdims=True) + EPS)
            beta = beta_all[:, h:h + 1]
            gc = gc_all[:, C_HEADS + h:C_HEADS + h + 1]
            decay = jnp.exp(jnp.where(incl, gc - gc_rows[C_HEADS + h:C_HEADS + h + 1, :], -jnp.inf))
            units.append(dict(b=b, h=h, q=q, k=k, kb=k * beta, vb=v * beta, gc=gc, decay=decay))

    def stage(fn):
        return [fn(un) for un in units]

    def put(name, vals):
        for un, val in zip(units, vals):
            un[name] = val

    put('a', stage(lambda un: _dot_nt(un['kb'], un['k']) * un['decay'] * (1.0 - eye)))
    put('inv', stage(lambda un: eye - un['a']))
    put('pw', stage(lambda un: un['a']))
    for _ in range(max(0, int(c - 1).bit_length() - 1)):
        put('pw', stage(lambda un: _dot_x3(un['pw'], un['pw'])))
        put('inv', stage(lambda un: un['inv'] + _dot_x3(un['inv'], un['pw'])))
    put('u', stage(lambda un: _dot_x3(un['inv'], un['vb'])))
    put('w', stage(lambda un: _dot_x3(un['inv'], un['kb'] * jnp.exp(un['gc']))))
    put('s', stage(lambda un: s_scr[un['b'], un['h']]))
    put('v_new', stage(lambda un: un['u'] - _dot(un['w'], un['s'])))
    put('intra', stage(lambda un: _dot_nt(un['q'], un['k']) * un['decay']))
    put('o', stage(lambda un: _dot(un['q'] * jnp.exp(un['gc']), un['s']) + _dot(un['intra'], un['v_new'])))
    for un in units:
        gl = un['gc'][c - 1:c, :]
        s_scr[un['b'], un['h']] = un['s'] * jnp.exp(gl) + _dot_tn(un['k'] * jnp.exp(gl - un['gc']), un['v_new'])
    for un in units:
        b, h, o = un['b'], un['h'], un['o']
        o = o * lax.rsqrt(jnp.mean(o * o, axis=-1, keepdims=True) + EPS) * nw_ref[...]
        o_ref[b, :, h * dv:(h + 1) * dv] = (o * _silu(cz_ref[b, :, h * dv:(h + 1) * dv])).astype(o_ref.dtype)

    @pl.when(ci == pl.num_programs(1) - 1)
    def _():
        sfin_ref[...] = s_scr[...]


def _gdn(xc, buf, cz, tail, s0, conv_w, a_log, dt_bias, norm_w, chunk, n_valid):
    bn, t, ch = xc.shape
    hw = cz.shape[2]
    nck = t // chunk
    dvh = hw // C_HEADS
    hist = SUBLANES
    per_chunk = chunk // hist
    bg = 2 if bn % 2 == 0 else 1
    gate_lanes = lambda p: jnp.pad(p.astype(F32), (C_HEADS, LANES - 2 * C_HEADS)).reshape(1, LANES)
    return pl.pallas_call(
        functools.partial(_gdn_kernel, n_valid=n_valid),
        grid=(bn // bg, nck),
        in_specs=[pl.BlockSpec((bg, chunk, ch), lambda b, i: (b, i, 0)),
                  pl.BlockSpec((bg, hist, ch), lambda b, i: (b, jnp.maximum(i * per_chunk - 1, 0), 0)),
                  pl.BlockSpec((bg, hist, ch), lambda b, i: (b, 0, 0)),
                  pl.BlockSpec((bg, chunk, hw), lambda b, i: (b, i, 0)),
                  pl.BlockSpec((bg, chunk, LANES), lambda b, i: (b, i, 0)),
                  pl.BlockSpec((bg, C_HEADS, C_DK, dvh), lambda b, i: (b, 0, 0, 0)),
                  pl.BlockSpec((C_CONV, ch), lambda b, i: (0, 0)),
                  pl.BlockSpec((1, LANES), lambda b, i: (0, 0)),
                  pl.BlockSpec((1, LANES), lambda b, i: (0, 0)),
                  pl.BlockSpec((1, dvh), lambda b, i: (0, 0))],
        out_specs=[pl.BlockSpec((bg, chunk, hw), lambda b, i: (b, i, 0)),
                   pl.BlockSpec((bg, C_HEADS, C_DK, dvh), lambda b, i: (b, 0, 0, 0))],
        out_shape=[jax.ShapeDtypeStruct((bn, t, hw), BF16), jax.ShapeDtypeStruct((bn, C_HEADS, C_DK, dvh), F32)],
        scratch_shapes=[pltpu.VMEM((bg, C_HEADS, C_DK, dvh), F32)],
        compiler_params=_cparams(("arbitrary", "arbitrary")),
        name="gated_delta",
    )(xc, xc, buf, cz, tail, s0, conv_w, gate_lanes(a_log), gate_lanes(dt_bias), norm_w.reshape(1, dvh))


def _moe_kernel(be_ref, x_ref, gate_ref, w1_ref, b1_ref, w2_ref, b2_ref, o_ref, w1_scr, w2_scr):
    i = pl.program_id(0)
    dff = w2_ref.shape[0]
    changed = jnp.logical_or(i == 0, be_ref[i] != be_ref[jnp.maximum(i - 1, 0)])

    @pl.when(changed)
    def _():
        w1_scr[...] = w1_ref[...].astype(BF16)
        w2_scr[...] = w2_ref[...].astype(BF16)

    hmid = jnp.dot(x_ref[...].astype(BF16), w1_scr[...], preferred_element_type=F32) + b1_ref[...]
    gate = jnp.minimum(hmid[:, :dff], SWIGLU_LIMIT)
    up = jnp.clip(hmid[:, dff:], -SWIGLU_LIMIT, SWIGLU_LIMIT)
    act = (up + 1.0) * gate * jax.nn.sigmoid(SWIGLU_ALPHA * gate)
    y = jnp.dot(act.astype(BF16), w2_scr[...], preferred_element_type=F32) + b2_ref[...]
    o_ref[...] = y * gate_ref[...]


def _moe_ffn(xs, slot_gate, block_e, layer, w1, b1, w2, b2, bm):
    n_slots, d = xs.shape
    _, ne, _, d2 = w1.shape
    dff = w2.shape[2]
    return pl.pallas_call(
        _moe_kernel,
        grid_spec=pltpu.PrefetchScalarGridSpec(
            num_scalar_prefetch=1,
            grid=(n_slots // bm,),
            in_specs=[pl.BlockSpec((bm, d), lambda i, be: (i, 0)),
                      pl.BlockSpec((bm, 1), lambda i, be: (i, 0)),
                      pl.BlockSpec((None, None, d, d2), lambda i, be: (layer, be[i], 0, 0)),
                      pl.BlockSpec((None, None, 1, d2), lambda i, be: (layer, be[i], 0, 0)),
                      pl.BlockSpec((None, None, dff, d), lambda i, be: (layer, be[i], 0, 0)),
                      pl.BlockSpec((None, None, 1, d), lambda i, be: (layer, be[i], 0, 0))],
            out_specs=pl.BlockSpec((bm, d), lambda i, be: (i, 0)),
            scratch_shapes=[pltpu.VMEM((d, d2), BF16), pltpu.VMEM((dff, d), BF16)]),
        out_shape=jax.ShapeDtypeStruct((n_slots, d), F32),
        compiler_params=_cparams(("arbitrary",)),
        name="moe_ffn",
    )(block_e, xs, slot_gate.reshape(n_slots, 1), w1, b1.reshape(b1.shape[0], ne, 1, d2), w2, b2.reshape(b2.shape[0], ne, 1, d))


def _route_kernel(lg_ref, e_ref, g_ref, r_ref, cnt_ref, run_scr, *, ne):
    i = pl.program_id(0)
    tm = lg_ref.shape[0]

    @pl.when(i == 0)
    def _():
        run_scr[...] = jnp.zeros(run_scr.shape, F32)

    lane = lax.broadcasted_iota(I32, (tm, LANES), 1)
    lane_f = lane.astype(F32)
    x = jnp.where(lane < ne, lg_ref[...], -jnp.inf)
    vals, hots = [], []
    e_out = jnp.zeros((tm, LANES), F32)
    for j in range(TOP_K):
        m = jnp.max(x, axis=-1, keepdims=True)
        idx = jnp.min(jnp.where(x == m, lane_f, float(LANES)), axis=-1, keepdims=True)
        hot = lane_f == idx
        vals.append(m)
        hots.append(hot)
        e_out = jnp.where(lane == j, idx, e_out)
        x = jnp.where(hot, -jnp.inf, x)
    ex = [jnp.exp(v - vals[0]) for v in vals]
    denom = ex[0]
    for j in range(1, TOP_K):
        denom = denom + ex[j]
    chosen = jnp.where(hots[0], 1.0, 0.0)
    for j in range(1, TOP_K):
        chosen = chosen + jnp.where(hots[j], 1.0, 0.0)
    before = (lax.broadcasted_iota(I32, (tm, tm), 0) > lax.broadcasted_iota(I32, (tm, tm), 1)).astype(BF16)
    base = run_scr[...] + jnp.dot(before, chosen.astype(BF16), preferred_element_type=F32)
    g_out = jnp.zeros((tm, LANES), F32)
    r_out = jnp.zeros((tm, LANES), F32)
    for j in range(TOP_K):
        g_out = jnp.where(lane == j, ex[j] / denom, g_out)
        r_out = jnp.where(lane == j, jnp.sum(jnp.where(hots[j], base, 0.0), axis=-1, keepdims=True), r_out)
    e_ref[...] = e_out[:, :TOP_K].astype(I32)
    g_ref[...] = g_out[:, :TOP_K]
    r_ref[...] = r_out[:, :TOP_K].astype(I32)
    run_scr[...] = run_scr[...] + jnp.sum(chosen, axis=0, keepdims=True)

    @pl.when(i == pl.num_programs(0) - 1)
    def _():
        cnt_ref[...] = run_scr[...]


def _route(logits, ne):
    n = logits.shape[0]
    tm = next(t for t in (512, 384, 256, 128, 64, 32, 16, 8) if n % t == 0)
    small = lambda dt: jax.ShapeDtypeStruct((n, TOP_K), dt)
    return pl.pallas_call(
        functools.partial(_route_kernel, ne=ne),
        grid=(n // tm,),
        in_specs=[pl.BlockSpec((tm, LANES), lambda i: (i, 0))],
        out_specs=[pl.BlockSpec((tm, TOP_K), lambda i: (i, 0))] * 3 + [pl.BlockSpec((1, LANES), lambda i: (0, 0))],
        out_shape=[small(I32), small(F32), small(I32), jax.ShapeDtypeStruct((1, LANES), F32)],
        scratch_shapes=[pltpu.VMEM((1, LANES), F32)],
        compiler_params=_cparams(("arbitrary",)),
        name="moe_route",
    )(logits)


def _moe(h2, logits, layer, w1, b1, w2, b2, bm):
    n_tok = h2.shape[0]
    ne = w1.shape[1]
    top_idx, gates, rank, counts = _route(logits, ne)
    counts = counts[0, :ne].astype(I32)
    padded = (counts + bm - 1) // bm * bm
    pad_end = jnp.cumsum(padded)
    pad_start = pad_end - padded
    n_assign = n_tok * TOP_K
    n_blocks = -(-(n_assign + ne * (bm - 1)) // bm)
    n_slots = n_blocks * bm
    hot = top_idx[..., None] == jnp.arange(ne, dtype=I32)
    dest = (jnp.sum(jnp.where(hot, pad_start, 0), axis=-1) + rank).reshape(-1)
    block_e = jnp.minimum(jnp.sum(pad_end[None, :] <= (jnp.arange(n_blocks, dtype=I32) * bm)[:, None], axis=1),
                          ne - 1).astype(I32)
    payload = jnp.stack([jnp.repeat(jnp.arange(n_tok, dtype=I32), TOP_K),
                         lax.bitcast_convert_type(gates.reshape(-1), I32)], axis=-1)
    slots = jnp.zeros((n_slots, 2), I32).at[dest].set(payload)
    slot_gate = lax.bitcast_convert_type(slots[:, 1], F32)
    ys = _moe_ffn(h2[slots[:, 0]], slot_gate, block_e, layer, w1, b1, w2, b2, bm)
    return jnp.sum(ys[dest.reshape(n_tok, TOP_K).T], axis=0)


def _rel_bias_sample(rel_bias, past_len, t):
    dist = (past_len + np.arange(t))[:, None] - np.arange(past_len + t)[None, :]
    return _bias_by_bucket(rel_bias, _bucket_of(dist)), dist >= 0


def kernel(x_prompt, x_sample, cache_b_k, cache_b_v, cache_b_idx, state_c_rec, state_c_conv, cache_d_k, cache_d_v,
           page_table, c_prompt, c_sample, rel_bias, ada_w, ada_b, even_w_in, even_w_out, a_ln_g, a_ln_b, a_w_sp,
           a_b_sp, b_q_norm, b_k_norm, b_idx_norm, odd_w_in, odd_w_out, c_conv_w, c_a_log, c_dt_bias, c_norm_w,
           d_q_norm, d_k_norm, d_lambda, d_subln, router_w, router_b, moe_w1, moe_b1, moe_w2, moe_b2):
    bp, sp, d = x_prompt.shape
    bs, ts, _ = x_sample.shape
    depth = ada_w.shape[0]
    hw = d // 2
    page = cache_b_k.shape[2]
    past_len = page_table.shape[1] * page
    n_p, n_s = bp * sp, bs * ts
    tm_p = TOKEN_TILE
    tq = ATTN_TILE
    tp = SUBLANES
    gp = PAGES_PER_STEP if page_table.shape[1] % PAGES_PER_STEP == 0 else 1
    assert sp % tm_p == 0 and n_s % SUBLANES == 0 and ts <= tp and ts >= C_CONV - 1

    n_c = bp + bs
    c_all = jnp.pad(jnp.concatenate([c_prompt, c_sample], axis=0), ((0, -n_c % SUBLANES), (0, 0)))
    mod_all = _ada_mod(c_all, ada_w, ada_b)

    def mods(layer):
        m = mod_all[layer].reshape(-1, 6, d)
        mp = [m[:bp, i][:, None, :] for i in range(6)]
        ms = [jnp.repeat(m[bp:n_c, i], ts, axis=0) for i in range(6)]
        return mp, ms

    xp = x_prompt.reshape(n_p, d)
    xs = x_sample.reshape(n_s, d)
    tpb = sp // tm_p
    tbl = _bias_tables(rel_bias, tq)
    outs_p, outs_s = {}, {}

    for layer in range(depth):
        li = layer // 2
        mp, ms = mods(layer)
        if layer % 2 == 0:
            ws_p = jnp.where(np.tril(np.ones((A_CHUNK, A_CHUNK), bool)), a_w_sp[li], 0.0).astype(BF16)
            bsp_p = a_b_sp[li].T
            (aout, _, q, k32, kbf, v32, vbf, iq, ik32, ikbf, iw) = _even_in(
                xp, mp[0], mp[1], False, tpb, tm_p, even_w_in[li], a_ln_g[li], a_ln_b[li], ws_p, bsp_p,
                b_q_norm[li], b_k_norm[li], b_idx_norm[li])
            ksel = min(TOPK_MAX, sp // 4)
            kc = SELECT_CHUNK if sp % SELECT_CHUNK == 0 else sp
            madd = _select_mask_prompt(iq.reshape(bp, sp, hw), iw.reshape(bp, sp, IDX_HEADS),
                                       ikbf.reshape(bp, sp, IDX_DIM), ksel, kc, SELECT_ROWS)
            b_out = _prompt_attn(q.reshape(bp, sp, hw), kbf.reshape(bp, sp, hw), vbf.reshape(bp, sp, -1), tbl, madd,
                                 None, None, nh=B_HEADS, nm=1, dqk=hw // B_HEADS, dv=hw // B_HEADS, tq=tq, lam_init=0.0)
            outs_p.setdefault('b_k', []).append(k32.reshape(bp, sp, B_HEADS, -1))
            outs_p.setdefault('b_v', []).append(v32.reshape(bp, sp, B_HEADS, -1))
            outs_p.setdefault('b_idx', []).append(ik32.reshape(bp, sp, IDX_DIM))
            mix_p = (aout, b_out.reshape(n_p, hw))

            cs = min(ts, A_CHUNK)
            ws_s = jnp.where(np.tril(np.ones((cs, cs), bool)), a_w_sp[li][:, :cs, :cs], 0.0)
            ws_s = jnp.einsum('ab,gts->gatbs', jnp.eye(n_s // cs, dtype=F32), ws_s).reshape(A_GROUPS, n_s, n_s).astype(BF16)
            bsp_s = jnp.tile(a_b_sp[li][:, :cs].T, (n_s // cs, 1))
            (aout, av, q, k32, kbf, v32, vbf, iq, ik32, ikbf, iw) = _even_in(
                xs, ms[0], ms[1], True, 1, n_s, even_w_in[li], a_ln_g[li], a_ln_b[li], ws_s, bsp_s,
                b_q_norm[li], b_k_norm[li], b_idx_norm[li])
            ltot = past_len + ts
            ksel = min(TOPK_MAX, ltot // 4)
            iq_rows = jnp.pad(iq.reshape(bs, ts, IDX_HEADS, IDX_DIM), ((0, 0), (0, tp - ts), (0, 0), (0, 0)))
            iq_rows = jnp.transpose(iq_rows, (0, 2, 1, 3)).reshape(bs, IDX_HEADS * tp, IDX_DIM)
            iw_rows = jnp.pad(iw.reshape(bs, ts, IDX_HEADS), ((0, 0), (0, tp - ts), (0, 0)))
            iw_rows = jnp.transpose(iw_rows, (0, 2, 1)).reshape(bs, IDX_HEADS * tp, 1)
            pad_new = lambda a: jnp.pad(a.reshape(bs, ts, -1), ((0, 0), (0, page - ts), (0, 0)))
            sc_s = _paged_scores(page_table, li, iq_rows, iw_rows, cache_b_idx, pad_new(ikbf), gp)
            width = sc_s.shape[2]
            kc_s = gp * page
            lim_s = jnp.where(np.arange(tp) < ts, past_len + np.arange(tp), -1).astype(I32)
            lim_s = jnp.tile(lim_s, bs).reshape(bs * tp, 1)
            nch_s = jnp.full((bs * tp // 128,), width // kc_s, I32)
            sel_s = _select_mask(sc_s.reshape(bs * tp, width), lim_s, nch_s, ksel, kc_s, F32).reshape(bs, tp, width)
            bias_s, _ = _rel_bias_sample(rel_bias, past_len, ts)
            bias_s = jnp.pad(bias_s, ((0, 0), (0, tp - ts), (0, width - ltot)))
            am = (sel_s[:, None] + bias_s[None]).reshape(bs, B_HEADS * tp, width)
            heads = lambda a: a.reshape(bs, ts, B_HEADS, -1)
            b_out = _paged_attn(page_table, li, _head_rows(heads(q), tp), am, cache_b_k, cache_b_v, heads(k32), heads(v32),
                                None, None, gp=gp, nh=B_HEADS, nm=1, tp=tp, lam_init=0.0)
            outs_s.setdefault('a_v', []).append(av.reshape(bs, ts, hw))
            outs_s.setdefault('b_k', []).append(k32.reshape(bs, ts, B_HEADS, -1))
            outs_s.setdefault('b_v', []).append(v32.reshape(bs, ts, B_HEADS, -1))
            outs_s.setdefault('b_idx', []).append(ik32.reshape(bs, ts, IDX_DIM))
            mix_s = (aout, b_out[:, :ts].reshape(n_s, hw).astype(BF16))
            w_out = even_w_out[li]
        else:
            lam_init = 0.8 - 0.6 * math.exp(-0.3 * layer)
            lp = d_lambda[li].astype(F32)
            lam = (jnp.exp(jnp.sum(lp[0] * lp[1])) - jnp.exp(jnp.sum(lp[2] * lp[3])) + lam_init).reshape(1)
            subg = d_subln[li].reshape(1, -1)
            dvh = hw // D_HEADS
            xc, cz, tail, q2, k2_32, k2bf, v2_32, v2bf = _odd_in(xp, mp[0], mp[1], False, tpb, tm_p, odd_w_in[li],
                                                                d_q_norm[li], d_k_norm[li])
            ch = xc.shape[1]
            xc3 = xc.reshape(bp, sp, ch)
            zbuf = jnp.zeros((bp, SUBLANES, ch), F32)
            s0 = jnp.zeros((bp, C_HEADS, C_DK, hw // C_HEADS), F32)
            chunk = C_CHUNK if sp % C_CHUNK == 0 else sp
            c_out, c_rec = _gdn(xc3, zbuf, cz.reshape(bp, sp, hw), tail.reshape(bp, sp, LANES), s0, c_conv_w[li],
                                c_a_log[li], c_dt_bias[li], c_norm_w[li], chunk, chunk)
            d_out = _prompt_attn(q2.reshape(bp, sp, hw), k2bf.reshape(bp, sp, hw), v2bf.reshape(bp, sp, -1), tbl, None,
                                 lam, subg, nh=D_HEADS, nm=2, dqk=D_QK_DIM, dv=dvh, tq=tq, lam_init=lam_init)
            outs_p.setdefault('c_rec', []).append(c_rec)
            outs_p.setdefault('c_conv', []).append(xc3[:, sp - (C_CONV - 1):])
            outs_p.setdefault('d_k', []).append(k2_32.reshape(bp, sp, D_HEADS, -1))
            outs_p.setdefault('d_v', []).append(v2_32.reshape(bp, sp, D_HEADS, -1))
            mix_p = (c_out.reshape(n_p, hw), d_out.reshape(n_p, hw))

            xc, cz, tail, q2, k2_32, k2bf, v2_32, v2bf = _odd_in(xs, ms[0], ms[1], True, 1, n_s, odd_w_in[li],
                                                                d_q_norm[li], d_k_norm[li])
            pad_t = lambda a: jnp.pad(a.reshape(bs, ts, -1), ((0, 0), (0, C_CHUNK - ts), (0, 0)))
            buf = state_c_conv[li]
            buf8 = jnp.pad(buf, ((0, 0), (SUBLANES - (C_CONV - 1), 0), (0, 0)))
            c_out, c_rec = _gdn(pad_t(xc), buf8, pad_t(cz), pad_t(tail), state_c_rec[li], c_conv_w[li], c_a_log[li],
                                c_dt_bias[li], c_norm_w[li], C_CHUNK, ts)
            ltot = past_len + ts
            bias_s, allowed = _rel_bias_sample(rel_bias, past_len, ts)
            width = (page_table.shape[1] // gp + 1) * gp * page
            am = jnp.where(jnp.asarray(allowed)[None], bias_s, NEG)
            am = jnp.pad(am, ((0, 0), (0, tp - ts), (0, 0)))
            am = jnp.pad(am, ((0, 0), (0, 0), (0, width - ltot)), constant_values=NEG)
            am = jnp.broadcast_to(am[None, :, None], (bs, D_HEADS, 2, tp, width)).reshape(bs, D_HEADS * 2 * tp, width)
            q_rows = _head_rows(q2.reshape(bs, ts, 2 * D_HEADS, D_QK_DIM), tp)
            heads = lambda a: a.reshape(bs, ts, D_HEADS, -1)
            d_out = _paged_attn(page_table, li, q_rows, am, cache_d_k, cache_d_v, heads(k2_32), heads(v2_32), lam, subg,
                                gp=gp, nh=D_HEADS, nm=2, tp=tp, lam_init=lam_init)
            outs_s.setdefault('c_rec', []).append(c_rec)
            outs_s.setdefault('c_conv', []).append(
                jnp.concatenate([buf, xc.reshape(bs, ts, ch)], axis=1)[:, ts:])
            outs_s.setdefault('d_k', []).append(k2_32.reshape(bs, ts, D_HEADS, -1))
            outs_s.setdefault('d_v', []).append(v2_32.reshape(bs, ts, D_HEADS, -1))
            mix_s = (c_out[:, :ts].reshape(n_s, hw), d_out[:, :ts].reshape(n_s, hw).astype(BF16))
            w_out = odd_w_out[li]

        x1p, h2p, lgp = _out_proj(mix_p[0], mix_p[1], xp, mp[2], mp[3], mp[4], False, tpb, tm_p, w_out,
                                  router_w[layer], router_b[layer])
        x1s, h2s, lgs = _out_proj(mix_s[0], mix_s[1], xs, ms[2], ms[3], ms[4], True, 1, n_s, w_out,
                                  router_w[layer], router_b[layer])
        moe_out = _moe(jnp.concatenate([h2p, h2s], axis=0), jnp.concatenate([lgp, lgs], axis=0), layer,
                       moe_w1, moe_b1, moe_w2, moe_b2, MOE_BLOCK_ROWS)
        xp = (x1p.reshape(bp, sp, d) + mp[5] * moe_out[:n_p].reshape(bp, sp, d)).reshape(n_p, d)
        xs = x1s + ms[5] * moe_out[n_p:]

    st = lambda name, src: jnp.stack(src[name])
    return (xp.reshape(bp, sp, d), xs.reshape(bs, ts, d),
            st('b_k', outs_p), st('b_v', outs_p), st('b_idx', outs_p), st('c_rec', outs_p), st('c_conv', outs_p),
            st('d_k', outs_p), st('d_v', outs_p),
            st('a_v', outs_s), st('b_k', outs_s), st('b_v', outs_s), st('b_idx', outs_s), st('c_rec', outs_s),
            st('c_conv', outs_s), st('d_k', outs_s), st('d_v', outs_s))
```

```python
import functools
import math

import numpy as np
import jax
import jax.numpy as jnp
from jax import lax
from jax.experimental import pallas as pl
from jax.experimental.pallas import tpu as pltpu

F32 = jnp.float32
BF16 = jnp.bfloat16
I32 = jnp.int32
HIGHEST = lax.Precision.HIGHEST

A_GROUPS = 4
A_CHUNK = 128
B_HEADS = 8
IDX_HEADS = 8
IDX_DIM = 64
TOPK_MAX = 256
C_HEADS = 4
C_DK = 128
C_CONV = 4
C_CHUNK = 64
D_HEADS = 8
D_QK_DIM = 32
N_BUCKETS = 32
MAX_DISTANCE = 128
TOP_K = 4
SWIGLU_LIMIT = 7.0
SWIGLU_ALPHA = 1.702
EPS = 1e-6

LANES = 128
SUBLANES = 8
VMEM_LIMIT_BYTES = 56 * 1024 * 1024

TOKEN_TILE = 256
ATTN_TILE = 256
SELECT_ROWS = 128
SELECT_CHUNK = 512
MOE_BLOCK_ROWS = 256
PAGES_PER_STEP = 16

NEG = -1e30
SHIFT_MARGIN = 16.0
INT_MIN = -2 ** 31


def _cparams(sem):
    return pltpu.CompilerParams(dimension_semantics=sem, vmem_limit_bytes=VMEM_LIMIT_BYTES)


def _dot(a, b):
    return jnp.dot(a.astype(BF16), b.astype(BF16), preferred_element_type=F32)


def _dot_nt(a, b):
    return lax.dot_general(a.astype(BF16), b.astype(BF16), (((1,), (1,)), ((), ())), preferred_element_type=F32)


def _dot_hi(a, b):
    return jnp.dot(a, b, preferred_element_type=F32, precision=HIGHEST)


def _dot_tn(a, b):
    return lax.dot_general(a.astype(BF16), b.astype(BF16), (((0,), (0,)), ((), ())), preferred_element_type=F32)


def _dot_x3(a, b):
    ah = a.astype(BF16)
    al = (a - ah.astype(F32)).astype(BF16)
    bh = b.astype(BF16)
    bl = (b - bh.astype(F32)).astype(BF16)
    dot = lambda x, y: jnp.dot(x, y, preferred_element_type=F32)
    return dot(ah, bh) + (dot(ah, bl) + dot(al, bh))


def _silu(x):
    return x * jax.nn.sigmoid(x)


def _gelu_tanh(x):
    return 0.5 * x * (1.0 + jnp.tanh(math.sqrt(2.0 / math.pi) * (x + 0.044715 * (x * x * x))))


def _group_rms(x, ones_ref, gsize):
    xx = x * x
    hi = xx.astype(BF16)
    lo = (xx - hi.astype(F32)).astype(BF16)
    e = ones_ref[...]
    ss = jnp.dot(hi, e, preferred_element_type=F32) + jnp.dot(lo, e, preferred_element_type=F32)
    return x * lax.rsqrt(ss * (1.0 / gsize) + EPS)


def _store_v_with_ones(vx_ref, v):
    tm, w = v.shape
    nh = vx_ref.shape[1] // LANES
    dv = w // nh
    tail = jnp.where(lax.broadcasted_iota(I32, (tm, LANES - dv), 1) == 0, 1.0, 0.0)
    for h in range(nh):
        vx_ref[:, h * LANES:(h + 1) * LANES] = jnp.concatenate([v[:, h * dv:(h + 1) * dv], tail], axis=1).astype(BF16)


def _block_ones(width, gsize):
    g = np.arange(width) // gsize
    return jnp.asarray(g[:, None] == g[None, :], BF16)


def _bucket_table():
    n = np.arange(MAX_DISTANCE)
    exact = N_BUCKETS // 2
    scaled = np.log(np.maximum(n, 1).astype(np.float32) / np.float32(exact)) / np.float32(math.log(MAX_DISTANCE / exact))
    large = np.minimum(exact + (scaled.astype(np.float32) * (N_BUCKETS - exact)).astype(np.int32), N_BUCKETS - 1)
    return np.where(n < exact, n, large).astype(np.int32)


def _bucket_of(dist):
    n = np.maximum(dist, 0)
    return np.where(n < MAX_DISTANCE, _bucket_table()[np.minimum(n, MAX_DISTANCE - 1)], N_BUCKETS - 1)


def _bias_by_bucket(rel_bias, bucket):
    out = jnp.zeros((rel_bias.shape[1],) + bucket.shape, F32)
    for b in np.unique(bucket):
        out = jnp.where(jnp.asarray(bucket == b)[None], rel_bias[b].astype(F32).reshape((-1,) + (1,) * bucket.ndim), out)
    return out


def _ada_kernel(c_ref, w_ref, b_ref, o_ref):
    o_ref[...] = _dot(_silu(c_ref[...]), w_ref[...]) + b_ref[...]


def _ada_mod(c_all, ada_w, ada_b):
    depth, d, n6 = ada_w.shape
    rows = c_all.shape[0]
    tn = 1536 if n6 % 1536 == 0 else n6
    return pl.pallas_call(
        _ada_kernel,
        grid=(depth, n6 // tn),
        in_specs=[pl.BlockSpec((rows, d), lambda l, j: (0, 0)),
                  pl.BlockSpec((None, d, tn), lambda l, j: (l, 0, j)),
                  pl.BlockSpec((None, 1, tn), lambda l, j: (l, 0, j))],
        out_specs=pl.BlockSpec((None, rows, tn), lambda l, j: (l, 0, j)),
        out_shape=jax.ShapeDtypeStruct((depth, rows, n6), F32),
        compiler_params=_cparams(("arbitrary", "arbitrary")),
        name="ada_mod",
    )(c_all, ada_w, ada_b.reshape(depth, 1, n6))


def _row_spec(tm, w):
    return pl.BlockSpec((tm, w), lambda i: (i, 0))


def _row_shape(n, w, dt):
    return jax.ShapeDtypeStruct((n, w), dt)


def _mod_specs(per_token, tm, d, tiles_per_batch):
    if per_token:
        return pl.BlockSpec((tm, d), lambda i: (i, 0))
    return pl.BlockSpec((None, 1, d), lambda i: (i // tiles_per_batch, 0, 0))


def _modulated_rms(x, sh, sc):
    xn = x * lax.rsqrt(jnp.mean(x * x, axis=-1, keepdims=True) + EPS)
    return xn * (1.0 + sc) + sh


def _even_in_kernel(x_ref, sh_ref, sc_ref, wm_ref, wt_ref, lng_ref, lnb_ref, msp_ref, bsp_ref,
                    qg_ref, kg_ref, ikg_ref, e_ref,
                    aout_ref, av_ref, q_ref, k32_ref, kbf_ref, v32_ref, vx_ref, iq_ref, ik32_ref, ikbf_ref, iw_ref):
    tm = x_ref.shape[0]
    aw = aout_ref.shape[1]
    gd = aw // A_GROUPS
    h = _modulated_rms(x_ref[...], sh_ref[...], sc_ref[...]).astype(BF16)

    def seg(i):
        return jnp.dot(h, wm_ref[:, i * aw:(i + 1) * aw], preferred_element_type=F32)

    au = _gelu_tanh(seg(0))
    av = _gelu_tanh(seg(1))
    avc = av - jnp.mean(av, axis=-1, keepdims=True)
    vn = avc * lax.rsqrt(jnp.mean(avc * avc, axis=-1, keepdims=True) + EPS) * lng_ref[...] + lnb_ref[...]
    av_ref[...] = vn
    vnb = vn.astype(BF16)
    cr = msp_ref.shape[1]
    for c in range(tm // cr):
        r0 = c * cr
        for g in range(A_GROUPS):
            mixed = jnp.dot(msp_ref[g], vnb[r0:r0 + cr, g * gd:(g + 1) * gd], preferred_element_type=F32)
            mixed = mixed + bsp_ref[:, g:g + 1]
            aout_ref[r0:r0 + cr, g * gd:(g + 1) * gd] = (au[r0:r0 + cr, g * gd:(g + 1) * gd] * mixed).astype(BF16)

    hd = aw // B_HEADS
    q = _group_rms(seg(2), e_ref, hd) * qg_ref[...]
    q_ref[...] = (q * (hd ** -0.5)).astype(BF16)
    k = _group_rms(seg(3), e_ref, hd) * kg_ref[...]
    k32_ref[...] = k
    kbf_ref[...] = k.astype(BF16)
    v = seg(4)
    v32_ref[...] = v
    _store_v_with_ones(vx_ref, v)
    iq_ref[...] = (seg(5) * (IDX_DIM ** -0.5)).astype(BF16)
    tail = jnp.dot(h, wt_ref[...], preferred_element_type=F32)
    ik = tail[:, :IDX_DIM]
    ik = ik * lax.rsqrt(jnp.mean(ik * ik, axis=-1, keepdims=True) + EPS) * ikg_ref[...]
    ik32_ref[...] = ik
    ikbf_ref[...] = ik.astype(BF16)
    iw_ref[...] = tail[:, IDX_DIM:IDX_DIM + IDX_HEADS] * (IDX_HEADS ** -0.5)


def _even_in(x, sh, sc, per_token, tiles_per_batch, tm, w_in, ln_g, ln_b, msp, bsp, q_g, k_g, ik_g):
    n, d = x.shape
    aw = d // 2
    wm = w_in[:, :6 * aw].astype(BF16)
    wt = jnp.pad(w_in[:, 6 * aw:], ((0, 0), (0, LANES - (IDX_DIM + IDX_HEADS)))).astype(BF16)
    full = lambda shape: pl.BlockSpec(shape, lambda i: (0,) * len(shape))
    row = lambda w: pl.BlockSpec((tm, w), lambda i: (i, 0))
    mod = _mod_specs(per_token, tm, d, tiles_per_batch)
    outs = [(aw, BF16), (aw, F32), (aw, BF16), (aw, F32), (aw, BF16), (aw, F32), (B_HEADS * LANES, BF16),
            (aw, BF16), (IDX_DIM, F32), (IDX_DIM, BF16), (IDX_HEADS, F32)]
    return pl.pallas_call(
        _even_in_kernel,
        grid=(n // tm,),
        in_specs=[row(d), mod, mod, full(wm.shape), full(wt.shape), full((1, aw)), full((1, aw)),
                  full(msp.shape), full(bsp.shape), full((1, aw)), full((1, aw)), full((1, IDX_DIM)), full((aw, aw))],
        out_specs=[_row_spec(tm, w) for w, _ in outs],
        out_shape=[_row_shape(n, w, dt) for w, dt in outs],
        compiler_params=_cparams(("arbitrary",)),
        name="even_in",
    )(x, sh, sc, wm, wt, ln_g.reshape(1, aw), ln_b.reshape(1, aw), msp, bsp,
      jnp.tile(q_g, B_HEADS).reshape(1, aw), jnp.tile(k_g, B_HEADS).reshape(1, aw), ik_g.reshape(1, IDX_DIM),
      _block_ones(aw, aw // B_HEADS))


def _odd_in_kernel(x_ref, sh_ref, sc_ref, wm_ref, wt_ref, qg_ref, kg_ref, e_ref,
                   xc_ref, cz_ref, tail_ref, q_ref, k32_ref, kbf_ref, v32_ref, vx_ref):
    hw = cz_ref.shape[1]
    h = _modulated_rms(x_ref[...], sh_ref[...], sc_ref[...]).astype(BF16)

    def seg(i):
        return jnp.dot(h, wm_ref[:, i * hw:(i + 1) * hw], preferred_element_type=F32)

    for i in range(3):
        xc_ref[:, i * hw:(i + 1) * hw] = seg(i)
    cz_ref[...] = seg(3)
    tail_ref[...] = jnp.dot(h, wt_ref[...], preferred_element_type=F32)
    q = _group_rms(seg(4), e_ref, D_QK_DIM) * qg_ref[...]
    q_ref[...] = (q * (D_QK_DIM ** -0.5)).astype(BF16)
    k = _group_rms(seg(5), e_ref, D_QK_DIM) * kg_ref[...]
    k32_ref[...] = k
    kbf_ref[...] = k.astype(BF16)
    v = seg(6)
    v32_ref[...] = v
    _store_v_with_ones(vx_ref, v)


def _odd_in(x, sh, sc, per_token, tiles_per_batch, tm, w_in, q_g, k_g):
    n, d = x.shape
    hw = d // 2
    ng = 2 * C_HEADS
    wm = jnp.concatenate([w_in[:, :4 * hw], w_in[:, 4 * hw + ng:]], axis=1).astype(BF16)
    wt = jnp.pad(w_in[:, 4 * hw:4 * hw + ng], ((0, 0), (0, LANES - ng))).astype(BF16)
    full = lambda shape: pl.BlockSpec(shape, lambda i: (0,) * len(shape))
    row = lambda w: pl.BlockSpec((tm, w), lambda i: (i, 0))
    mod = _mod_specs(per_token, tm, d, tiles_per_batch)
    outs = [(3 * hw, F32), (hw, F32), (LANES, F32), (hw, BF16), (hw, F32), (hw, BF16), (hw, F32),
            (D_HEADS * LANES, BF16)]
    reps = hw // D_QK_DIM
    return pl.pallas_call(
        _odd_in_kernel,
        grid=(n // tm,),
        in_specs=[row(d), mod, mod, full(wm.shape), full(wt.shape), full((1, hw)), full((1, hw)), full((hw, hw))],
        out_specs=[_row_spec(tm, w) for w, _ in outs],
        out_shape=[_row_shape(n, w, dt) for w, dt in outs],
        compiler_params=_cparams(("arbitrary",)),
        name="odd_in",
    )(x, sh, sc, wm, wt, jnp.tile(q_g, reps).reshape(1, hw), jnp.tile(k_g, reps).reshape(1, hw),
      _block_ones(hw, D_QK_DIM))


def _out_proj_kernel(a_ref, b_ref, x_ref, g1_ref, sh_ref, sc_ref, w_ref, rw_ref, rb_ref, x1_ref, h2_ref, lg_ref):
    hw = a_ref.shape[1]
    y = jnp.dot(a_ref[...], w_ref[:hw, :], preferred_element_type=F32)
    y = y + jnp.dot(b_ref[...], w_ref[hw:, :], preferred_element_type=F32)
    x1 = x_ref[...] + g1_ref[...] * y
    x1_ref[...] = x1
    h2 = _modulated_rms(x1, sh_ref[...], sc_ref[...])
    h2_ref[...] = h2
    lg_ref[...] = _dot(h2, rw_ref[...]) + rb_ref[...]


def _out_proj(a, b, x, g1, sh, sc, per_token, tiles_per_batch, tm, w_out, router_w, router_b):
    n, d = x.shape
    hw = a.shape[1]
    ne = router_w.shape[1]
    rw = jnp.pad(router_w, ((0, 0), (0, LANES - ne)))
    rb = jnp.pad(router_b, (0, LANES - ne)).reshape(1, LANES)
    full = lambda shape: pl.BlockSpec(shape, lambda i: (0,) * len(shape))
    row = lambda w: pl.BlockSpec((tm, w), lambda i: (i, 0))
    mod = _mod_specs(per_token, tm, d, tiles_per_batch)
    x1, h2, lg = pl.pallas_call(
        _out_proj_kernel,
        grid=(n // tm,),
        in_specs=[row(hw), row(hw), row(d), mod, mod, mod, full((d, d)), full((d, LANES)), full((1, LANES))],
        out_specs=[row(d), row(d), row(LANES)],
        out_shape=[jax.ShapeDtypeStruct((n, d), F32), jax.ShapeDtypeStruct((n, d), F32),
                   jax.ShapeDtypeStruct((n, LANES), F32)],
        compiler_params=_cparams(("arbitrary",)),
        name="out_proj",
    )(a, b, x, g1, sh, sc, w_out.astype(BF16), rw, rb)
    return x1, h2, lg


def _select_kernel(nch_ref, s_ref, lim_ref, m_ref, key_ref, *, ksel, kc):
    _select_rows(lambda off: s_ref[:, pl.ds(off, kc)], nch_ref[pl.program_id(0)], lim_ref[...], m_ref, key_ref, ksel, kc)


def _select_prompt_kernel(iq_ref, iw_ref, ik_ref, m_ref, key_ref, *, ksel, kc):
    rows = iq_ref.shape[0]
    first = pl.program_id(1) * rows

    iq_heads = jnp.concatenate([iq_ref[:, h * IDX_DIM:(h + 1) * IDX_DIM] for h in range(IDX_HEADS)], axis=0)

    def scores(off):
        d = _dot_nt(iq_heads, ik_ref[pl.ds(off, kc), :])
        acc = jnp.zeros((rows, kc), F32)
        for h in range(IDX_HEADS):
            acc = acc + iw_ref[:, h:h + 1] * jnp.maximum(d[h * rows:(h + 1) * rows], 0.0)
        return acc

    lim = first + lax.broadcasted_iota(I32, (rows, 1), 0)
    _select_rows(scores, (first + rows - 1) // kc + 1, lim, m_ref, key_ref, ksel, kc)


def _select_rows(score_chunk, nch, lim, m_ref, key_ref, ksel, kc):
    rows, width = m_ref.shape
    fold = kc // LANES
    col_bits = max(1, int(width - 1).bit_length())

    def cols(c):
        return c * kc + lax.broadcasted_iota(I32, (rows, kc), 1)

    def fill(c, carry):
        off = pl.multiple_of(c * kc, kc)
        bits = lax.bitcast_convert_type(score_chunk(off) + 0.0, I32)
        key = jnp.where(bits < 0, bits ^ jnp.int32(0x7FFFFFFF), bits)
        key_ref[:, pl.ds(off, kc)] = jnp.where(cols(c) <= lim, key, jnp.int32(INT_MIN))
        return carry

    lax.fori_loop(0, nch, fill, 0)

    def count(pred):
        def body(c, acc):
            off = pl.multiple_of(c * kc, kc)
            hit = jnp.where(pred(c, key_ref[:, pl.ds(off, kc)]), 1.0, 0.0)
            part = hit[:, :LANES]
            for f in range(1, fold):
                part = part + hit[:, f * LANES:(f + 1) * LANES]
            return acc + part
        acc = lax.fori_loop(0, nch, body, jnp.zeros((rows, LANES), F32))
        return jnp.sum(acc, axis=1, keepdims=True)

    def thr_bit(b, carry):
        prefix, n_ge = carry
        cand = prefix + (jnp.int32(1) << (31 - b))
        cnt = count(lambda c, key: key >= cand)
        take = cnt >= ksel
        return jnp.where(take, cand, prefix), jnp.where(take, cnt, n_ge)

    thr, n_ge = lax.fori_loop(0, 32, thr_bit, (jnp.full((rows, 1), INT_MIN, I32),
                                               jnp.full((rows, 1), 1.0, F32) * (nch * kc).astype(F32)))

    def tie_break():
        need = ksel - count(lambda c, key: key > thr)

        def tie_bit(b, pos):
            cand = pos + (jnp.int32(1) << (col_bits - 1 - b))
            cnt = count(lambda c, key: (key == thr) & (cols(c) < cand))
            return jnp.where(cnt < need, cand, pos)

        return lax.fori_loop(0, col_bits, tie_bit, jnp.zeros((rows, 1), I32))

    surplus = jnp.where((n_ge > ksel) & (thr > jnp.int32(INT_MIN)), 1.0, 0.0)
    last = lax.cond(jnp.max(surplus) > 0.0, tie_break, lambda: jnp.full((rows, 1), width, I32))

    def emit(c, carry):
        off = pl.multiple_of(c * kc, kc)
        key = key_ref[:, pl.ds(off, kc)]
        col = cols(c)
        sel = ((key > thr) | ((key == thr) & (col <= last))) & (col <= lim)
        m_ref[:, pl.ds(off, kc)] = jnp.where(sel, 0.0, NEG).astype(m_ref.dtype)
        return carry

    lax.fori_loop(0, nch, emit, 0)

    def blank(c, carry):
        off = pl.multiple_of(c * kc, kc)
        m_ref[:, pl.ds(off, kc)] = jnp.full((rows, kc), NEG, m_ref.dtype)
        return carry

    lax.fori_loop(nch, width // kc, blank, 0)


def _select_mask(scores, lim, nch, ksel, kc, out_dtype):
    r, width = scores.shape
    tr = SELECT_ROWS
    return pl.pallas_call(
        functools.partial(_select_kernel, ksel=ksel, kc=kc),
        grid_spec=pltpu.PrefetchScalarGridSpec(
            num_scalar_prefetch=1,
            grid=(r // tr,),
            in_specs=[pl.BlockSpec((tr, width), lambda i, n: (i, 0)),
                      pl.BlockSpec((tr, 1), lambda i, n: (i, 0))],
            out_specs=pl.BlockSpec((tr, width), lambda i, n: (i, 0)),
            scratch_shapes=[pltpu.VMEM((tr, width), I32)]),
        out_shape=jax.ShapeDtypeStruct((r, width), out_dtype),
        compiler_params=_cparams(("arbitrary",)),
        name="select_mask",
    )(nch, scores, lim)


def _select_mask_prompt(iq, iw, ik, ksel, kc, tr):
    bn, s, _ = iq.shape
    assert s % tr == 0 and s % kc == 0 and kc >= ksel
    return pl.pallas_call(
        functools.partial(_select_prompt_kernel, ksel=ksel, kc=kc),
        grid=(bn, s // tr),
        in_specs=[pl.BlockSpec((None, tr, IDX_HEADS * IDX_DIM), lambda b, i: (b, i, 0)),
                  pl.BlockSpec((None, tr, IDX_HEADS), lambda b, i: (b, i, 0)),
                  pl.BlockSpec((None, s, IDX_DIM), lambda b, i: (b, 0, 0))],
        out_specs=pl.BlockSpec((None, tr, s), lambda b, i: (b, i, 0)),
        out_shape=jax.ShapeDtypeStruct((bn, s, s), BF16),
        scratch_shapes=[pltpu.VMEM((tr, s), I32)],
        compiler_params=_cparams(("arbitrary", "arbitrary")),
        name="select_mask_prompt",
    )(iq, iw, ik)


def _prompt_attn_kernel(*refs, nh, nm, dqk, dv, tq, has_mask, lam_init):
    refs = list(refs)
    q_ref, k_ref, vx_ref, tbl_ref = refs[:4]
    pos = 4
    madd_ref = None
    if has_mask:
        madd_ref = refs[pos]
        pos += 1
    lam_ref = subg_ref = None
    if nm == 2:
        lam_ref, subg_ref = refs[pos], refs[pos + 1]
        pos += 2
    o_ref, mx_scr, acc_scr, qz_scr, s_scr, top_scr = refs[pos:pos + 6]
    ma_scr = refs[pos + 6] if has_mask else None
    nu = nh * nm
    nf = tq // LANES
    gu = LANES // dqk
    qi = pl.program_id(1)
    mx_scr[...] = jnp.full(mx_scr.shape, -jnp.inf, F32)
    acc_scr[...] = jnp.zeros(acc_scr.shape, F32)

    lane = lax.broadcasted_iota(I32, (tq, LANES), 1)
    for g in range(nu // gu):
        qt = q_ref[:, g * LANES:(g + 1) * LANES]
        for i in range(gu):
            keep = (lane >= i * dqk) & (lane < (i + 1) * dqk)
            qz_scr[g, i * tq:(i + 1) * tq, :] = jnp.where(keep, qt, jnp.zeros_like(qt))

    def chunk(j, back):
        koff = pl.multiple_of(j * tq, tq)
        if has_mask:
            ma_scr[...] = madd_ref[:, pl.ds(koff, tq)].astype(F32)
        excess = jnp.full((tq, LANES), -jnp.inf, F32)
        for g in range(nu // gu):
            stacked = _dot_nt(qz_scr[g], k_ref[pl.ds(koff, tq), g * LANES:(g + 1) * LANES])
            for i in range(gu):
                u = g * gu + i
                s = stacked[i * tq:(i + 1) * tq]
                if back is not None:
                    s = s + tbl_ref[u // nm, back]
                if has_mask:
                    s = s + ma_scr[...]
                s_scr[u] = s
                r = s[:, :LANES]
                for f in range(1, nf):
                    r = jnp.maximum(r, s[:, f * LANES:(f + 1) * LANES])
                top_scr[u] = r
                excess = jnp.maximum(excess, r - mx_scr[u])

        @pl.when(jnp.max(excess) > SHIFT_MARGIN)
        def _():
            for u in range(nu):
                old = mx_scr[u]
                new = jnp.maximum(old, jnp.broadcast_to(jnp.max(top_scr[u], axis=-1, keepdims=True), (tq, LANES)))
                acc_scr[u] = acc_scr[u] * jnp.exp(old - new)
                mx_scr[u] = new

        for g in range(nu // gu):
            ps = []
            for i in range(gu):
                u = g * gu + i
                mx = mx_scr[u]
                p = jnp.concatenate([jnp.exp(s_scr[u, :, f * LANES:(f + 1) * LANES] - mx) for f in range(nf)], axis=1)
                ps.append(p.astype(BF16))
            for i0 in range(0, gu, nm):
                u0 = g * gu + i0
                h = u0 // nm
                stack = ps[i0] if nm == 1 else jnp.concatenate(ps[i0:i0 + nm], axis=0)
                pv = jnp.dot(stack, vx_ref[pl.ds(koff, tq), h * LANES:(h + 1) * LANES], preferred_element_type=F32)
                for m in range(nm):
                    acc_scr[u0 + m] += pv[m * tq:(m + 1) * tq]

    def far(j, carry):
        chunk(j, None)
        return carry

    lax.fori_loop(0, jnp.maximum(qi - 1, 0), far, 0)

    @pl.when(qi >= 1)
    def _():
        chunk(qi - 1, 1)

    chunk(qi, 0)

    def normalised(u):
        a = acc_scr[u]
        return a[:, :dv] / a[:, dv:dv + 1]

    for h in range(nh):
        if nm == 1:
            o_ref[:, h * dv:(h + 1) * dv] = normalised(h).astype(o_ref.dtype)
        else:
            att = normalised(2 * h) - lam_ref[0] * normalised(2 * h + 1)
            att = att * lax.rsqrt(jnp.mean(att * att, axis=-1, keepdims=True) + EPS)
            o_ref[:, h * dv:(h + 1) * dv] = (att * subg_ref[...] * (1.0 - lam_init)).astype(o_ref.dtype)


def _bias_tables(rel_bias, tq):
    far = rel_bias[N_BUCKETS - 1].astype(F32)[:, None, None]
    r = np.arange(tq)[:, None]
    c = np.arange(tq)[None, :]
    tabs = []
    for d in range(2):
        dist = d * tq + r - c
        vals = _bias_by_bucket(rel_bias, _bucket_of(dist)) - far
        tabs.append(jnp.where(jnp.asarray(dist >= 0)[None], vals, NEG))
    return jnp.stack(tabs, axis=1)


def _prompt_attn(q, k, vx, tbl, madd, lam, subg, *, nh, nm, dqk, dv, tq, lam_init):
    bn, s, w = q.shape
    assert MAX_DISTANCE <= tq and s % tq == 0 and tq % LANES == 0 and dv < LANES
    has_mask = madd is not None
    once = lambda width: pl.BlockSpec((None, s, width), lambda b, i: (b, 0, 0), pipeline_mode=pl.Buffered(1))
    in_specs = [pl.BlockSpec((None, tq, w), lambda b, i: (b, i, 0)), once(w), once(nh * LANES),
                pl.BlockSpec(tbl.shape, lambda b, i: (0, 0, 0, 0), pipeline_mode=pl.Buffered(1))]
    args = [q, k, vx, tbl]
    if has_mask:
        in_specs.append(pl.BlockSpec((None, tq, s), lambda b, i: (b, i, 0)))
        args.append(madd)
    if nm == 2:
        in_specs += [pl.BlockSpec(memory_space=pltpu.SMEM), pl.BlockSpec((1, dv), lambda b, i: (0, 0))]
        args += [lam, subg]
    return pl.pallas_call(
        functools.partial(_prompt_attn_kernel, nh=nh, nm=nm, dqk=dqk, dv=dv, tq=tq, has_mask=has_mask, lam_init=lam_init),
        grid=(bn, s // tq),
        in_specs=in_specs,
        out_specs=pl.BlockSpec((None, tq, nh * dv), lambda b, i: (b, i, 0)),
        out_shape=jax.ShapeDtypeStruct((bn, s, nh * dv), BF16),
        scratch_shapes=[pltpu.VMEM((nh * nm, tq, LANES), F32), pltpu.VMEM((nh * nm, tq, LANES), F32),
                        pltpu.VMEM((nh * nm * dqk // LANES, LANES // dqk * tq, LANES), BF16),
                        pltpu.VMEM((nh * nm, tq, tq), F32), pltpu.VMEM((nh * nm, tq, LANES), F32)]
                       + ([pltpu.VMEM((tq, tq), F32)] if has_mask else []),
        compiler_params=_cparams(("arbitrary", "arbitrary")),
        name="prompt_attn_dsa" if nm == 1 else "prompt_attn_diff",
    )(*args)


def _paged_scores_kernel(pt_ref, iq_ref, iw_ref, *refs, gp, page):
    pages = refs[:gp]
    new_ref, o_ref = refs[gp], refs[gp + 1]
    s_id = pl.program_id(1)
    last = pl.num_programs(1) - 1
    rows = iq_ref.shape[0]
    tp = rows // IDX_HEADS

    def score(ik_t):
        d = jnp.maximum(_dot(iq_ref[...], ik_t), 0.0) * iw_ref[...]
        acc = d[:tp]
        for h in range(1, IDX_HEADS):
            acc = acc + d[h * tp:(h + 1) * tp]
        return acc

    @pl.when(s_id < last)
    def _():
        for g in range(gp):
            o_ref[:, g * page:(g + 1) * page] = score(pages[g][...])

    @pl.when(s_id == last)
    def _():
        o_ref[...] = jnp.zeros(o_ref.shape, F32)
        o_ref[:, :page] = score(new_ref[...])


def _paged_scores(page_table, li, iq_rows, iw_rows, cache_idx, ik_new, gp):
    bn, rows, _ = iq_rows.shape
    tp = rows // IDX_HEADS
    page = cache_idx.shape[2]
    npages = page_table.shape[1]
    nsteps = npages // gp + 1
    width = nsteps * gp * page

    def page_map(g):
        return lambda b, s, pt: (li, pt[b, jnp.minimum(s * gp + g, npages - 1)], 0, 0)

    return pl.pallas_call(
        functools.partial(_paged_scores_kernel, gp=gp, page=page),
        grid_spec=pltpu.PrefetchScalarGridSpec(
            num_scalar_prefetch=1,
            grid=(bn, nsteps),
            in_specs=[pl.BlockSpec((None, rows, IDX_DIM), lambda b, s, pt: (b, 0, 0)),
                      pl.BlockSpec((None, rows, 1), lambda b, s, pt: (b, 0, 0))]
                     + [pl.BlockSpec((None, None, IDX_DIM, page), page_map(g)) for g in range(gp)]
                     + [pl.BlockSpec((None, IDX_DIM, page), lambda b, s, pt: (b, 0, 0))],
            out_specs=pl.BlockSpec((None, tp, gp * page), lambda b, s, pt: (b, 0, s))),
        out_shape=jax.ShapeDtypeStruct((bn, tp, width), F32),
        compiler_params=_cparams(("arbitrary", "arbitrary")),
        name="paged_idx_scores",
    )(page_table, iq_rows, iw_rows, *([jnp.transpose(cache_idx, (0, 1, 3, 2))] * gp), jnp.transpose(ik_new, (0, 2, 1)))


def _paged_attn_kernel(pt_ref, q_ref, am_ref, *refs, gp, nh, nm, tp, lam_init):
    kpages = refs[:gp]
    vpages = refs[gp:2 * gp]
    knew_ref, vnew_ref = refs[2 * gp], refs[2 * gp + 1]
    pos = 2 * gp + 2
    lam_ref = subg_ref = None
    if nm == 2:
        lam_ref, subg_ref = refs[pos], refs[pos + 1]
        pos += 2
    o_ref, m_scr, l_scr, acc_scr = refs[pos:pos + 4]
    _, hd, page = knew_ref.shape
    dv = vnew_ref.shape[1]
    s_id = pl.program_id(1)
    last = pl.num_programs(1) - 1

    @pl.when(s_id == 0)
    def _():
        m_scr[...] = jnp.full(m_scr.shape, -jnp.inf, F32)
        l_scr[...] = jnp.zeros(l_scr.shape, F32)
        acc_scr[...] = jnp.zeros(acc_scr.shape, F32)

    def blocks(kvs):
        ss = [_dot(q_ref[...], kp[...].reshape(nh * hd, page)) + am_ref[:, g * page:(g + 1) * page]
              for g, (kp, _) in enumerate(kvs)]
        mx = ss[0]
        for s in ss[1:]:
            mx = jnp.maximum(mx, s)
        m_prev = m_scr[...]
        m_new = jnp.maximum(m_prev, jnp.max(mx, axis=-1, keepdims=True))
        alpha = jnp.exp(m_prev - m_new)
        ps = [jnp.exp(s - m_new) for s in ss]
        psum = ps[0]
        for p in ps[1:]:
            psum = psum + p
        pv = [_dot_nt(p, vp[...].reshape(nh * dv, page)) for p, (_, vp) in zip(ps, kvs)]
        tot = pv[0]
        for x in pv[1:]:
            tot = tot + x
        l_scr[...] = alpha * l_scr[...] + jnp.sum(psum, axis=-1, keepdims=True)
        acc_scr[...] = alpha * acc_scr[...] + tot
        m_scr[...] = m_new

    @pl.when(s_id < last)
    def _():
        blocks(list(zip(kpages, vpages)))

    @pl.when(s_id == last)
    def _():
        blocks([(knew_ref, vnew_ref)])
        accn = acc_scr[...] / l_scr[...]
        for h in range(nh):
            cols = slice(h * dv, (h + 1) * dv)
            if nm == 1:
                o_ref[:, cols] = accn[h * tp:(h + 1) * tp, cols]
            else:
                att = accn[2 * h * tp:(2 * h + 1) * tp, cols] - lam_ref[0] * accn[(2 * h + 1) * tp:(2 * h + 2) * tp, cols]
                att = att * lax.rsqrt(jnp.mean(att * att, axis=-1, keepdims=True) + EPS)
                o_ref[:, cols] = att * subg_ref[...] * (1.0 - lam_init)


def _pages_first(cache):
    return jnp.transpose(cache, (0, 1, 3, 4, 2))


def _head_rows(x, tp):
    bn, t, g, w = x.shape
    xp = jnp.pad(x, ((0, 0), (0, tp - t), (0, 0), (0, 0)))
    eye = jnp.eye(g, dtype=x.dtype)
    return jnp.einsum('btgw,gk->bgtkw', xp, eye).reshape(bn, g * tp, g * w)


def _paged_attn(page_table, li, q_rows, addmask, cache_k, cache_v, k_new, v_new, lam, subg, *, gp, nh, nm, tp, lam_init):
    bn, rows, w = q_rows.shape
    page, hd, dv = cache_k.shape[2], cache_k.shape[4], cache_v.shape[4]
    t = k_new.shape[1]
    npages = page_table.shape[1]
    nsteps = npages // gp + 1

    def page_map(g):
        return lambda b, s, pt: (li, pt[b, jnp.minimum(s * gp + g, npages - 1)], 0, 0, 0)

    new_t = lambda a: jnp.transpose(jnp.pad(a, ((0, 0), (0, page - t), (0, 0), (0, 0))), (0, 2, 3, 1))
    in_specs = ([pl.BlockSpec((None, rows, w), lambda b, s, pt: (b, 0, 0)),
                 pl.BlockSpec((None, rows, gp * page), lambda b, s, pt: (b, 0, s))]
                + [pl.BlockSpec((None, None, nh, hd, page), page_map(g)) for g in range(gp)]
                + [pl.BlockSpec((None, None, nh, dv, page), page_map(g)) for g in range(gp)]
                + [pl.BlockSpec((None, nh, hd, page), lambda b, s, pt: (b, 0, 0, 0)),
                   pl.BlockSpec((None, nh, dv, page), lambda b, s, pt: (b, 0, 0, 0))])
    args = [q_rows, addmask] + [_pages_first(cache_k)] * gp + [_pages_first(cache_v)] * gp + [new_t(k_new), new_t(v_new)]
    if nm == 2:
        in_specs += [pl.BlockSpec(memory_space=pltpu.SMEM), pl.BlockSpec((1, dv), lambda b, s, pt: (0, 0))]
        args += [lam, subg]
    return pl.pallas_call(
        functools.partial(_paged_attn_kernel, gp=gp, nh=nh, nm=nm, tp=tp, lam_init=lam_init),
        grid_spec=pltpu.PrefetchScalarGridSpec(
            num_scalar_prefetch=1,
            grid=(bn, nsteps),
            in_specs=in_specs,
            out_specs=pl.BlockSpec((None, tp, nh * dv), lambda b, s, pt: (b, 0, 0)),
            scratch_shapes=[pltpu.VMEM((rows, 1), F32), pltpu.VMEM((rows, 1), F32), pltpu.VMEM((rows, nh * dv), F32)]),
        out_shape=jax.ShapeDtypeStruct((bn, tp, nh * dv), F32),
        compiler_params=_cparams(("arbitrary", "arbitrary")),
        name="paged_attn_dsa" if nm == 1 else "paged_attn_diff",
    )(page_table, *args)


def _gdn_kernel(xc_ref, prev_ref, buf_ref, cz_ref, tail_ref, s0_ref, cw_ref, alog_ref, dtb_ref, nw_ref,
                o_ref, sfin_ref, s_scr, *, n_valid):
    ci = pl.program_id(1)
    bg, c, ch = xc_ref.shape
    dk = C_DK
    dv = (ch - 2 * C_HEADS * dk) // C_HEADS

    @pl.when(ci == 0)
    def _():
        s_scr[...] = s0_ref[...]

    row = lax.broadcasted_iota(I32, (c, c), 0)
    colm = lax.broadcasted_iota(I32, (c, c), 1)
    incl = row >= colm
    tril = incl.astype(F32)
    eye = (row == colm).astype(F32)

    units = []
    for b in range(bg):
        hist = jnp.where(ci == 0, buf_ref[b], prev_ref[b])
        nh_rows = hist.shape[0]
        xfull = jnp.concatenate([hist, xc_ref[b]], axis=0)
        y = cw_ref[C_CONV - 1:C_CONV, :] * xfull[nh_rows:, :]
        for j in range(1, C_CONV):
            y = y + cw_ref[C_CONV - 1 - j:C_CONV - j, :] * pltpu.roll(xfull, j, 0)[nh_rows:, :]
        y = _silu(y)

        tail = tail_ref[b]
        beta_all = jax.nn.sigmoid(tail)
        sp_in = tail + dtb_ref[...]
        g_all = -jnp.exp(alog_ref[...]) * (jnp.maximum(sp_in, 0.0) + jnp.log1p(jnp.exp(-jnp.abs(sp_in))))
        if n_valid < c:
            valid = lax.broadcasted_iota(I32, tail.shape, 0) < n_valid
            beta_all = jnp.where(valid, beta_all, 0.0)
            g_all = jnp.where(valid, g_all, 0.0)
        gc_all = _dot_hi(tril, g_all)
        gc_rows = jnp.transpose(gc_all)

        for h in range(C_HEADS):
            q = y[:, h * dk:(h + 1) * dk]
            k = y[:, (C_HEADS + h) * dk:(C_HEADS + h + 1) * dk]
            v = y[:, 2 * C_HEADS * dk + h * dv:2 * C_HEADS * dk + (h + 1) * dv]
            q = q * lax.rsqrt(jnp.sum(q * q, axis=-1, keepdims=True) + EPS) * (dk ** -0.5)
            k = k * lax.rsqrt(jnp.sum(k * k, axis=-1, keepdims=True) + EPS)
            beta = beta_all[:, h:h + 1]
            gc = gc_all[:, C_HEADS + h:C_HEADS + h + 1]
            decay = jnp.exp(jnp.where(incl, gc - gc_rows[C_HEADS + h:C_HEADS + h + 1, :], -jnp.inf))
            units.append(dict(b=b, h=h, q=q, k=k, kb=k * beta, vb=v * beta, gc=gc, decay=decay))

    def stage(fn):
        return [fn(un) for un in units]

    def put(name, vals):
        for un, val in zip(units, vals):
            un[name] = val

    put('a', stage(lambda un: _dot_nt(un['kb'], un['k']) * un['decay'] * (1.0 - eye)))
    put('inv', stage(lambda un: eye - un['a']))
    put('pw', stage(lambda un: un['a']))
    for _ in range(max(0, int(c - 1).bit_length() - 1)):
        put('pw', stage(lambda un: _dot_x3(un['pw'], un['pw'])))
        put('inv', stage(lambda un: un['inv'] + _dot_x3(un['inv'], un['pw'])))
    put('u', stage(lambda un: _dot_x3(un['inv'], un['vb'])))
    put('w', stage(lambda un: _dot_x3(un['inv'], un['kb'] * jnp.exp(un['gc']))))
    put('s', stage(lambda un: s_scr[un['b'], un['h']]))
    put('v_new', stage(lambda un: un['u'] - _dot(un['w'], un['s'])))
    put('intra', stage(lambda un: _dot_nt(un['q'], un['k']) * un['decay']))
    put('o', stage(lambda un: _dot(un['q'] * jnp.exp(un['gc']), un['s']) + _dot(un['intra'], un['v_new'])))
    for un in units:
        gl = un['gc'][c - 1:c, :]
        s_scr[un['b'], un['h']] = un['s'] * jnp.exp(gl) + _dot_tn(un['k'] * jnp.exp(gl - un['gc']), un['v_new'])
    for un in units:
        b, h, o = un['b'], un['h'], un['o']
        o = o * lax.rsqrt(jnp.mean(o * o, axis=-1, keepdims=True) + EPS) * nw_ref[...]
        o_ref[b, :, h * dv:(h + 1) * dv] = (o * _silu(cz_ref[b, :, h * dv:(h + 1) * dv])).astype(o_ref.dtype)

    @pl.when(ci == pl.num_programs(1) - 1)
    def _():
        sfin_ref[...] = s_scr[...]


def _gdn(xc, buf, cz, tail, s0, conv_w, a_log, dt_bias, norm_w, chunk, n_valid):
    bn, t, ch = xc.shape
    hw = cz.shape[2]
    nck = t // chunk
    dvh = hw // C_HEADS
    hist = SUBLANES
    per_chunk = chunk // hist
    bg = 2 if bn % 2 == 0 else 1
    gate_lanes = lambda p: jnp.pad(p.astype(F32), (C_HEADS, LANES - 2 * C_HEADS)).reshape(1, LANES)
    return pl.pallas_call(
        functools.partial(_gdn_kernel, n_valid=n_valid),
        grid=(bn // bg, nck),
        in_specs=[pl.BlockSpec((bg, chunk, ch), lambda b, i: (b, i, 0)),
                  pl.BlockSpec((bg, hist, ch), lambda b, i: (b, jnp.maximum(i * per_chunk - 1, 0), 0)),
                  pl.BlockSpec((bg, hist, ch), lambda b, i: (b, 0, 0)),
                  pl.BlockSpec((bg, chunk, hw), lambda b, i: (b, i, 0)),
                  pl.BlockSpec((bg, chunk, LANES), lambda b, i: (b, i, 0)),
                  pl.BlockSpec((bg, C_HEADS, C_DK, dvh), lambda b, i: (b, 0, 0, 0)),
                  pl.BlockSpec((C_CONV, ch), lambda b, i: (0, 0)),
                  pl.BlockSpec((1, LANES), lambda b, i: (0, 0)),
                  pl.BlockSpec((1, LANES), lambda b, i: (0, 0)),
                  pl.BlockSpec((1, dvh), lambda b, i: (0, 0))],
        out_specs=[pl.BlockSpec((bg, chunk, hw), lambda b, i: (b, i, 0)),
                   pl.BlockSpec((bg, C_HEADS, C_DK, dvh), lambda b, i: (b, 0, 0, 0))],
        out_shape=[jax.ShapeDtypeStruct((bn, t, hw), BF16), jax.ShapeDtypeStruct((bn, C_HEADS, C_DK, dvh), F32)],
        scratch_shapes=[pltpu.VMEM((bg, C_HEADS, C_DK, dvh), F32)],
        compiler_params=_cparams(("arbitrary", "arbitrary")),
        name="gated_delta",
    )(xc, xc, buf, cz, tail, s0, conv_w, gate_lanes(a_log), gate_lanes(dt_bias), norm_w.reshape(1, dvh))


def _moe_kernel(be_ref, x_ref, gate_ref, w1_ref, b1_ref, w2_ref, b2_ref, o_ref, w1_scr, w2_scr):
    i = pl.program_id(0)
    dff = w2_ref.shape[0]
    changed = jnp.logical_or(i == 0, be_ref[i] != be_ref[jnp.maximum(i - 1, 0)])
    used = i < be_ref[pl.num_programs(0)]

    @pl.when(jnp.logical_and(used, changed))
    def _():
        w1_scr[...] = w1_ref[...].astype(BF16)
        w2_scr[...] = w2_ref[...].astype(BF16)

    @pl.when(used)
    def _():
        hmid = jnp.dot(x_ref[...].astype(BF16), w1_scr[...], preferred_element_type=F32) + b1_ref[...]
        gate = jnp.minimum(hmid[:, :dff], SWIGLU_LIMIT)
        up = jnp.clip(hmid[:, dff:], -SWIGLU_LIMIT, SWIGLU_LIMIT)
        act = (up + 1.0) * gate * jax.nn.sigmoid(SWIGLU_ALPHA * gate)
        y = jnp.dot(act.astype(BF16), w2_scr[...], preferred_element_type=F32) + b2_ref[...]
        o_ref[...] = y * gate_ref[...]

    @pl.when(jnp.logical_not(used))
    def _():
        o_ref[...] = jnp.zeros(o_ref.shape, F32)


def _moe_ffn(xs, slot_gate, block_e, layer, w1, b1, w2, b2, bm):
    n_slots, d = xs.shape
    _, ne, _, d2 = w1.shape
    dff = w2.shape[2]
    return pl.pallas_call(
        _moe_kernel,
        grid_spec=pltpu.PrefetchScalarGridSpec(
            num_scalar_prefetch=1,
            grid=(n_slots // bm,),
            in_specs=[pl.BlockSpec((bm, d), lambda i, be: (i, 0)),
                      pl.BlockSpec((bm, 1), lambda i, be: (i, 0)),
                      pl.BlockSpec((None, None, d, d2), lambda i, be: (layer, be[i], 0, 0)),
                      pl.BlockSpec((None, None, 1, d2), lambda i, be: (layer, be[i], 0, 0)),
                      pl.BlockSpec((None, None, dff, d), lambda i, be: (layer, be[i], 0, 0)),
                      pl.BlockSpec((None, None, 1, d), lambda i, be: (layer, be[i], 0, 0))],
            out_specs=pl.BlockSpec((bm, d), lambda i, be: (i, 0)),
            scratch_shapes=[pltpu.VMEM((d, d2), BF16), pltpu.VMEM((dff, d), BF16)]),
        out_shape=jax.ShapeDtypeStruct((n_slots, d), F32),
        compiler_params=_cparams(("arbitrary",)),
        name="moe_ffn",
    )(block_e, xs, slot_gate.reshape(n_slots, 1), w1, b1.reshape(b1.shape[0], ne, 1, d2), w2, b2.reshape(b2.shape[0], ne, 1, d))


def _route_kernel(lg_ref, e_ref, g_ref, r_ref, cnt_ref, run_scr, *, ne):
    i = pl.program_id(0)
    tm = lg_ref.shape[0]

    @pl.when(i == 0)
    def _():
        run_scr[...] = jnp.zeros(run_scr.shape, F32)

    lane = lax.broadcasted_iota(I32, (tm, LANES), 1)
    lane_f = lane.astype(F32)
    x = jnp.where(lane < ne, lg_ref[...], -jnp.inf)
    vals, hots = [], []
    e_out = jnp.zeros((tm, LANES), F32)
    for j in range(TOP_K):
        m = jnp.max(x, axis=-1, keepdims=True)
        idx = jnp.min(jnp.where(x == m, lane_f, float(LANES)), axis=-1, keepdims=True)
        hot = lane_f == idx
        vals.append(m)
        hots.append(hot)
        e_out = jnp.where(lane == j, idx, e_out)
        x = jnp.where(hot, -jnp.inf, x)
    ex = [jnp.exp(v - vals[0]) for v in vals]
    denom = ex[0]
    for j in range(1, TOP_K):
        denom = denom + ex[j]
    chosen = jnp.where(hots[0], 1.0, 0.0)
    for j in range(1, TOP_K):
        chosen = chosen + jnp.where(hots[j], 1.0, 0.0)
    before = (lax.broadcasted_iota(I32, (tm, tm), 0) > lax.broadcasted_iota(I32, (tm, tm), 1)).astype(BF16)
    base = run_scr[...] + jnp.dot(before, chosen.astype(BF16), preferred_element_type=F32)
    g_out = jnp.zeros((tm, LANES), F32)
    r_out = jnp.zeros((tm, LANES), F32)
    for j in range(TOP_K):
        g_out = jnp.where(lane == j, ex[j] / denom, g_out)
        r_out = jnp.where(lane == j, jnp.sum(jnp.where(hots[j], base, 0.0), axis=-1, keepdims=True), r_out)
    e_ref[...] = e_out[:, :TOP_K].astype(I32)
    g_ref[...] = g_out[:, :TOP_K]
    r_ref[...] = r_out[:, :TOP_K].astype(I32)
    run_scr[...] = run_scr[...] + jnp.sum(chosen, axis=0, keepdims=True)

    @pl.when(i == pl.num_programs(0) - 1)
    def _():
        cnt_ref[...] = run_scr[...]


def _route(logits, ne):
    n = logits.shape[0]
    tm = next(t for t in (512, 384, 256, 128, 64, 32, 16, 8) if n % t == 0)
    small = lambda dt: jax.ShapeDtypeStruct((n, TOP_K), dt)
    return pl.pallas_call(
        functools.partial(_route_kernel, ne=ne),
        grid=(n // tm,),
        in_specs=[pl.BlockSpec((tm, LANES), lambda i: (i, 0))],
        out_specs=[pl.BlockSpec((tm, TOP_K), lambda i: (i, 0))] * 3 + [pl.BlockSpec((1, LANES), lambda i: (0, 0))],
        out_shape=[small(I32), small(F32), small(I32), jax.ShapeDtypeStruct((1, LANES), F32)],
        scratch_shapes=[pltpu.VMEM((1, LANES), F32)],
        compiler_params=_cparams(("arbitrary",)),
        name="moe_route",
    )(logits)


def _moe(h2, logits, layer, w1, b1, w2, b2, bm):
    n_tok = h2.shape[0]
    ne = w1.shape[1]
    top_idx, gates, rank, counts = _route(logits, ne)
    counts = counts[0, :ne].astype(I32)
    padded = (counts + bm - 1) // bm * bm
    pad_end = jnp.cumsum(padded)
    pad_start = pad_end - padded
    n_assign = n_tok * TOP_K
    n_blocks = -(-(n_assign + ne * (bm - 1)) // bm)
    n_slots = n_blocks * bm
    hot = top_idx[..., None] == jnp.arange(ne, dtype=I32)
    dest = (jnp.sum(jnp.where(hot, pad_start, 0), axis=-1) + rank).reshape(-1)
    block_e = jnp.minimum(jnp.sum(pad_end[None, :] <= (jnp.arange(n_blocks, dtype=I32) * bm)[:, None], axis=1),
                          ne - 1).astype(I32)
    block_e = jnp.concatenate([block_e, (pad_end[-1:] // bm).astype(I32)])
    payload = jnp.stack([jnp.repeat(jnp.arange(n_tok, dtype=I32), TOP_K),
                         lax.bitcast_convert_type(gates.reshape(-1), I32)], axis=-1)
    slots = jnp.zeros((n_slots, 2), I32).at[dest].set(payload)
    slot_gate = lax.bitcast_convert_type(slots[:, 1], F32)
    ys = _moe_ffn(h2[slots[:, 0]], slot_gate, block_e, layer, w1, b1, w2, b2, bm)
    return jnp.sum(ys[dest.reshape(n_tok, TOP_K).T], axis=0)


def _rel_bias_sample(rel_bias, past_len, t):
    dist = (past_len + np.arange(t))[:, None] - np.arange(past_len + t)[None, :]
    return _bias_by_bucket(rel_bias, _bucket_of(dist)), dist >= 0


def kernel(x_prompt, x_sample, cache_b_k, cache_b_v, cache_b_idx, state_c_rec, state_c_conv, cache_d_k, cache_d_v,
           page_table, c_prompt, c_sample, rel_bias, ada_w, ada_b, even_w_in, even_w_out, a_ln_g, a_ln_b, a_w_sp,
           a_b_sp, b_q_norm, b_k_norm, b_idx_norm, odd_w_in, odd_w_out, c_conv_w, c_a_log, c_dt_bias, c_norm_w,
           d_q_norm, d_k_norm, d_lambda, d_subln, router_w, router_b, moe_w1, moe_b1, moe_w2, moe_b2):
    bp, sp, d = x_prompt.shape
    bs, ts, _ = x_sample.shape
    depth = ada_w.shape[0]
    hw = d // 2
    page = cache_b_k.shape[2]
    past_len = page_table.shape[1] * page
    n_p, n_s = bp * sp, bs * ts
    tm_p = TOKEN_TILE
    tq = ATTN_TILE
    tp = SUBLANES
    gp = PAGES_PER_STEP if page_table.shape[1] % PAGES_PER_STEP == 0 else 1
    assert sp % tm_p == 0 and n_s % SUBLANES == 0 and ts <= tp and ts >= C_CONV - 1

    n_c = bp + bs
    c_all = jnp.pad(jnp.concatenate([c_prompt, c_sample], axis=0), ((0, -n_c % SUBLANES), (0, 0)))
    mod_all = _ada_mod(c_all, ada_w, ada_b)

    def mods(layer):
        m = mod_all[layer].reshape(-1, 6, d)
        mp = [m[:bp, i][:, None, :] for i in range(6)]
        ms = [jnp.repeat(m[bp:n_c, i], ts, axis=0) for i in range(6)]
        return mp, ms

    xp = x_prompt.reshape(n_p, d)
    xs = x_sample.reshape(n_s, d)
    tpb = sp // tm_p
    tbl = _bias_tables(rel_bias, tq)
    outs_p, outs_s = {}, {}

    for layer in range(depth):
        li = layer // 2
        mp, ms = mods(layer)
        if layer % 2 == 0:
            ws_p = jnp.where(np.tril(np.ones((A_CHUNK, A_CHUNK), bool)), a_w_sp[li], 0.0).astype(BF16)
            bsp_p = a_b_sp[li].T
            (aout, _, q, k32, kbf, v32, vbf, iq, ik32, ikbf, iw) = _even_in(
                xp, mp[0], mp[1], False, tpb, tm_p, even_w_in[li], a_ln_g[li], a_ln_b[li], ws_p, bsp_p,
                b_q_norm[li], b_k_norm[li], b_idx_norm[li])
            ksel = min(TOPK_MAX, sp // 4)
            kc = SELECT_CHUNK if sp % SELECT_CHUNK == 0 else sp
            madd = _select_mask_prompt(iq.reshape(bp, sp, hw), iw.reshape(bp, sp, IDX_HEADS),
                                       ikbf.reshape(bp, sp, IDX_DIM), ksel, kc, SELECT_ROWS)
            b_out = _prompt_attn(q.reshape(bp, sp, hw), kbf.reshape(bp, sp, hw), vbf.reshape(bp, sp, -1), tbl, madd,
                                 None, None, nh=B_HEADS, nm=1, dqk=hw // B_HEADS, dv=hw // B_HEADS, tq=tq, lam_init=0.0)
            outs_p.setdefault('b_k', []).append(k32.reshape(bp, sp, B_HEADS, -1))
            outs_p.setdefault('b_v', []).append(v32.reshape(bp, sp, B_HEADS, -1))
            outs_p.setdefault('b_idx', []).append(ik32.reshape(bp, sp, IDX_DIM))
            mix_p = (aout, b_out.reshape(n_p, hw))

            cs = min(ts, A_CHUNK)
            ws_s = jnp.where(np.tril(np.ones((cs, cs), bool)), a_w_sp[li][:, :cs, :cs], 0.0)
            ws_s = jnp.einsum('ab,gts->gatbs', jnp.eye(n_s // cs, dtype=F32), ws_s).reshape(A_GROUPS, n_s, n_s).astype(BF16)
            bsp_s = jnp.tile(a_b_sp[li][:, :cs].T, (n_s // cs, 1))
            (aout, av, q, k32, kbf, v32, vbf, iq, ik32, ikbf, iw) = _even_in(
                xs, ms[0], ms[1], True, 1, n_s, even_w_in[li], a_ln_g[li], a_ln_b[li], ws_s, bsp_s,
                b_q_norm[li], b_k_norm[li], b_idx_norm[li])
            ltot = past_len + ts
            ksel = min(TOPK_MAX, ltot // 4)
            iq_rows = jnp.pad(iq.reshape(bs, ts, IDX_HEADS, IDX_DIM), ((0, 0), (0, tp - ts), (0, 0), (0, 0)))
            iq_rows = jnp.transpose(iq_rows, (0, 2, 1, 3)).reshape(bs, IDX_HEADS * tp, IDX_DIM)
            iw_rows = jnp.pad(iw.reshape(bs, ts, IDX_HEADS), ((0, 0), (0, tp - ts), (0, 0)))
            iw_rows = jnp.transpose(iw_rows, (0, 2, 1)).reshape(bs, IDX_HEADS * tp, 1)
            pad_new = lambda a: jnp.pad(a.reshape(bs, ts, -1), ((0, 0), (0, page - ts), (0, 0)))
            sc_s = _paged_scores(page_table, li, iq_rows, iw_rows, cache_b_idx, pad_new(ikbf), gp)
            width = sc_s.shape[2]
            kc_s = gp * page
            lim_s = jnp.where(np.arange(tp) < ts, past_len + np.arange(tp), -1).astype(I32)
            lim_s = jnp.tile(lim_s, bs).reshape(bs * tp, 1)
            nch_s = jnp.full((bs * tp // 128,), width // kc_s, I32)
            sel_s = _select_mask(sc_s.reshape(bs * tp, width), lim_s, nch_s, ksel, kc_s, F32).reshape(bs, tp, width)
            bias_s, _ = _rel_bias_sample(rel_bias, past_len, ts)
            bias_s = jnp.pad(bias_s, ((0, 0), (0, tp - ts), (0, width - ltot)))
            am = (sel_s[:, None] + bias_s[None]).reshape(bs, B_HEADS * tp, width)
            heads = lambda a: a.reshape(bs, ts, B_HEADS, -1)
            b_out = _paged_attn(page_table, li, _head_rows(heads(q), tp), am, cache_b_k, cache_b_v, heads(k32), heads(v32),
                                None, None, gp=gp, nh=B_HEADS, nm=1, tp=tp, lam_init=0.0)
            outs_s.setdefault('a_v', []).append(av.reshape(bs, ts, hw))
            outs_s.setdefault('b_k', []).append(k32.reshape(bs, ts, B_HEADS, -1))
            outs_s.setdefault('b_v', []).append(v32.reshape(bs, ts, B_HEADS, -1))
            outs_s.setdefault('b_idx', []).append(ik32.reshape(bs, ts, IDX_DIM))
            mix_s = (aout, b_out[:, :ts].reshape(n_s, hw).astype(BF16))
            w_out = even_w_out[li]
        else:
            lam_init = 0.8 - 0.6 * math.exp(-0.3 * layer)
            lp = d_lambda[li].astype(F32)
            lam = (jnp.exp(jnp.sum(lp[0] * lp[1])) - jnp.exp(jnp.sum(lp[2] * lp[3])) + lam_init).reshape(1)
            subg = d_subln[li].reshape(1, -1)
            dvh = hw // D_HEADS
            xc, cz, tail, q2, k2_32, k2bf, v2_32, v2bf = _odd_in(xp, mp[0], mp[1], False, tpb, tm_p, odd_w_in[li],
                                                                d_q_norm[li], d_k_norm[li])
            ch = xc.shape[1]
            xc3 = xc.reshape(bp, sp, ch)
            zbuf = jnp.zeros((bp, SUBLANES, ch), F32)
            s0 = jnp.zeros((bp, C_HEADS, C_DK, hw // C_HEADS), F32)
            chunk = C_CHUNK if sp % C_CHUNK == 0 else sp
            c_out, c_rec = _gdn(xc3, zbuf, cz.reshape(bp, sp, hw), tail.reshape(bp, sp, LANES), s0, c_conv_w[li],
                                c_a_log[li], c_dt_bias[li], c_norm_w[li], chunk, chunk)
            d_out = _prompt_attn(q2.reshape(bp, sp, hw), k2bf.reshape(bp, sp, hw), v2bf.reshape(bp, sp, -1), tbl, None,
                                 lam, subg, nh=D_HEADS, nm=2, dqk=D_QK_DIM, dv=dvh, tq=tq, lam_init=lam_init)
            outs_p.setdefault('c_rec', []).append(c_rec)
            outs_p.setdefault('c_conv', []).append(xc3[:, sp - (C_CONV - 1):])
            outs_p.setdefault('d_k', []).append(k2_32.reshape(bp, sp, D_HEADS, -1))
            outs_p.setdefault('d_v', []).append(v2_32.reshape(bp, sp, D_HEADS, -1))
            mix_p = (c_out.reshape(n_p, hw), d_out.reshape(n_p, hw))

            xc, cz, tail, q2, k2_32, k2bf, v2_32, v2bf = _odd_in(xs, ms[0], ms[1], True, 1, n_s, odd_w_in[li],
                                                                d_q_norm[li], d_k_norm[li])
            pad_t = lambda a: jnp.pad(a.reshape(bs, ts, -1), ((0, 0), (0, C_CHUNK - ts), (0, 0)))
            buf = state_c_conv[li]
            buf8 = jnp.pad(buf, ((0, 0), (SUBLANES - (C_CONV - 1), 0), (0, 0)))
            c_out, c_rec = _gdn(pad_t(xc), buf8, pad_t(cz), pad_t(tail), state_c_rec[li], c_conv_w[li], c_a_log[li],
                                c_dt_bias[li], c_norm_w[li], C_CHUNK, ts)
            ltot = past_len + ts
            bias_s, allowed = _rel_bias_sample(rel_bias, past_len, ts)
            width = (page_table.shape[1] // gp + 1) * gp * page
            am = jnp.where(jnp.asarray(allowed)[None], bias_s, NEG)
            am = jnp.pad(am, ((0, 0), (0, tp - ts), (0, 0)))
            am = jnp.pad(am, ((0, 0), (0, 0), (0, width - ltot)), constant_values=NEG)
            am = jnp.broadcast_to(am[None, :, None], (bs, D_HEADS, 2, tp, width)).reshape(bs, D_HEADS * 2 * tp, width)
            q_rows = _head_rows(q2.reshape(bs, ts, 2 * D_HEADS, D_QK_DIM), tp)
            heads = lambda a: a.reshape(bs, ts, D_HEADS, -1)
            d_out = _paged_attn(page_table, li, q_rows, am, cache_d_k, cache_d_v, heads(k2_32), heads(v2_32), lam, subg,
                                gp=gp, nh=D_HEADS, nm=2, tp=tp, lam_init=lam_init)
            outs_s.setdefault('c_rec', []).append(c_rec)
            outs_s.setdefault('c_conv', []).append(
                jnp.concatenate([buf, xc.reshape(bs, ts, ch)], axis=1)[:, ts:])
            outs_s.setdefault('d_k', []).append(k2_32.reshape(bs, ts, D_HEADS, -1))
            outs_s.setdefault('d_v', []).append(v2_32.reshape(bs, ts, D_HEADS, -1))
            mix_s = (c_out[:, :ts].reshape(n_s, hw), d_out[:, :ts].reshape(n_s, hw).astype(BF16))
            w_out = odd_w_out[li]

        x1p, h2p, lgp = _out_proj(mix_p[0], mix_p[1], xp, mp[2], mp[3], mp[4], False, tpb, tm_p, w_out,
                                  router_w[layer], router_b[layer])
        x1s, h2s, lgs = _out_proj(mix_s[0], mix_s[1], xs, ms[2], ms[3], ms[4], True, 1, n_s, w_out,
                                  router_w[layer], router_b[layer])
        moe_out = _moe(jnp.concatenate([h2p, h2s], axis=0), jnp.concatenate([lgp, lgs], axis=0), layer,
                       moe_w1, moe_b1, moe_w2, moe_b2, MOE_BLOCK_ROWS)
        xp = (x1p.reshape(bp, sp, d) + mp[5] * moe_out[:n_p].reshape(bp, sp, d)).reshape(n_p, d)
        xs = x1s + ms[5] * moe_out[n_p:]

    st = lambda name, src: jnp.stack(src[name])
    return (xp.reshape(bp, sp, d), xs.reshape(bs, ts, d),
            st('b_k', outs_p), st('b_v', outs_p), st('b_idx', outs_p), st('c_rec', outs_p), st('c_conv', outs_p),
            st('d_k', outs_p), st('d_v', outs_p),
            st('a_v', outs_s), st('b_k', outs_s), st('b_v', outs_s), st('b_idx', outs_s), st('c_rec', outs_s),
            st('c_conv', outs_s), st('d_k', outs_s), st('d_v', outs_s))
```
